```python
import jax, jax.numpy as jnp
from jax import lax
import numpy as np

D_MODEL = 1024
BATCH = 32
SEQ = 256
DEPTH = 4
DEC_BATCH = 4
DEC_SEQ = 1024
PAST_LEN = 512

GRID_W = 64
N_MIXERS = 3
N_A = (DEPTH + 2) // 3
N_B = (DEPTH + 1) // 3
N_C = DEPTH // 3
BRANCH = D_MODEL
HEAD_DIM = 64
N_HEADS = BRANCH // HEAD_DIM
KV_HEADS = 4
GROUPS = N_HEADS // KV_HEADS
MLA_Q_LORA = 384
MLA_KV_LORA = 256
MLA_NOPE = 64
MLA_ROPE = 32
MLA_V = HEAD_DIM
WINDOW = 128
Q_BLOCK = 128
ROPE_THETA = 10000.0
EPS = 1e-6
NEG_INF = -1e30
MLA_IN = MLA_Q_LORA + MLA_KV_LORA + MLA_ROPE + BRANCH
GQA_IN = N_HEADS * HEAD_DIM + 2 * KV_HEADS * HEAD_DIM + BRANCH

kernel_name = "hybrid_diffusion_prefix_mla_gqa_swa_step"

F32 = jnp.float32


def rms_norm(x, g):
    xf = x.astype(F32)
    y = xf * lax.rsqrt(jnp.mean(xf * xf, axis=-1, keepdims=True) + EPS)
    return (y * g.astype(F32)).astype(x.dtype)


def adaln(cvec, w, b):
    m = jax.nn.silu(cvec) @ w + b
    return jnp.split(m, 3, axis=-1)


def mod_norm(x, g, shift, scale):
    return rms_norm(x, g) * (1.0 + scale) + shift


def axial_rope(T, rot_dim):
    rows = T // GRID_W
    row = jnp.repeat(jnp.arange(rows), GRID_W).astype(F32)
    col = jnp.tile(jnp.arange(GRID_W), rows).astype(F32)
    nf = rot_dim // 4
    freqs = ROPE_THETA ** (-jnp.arange(nf, dtype=F32) / nf)
    ang = jnp.concatenate([row[:, None] * freqs, col[:, None] * freqs], axis=-1)
    return jnp.cos(ang), jnp.sin(ang)


def rope_2d(x, cos, sin):
    xp = x.reshape(x.shape[:-1] + (x.shape[-1] // 2, 2)).astype(F32)
    a, b = xp[..., 0], xp[..., 1]
    shp = (cos.shape[0],) + (1,) * (x.ndim - 3) + (cos.shape[1],)
    c, s = cos.reshape(shp), sin.reshape(shp)
    out = jnp.stack([a * c - b * s, a * s + b * c], axis=-1).reshape(x.shape)
    return out.astype(x.dtype)


def sweep_queries(attend, qs):
    B, T = qs[0].shape[:2]
    nb = T // Q_BLOCK
    blocks = tuple(jnp.moveaxis(q.reshape((B, nb, Q_BLOCK) + q.shape[2:]), 1, 0) for q in qs)
    out = lax.map(lambda xs: attend(*xs), blocks)
    out = jnp.moveaxis(out, 0, 1)
    return out.reshape((B, T) + out.shape[3:])


def mla_project(h, w_in, q_norm, w_uq, kv_norm):
    B, T, _ = h.shape
    z = h @ w_in
    cq, ckv, kpe, gate = jnp.split(z, [MLA_Q_LORA, MLA_Q_LORA + MLA_KV_LORA,
                                       MLA_Q_LORA + MLA_KV_LORA + MLA_ROPE], axis=-1)
    q = (rms_norm(cq, q_norm) @ w_uq).reshape(B, T, N_HEADS, MLA_NOPE + MLA_ROPE)
    return q[..., :MLA_NOPE], q[..., MLA_NOPE:], rms_norm(ckv, kv_norm), kpe, gate


def mla_expand(ckv, w_ukv):
    B, L, _ = ckv.shape
    kv = (ckv @ w_ukv).reshape(B, L, N_HEADS, MLA_NOPE + MLA_V)
    return kv[..., :MLA_NOPE], kv[..., MLA_NOPE:]


def mla_attend(q_nope, q_rope, k_nope, k_rope, v):
    scale = (MLA_NOPE + MLA_ROPE) ** -0.5

    def attend(qn, qr):
        s = (jnp.einsum('bqhd,bkhd->bhqk', qn, k_nope, preferred_element_type=F32)
             + jnp.einsum('bqhr,bkr->bhqk', qr, k_rope, preferred_element_type=F32)) * scale
        p = jax.nn.softmax(s, axis=-1)
        return jnp.einsum('bhqk,bkhd->bqhd', p.astype(v.dtype), v)

    return sweep_queries(attend, (q_nope, q_rope))


def gqa_project(h, w_in):
    B, T, _ = h.shape
    z = h @ w_in
    nq, nk = N_HEADS * HEAD_DIM, KV_HEADS * HEAD_DIM
    q, k, v, gate = jnp.split(z, [nq, nq + nk, nq + 2 * nk], axis=-1)
    return (q.reshape(B, T, KV_HEADS, GROUPS, HEAD_DIM), k.reshape(B, T, KV_HEADS, HEAD_DIM),
            v.reshape(B, T, KV_HEADS, HEAD_DIM), gate)


def gqa_dense(q, k, v, sink=None):
    scale = HEAD_DIM ** -0.5

    def attend(qb):
        s = jnp.einsum('bqhgd,bkhd->bhgqk', qb, k, preferred_element_type=F32) * scale
        if sink is not None:
            sb = jnp.broadcast_to(sink.reshape(KV_HEADS, GROUPS, 1, 1).astype(F32), s.shape[:-1] + (1,))
            p = jax.nn.softmax(jnp.concatenate([s, sb], axis=-1), axis=-1)[..., :-1]
        else:
            p = jax.nn.softmax(s, axis=-1)
        return jnp.einsum('bhgqk,bkhd->bqhgd', p.astype(v.dtype), v)

    return sweep_queries(attend, (q,))


def swa_latent(q, k, v, k_ctx, v_ctx, sink):
    B, T = q.shape[:2]
    nb = T // Q_BLOCK
    Lc = k_ctx.shape[1]
    scale = HEAD_DIM ** -0.5

    def band(x):
        xp = jnp.pad(x, ((0, 0), (Q_BLOCK, Q_BLOCK), (0, 0), (0, 0)))
        xp = xp.reshape((B, nb + 2, Q_BLOCK) + x.shape[2:])
        xw = jnp.concatenate([xp[:, :-2], xp[:, 1:-1], xp[:, 2:]], axis=2)
        return jnp.moveaxis(xw, 1, 0)

    qb = jnp.moveaxis(q.reshape((B, nb, Q_BLOCK) + q.shape[2:]), 1, 0)
    qi = jnp.arange(Q_BLOCK)[:, None]
    kj = jnp.arange(3 * Q_BLOCK)[None, :]
    kpos = jnp.arange(nb)[:, None, None] * Q_BLOCK - Q_BLOCK + kj[None]
    valid = (jnp.abs(kj - Q_BLOCK - qi) <= WINDOW)[None] & (kpos >= 0) & (kpos < T)
    sink_b = sink.reshape(KV_HEADS, GROUPS, 1, 1).astype(F32)

    def attend(xs):
        qn, kw, vw, m = xs
        s_ctx = jnp.einsum('bqhgd,bkhd->bhgqk', qn, k_ctx, preferred_element_type=F32) * scale
        s_loc = jnp.einsum('bqhgd,bkhd->bhgqk', qn, kw, preferred_element_type=F32) * scale
        s_loc = jnp.where(m, s_loc, NEG_INF)
        s_sink = jnp.broadcast_to(sink_b, s_ctx.shape[:-1] + (1,))
        p = jax.nn.softmax(jnp.concatenate([s_ctx, s_loc, s_sink], axis=-1), axis=-1).astype(v.dtype)
        return (jnp.einsum('bhgqk,bkhd->bqhgd', p[..., :Lc], v_ctx)
                + jnp.einsum('bhgqk,bkhd->bqhgd', p[..., Lc:-1], vw))

    out = lax.map(attend, (qb, band(k), band(v), valid))
    out = jnp.moveaxis(out, 0, 1)
    return out.reshape((B, T) + out.shape[3:])


def finish(o, gate, w_out):
    B, T = o.shape[:2]
    return (o.reshape(B, T, BRANCH) * jax.nn.silu(gate)) @ w_out


def setup_inputs(seed: int = 0) -> dict:
    key = jax.random.key(seed)
    ks = jax.random.split(key, 26)
    nrm = jax.random.normal
    D = D_MODEL
    return {
        "x_prompt": nrm(ks[0], (BATCH, SEQ, D), F32),
        "x_sample": nrm(ks[1], (DEC_BATCH, DEC_SEQ, D), F32),
        "cache_mla_ckv": nrm(ks[2], (DEC_BATCH, N_A, PAST_LEN, MLA_KV_LORA), F32),
        "cache_mla_kpe": nrm(ks[3], (DEC_BATCH, N_A, PAST_LEN, MLA_ROPE), F32),
        "cache_gqa_k": nrm(ks[4], (DEC_BATCH, N_B, PAST_LEN, KV_HEADS, HEAD_DIM), F32),
        "cache_gqa_v": nrm(ks[5], (DEC_BATCH, N_B, PAST_LEN, KV_HEADS, HEAD_DIM), F32),
        "cache_swa_k": nrm(ks[6], (DEC_BATCH, N_C, PAST_LEN, KV_HEADS, HEAD_DIM), F32),
        "cache_swa_v": nrm(ks[7], (DEC_BATCH, N_C, PAST_LEN, KV_HEADS, HEAD_DIM), F32),
        "c": nrm(ks[8], (DEC_BATCH, D), F32),
        "c_ctx": nrm(ks[9], (D,), F32),
        "norm_g": 1.0 + 0.1 * nrm(ks[10], (DEPTH, D), F32),
        "w_ada": 0.5 * D ** -0.5 * nrm(ks[11], (DEPTH, D, 3 * D), F32),
        "b_ada": 0.01 * nrm(ks[12], (DEPTH, 3 * D), F32),
        "w_out": BRANCH ** -0.5 * nrm(ks[13], (DEPTH, BRANCH, D), F32),
        "mla_w_in": D ** -0.5 * nrm(ks[14], (N_A, D, MLA_IN), F32),
        "mla_q_norm": 1.0 + 0.1 * nrm(ks[15], (N_A, MLA_Q_LORA), F32),
        "mla_w_uq": MLA_Q_LORA ** -0.5 * nrm(ks[16], (N_A, MLA_Q_LORA, N_HEADS * (MLA_NOPE + MLA_ROPE)), F32),
        "mla_kv_norm": 1.0 + 0.1 * nrm(ks[17], (N_A, MLA_KV_LORA), F32),
        "mla_w_ukv": MLA_KV_LORA ** -0.5 * nrm(ks[18], (N_A, MLA_KV_LORA, N_HEADS * (MLA_NOPE + MLA_V)), F32),
        "gqa_w_in": D ** -0.5 * nrm(ks[19], (N_B, D, GQA_IN), F32),
        "gqa_q_norm": 1.0 + 0.1 * nrm(ks[20], (N_B, HEAD_DIM), F32),
        "gqa_k_norm": 1.0 + 0.1 * nrm(ks[21], (N_B, HEAD_DIM), F32),
        "swa_w_in": D ** -0.5 * nrm(ks[22], (N_C, D, GQA_IN), F32),
        "swa_sink": nrm(ks[23], (N_C, N_HEADS), F32),
        "final_norm_g": 1.0 + 0.1 * nrm(ks[24], (D,), F32),
    }


def reference(x_prompt, x_sample, cache_mla_ckv, cache_mla_kpe, cache_gqa_k, cache_gqa_v,
              cache_swa_k, cache_swa_v, c, c_ctx, norm_g, w_ada, b_ada, w_out,
              mla_w_in, mla_q_norm, mla_w_uq, mla_kv_norm, mla_w_ukv,
              gqa_w_in, gqa_q_norm, gqa_k_norm, swa_w_in, swa_sink, final_norm_g):
    T = x_sample.shape[1]
    cos_r, sin_r = axial_rope(T, MLA_ROPE)
    cos_h, sin_h = axial_rope(T, HEAD_DIM)
    xc, xl = x_prompt, x_sample
    st_ckv, st_kpe, st_gk, st_gv, st_sk, st_sv = [], [], [], [], [], []
    for i in range(DEPTH):
        kind, j = i % N_MIXERS, i // N_MIXERS
        sh_c, sc_c, gt_c = adaln(c_ctx, w_ada[i], b_ada[i])
        sh_l, sc_l, gt_l = (m[:, None, :] for m in adaln(c, w_ada[i], b_ada[i]))
        hc = mod_norm(xc, norm_g[i], sh_c, sc_c)
        hl = mod_norm(xl, norm_g[i], sh_l, sc_l)
        if kind == 0:
            qn, qr, ckv, kpe, g_c = mla_project(hc, mla_w_in[j], mla_q_norm[j], mla_w_uq[j], mla_kv_norm[j])
            kn, vv = mla_expand(ckv, mla_w_ukv[j])
            o_c = mla_attend(qn, qr, kn, kpe, vv)
            st_ckv.append(ckv)
            st_kpe.append(kpe)
            qn_l, qr_l, ckv_l, kpe_l, g_l = mla_project(hl, mla_w_in[j], mla_q_norm[j], mla_w_uq[j], mla_kv_norm[j])
            qr_l = rope_2d(qr_l, cos_r, sin_r)
            kpe_l = rope_2d(kpe_l, cos_r, sin_r)
            ckv_all = jnp.concatenate([cache_mla_ckv[:, j], ckv_l], axis=1)
            kpe_all = jnp.concatenate([cache_mla_kpe[:, j], kpe_l], axis=1)
            kn_all, v_all = mla_expand(ckv_all, mla_w_ukv[j])
            o_l = mla_attend(qn_l, qr_l, kn_all, kpe_all, v_all)
        elif kind == 1:
            q, k, vv, g_c = gqa_project(hc, gqa_w_in[j])
            q, k = rms_norm(q, gqa_q_norm[j]), rms_norm(k, gqa_k_norm[j])
            o_c = gqa_dense(q, k, vv)
            st_gk.append(k)
            st_gv.append(vv)
            q_l, k_l, v_l, g_l = gqa_project(hl, gqa_w_in[j])
            q_l = rope_2d(rms_norm(q_l, gqa_q_norm[j]), cos_h, sin_h)
            k_l = rope_2d(rms_norm(k_l, gqa_k_norm[j]), cos_h, sin_h)
            k_all = jnp.concatenate([cache_gqa_k[:, j], k_l], axis=1)
            v_all = jnp.concatenate([cache_gqa_v[:, j], v_l], axis=1)
            o_l = gqa_dense(q_l, k_all, v_all)
        else:
            q, k, vv, g_c = gqa_project(hc, swa_w_in[j])
            o_c = gqa_dense(q, k, vv, sink=swa_sink[j])
            st_sk.append(k)
            st_sv.append(vv)
            q_l, k_l, v_l, g_l = gqa_project(hl, swa_w_in[j])
            q_l = rope_2d(q_l, cos_h, sin_h)
            k_l = rope_2d(k_l, cos_h, sin_h)
            o_l = swa_latent(q_l, k_l, v_l, cache_swa_k[:, j], cache_swa_v[:, j], swa_sink[j])
        xc = xc + gt_c * finish(o_c, g_c, w_out[i])
        xl = xl + gt_l * finish(o_l, g_l, w_out[i])
    y_prompt = rms_norm(xc, final_norm_g)
    y_sample = rms_norm(xl, final_norm_g)
    new_mla_ckv = jnp.stack(st_ckv, axis=1)
    new_mla_kpe = jnp.stack(st_kpe, axis=1)
    new_gqa_k = jnp.stack(st_gk, axis=1)
    new_gqa_v = jnp.stack(st_gv, axis=1)
    new_swa_k = jnp.stack(st_sk, axis=1)
    new_swa_v = jnp.stack(st_sv, axis=1)
    return (y_prompt, y_sample, new_mla_ckv, new_mla_kpe, new_gqa_k, new_gqa_v, new_swa_k, new_swa_v)
```

```python
import functools

import jax
import jax.numpy as jnp
from jax import lax
from jax.experimental import pallas as pl
from jax.experimental.pallas import tpu as pltpu

F32 = jnp.float32
BF16 = jnp.bfloat16

D_MODEL = 1024
DEPTH = 4
GRID_W = 64
N_HEADS = 16
HEAD_DIM = 64
KV_HEADS = 4
GROUPS = N_HEADS // KV_HEADS
MLA_Q_LORA = 384
MLA_KV_LORA = 256
MLA_NOPE = 64
MLA_ROPE = 32
MLA_V = 64
WINDOW = 128
ROPE_THETA = 10000.0
EPS = 1e-6
NEG_INF = -1e30

LANES = 128
TOKEN_TILE = 256
MLA_HEAD_PAD = 128
MLA_IN_PAD = 1792
VMEM_LIMIT = 56 * 1024 * 1024
NT_DIMS = (((1,), (1,)), ((), ()))


def _silu(v):
    return v * jax.nn.sigmoid(v)


def _rms(v, g):
    return v * lax.rsqrt(jnp.mean(v * v, axis=-1, keepdims=True) + EPS) * g


def _mod_norm(x, g, mod):
    shift = mod[:, :D_MODEL]
    scale = mod[:, D_MODEL:2 * D_MODEL]
    return _rms(x, g) * (1.0 + scale) + shift


def _rope_tile(v, cos, sin_signed):
    lane = lax.broadcasted_iota(jnp.int32, v.shape, 1)
    nxt = pltpu.roll(v, LANES - 1, axis=1)
    prv = pltpu.roll(v, 1, axis=1)
    swapped = jnp.where((lane & 1) == 0, nxt, prv)
    return v * cos + swapped * sin_signed


def _ada_kernel(c_ref, w_ref, b_ref, o_ref):
    s = _silu(c_ref[...]).astype(BF16)
    o_ref[...] = jnp.dot(s, w_ref[...].astype(BF16), preferred_element_type=F32) + b_ref[...]


def _ada_mods(cc, w_ada, b_ada):
    tn = D_MODEL
    return pl.pallas_call(
        _ada_kernel,
        grid=(DEPTH, 3 * D_MODEL // tn),
        in_specs=[
            pl.BlockSpec((8, D_MODEL), lambda i, n: (0, 0)),
            pl.BlockSpec((None, D_MODEL, tn), lambda i, n: (i, 0, n)),
            pl.BlockSpec((None, 1, tn), lambda i, n: (i, 0, n)),
        ],
        out_specs=pl.BlockSpec((None, 8, tn), lambda i, n: (i, 0, n)),
        out_shape=jax.ShapeDtypeStruct((DEPTH, 8, 3 * D_MODEL), F32),
        compiler_params=pltpu.CompilerParams(vmem_limit_bytes=VMEM_LIMIT),
        name="ada_mods",
    )(cc, w_ada, b_ada.reshape(DEPTH, 1, 3 * D_MODEL))


def _mla_expand(ckv_bf, kpe_bf, wk_ref, rp_ref, wvt_ref):
    k = (jnp.dot(ckv_bf, wk_ref[...], preferred_element_type=F32)
         + jnp.dot(kpe_bf, rp_ref[...], preferred_element_type=F32))
    vt = lax.dot_general(wvt_ref[...], ckv_bf, NT_DIMS, preferred_element_type=F32)
    return k.astype(BF16), vt.astype(BF16)


def _proj_mla_kernel(*refs, rope, ctx):
    refs = list(refs)
    x_ref, mod_ref, g_ref, win_ref, qn_ref, wuq_ref, kvn_ref, wk_ref, rp_ref, wvt_ref = refs[:10]
    pos = 10
    if rope:
        cq_ref, sq_ref, ck_ref, sk_ref = refs[pos:pos + 4]
        pos += 4
    q_out, k_out, vt_out, gate_out = refs[pos:pos + 4]
    pos += 4
    if ctx:
        ckv_out, kpe_out = refs[pos:pos + 2]

    h = _mod_norm(x_ref[...], g_ref[...], mod_ref[...]).astype(BF16)
    z = jnp.dot(h, win_ref[...], preferred_element_type=F32)
    cq = z[:, :MLA_Q_LORA]
    ckv = z[:, MLA_Q_LORA:MLA_Q_LORA + MLA_KV_LORA]
    gate = z[:, 640:640 + D_MODEL]
    kpe = z[:, 640 + D_MODEL:]
    gate_out[...] = gate.astype(BF16)

    qf = jnp.dot(_rms(cq, qn_ref[...]).astype(BF16), wuq_ref[...], preferred_element_type=F32)
    ckvn = _rms(ckv, kvn_ref[...])
    if ctx:
        ckv_out[...] = ckvn
        kpe_out[...] = kpe[:, :MLA_ROPE]
    if rope:
        kpe = _rope_tile(kpe, ck_ref[...], sk_ref[...])
    k, vt = _mla_expand(ckvn.astype(BF16), kpe.astype(BF16), wk_ref, rp_ref, wvt_ref)
    k_out[...] = k
    vt_out[...] = vt
    for hd in range(N_HEADS):
        blk = qf[:, MLA_HEAD_PAD * hd:MLA_HEAD_PAD * (hd + 1)]
        if rope:
            blk = _rope_tile(blk, cq_ref[...], sq_ref[...])
        q_out[:, MLA_HEAD_PAD * hd:MLA_HEAD_PAD * (hd + 1)] = blk.astype(BF16)


def _proj_gqa_kernel(*refs, qk_norm, rope, ctx):
    refs = list(refs)
    x_ref, mod_ref, g_ref, win_ref = refs[:4]
    pos = 4
    if qk_norm:
        qn_ref, kn_ref = refs[pos:pos + 2]
        pos += 2
    if rope:
        cos_ref, sin_ref = refs[pos:pos + 2]
        pos += 2
    q_out, k_out, vt_out, gate_out = refs[pos:pos + 4]
    pos += 4
    if ctx:
        kc_out, vc_out = refs[pos:pos + 2]

    nq = N_HEADS * HEAD_DIM
    nk = KV_HEADS * HEAD_DIM
    h = _mod_norm(x_ref[...], g_ref[...], mod_ref[...]).astype(BF16)
    z = jnp.dot(h, win_ref[...], preferred_element_type=F32)
    tm = z.shape[0]
    lo = lax.broadcasted_iota(jnp.int32, (tm, LANES), 1) < HEAD_DIM
    n_q_tiles = nq // LANES
    for c in range((nq + nk) // LANES):
        blk = z[:, LANES * c:LANES * (c + 1)]
        is_q = c < n_q_tiles
        if qk_norm:
            sq = blk * blk
            s_lo = jnp.sum(jnp.where(lo, sq, 0.0), axis=-1, keepdims=True)
            s_hi = jnp.sum(jnp.where(lo, 0.0, sq), axis=-1, keepdims=True)
            inv = jnp.where(lo, lax.rsqrt(s_lo * (1.0 / HEAD_DIM) + EPS),
                            lax.rsqrt(s_hi * (1.0 / HEAD_DIM) + EPS))
            blk = blk * inv * (qn_ref[...] if is_q else kn_ref[...])
        if ctx and not is_q:
            kc_out[:, LANES * (c - n_q_tiles):LANES * (c - n_q_tiles + 1)] = blk
        if rope:
            blk = _rope_tile(blk, cos_ref[...], sin_ref[...])
        if is_q:
            q_out[:, LANES * c:LANES * (c + 1)] = (blk * (HEAD_DIM ** -0.5)).astype(BF16)
        else:
            k_out[:, LANES * (c - n_q_tiles):LANES * (c - n_q_tiles + 1)] = blk.astype(BF16)
    v = z[:, nq + nk:nq + 2 * nk]
    if ctx:
        vc_out[...] = v
    vt_out[...] = v.T.astype(BF16)
    gate_out[...] = z[:, nq + 2 * nk:].astype(BF16)


def _const_spec(shape):
    return pl.BlockSpec(shape, lambda i: (0,) * len(shape))


def _project(kind, x, mods_l, norm_g, weights, tables, *, batch, seq, ctx):
    n_tok = batch * seq
    tm = TOKEN_TILE
    tiles_per_seq = seq // tm
    rope = not ctx
    mod_row = (lambda i: (4, 0, 0)) if ctx else (lambda i: (i // tiles_per_seq, 0, 0))
    in_specs = [
        pl.BlockSpec((tm, D_MODEL), lambda i: (i, 0)),
        pl.BlockSpec((None, 1, 3 * D_MODEL), mod_row),
        _const_spec((1, D_MODEL)),
    ]
    args = [x, mods_l, norm_g.reshape(1, D_MODEL)]
    for w in weights:
        in_specs.append(_const_spec(w.shape))
        args.append(w)
    if rope:
        for t in tables:
            in_specs.append(pl.BlockSpec((tm, LANES), lambda i: (i % tiles_per_seq, 0)))
            args.append(t)
    if kind == 0:
        qw, kw, vw = N_HEADS * MLA_HEAD_PAD, N_HEADS * MLA_HEAD_PAD, N_HEADS * MLA_V
        body = functools.partial(_proj_mla_kernel, rope=rope, ctx=ctx)
        cache_shapes = [(n_tok, MLA_KV_LORA), (n_tok, MLA_ROPE)]
    else:
        qw, kw, vw = N_HEADS * HEAD_DIM, KV_HEADS * HEAD_DIM, KV_HEADS * HEAD_DIM
        body = functools.partial(_proj_gqa_kernel, qk_norm=(kind == 1), rope=rope, ctx=ctx)
        cache_shapes = [(n_tok, kw), (n_tok, vw)]
    out_shape = [
        jax.ShapeDtypeStruct((n_tok, qw), BF16),
        jax.ShapeDtypeStruct((n_tok, kw), BF16),
        jax.ShapeDtypeStruct((batch, vw, seq), BF16),
        jax.ShapeDtypeStruct((n_tok, D_MODEL), BF16),
    ]
    out_specs = [
        pl.BlockSpec((tm, qw), lambda i: (i, 0)),
        pl.BlockSpec((tm, kw), lambda i: (i, 0)),
        pl.BlockSpec((None, vw, tm), lambda i: (i // tiles_per_seq, 0, i % tiles_per_seq)),
        pl.BlockSpec((tm, D_MODEL), lambda i: (i, 0)),
    ]
    if ctx:
        for s in cache_shapes:
            out_shape.append(jax.ShapeDtypeStruct(s, F32))
            out_specs.append(pl.BlockSpec((tm, s[1]), lambda i: (i, 0)))
    return pl.pallas_call(
        body,
        grid=(n_tok // tm,),
        in_specs=in_specs,
        out_specs=out_specs,
        out_shape=out_shape,
        compiler_params=pltpu.CompilerParams(
            dimension_semantics=("arbitrary",), vmem_limit_bytes=VMEM_LIMIT),
        name=f"proj_k{kind}_{'ctx' if ctx else 'lat'}",
    )(*args)


def _mla_cache_kernel(ckv_ref, kpe_ref, wk_ref, rp_ref, wvt_ref, k_out, vt_out):
    k, vt = _mla_expand(ckv_ref[...].astype(BF16), kpe_ref[...].astype(BF16), wk_ref, rp_ref, wvt_ref)
    k_out[...] = k
    vt_out[...] = vt


def _mla_cache(ckv, kpe_pad, wk, rp, wvt):
    b, n, _ = ckv.shape
    return pl.pallas_call(
        _mla_cache_kernel,
        grid=(b,),
        in_specs=[
            pl.BlockSpec((None, n, MLA_KV_LORA), lambda i: (i, 0, 0)),
            pl.BlockSpec((None, n, LANES), lambda i: (i, 0, 0)),
            _const_spec(wk.shape), _const_spec(rp.shape), _const_spec(wvt.shape),
        ],
        out_specs=[
            pl.BlockSpec((None, n, N_HEADS * MLA_HEAD_PAD), lambda i: (i, 0, 0)),
            pl.BlockSpec((None, N_HEADS * MLA_V, n), lambda i: (i, 0, 0)),
        ],
        out_shape=[
            jax.ShapeDtypeStruct((b, n, N_HEADS * MLA_HEAD_PAD), BF16),
            jax.ShapeDtypeStruct((b, N_HEADS * MLA_V, n), BF16),
        ],
        compiler_params=pltpu.CompilerParams(
            dimension_semantics=("arbitrary",), vmem_limit_bytes=VMEM_LIMIT),
        name="mla_cache",
    )(ckv, kpe_pad, wk, rp, wvt)


def _gqa_cache_kernel(k_ref, v_ref, k_out, vt_out):
    k_out[...] = k_ref[...].astype(BF16)
    vt_out[...] = v_ref[...].T.astype(BF16)


def _gqa_cache(k, v):
    b, n, w = k.shape
    return pl.pallas_call(
        _gqa_cache_kernel,
        grid=(b,),
        in_specs=[pl.BlockSpec((None, n, w), lambda i: (i, 0, 0))] * 2,
        out_specs=[
            pl.BlockSpec((None, n, w), lambda i: (i, 0, 0)),
            pl.BlockSpec((None, w, n), lambda i: (i, 0, 0)),
        ],
        out_shape=[
            jax.ShapeDtypeStruct((b, n, w), BF16),
            jax.ShapeDtypeStruct((b, w, n), BF16),
        ],
        compiler_params=pltpu.CompilerParams(dimension_semantics=("arbitrary",)),
        name="gqa_cache",
    )(k, v)


def _attn_kernel(*refs, head_w, group, score_scale, has_cache, has_sink, band, final):
    refs = list(refs)
    q_ref, kn_ref, vtn_ref = refs[:3]
    pos = 3
    segs = []
    if has_cache:
        segs.append((refs[pos], refs[pos + 1]))
        pos += 2
    segs.append((kn_ref, vtn_ref))
    gate_ref, x_ref, mod_ref, wo_ref = refs[pos:pos + 4]
    pos += 4
    if has_sink:
        sink_ref = refs[pos]
        pos += 1
    if final:
        fg_ref = refs[pos]
        pos += 1
    o_ref, ot_scr = refs[pos:pos + 2]

    tq = q_ref.shape[0]
    if band:
        n_own = kn_ref.shape[0]
        kpos = lax.broadcasted_iota(jnp.int32, (n_own, tq), 0)
        qpos = pl.program_id(1) * tq + lax.broadcasted_iota(jnp.int32, (n_own, tq), 1)
        in_band = jnp.abs(kpos - qpos) <= WINDOW

    for hd in range(N_HEADS):
        kh = hd // group
        qh = q_ref[:, head_w * hd:head_w * (hd + 1)]
        scores = []
        for si, (k_ref, _) in enumerate(segs):
            kk = k_ref[:, head_w * kh:head_w * (kh + 1)]
            s = lax.dot_general(kk, qh, NT_DIMS, preferred_element_type=F32)
            if score_scale != 1.0:
                s = s * score_scale
            if band and si == len(segs) - 1:
                s = jnp.where(in_band, s, NEG_INF)
            scores.append(s)
        m = functools.reduce(jnp.maximum, [jnp.max(s, axis=0, keepdims=True) for s in scores])
        if has_sink:
            sink = sink_ref[hd]
            m = jnp.maximum(m, sink)
        den = jnp.exp(sink - m) if has_sink else jnp.zeros_like(m)
        acc = jnp.zeros((HEAD_DIM, tq), F32)
        for s, (_, vt_ref) in zip(scores, segs):
            p = jnp.exp(s - m)
            den = den + jnp.sum(p, axis=0, keepdims=True)
            vt = vt_ref[HEAD_DIM * kh:HEAD_DIM * (kh + 1), :]
            acc = acc + jnp.dot(vt, p.astype(BF16), preferred_element_type=F32)
        ot_scr[HEAD_DIM * hd:HEAD_DIM * (hd + 1), :] = acc / den

    o = ot_scr[...].T
    gated = (o * _silu(gate_ref[...].astype(F32))).astype(BF16)
    y = jnp.dot(gated, wo_ref[...], preferred_element_type=F32)
    xn = x_ref[...] + mod_ref[...][:, 2 * D_MODEL:] * y
    if final:
        xn = _rms(xn, fg_ref[...])
    o_ref[...] = xn


def _attend(kind, q, k_new, vt_new, cache, gate, x, mods_l, w_out, sink, final_g, *, batch, seq, ctx):
    tq = TOKEN_TILE
    n_tok = batch * seq
    tiles = seq // tq
    head_w = MLA_HEAD_PAD if kind == 0 else HEAD_DIM
    kw = k_new.shape[1]
    vw = vt_new.shape[1]
    tok_spec = lambda w: pl.BlockSpec((tq, w), lambda b, t: (b * tiles + t, 0))
    in_specs = [
        tok_spec(q.shape[1]),
        pl.BlockSpec((seq, kw), lambda b, t: (b, 0)),
        pl.BlockSpec((None, vw, seq), lambda b, t: (b, 0, 0)),
    ]
    args = [q, k_new, vt_new]
    if cache is not None:
        kc, vtc = cache
        n_c = kc.shape[1]
        in_specs += [
            pl.BlockSpec((None, n_c, kw), lambda b, t: (b, 0, 0)),
            pl.BlockSpec((None, vw, n_c), lambda b, t: (b, 0, 0)),
        ]
        args += [kc, vtc]
    mod_row = (lambda b, t: (4, 0, 0)) if ctx else (lambda b, t: (b, 0, 0))
    in_specs += [
        tok_spec(D_MODEL),
        tok_spec(D_MODEL),
        pl.BlockSpec((None, 1, 3 * D_MODEL), mod_row),
        pl.BlockSpec((D_MODEL, D_MODEL), lambda b, t: (0, 0)),
    ]
    args += [gate, x, mods_l, w_out]
    if sink is not None:
        in_specs.append(pl.BlockSpec(memory_space=pltpu.SMEM))
        args.append(sink)
    if final_g is not None:
        in_specs.append(pl.BlockSpec((1, D_MODEL), lambda b, t: (0, 0)))
        args.append(final_g.reshape(1, D_MODEL))
    body = functools.partial(
        _attn_kernel,
        head_w=head_w,
        group=1 if kind == 0 else GROUPS,
        score_scale=(MLA_NOPE + MLA_ROPE) ** -0.5 if kind == 0 else 1.0,
        has_cache=cache is not None,
        has_sink=sink is not None,
        band=(kind == 2 and not ctx),
        final=final_g is not None,
    )
    return pl.pallas_call(
        body,
        grid=(batch, tiles),
        in_specs=in_specs,
        out_specs=tok_spec(D_MODEL),
        out_shape=jax.ShapeDtypeStruct((n_tok, D_MODEL), F32),
        scratch_shapes=[pltpu.VMEM((N_HEADS * HEAD_DIM, tq), F32)],
        compiler_params=pltpu.CompilerParams(
            dimension_semantics=("arbitrary", "arbitrary"), vmem_limit_bytes=VMEM_LIMIT),
        name=f"attn_k{kind}_{'ctx' if ctx else 'lat'}",
    )(*args)


def _rope_lane_tables(seq, rot_dim):
    rows = seq // GRID_W
    row = jnp.repeat(jnp.arange(rows), GRID_W).astype(F32)
    col = jnp.tile(jnp.arange(GRID_W), rows).astype(F32)
    nf = rot_dim // 4
    freqs = ROPE_THETA ** (-jnp.arange(nf, dtype=F32) / nf)
    ang = jnp.concatenate([row[:, None] * freqs, col[:, None] * freqs], axis=-1)
    cos, sin = jnp.cos(ang), jnp.sin(ang)
    cos_l = jnp.repeat(cos, 2, axis=-1)
    sin_l = jnp.stack([-sin, sin], axis=-1).reshape(seq, rot_dim)
    return cos_l, sin_l


def _embed_lanes(table, start, fill):
    seq, w = table.shape
    return jnp.concatenate(
        [jnp.full((seq, start), fill, F32), table, jnp.full((seq, LANES - start - w), fill, F32)], axis=1)


def _mla_weights(w_in, q_norm, w_uq, kv_norm, w_ukv):
    c0 = MLA_Q_LORA + MLA_KV_LORA
    win = jnp.concatenate(
        [w_in[:, :c0], w_in[:, c0 + MLA_ROPE:], w_in[:, c0:c0 + MLA_ROPE],
         jnp.zeros((D_MODEL, MLA_IN_PAD - w_in.shape[1]), F32)], axis=1).astype(BF16)
    hq = MLA_NOPE + MLA_ROPE
    wuq = jnp.pad(w_uq.reshape(MLA_Q_LORA, N_HEADS, hq),
                  ((0, 0), (0, 0), (0, MLA_HEAD_PAD - hq))).reshape(MLA_Q_LORA, -1).astype(BF16)
    wkv = w_ukv.reshape(MLA_KV_LORA, N_HEADS, MLA_NOPE + MLA_V)
    wk = jnp.pad(wkv[:, :, :MLA_NOPE],
                 ((0, 0), (0, 0), (0, MLA_HEAD_PAD - MLA_NOPE))).reshape(MLA_KV_LORA, -1).astype(BF16)
    wvt = wkv[:, :, MLA_NOPE:].reshape(MLA_KV_LORA, -1).T.astype(BF16)
    src = lax.broadcasted_iota(jnp.int32, (LANES, N_HEADS * MLA_HEAD_PAD), 0)
    dst = lax.broadcasted_iota(jnp.int32, (LANES, N_HEADS * MLA_HEAD_PAD), 1)
    rp = ((dst % MLA_HEAD_PAD == src + MLA_NOPE) & (src < MLA_ROPE)).astype(BF16)
    return [win, q_norm.reshape(1, -1), wuq, kv_norm.reshape(1, -1), wk, rp, wvt]


def kernel(x_prompt, x_sample, cache_mla_ckv, cache_mla_kpe, cache_gqa_k, cache_gqa_v,
           cache_swa_k, cache_swa_v, c, c_ctx, norm_g, w_ada, b_ada, w_out,
           mla_w_in, mla_q_norm, mla_w_uq, mla_kv_norm, mla_w_ukv,
           gqa_w_in, gqa_q_norm, gqa_k_norm, swa_w_in, swa_sink, final_norm_g):
    bc, sc, _ = x_prompt.shape
    bl, sl, _ = x_sample.shape
    n_past = cache_mla_ckv.shape[2]
    kvw = KV_HEADS * HEAD_DIM

    cc = jnp.concatenate([c, c_ctx[None, :], jnp.zeros((8 - bl - 1, D_MODEL), F32)], axis=0)
    mods = _ada_mods(cc, w_ada, b_ada).reshape(DEPTH, 8, 1, 3 * D_MODEL)

    cos_h, sin_h = _rope_lane_tables(sl, HEAD_DIM)
    gqa_tables = [jnp.tile(cos_h, (1, 2)), jnp.tile(sin_h, (1, 2))]
    cos_r, sin_r = _rope_lane_tables(sl, MLA_ROPE)
    mla_tables = [_embed_lanes(cos_r, MLA_NOPE, 1.0), _embed_lanes(sin_r, MLA_NOPE, 0.0),
                  _embed_lanes(cos_r, 0, 1.0), _embed_lanes(sin_r, 0, 0.0)]

    xc = x_prompt.reshape(bc * sc, D_MODEL)
    xl = x_sample.reshape(bl * sl, D_MODEL)
    new_caches = {0: ([], []), 1: ([], []), 2: ([], [])}
    for i in range(DEPTH):
        kind, j = i % 3, i // 3
        mods_l = mods[i]
        wo = w_out[i].astype(BF16)
        final_g = final_norm_g if i == DEPTH - 1 else None
        sink = None
        if kind == 0:
            weights = _mla_weights(mla_w_in[j], mla_q_norm[j], mla_w_uq[j], mla_kv_norm[j], mla_w_ukv[j])
            tables = mla_tables
            kpe_pad = jnp.pad(cache_mla_kpe[:, j], ((0, 0), (0, 0), (0, LANES - MLA_ROPE)))
            cache = _mla_cache(cache_mla_ckv[:, j], kpe_pad, weights[4], weights[5], weights[6])
        else:
            tables = gqa_tables
            if kind == 1:
                weights = [gqa_w_in[j].astype(BF16),
                           jnp.tile(gqa_q_norm[j], 2).reshape(1, LANES),
                           jnp.tile(gqa_k_norm[j], 2).reshape(1, LANES)]
                cache = _gqa_cache(cache_gqa_k[:, j].reshape(bl, n_past, kvw),
                                   cache_gqa_v[:, j].reshape(bl, n_past, kvw))
            else:
                weights = [swa_w_in[j].astype(BF16)]
                sink = swa_sink[j]
                cache = _gqa_cache(cache_swa_k[:, j].reshape(bl, n_past, kvw),
                                   cache_swa_v[:, j].reshape(bl, n_past, kvw))

        q, k, vt, gate, c_a, c_b = _project(kind, xc, mods_l, norm_g[i], weights, tables,
                                            batch=bc, seq=sc, ctx=True)
        new_caches[kind][0].append(c_a)
        new_caches[kind][1].append(c_b)
        xc = _attend(kind, q, k, vt, None, gate, xc, mods_l, wo, sink, final_g,
                     batch=bc, seq=sc, ctx=True)

        q, k, vt, gate = _project(kind, xl, mods_l, norm_g[i], weights, tables,
                                  batch=bl, seq=sl, ctx=False)
        xl = _attend(kind, q, k, vt, cache, gate, xl, mods_l, wo, sink, final_g,
                     batch=bl, seq=sl, ctx=False)

    def stack(parts, tail):
        return jnp.stack([p.reshape((bc, sc) + tail) for p in parts], axis=1)

    return (xc.reshape(bc, sc, D_MODEL), xl.reshape(bl, sl, D_MODEL),
            stack(new_caches[0][0], (MLA_KV_LORA,)), stack(new_caches[0][1], (MLA_ROPE,)),
            stack(new_caches[1][0], (KV_HEADS, HEAD_DIM)), stack(new_caches[1][1], (KV_HEADS, HEAD_DIM)),
            stack(new_caches[2][0], (KV_HEADS, HEAD_DIM)), stack(new_caches[2][1], (KV_HEADS, HEAD_DIM)))
```

```python
import functools

import jax
import jax.numpy as jnp
from jax import lax
from jax.experimental import pallas as pl
from jax.experimental.pallas import tpu as pltpu

F32 = jnp.float32
BF16 = jnp.bfloat16

D_MODEL = 1024
DEPTH = 4
GRID_W = 64
N_HEADS = 16
HEAD_DIM = 64
KV_HEADS = 4
GROUPS = N_HEADS // KV_HEADS
MLA_Q_LORA = 384
MLA_KV_LORA = 256
MLA_NOPE = 64
MLA_ROPE = 32
MLA_V = 64
WINDOW = 128
ROPE_THETA = 10000.0
EPS = 1e-6
NEG_INF = -1e30

LANES = 128
TOKEN_TILE = 256
MLA_HEAD_PAD = 128
MLA_IN_PAD = 1792
VMEM_LIMIT = 56 * 1024 * 1024
NT_DIMS = (((1,), (1,)), ((), ()))


def _silu(v):
    return v * jax.nn.sigmoid(v)


def _rms(v, g):
    return v * lax.rsqrt(jnp.mean(v * v, axis=-1, keepdims=True) + EPS) * g


def _mod_norm(x, g, mod):
    shift = mod[:, :D_MODEL]
    scale = mod[:, D_MODEL:2 * D_MODEL]
    return _rms(x, g) * (1.0 + scale) + shift


def _rope_tile(v, cos, sin_signed):
    lane = lax.broadcasted_iota(jnp.int32, v.shape, 1)
    nxt = pltpu.roll(v, LANES - 1, axis=1)
    prv = pltpu.roll(v, 1, axis=1)
    swapped = jnp.where((lane & 1) == 0, nxt, prv)
    return v * cos + swapped * sin_signed


def _ada_kernel(c_ref, w_ref, b_ref, o_ref):
    s = _silu(c_ref[...]).astype(BF16)
    o_ref[...] = jnp.dot(s, w_ref[...].astype(BF16), preferred_element_type=F32) + b_ref[...]


def _ada_mods(cc, w_ada, b_ada):
    tn = D_MODEL
    return pl.pallas_call(
        _ada_kernel,
        grid=(DEPTH, 3 * D_MODEL // tn),
        in_specs=[
            pl.BlockSpec((8, D_MODEL), lambda i, n: (0, 0)),
            pl.BlockSpec((None, D_MODEL, tn), lambda i, n: (i, 0, n)),
            pl.BlockSpec((None, 1, tn), lambda i, n: (i, 0, n)),
        ],
        out_specs=pl.BlockSpec((None, 8, tn), lambda i, n: (i, 0, n)),
        out_shape=jax.ShapeDtypeStruct((DEPTH, 8, 3 * D_MODEL), F32),
        compiler_params=pltpu.CompilerParams(vmem_limit_bytes=VMEM_LIMIT),
        name="ada_mods",
    )(cc, w_ada, b_ada.reshape(DEPTH, 1, 3 * D_MODEL))


def _mla_expand(ckv_bf, kpe_bf, wk_ref, rp_ref, wvt_ref):
    k = (jnp.dot(ckv_bf, wk_ref[...], preferred_element_type=F32)
         + jnp.dot(kpe_bf, rp_ref[...], preferred_element_type=F32))
    vt = lax.dot_general(wvt_ref[...], ckv_bf, NT_DIMS, preferred_element_type=F32)
    return k.astype(BF16), vt.astype(BF16)


def _proj_mla_kernel(*refs, rope, ctx):
    refs = list(refs)
    x_ref, mod_ref, g_ref, win_ref, qn_ref, wuq_ref, kvn_ref, wk_ref, rp_ref, wvt_ref = refs[:10]
    pos = 10
    if rope:
        cq_ref, sq_ref, ck_ref, sk_ref = refs[pos:pos + 4]
        pos += 4
    q_out, k_out, vt_out, gate_out = refs[pos:pos + 4]
    pos += 4
    if ctx:
        ckv_out, kpe_out = refs[pos:pos + 2]

    h = _mod_norm(x_ref[...], g_ref[...], mod_ref[...]).astype(BF16)
    z = jnp.dot(h, win_ref[...], preferred_element_type=F32)
    cq = z[:, :MLA_Q_LORA]
    ckv = z[:, MLA_Q_LORA:MLA_Q_LORA + MLA_KV_LORA]
    gate = z[:, 640:640 + D_MODEL]
    kpe = z[:, 640 + D_MODEL:]
    gate_out[...] = gate.astype(BF16)

    qf = jnp.dot(_rms(cq, qn_ref[...]).astype(BF16), wuq_ref[...], preferred_element_type=F32)
    ckvn = _rms(ckv, kvn_ref[...])
    if ctx:
        ckv_out[...] = ckvn
        kpe_out[...] = kpe[:, :MLA_ROPE]
    if rope:
        kpe = _rope_tile(kpe, ck_ref[...], sk_ref[...])
    k, vt = _mla_expand(ckvn.astype(BF16), kpe.astype(BF16), wk_ref, rp_ref, wvt_ref)
    k_out[...] = k
    vt_out[...] = vt
    for hd in range(N_HEADS):
        blk = qf[:, MLA_HEAD_PAD * hd:MLA_HEAD_PAD * (hd + 1)]
        if rope:
            blk = _rope_tile(blk, cq_ref[...], sq_ref[...])
        q_out[:, MLA_HEAD_PAD * hd:MLA_HEAD_PAD * (hd + 1)] = blk.astype(BF16)


def _proj_gqa_kernel(*refs, qk_norm, rope, ctx):
    refs = list(refs)
    x_ref, mod_ref, g_ref, win_ref = refs[:4]
    pos = 4
    if qk_norm:
        qn_ref, kn_ref = refs[pos:pos + 2]
        pos += 2
    if rope:
        cos_ref, sin_ref = refs[pos:pos + 2]
        pos += 2
    q_out, k_out, vt_out, gate_out = refs[pos:pos + 4]
    pos += 4
    if ctx:
        kc_out, vc_out = refs[pos:pos + 2]

    nq = N_HEADS * HEAD_DIM
    nk = KV_HEADS * HEAD_DIM
    h = _mod_norm(x_ref[...], g_ref[...], mod_ref[...]).astype(BF16)
    z = jnp.dot(h, win_ref[...], preferred_element_type=F32)
    tm = z.shape[0]
    lo = lax.broadcasted_iota(jnp.int32, (tm, LANES), 1) < HEAD_DIM
    n_q_tiles = nq // LANES
    for c in range((nq + nk) // LANES):
        blk = z[:, LANES * c:LANES * (c + 1)]
        is_q = c < n_q_tiles
        if qk_norm:
            sq = blk * blk
            s_lo = jnp.sum(jnp.where(lo, sq, 0.0), axis=-1, keepdims=True)
            s_hi = jnp.sum(jnp.where(lo, 0.0, sq), axis=-1, keepdims=True)
            inv = jnp.where(lo, lax.rsqrt(s_lo * (1.0 / HEAD_DIM) + EPS),
                            lax.rsqrt(s_hi * (1.0 / HEAD_DIM) + EPS))
            blk = blk * inv * (qn_ref[...] if is_q else kn_ref[...])
        if ctx and not is_q:
            kc_out[:, LANES * (c - n_q_tiles):LANES * (c - n_q_tiles + 1)] = blk
        if rope:
            blk = _rope_tile(blk, cos_ref[...], sin_ref[...])
        if is_q:
            q_out[:, LANES * c:LANES * (c + 1)] = (blk * (HEAD_DIM ** -0.5)).astype(BF16)
        else:
            k_out[:, LANES * (c - n_q_tiles):LANES * (c - n_q_tiles + 1)] = blk.astype(BF16)
    v = z[:, nq + nk:nq + 2 * nk]
    if ctx:
        vc_out[...] = v
    vt_out[...] = v.T.astype(BF16)
    gate_out[...] = z[:, nq + 2 * nk:].astype(BF16)


def _const_spec(shape):
    return pl.BlockSpec(shape, lambda i: (0,) * len(shape))


def _project(kind, x, mods_l, norm_g, weights, tables, *, batch, seq, ctx):
    n_tok = batch * seq
    tm = TOKEN_TILE
    tiles_per_seq = seq // tm
    rope = not ctx
    mod_row = (lambda i: (4, 0, 0)) if ctx else (lambda i: (i // tiles_per_seq, 0, 0))
    in_specs = [
        pl.BlockSpec((tm, D_MODEL), lambda i: (i, 0)),
        pl.BlockSpec((None, 1, 3 * D_MODEL), mod_row),
        _const_spec((1, D_MODEL)),
    ]
    args = [x, mods_l, norm_g.reshape(1, D_MODEL)]
    for w in weights:
        in_specs.append(_const_spec(w.shape))
        args.append(w)
    if rope:
        for t in tables:
            in_specs.append(pl.BlockSpec((tm, LANES), lambda i: (i % tiles_per_seq, 0)))
            args.append(t)
    if kind == 0:
        qw, kw, vw = N_HEADS * MLA_HEAD_PAD, N_HEADS * MLA_HEAD_PAD, N_HEADS * MLA_V
        body = functools.partial(_proj_mla_kernel, rope=rope, ctx=ctx)
        cache_shapes = [(n_tok, MLA_KV_LORA), (n_tok, MLA_ROPE)]
    else:
        qw, kw, vw = N_HEADS * HEAD_DIM, KV_HEADS * HEAD_DIM, KV_HEADS * HEAD_DIM
        body = functools.partial(_proj_gqa_kernel, qk_norm=(kind == 1), rope=rope, ctx=ctx)
        cache_shapes = [(n_tok, kw), (n_tok, vw)]
    out_shape = [
        jax.ShapeDtypeStruct((n_tok, qw), BF16),
        jax.ShapeDtypeStruct((n_tok, kw), BF16),
        jax.ShapeDtypeStruct((batch, vw, seq), BF16),
        jax.ShapeDtypeStruct((n_tok, D_MODEL), BF16),
    ]
    out_specs = [
        pl.BlockSpec((tm, qw), lambda i: (i, 0)),
        pl.BlockSpec((tm, kw), lambda i: (i, 0)),
        pl.BlockSpec((None, vw, tm), lambda i: (i // tiles_per_seq, 0, i % tiles_per_seq)),
        pl.BlockSpec((tm, D_MODEL), lambda i: (i, 0)),
    ]
    if ctx:
        for s in cache_shapes:
            out_shape.append(jax.ShapeDtypeStruct(s, F32))
            out_specs.append(pl.BlockSpec((tm, s[1]), lambda i: (i, 0)))
    return pl.pallas_call(
        body,
        grid=(n_tok // tm,),
        in_specs=in_specs,
        out_specs=out_specs,
        out_shape=out_shape,
        compiler_params=pltpu.CompilerParams(
            dimension_semantics=("arbitrary",), vmem_limit_bytes=VMEM_LIMIT),
        name=f"proj_k{kind}_{'ctx' if ctx else 'lat'}",
    )(*args)


def _mla_cache_kernel(ckv_ref, kpe_ref, wk_ref, rp_ref, wvt_ref, k_out, vt_out):
    k, vt = _mla_expand(ckv_ref[...].astype(BF16), kpe_ref[...].astype(BF16), wk_ref, rp_ref, wvt_ref)
    k_out[...] = k
    vt_out[...] = vt


def _mla_cache(ckv, kpe_pad, wk, rp, wvt):
    b, n, _ = ckv.shape
    return pl.pallas_call(
        _mla_cache_kernel,
        grid=(b,),
        in_specs=[
            pl.BlockSpec((None, n, MLA_KV_LORA), lambda i: (i, 0, 0)),
            pl.BlockSpec((None, n, LANES), lambda i: (i, 0, 0)),
            _const_spec(wk.shape), _const_spec(rp.shape), _const_spec(wvt.shape),
        ],
        out_specs=[
            pl.BlockSpec((None, n, N_HEADS * MLA_HEAD_PAD), lambda i: (i, 0, 0)),
            pl.BlockSpec((None, N_HEADS * MLA_V, n), lambda i: (i, 0, 0)),
        ],
        out_shape=[
            jax.ShapeDtypeStruct((b, n, N_HEADS * MLA_HEAD_PAD), BF16),
            jax.ShapeDtypeStruct((b, N_HEADS * MLA_V, n), BF16),
        ],
        compiler_params=pltpu.CompilerParams(
            dimension_semantics=("arbitrary",), vmem_limit_bytes=VMEM_LIMIT),
        name="mla_cache",
    )(ckv, kpe_pad, wk, rp, wvt)


def _gqa_cache_kernel(k_ref, v_ref, k_out, vt_out):
    k_out[...] = k_ref[...].astype(BF16)
    vt_out[...] = v_ref[...].T.astype(BF16)


def _gqa_cache(k, v):
    b, n, w = k.shape
    return pl.pallas_call(
        _gqa_cache_kernel,
        grid=(b,),
        in_specs=[pl.BlockSpec((None, n, w), lambda i: (i, 0, 0))] * 2,
        out_specs=[
            pl.BlockSpec((None, n, w), lambda i: (i, 0, 0)),
            pl.BlockSpec((None, w, n), lambda i: (i, 0, 0)),
        ],
        out_shape=[
            jax.ShapeDtypeStruct((b, n, w), BF16),
            jax.ShapeDtypeStruct((b, w, n), BF16),
        ],
        compiler_params=pltpu.CompilerParams(dimension_semantics=("arbitrary",)),
        name="gqa_cache",
    )(k, v)


def _attn_kernel(*refs, head_w, group, score_scale, has_cache, has_sink, band, final):
    refs = list(refs)
    q_ref, kn_ref, vtn_ref = refs[:3]
    pos = 3
    segs = []
    if has_cache:
        segs.append((refs[pos], refs[pos + 1]))
        pos += 2
    segs.append((kn_ref, vtn_ref))
    gate_ref, x_ref, mod_ref, wo_ref = refs[pos:pos + 4]
    pos += 4
    if has_sink:
        sink_ref = refs[pos]
        pos += 1
    if final:
        fg_ref = refs[pos]
        pos += 1
    o_ref, ot_scr, s_scr, p_scr = refs[pos:pos + 4]

    tq = q_ref.shape[0]
    seg_rows = []
    off = 0
    for k_ref, _ in segs:
        seg_rows.append((off, k_ref.shape[0]))
        off += k_ref.shape[0]
    if band:
        n_own = kn_ref.shape[0]
        kpos = lax.broadcasted_iota(jnp.int32, (n_own, tq), 0)
        qpos = pl.program_id(1) * tq + lax.broadcasted_iota(jnp.int32, (n_own, tq), 1)
        in_band = jnp.abs(kpos - qpos) <= WINDOW

    def scores_and_max(hd):
        kh = hd // group
        qh = q_ref[:, head_w * hd:head_w * (hd + 1)]
        maxes = []
        for si, (k_ref, _) in enumerate(segs):
            r0, n = seg_rows[si]
            kk = k_ref[:, head_w * kh:head_w * (kh + 1)]
            s = lax.dot_general(kk, qh, NT_DIMS, preferred_element_type=F32)
            if score_scale != 1.0:
                s = s * score_scale
            if band and si == len(segs) - 1:
                s = jnp.where(in_band, s, NEG_INF)
            s_scr[hd % 2, r0:r0 + n, :] = s
            maxes.append(jnp.max(s, axis=0, keepdims=True))
        m = functools.reduce(jnp.maximum, maxes)
        if has_sink:
            m = jnp.maximum(m, sink_ref[hd])
        return m

    def exponentials(hd, m):
        p = jnp.exp(s_scr[hd % 2] - m)
        den = jnp.sum(p, axis=0, keepdims=True)
        if has_sink:
            den = den + jnp.exp(sink_ref[hd] - m)
        p_scr[hd % 2] = p.astype(BF16)
        return den

    def weighted_values(hd, den):
        kh = hd // group
        acc = jnp.zeros((HEAD_DIM, tq), F32)
        for si, (_, vt_ref) in enumerate(segs):
            r0, n = seg_rows[si]
            vt = vt_ref[HEAD_DIM * kh:HEAD_DIM * (kh + 1), :]
            acc = acc + jnp.dot(vt, p_scr[hd % 2, r0:r0 + n, :], preferred_element_type=F32)
        ot_scr[HEAD_DIM * hd:HEAD_DIM * (hd + 1), :] = acc / den

    m_next = scores_and_max(0)
    for hd in range(N_HEADS):
        m_cur = m_next
        if hd + 1 < N_HEADS:
            m_next = scores_and_max(hd + 1)
        weighted_values(hd, exponentials(hd, m_cur))

    o = ot_scr[...].T
    gated = (o * _silu(gate_ref[...].astype(F32))).astype(BF16)
    y = jnp.dot(gated, wo_ref[...], preferred_element_type=F32)
    xn = x_ref[...] + mod_ref[...][:, 2 * D_MODEL:] * y
    if final:
        xn = _rms(xn, fg_ref[...])
    o_ref[...] = xn


def _attend(kind, q, k_new, vt_new, cache, gate, x, mods_l, w_out, sink, final_g, *, batch, seq, ctx):
    tq = TOKEN_TILE
    n_tok = batch * seq
    tiles = seq // tq
    head_w = MLA_HEAD_PAD if kind == 0 else HEAD_DIM
    kw = k_new.shape[1]
    vw = vt_new.shape[1]
    tok_spec = lambda w: pl.BlockSpec((tq, w), lambda b, t: (b * tiles + t, 0))
    in_specs = [
        tok_spec(q.shape[1]),
        pl.BlockSpec((seq, kw), lambda b, t: (b, 0)),
        pl.BlockSpec((None, vw, seq), lambda b, t: (b, 0, 0)),
    ]
    args = [q, k_new, vt_new]
    n_keys = seq
    if cache is not None:
        kc, vtc = cache
        n_c = kc.shape[1]
        n_keys += n_c
        in_specs += [
            pl.BlockSpec((None, n_c, kw), lambda b, t: (b, 0, 0)),
            pl.BlockSpec((None, vw, n_c), lambda b, t: (b, 0, 0)),
        ]
        args += [kc, vtc]
    mod_row = (lambda b, t: (4, 0, 0)) if ctx else (lambda b, t: (b, 0, 0))
    in_specs += [
        tok_spec(D_MODEL),
        tok_spec(D_MODEL),
        pl.BlockSpec((None, 1, 3 * D_MODEL), mod_row),
        pl.BlockSpec((D_MODEL, D_MODEL), lambda b, t: (0, 0)),
    ]
    args += [gate, x, mods_l, w_out]
    if sink is not None:
        in_specs.append(pl.BlockSpec(memory_space=pltpu.SMEM))
        args.append(sink)
    if final_g is not None:
        in_specs.append(pl.BlockSpec((1, D_MODEL), lambda b, t: (0, 0)))
        args.append(final_g.reshape(1, D_MODEL))
    body = functools.partial(
        _attn_kernel,
        head_w=head_w,
        group=1 if kind == 0 else GROUPS,
        score_scale=(MLA_NOPE + MLA_ROPE) ** -0.5 if kind == 0 else 1.0,
        has_cache=cache is not None,
        has_sink=sink is not None,
        band=(kind == 2 and not ctx),
        final=final_g is not None,
    )
    return pl.pallas_call(
        body,
        grid=(batch, tiles),
        in_specs=in_specs,
        out_specs=tok_spec(D_MODEL),
        out_shape=jax.ShapeDtypeStruct((n_tok, D_MODEL), F32),
        scratch_shapes=[pltpu.VMEM((N_HEADS * HEAD_DIM, tq), F32),
                        pltpu.VMEM((2, n_keys, tq), F32),
                        pltpu.VMEM((2, n_keys, tq), BF16)],
        compiler_params=pltpu.CompilerParams(
            dimension_semantics=("arbitrary", "arbitrary"), vmem_limit_bytes=VMEM_LIMIT),
        name=f"attn_k{kind}_{'ctx' if ctx else 'lat'}",
    )(*args)


def _rope_lane_tables(seq, rot_dim):
    rows = seq // GRID_W
    row = jnp.repeat(jnp.arange(rows), GRID_W).astype(F32)
    col = jnp.tile(jnp.arange(GRID_W), rows).astype(F32)
    nf = rot_dim // 4
    freqs = ROPE_THETA ** (-jnp.arange(nf, dtype=F32) / nf)
    ang = jnp.concatenate([row[:, None] * freqs, col[:, None] * freqs], axis=-1)
    cos, sin = jnp.cos(ang), jnp.sin(ang)
    cos_l = jnp.repeat(cos, 2, axis=-1)
    sin_l = jnp.stack([-sin, sin], axis=-1).reshape(seq, rot_dim)
    return cos_l, sin_l


def _embed_lanes(table, start, fill):
    seq, w = table.shape
    return jnp.concatenate(
        [jnp.full((seq, start), fill, F32), table, jnp.full((seq, LANES - start - w), fill, F32)], axis=1)


def _mla_weights(w_in, q_norm, w_uq, kv_norm, w_ukv):
    c0 = MLA_Q_LORA + MLA_KV_LORA
    win = jnp.concatenate(
        [w_in[:, :c0], w_in[:, c0 + MLA_ROPE:], w_in[:, c0:c0 + MLA_ROPE],
         jnp.zeros((D_MODEL, MLA_IN_PAD - w_in.shape[1]), F32)], axis=1).astype(BF16)
    hq = MLA_NOPE + MLA_ROPE
    wuq = jnp.pad(w_uq.reshape(MLA_Q_LORA, N_HEADS, hq),
                  ((0, 0), (0, 0), (0, MLA_HEAD_PAD - hq))).reshape(MLA_Q_LORA, -1).astype(BF16)
    wkv = w_ukv.reshape(MLA_KV_LORA, N_HEADS, MLA_NOPE + MLA_V)
    wk = jnp.pad(wkv[:, :, :MLA_NOPE],
                 ((0, 0), (0, 0), (0, MLA_HEAD_PAD - MLA_NOPE))).reshape(MLA_KV_LORA, -1).astype(BF16)
    wvt = wkv[:, :, MLA_NOPE:].reshape(MLA_KV_LORA, -1).T.astype(BF16)
    src = lax.broadcasted_iota(jnp.int32, (LANES, N_HEADS * MLA_HEAD_PAD), 0)
    dst = lax.broadcasted_iota(jnp.int32, (LANES, N_HEADS * MLA_HEAD_PAD), 1)
    rp = ((dst % MLA_HEAD_PAD == src + MLA_NOPE) & (src < MLA_ROPE)).astype(BF16)
    return [win, q_norm.reshape(1, -1), wuq, kv_norm.reshape(1, -1), wk, rp, wvt]


def kernel(x_prompt, x_sample, cache_mla_ckv, cache_mla_kpe, cache_gqa_k, cache_gqa_v,
           cache_swa_k, cache_swa_v, c, c_ctx, norm_g, w_ada, b_ada, w_out,
           mla_w_in, mla_q_norm, mla_w_uq, mla_kv_norm, mla_w_ukv,
           gqa_w_in, gqa_q_norm, gqa_k_norm, swa_w_in, swa_sink, final_norm_g):
    bc, sc, _ = x_prompt.shape
    bl, sl, _ = x_sample.shape
    n_past = cache_mla_ckv.shape[2]
    kvw = KV_HEADS * HEAD_DIM

    cc = jnp.concatenate([c, c_ctx[None, :], jnp.zeros((8 - bl - 1, D_MODEL), F32)], axis=0)
    mods = _ada_mods(cc, w_ada, b_ada).reshape(DEPTH, 8, 1, 3 * D_MODEL)

    cos_h, sin_h = _rope_lane_tables(sl, HEAD_DIM)
    gqa_tables = [jnp.tile(cos_h, (1, 2)), jnp.tile(sin_h, (1, 2))]
    cos_r, sin_r = _rope_lane_tables(sl, MLA_ROPE)
    mla_tables = [_embed_lanes(cos_r, MLA_NOPE, 1.0), _embed_lanes(sin_r, MLA_NOPE, 0.0),
                  _embed_lanes(cos_r, 0, 1.0), _embed_lanes(sin_r, 0, 0.0)]

    xc = x_prompt.reshape(bc * sc, D_MODEL)
    xl = x_sample.reshape(bl * sl, D_MODEL)
    new_caches = {0: ([], []), 1: ([], []), 2: ([], [])}
    for i in range(DEPTH):
        kind, j = i % 3, i // 3
        mods_l = mods[i]
        wo = w_out[i].astype(BF16)
        final_g = final_norm_g if i == DEPTH - 1 else None
        sink = None
        if kind == 0:
            weights = _mla_weights(mla_w_in[j], mla_q_norm[j], mla_w_uq[j], mla_kv_norm[j], mla_w_ukv[j])
            tables = mla_tables
            kpe_pad = jnp.pad(cache_mla_kpe[:, j], ((0, 0), (0, 0), (0, LANES - MLA_ROPE)))
            cache = _mla_cache(cache_mla_ckv[:, j], kpe_pad, weights[4], weights[5], weights[6])
        else:
            tables = gqa_tables
            if kind == 1:
                weights = [gqa_w_in[j].astype(BF16),
                           jnp.tile(gqa_q_norm[j], 2).reshape(1, LANES),
                           jnp.tile(gqa_k_norm[j], 2).reshape(1, LANES)]
                cache = _gqa_cache(cache_gqa_k[:, j].reshape(bl, n_past, kvw),
                                   cache_gqa_v[:, j].reshape(bl, n_past, kvw))
            else:
                weights = [swa_w_in[j].astype(BF16)]
                sink = swa_sink[j]
                cache = _gqa_cache(cache_swa_k[:, j].reshape(bl, n_past, kvw),
                                   cache_swa_v[:, j].reshape(bl, n_past, kvw))

        q, k, vt, gate, c_a, c_b = _project(kind, xc, mods_l, norm_g[i], weights, tables,
                                            batch=bc, seq=sc, ctx=True)
        new_caches[kind][0].append(c_a)
        new_caches[kind][1].append(c_b)
        xc = _attend(kind, q, k, vt, None, gate, xc, mods_l, wo, sink, final_g,
                     batch=bc, seq=sc, ctx=True)

        q, k, vt, gate = _project(kind, xl, mods_l, norm_g[i], weights, tables,
                                  batch=bl, seq=sl, ctx=False)
        xl = _attend(kind, q, k, vt, cache, gate, xl, mods_l, wo, sink, final_g,
                     batch=bl, seq=sl, ctx=False)

    def stack(parts, tail):
        return jnp.stack([p.reshape((bc, sc) + tail) for p in parts], axis=1)

    return (xc.reshape(bc, sc, D_MODEL), xl.reshape(bl, sl, D_MODEL),
            stack(new_caches[0][0], (MLA_KV_LORA,)), stack(new_caches[0][1], (MLA_ROPE,)),
            stack(new_caches[1][0], (KV_HEADS, HEAD_DIM)), stack(new_caches[1][1], (KV_HEADS, HEAD_DIM)),
            stack(new_caches[2][0], (KV_HEADS, HEAD_DIM)), stack(new_caches[2][1], (KV_HEADS, HEAD_DIM)))
```

```python
import functools

import jax
import jax.numpy as jnp
from jax import lax
from jax.experimental import pallas as pl
from jax.experimental.pallas import tpu as pltpu

F32 = jnp.float32
BF16 = jnp.bfloat16

D_MODEL = 1024
DEPTH = 4
GRID_W = 64
N_HEADS = 16
HEAD_DIM = 64
KV_HEADS = 4
GROUPS = N_HEADS // KV_HEADS
MLA_Q_LORA = 384
MLA_KV_LORA = 256
MLA_NOPE = 64
MLA_ROPE = 32
MLA_V = 64
WINDOW = 128
ROPE_THETA = 10000.0
EPS = 1e-6
NEG_INF = -1e30

LOG2E = 1.4426950408889634
GQA_Q_SCALE = HEAD_DIM ** -0.5 * LOG2E
MLA_Q_SCALE = (MLA_NOPE + MLA_ROPE) ** -0.5 * LOG2E

LANES = 128
TOKEN_TILE = 256
KEY_CHUNK = 256
STAGE_SLOTS = 3
MLA_HEAD_PAD = 128
MLA_IN_PAD = 1792
VMEM_LIMIT = 56 * 1024 * 1024
NT_DIMS = (((1,), (1,)), ((), ()))


def _silu(v):
    return v * jax.nn.sigmoid(v)


def _rms(v, g):
    return v * lax.rsqrt(jnp.mean(v * v, axis=-1, keepdims=True) + EPS) * g


def _mod_norm(x, g, mod):
    shift = mod[:, :D_MODEL]
    scale = mod[:, D_MODEL:2 * D_MODEL]
    return _rms(x, g) * (1.0 + scale) + shift


def _rope_tile(v, cos, sin_signed):
    lane = lax.broadcasted_iota(jnp.int32, v.shape, 1)
    nxt = pltpu.roll(v, LANES - 1, axis=1)
    prv = pltpu.roll(v, 1, axis=1)
    swapped = jnp.where((lane & 1) == 0, nxt, prv)
    return v * cos + swapped * sin_signed


def _ada_kernel(c_ref, w_ref, b_ref, o_ref):
    s = _silu(c_ref[...]).astype(BF16)
    o_ref[...] = jnp.dot(s, w_ref[...].astype(BF16), preferred_element_type=F32) + b_ref[...]


def _ada_mods(cc, w_ada, b_ada):
    tn = D_MODEL
    return pl.pallas_call(
        _ada_kernel,
        grid=(DEPTH, 3 * D_MODEL // tn),
        in_specs=[
            pl.BlockSpec((8, D_MODEL), lambda i, n: (0, 0)),
            pl.BlockSpec((None, D_MODEL, tn), lambda i, n: (i, 0, n)),
            pl.BlockSpec((None, 1, tn), lambda i, n: (i, 0, n)),
        ],
        out_specs=pl.BlockSpec((None, 8, tn), lambda i, n: (i, 0, n)),
        out_shape=jax.ShapeDtypeStruct((DEPTH, 8, 3 * D_MODEL), F32),
        compiler_params=pltpu.CompilerParams(vmem_limit_bytes=VMEM_LIMIT),
        name="ada_mods",
    )(cc, w_ada, b_ada.reshape(DEPTH, 1, 3 * D_MODEL))


def _mla_expand(ckv_bf, kpe_bf, wk_ref, rp_ref, wvt_ref):
    k = (jnp.dot(ckv_bf, wk_ref[...], preferred_element_type=F32)
         + jnp.dot(kpe_bf, rp_ref[...], preferred_element_type=F32))
    vt = lax.dot_general(wvt_ref[...], ckv_bf, NT_DIMS, preferred_element_type=F32)
    return k.astype(BF16), vt.astype(BF16)


def _proj_mla_kernel(*refs, rope, ctx):
    refs = list(refs)
    x_ref, mod_ref, g_ref, win_ref, qn_ref, wuq_ref, kvn_ref, wk_ref, rp_ref, wvt_ref = refs[:10]
    pos = 10
    if rope:
        cq_ref, sq_ref, ck_ref, sk_ref = refs[pos:pos + 4]
        pos += 4
    q_out, k_out, vt_out, gate_out = refs[pos:pos + 4]
    pos += 4
    if ctx:
        ckv_out, kpe_out = refs[pos:pos + 2]

    h = _mod_norm(x_ref[...], g_ref[...], mod_ref[...]).astype(BF16)
    z = jnp.dot(h, win_ref[...], preferred_element_type=F32)
    cq = z[:, :MLA_Q_LORA]
    ckv = z[:, MLA_Q_LORA:MLA_Q_LORA + MLA_KV_LORA]
    gate = z[:, 640:640 + D_MODEL]
    kpe = z[:, 640 + D_MODEL:]
    gate_out[...] = gate.astype(BF16)

    qf = jnp.dot(_rms(cq, qn_ref[...]).astype(BF16), wuq_ref[...], preferred_element_type=F32)
    ckvn = _rms(ckv, kvn_ref[...])
    if ctx:
        ckv_out[...] = ckvn
        kpe_out[...] = kpe[:, :MLA_ROPE]
    if rope:
        kpe = _rope_tile(kpe, ck_ref[...], sk_ref[...])
    k, vt = _mla_expand(ckvn.astype(BF16), kpe.astype(BF16), wk_ref, rp_ref, wvt_ref)
    k_out[...] = k
    vt_out[...] = vt
    for hd in range(N_HEADS):
        blk = qf[:, MLA_HEAD_PAD * hd:MLA_HEAD_PAD * (hd + 1)]
        if rope:
            blk = _rope_tile(blk, cq_ref[...], sq_ref[...])
        q_out[:, MLA_HEAD_PAD * hd:MLA_HEAD_PAD * (hd + 1)] = (blk * MLA_Q_SCALE).astype(BF16)


def _proj_gqa_kernel(*refs, qk_norm, rope, ctx):
    refs = list(refs)
    x_ref, mod_ref, g_ref, win_ref = refs[:4]
    pos = 4
    if qk_norm:
        qn_ref, kn_ref = refs[pos:pos + 2]
        pos += 2
    if rope:
        cos_ref, sin_ref = refs[pos:pos + 2]
        pos += 2
    q_out, k_out, vt_out, gate_out = refs[pos:pos + 4]
    pos += 4
    if ctx:
        kc_out, vc_out = refs[pos:pos + 2]

    nq = N_HEADS * HEAD_DIM
    nk = KV_HEADS * HEAD_DIM
    h = _mod_norm(x_ref[...], g_ref[...], mod_ref[...]).astype(BF16)
    z = jnp.dot(h, win_ref[...], preferred_element_type=F32)
    tm = z.shape[0]
    lo = lax.broadcasted_iota(jnp.int32, (tm, LANES), 1) < HEAD_DIM
    n_q_tiles = nq // LANES
    for c in range((nq + nk) // LANES):
        blk = z[:, LANES * c:LANES * (c + 1)]
        is_q = c < n_q_tiles
        if qk_norm:
            sq = blk * blk
            s_lo = jnp.sum(jnp.where(lo, sq, 0.0), axis=-1, keepdims=True)
            s_hi = jnp.sum(jnp.where(lo, 0.0, sq), axis=-1, keepdims=True)
            inv = jnp.where(lo, lax.rsqrt(s_lo * (1.0 / HEAD_DIM) + EPS),
                            lax.rsqrt(s_hi * (1.0 / HEAD_DIM) + EPS))
            blk = blk * inv * (qn_ref[...] if is_q else kn_ref[...])
        if ctx and not is_q:
            kc_out[:, LANES * (c - n_q_tiles):LANES * (c - n_q_tiles + 1)] = blk
        if rope:
            blk = _rope_tile(blk, cos_ref[...], sin_ref[...])
        if is_q:
            q_out[:, LANES * c:LANES * (c + 1)] = (blk * GQA_Q_SCALE).astype(BF16)
        else:
            k_out[:, LANES * (c - n_q_tiles):LANES * (c - n_q_tiles + 1)] = blk.astype(BF16)
    v = z[:, nq + nk:nq + 2 * nk]
    if ctx:
        vc_out[...] = v
    vt_out[...] = v.T.astype(BF16)
    gate_out[...] = z[:, nq + 2 * nk:].astype(BF16)


def _const_spec(shape):
    return pl.BlockSpec(shape, lambda i: (0,) * len(shape))


def _project(kind, x, mods_l, norm_g, weights, tables, *, batch, seq, ctx):
    n_tok = batch * seq
    tm = TOKEN_TILE
    tiles_per_seq = seq // tm
    rope = not ctx
    mod_row = (lambda i: (4, 0, 0)) if ctx else (lambda i: (i // tiles_per_seq, 0, 0))
    in_specs = [
        pl.BlockSpec((tm, D_MODEL), lambda i: (i, 0)),
        pl.BlockSpec((None, 1, 3 * D_MODEL), mod_row),
        _const_spec((1, D_MODEL)),
    ]
    args = [x, mods_l, norm_g.reshape(1, D_MODEL)]
    for w in weights:
        in_specs.append(_const_spec(w.shape))
        args.append(w)
    if rope:
        for t in tables:
            in_specs.append(pl.BlockSpec((tm, LANES), lambda i: (i % tiles_per_seq, 0)))
            args.append(t)
    if kind == 0:
        qw, kw, vw = N_HEADS * MLA_HEAD_PAD, N_HEADS * MLA_HEAD_PAD, N_HEADS * MLA_V
        body = functools.partial(_proj_mla_kernel, rope=rope, ctx=ctx)
        cache_shapes = [(n_tok, MLA_KV_LORA), (n_tok, MLA_ROPE)]
    else:
        qw, kw, vw = N_HEADS * HEAD_DIM, KV_HEADS * HEAD_DIM, KV_HEADS * HEAD_DIM
        body = functools.partial(_proj_gqa_kernel, qk_norm=(kind == 1), rope=rope, ctx=ctx)
        cache_shapes = [(n_tok, kw), (n_tok, vw)]
    out_shape = [
        jax.ShapeDtypeStruct((n_tok, qw), BF16),
        jax.ShapeDtypeStruct((n_tok, kw), BF16),
        jax.ShapeDtypeStruct((batch, vw, seq), BF16),
        jax.ShapeDtypeStruct((n_tok, D_MODEL), BF16),
    ]
    out_specs = [
        pl.BlockSpec((tm, qw), lambda i: (i, 0)),
        pl.BlockSpec((tm, kw), lambda i: (i, 0)),
        pl.BlockSpec((None, vw, tm), lambda i: (i // tiles_per_seq, 0, i % tiles_per_seq)),
        pl.BlockSpec((tm, D_MODEL), lambda i: (i, 0)),
    ]
    if ctx:
        for s in cache_shapes:
            out_shape.append(jax.ShapeDtypeStruct(s, F32))
            out_specs.append(pl.BlockSpec((tm, s[1]), lambda i: (i, 0)))
    return pl.pallas_call(
        body,
        grid=(n_tok // tm,),
        in_specs=in_specs,
        out_specs=out_specs,
        out_shape=out_shape,
        compiler_params=pltpu.CompilerParams(
            dimension_semantics=("arbitrary",), vmem_limit_bytes=VMEM_LIMIT),
        name=f"proj_k{kind}_{'ctx' if ctx else 'lat'}",
    )(*args)


def _mla_cache_kernel(ckv_ref, kpe_ref, wk_ref, rp_ref, wvt_ref, k_out, vt_out):
    k, vt = _mla_expand(ckv_ref[...].astype(BF16), kpe_ref[...].astype(BF16), wk_ref, rp_ref, wvt_ref)
    k_out[...] = k
    vt_out[...] = vt


def _mla_cache(ckv, kpe_pad, wk, rp, wvt):
    b, n, _ = ckv.shape
    return pl.pallas_call(
        _mla_cache_kernel,
        grid=(b,),
        in_specs=[
            pl.BlockSpec((None, n, MLA_KV_LORA), lambda i: (i, 0, 0)),
            pl.BlockSpec((None, n, LANES), lambda i: (i, 0, 0)),
            _const_spec(wk.shape), _const_spec(rp.shape), _const_spec(wvt.shape),
        ],
        out_specs=[
            pl.BlockSpec((None, n, N_HEADS * MLA_HEAD_PAD), lambda i: (i, 0, 0)),
            pl.BlockSpec((None, N_HEADS * MLA_V, n), lambda i: (i, 0, 0)),
        ],
        out_shape=[
            jax.ShapeDtypeStruct((b, n, N_HEADS * MLA_HEAD_PAD), BF16),
            jax.ShapeDtypeStruct((b, N_HEADS * MLA_V, n), BF16),
        ],
        compiler_params=pltpu.CompilerParams(
            dimension_semantics=("arbitrary",), vmem_limit_bytes=VMEM_LIMIT),
        name="mla_cache",
    )(ckv, kpe_pad, wk, rp, wvt)


def _gqa_cache_kernel(k_ref, v_ref, k_out, vt_out):
    k_out[...] = k_ref[...].astype(BF16)
    vt_out[...] = v_ref[...].T.astype(BF16)


def _gqa_cache(k, v):
    b, n, w = k.shape
    return pl.pallas_call(
        _gqa_cache_kernel,
        grid=(b,),
        in_specs=[pl.BlockSpec((None, n, w), lambda i: (i, 0, 0))] * 2,
        out_specs=[
            pl.BlockSpec((None, n, w), lambda i: (i, 0, 0)),
            pl.BlockSpec((None, w, n), lambda i: (i, 0, 0)),
        ],
        out_shape=[
            jax.ShapeDtypeStruct((b, n, w), BF16),
            jax.ShapeDtypeStruct((b, w, n), BF16),
        ],
        compiler_params=pltpu.CompilerParams(dimension_semantics=("arbitrary",)),
        name="gqa_cache",
    )(k, v)


def _attn_kernel(*refs, head_w, group, has_cache, has_sink, band, final):
    refs = list(refs)
    q_ref, kn_ref, vtn_ref = refs[:3]
    pos = 3
    segs = []
    if has_cache:
        segs.append((refs[pos], refs[pos + 1]))
        pos += 2
    segs.append((kn_ref, vtn_ref))
    gate_ref, x_ref, mod_ref, wo_ref = refs[pos:pos + 4]
    pos += 4
    if has_sink:
        sink_ref = refs[pos]
        pos += 1
    if final:
        fg_ref = refs[pos]
        pos += 1
    o_ref, ot_scr, s_scr, p_scr = refs[pos:pos + 4]
    if band:
        bias_scr = refs[pos + 4]

    tq = q_ref.shape[0]
    ch = KEY_CHUNK
    seg_rows = []
    chunks = []
    off = 0
    for si, (k_ref, _) in enumerate(segs):
        seg_rows.append((off, k_ref.shape[0]))
        chunks += [(si, r, off + r) for r in range(0, k_ref.shape[0], ch)]
        off += k_ref.shape[0]
    own = len(segs) - 1
    if band:
        n_own = kn_ref.shape[0]
        kpos = lax.broadcasted_iota(jnp.int32, (n_own, tq), 0)
        qpos = pl.program_id(1) * tq + lax.broadcasted_iota(jnp.int32, (n_own, tq), 1)
        bias_scr[...] = jnp.where(jnp.abs(kpos - qpos) <= WINDOW, 0.0, NEG_INF)

    def scores_and_max(hd):
        kh = hd // group
        qh = q_ref[:, head_w * hd:head_w * (hd + 1)]
        mx = None
        for si, r, g in chunks:
            kk = segs[si][0][r:r + ch, head_w * kh:head_w * (kh + 1)]
            s = lax.dot_general(kk, qh, NT_DIMS, preferred_element_type=F32)
            if band and si == own:
                s = s + bias_scr[r:r + ch, :]
            s_scr[hd % STAGE_SLOTS, g:g + ch, :] = s
            part = jnp.max(s.reshape(ch // 8, 8, tq), axis=0)
            mx = part if mx is None else jnp.maximum(mx, part)
        m = jnp.max(mx, axis=0, keepdims=True)
        if has_sink:
            m = jnp.maximum(m, sink_ref[hd] * LOG2E)
        return m

    def exponentials(hd, m):
        mb = jnp.broadcast_to(m, (8, tq))
        tot = None
        for _, _, g in chunks:
            p = jnp.exp2(s_scr[hd % STAGE_SLOTS, g:g + ch, :].reshape(ch // 8, 8, tq) - mb)
            part = jnp.sum(p, axis=0)
            tot = part if tot is None else tot + part
            p_scr[hd % STAGE_SLOTS, g:g + ch, :] = p.reshape(ch, tq).astype(BF16)
        den = jnp.sum(tot, axis=0, keepdims=True)
        if has_sink:
            den = den + jnp.exp2(sink_ref[hd] * LOG2E - m)
        return den

    def weighted_values(hd, den):
        kh = hd // group
        acc = None
        for si, (_, vt_ref) in enumerate(segs):
            r0, n = seg_rows[si]
            vt = vt_ref[HEAD_DIM * kh:HEAD_DIM * (kh + 1), :]
            part = jnp.dot(vt, p_scr[hd % STAGE_SLOTS, r0:r0 + n, :], preferred_element_type=F32)
            acc = part if acc is None else acc + part
        ot_scr[HEAD_DIM * hd:HEAD_DIM * (hd + 1), :] = acc / den

    m_next = scores_and_max(0)
    for hd in range(N_HEADS):
        m_cur = m_next
        if hd + 1 < N_HEADS:
            m_next = scores_and_max(hd + 1)
        weighted_values(hd, exponentials(hd, m_cur))

    o = ot_scr[...].T
    gated = (o * _silu(gate_ref[...].astype(F32))).astype(BF16)
    y = jnp.dot(gated, wo_ref[...], preferred_element_type=F32)
    xn = x_ref[...] + mod_ref[...][:, 2 * D_MODEL:] * y
    if final:
        xn = _rms(xn, fg_ref[...])
    o_ref[...] = xn


def _attend(kind, q, k_new, vt_new, cache, gate, x, mods_l, w_out, sink, final_g, *, batch, seq, ctx):
    tq = TOKEN_TILE
    n_tok = batch * seq
    tiles = seq // tq
    band = kind == 2 and not ctx
    head_w = MLA_HEAD_PAD if kind == 0 else HEAD_DIM
    kw = k_new.shape[1]
    vw = vt_new.shape[1]
    tok_spec = lambda w: pl.BlockSpec((tq, w), lambda b, t: (b * tiles + t, 0))
    in_specs = [
        tok_spec(q.shape[1]),
        pl.BlockSpec((seq, kw), lambda b, t: (b, 0)),
        pl.BlockSpec((None, vw, seq), lambda b, t: (b, 0, 0)),
    ]
    args = [q, k_new, vt_new]
    n_keys = seq
    if cache is not None:
        kc, vtc = cache
        n_c = kc.shape[1]
        n_keys += n_c
        in_specs += [
            pl.BlockSpec((None, n_c, kw), lambda b, t: (b, 0, 0)),
            pl.BlockSpec((None, vw, n_c), lambda b, t: (b, 0, 0)),
        ]
        args += [kc, vtc]
    mod_row = (lambda b, t: (4, 0, 0)) if ctx else (lambda b, t: (b, 0, 0))
    in_specs += [
        tok_spec(D_MODEL),
        tok_spec(D_MODEL),
        pl.BlockSpec((None, 1, 3 * D_MODEL), mod_row),
        pl.BlockSpec((D_MODEL, D_MODEL), lambda b, t: (0, 0)),
    ]
    args += [gate, x, mods_l, w_out]
    if sink is not None:
        in_specs.append(pl.BlockSpec(memory_space=pltpu.SMEM))
        args.append(sink)
    if final_g is not None:
        in_specs.append(pl.BlockSpec((1, D_MODEL), lambda b, t: (0, 0)))
        args.append(final_g.reshape(1, D_MODEL))
    body = functools.partial(
        _attn_kernel,
        head_w=head_w,
        group=1 if kind == 0 else GROUPS,
        has_cache=cache is not None,
        has_sink=sink is not None,
        band=band,
        final=final_g is not None,
    )
    scratch = [pltpu.VMEM((N_HEADS * HEAD_DIM, tq), F32),
               pltpu.VMEM((STAGE_SLOTS, n_keys, tq), F32),
               pltpu.VMEM((STAGE_SLOTS, n_keys, tq), BF16)]
    if band:
        scratch.append(pltpu.VMEM((seq, tq), F32))
    return pl.pallas_call(
        body,
        grid=(batch, tiles),
        in_specs=in_specs,
        out_specs=tok_spec(D_MODEL),
        out_shape=jax.ShapeDtypeStruct((n_tok, D_MODEL), F32),
        scratch_shapes=scratch,
        compiler_params=pltpu.CompilerParams(
            dimension_semantics=("arbitrary", "arbitrary"), vmem_limit_bytes=VMEM_LIMIT),
        name=f"attn_k{kind}_{'ctx' if ctx else 'lat'}",
    )(*args)


def _rope_lane_tables(seq, rot_dim):
    rows = seq // GRID_W
    row = jnp.repeat(jnp.arange(rows), GRID_W).astype(F32)
    col = jnp.tile(jnp.arange(GRID_W), rows).astype(F32)
    nf = rot_dim // 4
    freqs = ROPE_THETA ** (-jnp.arange(nf, dtype=F32) / nf)
    ang = jnp.concatenate([row[:, None] * freqs, col[:, None] * freqs], axis=-1)
    cos, sin = jnp.cos(ang), jnp.sin(ang)
    cos_l = jnp.repeat(cos, 2, axis=-1)
    sin_l = jnp.stack([-sin, sin], axis=-1).reshape(seq, rot_dim)
    return cos_l, sin_l


def _embed_lanes(table, start, fill):
    seq, w = table.shape
    return jnp.concatenate(
        [jnp.full((seq, start), fill, F32), table, jnp.full((seq, LANES - start - w), fill, F32)], axis=1)


def _mla_weights(w_in, q_norm, w_uq, kv_norm, w_ukv):
    c0 = MLA_Q_LORA + MLA_KV_LORA
    win = jnp.concatenate(
        [w_in[:, :c0], w_in[:, c0 + MLA_ROPE:], w_in[:, c0:c0 + MLA_ROPE],
         jnp.zeros((D_MODEL, MLA_IN_PAD - w_in.shape[1]), F32)], axis=1).astype(BF16)
    hq = MLA_NOPE + MLA_ROPE
    wuq = jnp.pad(w_uq.reshape(MLA_Q_LORA, N_HEADS, hq),
                  ((0, 0), (0, 0), (0, MLA_HEAD_PAD - hq))).reshape(MLA_Q_LORA, -1).astype(BF16)
    wkv = w_ukv.reshape(MLA_KV_LORA, N_HEADS, MLA_NOPE + MLA_V)
    wk = jnp.pad(wkv[:, :, :MLA_NOPE],
                 ((0, 0), (0, 0), (0, MLA_HEAD_PAD - MLA_NOPE))).reshape(MLA_KV_LORA, -1).astype(BF16)
    wvt = wkv[:, :, MLA_NOPE:].reshape(MLA_KV_LORA, -1).T.astype(BF16)
    src = lax.broadcasted_iota(jnp.int32, (LANES, N_HEADS * MLA_HEAD_PAD), 0)
    dst = lax.broadcasted_iota(jnp.int32, (LANES, N_HEADS * MLA_HEAD_PAD), 1)
    rp = ((dst % MLA_HEAD_PAD == src + MLA_NOPE) & (src < MLA_ROPE)).astype(BF16)
    return [win, q_norm.reshape(1, -1), wuq, kv_norm.reshape(1, -1), wk, rp, wvt]


def kernel(x_prompt, x_sample, cache_mla_ckv, cache_mla_kpe, cache_gqa_k, cache_gqa_v,
           cache_swa_k, cache_swa_v, c, c_ctx, norm_g, w_ada, b_ada, w_out,
           mla_w_in, mla_q_norm, mla_w_uq, mla_kv_norm, mla_w_ukv,
           gqa_w_in, gqa_q_norm, gqa_k_norm, swa_w_in, swa_sink, final_norm_g):
    bc, sc, _ = x_prompt.shape
    bl, sl, _ = x_sample.shape
    n_past = cache_mla_ckv.shape[2]
    kvw = KV_HEADS * HEAD_DIM

    cc = jnp.concatenate([c, c_ctx[None, :], jnp.zeros((8 - bl - 1, D_MODEL), F32)], axis=0)
    mods = _ada_mods(cc, w_ada, b_ada).reshape(DEPTH, 8, 1, 3 * D_MODEL)

    cos_h, sin_h = _rope_lane_tables(sl, HEAD_DIM)
    gqa_tables = [jnp.tile(cos_h, (1, 2)), jnp.tile(sin_h, (1, 2))]
    cos_r, sin_r = _rope_lane_tables(sl, MLA_ROPE)
    mla_tables = [_embed_lanes(cos_r, MLA_NOPE, 1.0), _embed_lanes(sin_r, MLA_NOPE, 0.0),
                  _embed_lanes(cos_r, 0, 1.0), _embed_lanes(sin_r, 0, 0.0)]

    xc = x_prompt.reshape(bc * sc, D_MODEL)
    xl = x_sample.reshape(bl * sl, D_MODEL)
    new_caches = {0: ([], []), 1: ([], []), 2: ([], [])}
    for i in range(DEPTH):
        kind, j = i % 3, i // 3
        mods_l = mods[i]
        wo = w_out[i].astype(BF16)
        final_g = final_norm_g if i == DEPTH - 1 else None
        sink = None
        if kind == 0:
            weights = _mla_weights(mla_w_in[j], mla_q_norm[j], mla_w_uq[j], mla_kv_norm[j], mla_w_ukv[j])
            tables = mla_tables
            kpe_pad = jnp.pad(cache_mla_kpe[:, j], ((0, 0), (0, 0), (0, LANES - MLA_ROPE)))
            cache = _mla_cache(cache_mla_ckv[:, j], kpe_pad, weights[4], weights[5], weights[6])
        else:
            tables = gqa_tables
            if kind == 1:
                weights = [gqa_w_in[j].astype(BF16),
                           jnp.tile(gqa_q_norm[j], 2).reshape(1, LANES),
                           jnp.tile(gqa_k_norm[j], 2).reshape(1, LANES)]
                cache = _gqa_cache(cache_gqa_k[:, j].reshape(bl, n_past, kvw),
                                   cache_gqa_v[:, j].reshape(bl, n_past, kvw))
            else:
                weights = [swa_w_in[j].astype(BF16)]
                sink = swa_sink[j]
                cache = _gqa_cache(cache_swa_k[:, j].reshape(bl, n_past, kvw),
                                   cache_swa_v[:, j].reshape(bl, n_past, kvw))

        q, k, vt, gate, c_a, c_b = _project(kind, xc, mods_l, norm_g[i], weights, tables,
                                            batch=bc, seq=sc, ctx=True)
        new_caches[kind][0].append(c_a)
        new_caches[kind][1].append(c_b)
        xc = _attend(kind, q, k, vt, None, gate, xc, mods_l, wo, sink, final_g,
                     batch=bc, seq=sc, ctx=True)

        q, k, vt, gate = _project(kind, xl, mods_l, norm_g[i], weights, tables,
                                  batch=bl, seq=sl, ctx=False)
        xl = _attend(kind, q, k, vt, cache, gate, xl, mods_l, wo, sink, final_g,
                     batch=bl, seq=sl, ctx=False)

    def stack(parts, tail):
        return jnp.stack([p.reshape((bc, sc) + tail) for p in parts], axis=1)

    return (xc.reshape(bc, sc, D_MODEL), xl.reshape(bl, sl, D_MODEL),
            stack(new_caches[0][0], (MLA_KV_LORA,)), stack(new_caches[0][1], (MLA_ROPE,)),
            stack(new_caches[1][0], (KV_HEADS, HEAD_DIM)), stack(new_caches[1][1], (KV_HEADS, HEAD_DIM)),
            stack(new_caches[2][0], (KV_HEADS, HEAD_DIM)), stack(new_caches[2][1], (KV_HEADS, HEAD_DIM)))
```

```python
import functools

import jax
import jax.numpy as jnp
from jax import lax
from jax.experimental import pallas as pl
from jax.experimental.pallas import tpu as pltpu

F32 = jnp.float32
BF16 = jnp.bfloat16

D_MODEL = 1024
DEPTH = 4
GRID_W = 64
N_HEADS = 16
HEAD_DIM = 64
KV_HEADS = 4
GROUPS = N_HEADS // KV_HEADS
MLA_Q_LORA = 384
MLA_KV_LORA = 256
MLA_NOPE = 64
MLA_ROPE = 32
MLA_V = 64
WINDOW = 128
ROPE_THETA = 10000.0
EPS = 1e-6
NEG_INF = -1e30

LOG2E = 1.4426950408889634
GQA_Q_SCALE = HEAD_DIM ** -0.5 * LOG2E
MLA_Q_SCALE = (MLA_NOPE + MLA_ROPE) ** -0.5 * LOG2E

LANES = 128
TOKEN_TILE = 256
KEY_CHUNK = 256
CTX_STAGE_LAGS = (3, 3, True)
LAT_STAGE_LAGS = (2, 2, False)
MLA_HEAD_PAD = 128
MLA_IN_PAD = 1792
VMEM_LIMIT = 56 * 1024 * 1024
NT_DIMS = (((1,), (1,)), ((), ()))


def _silu(v):
    return v * jax.nn.sigmoid(v)


def _rms(v, g):
    return v * lax.rsqrt(jnp.mean(v * v, axis=-1, keepdims=True) + EPS) * g


def _mod_norm(x, g, mod):
    shift = mod[:, :D_MODEL]
    scale = mod[:, D_MODEL:2 * D_MODEL]
    return _rms(x, g) * (1.0 + scale) + shift


def _rope_tile(v, cos, sin_signed):
    lane = lax.broadcasted_iota(jnp.int32, v.shape, 1)
    nxt = pltpu.roll(v, LANES - 1, axis=1)
    prv = pltpu.roll(v, 1, axis=1)
    swapped = jnp.where((lane & 1) == 0, nxt, prv)
    return v * cos + swapped * sin_signed


def _ada_kernel(c_ref, w_ref, b_ref, o_ref):
    s = _silu(c_ref[...]).astype(BF16)
    o_ref[...] = jnp.dot(s, w_ref[...].astype(BF16), preferred_element_type=F32) + b_ref[...]


def _ada_mods(cc, w_ada, b_ada):
    tn = D_MODEL
    return pl.pallas_call(
        _ada_kernel,
        grid=(DEPTH, 3 * D_MODEL // tn),
        in_specs=[
            pl.BlockSpec((8, D_MODEL), lambda i, n: (0, 0)),
            pl.BlockSpec((None, D_MODEL, tn), lambda i, n: (i, 0, n)),
            pl.BlockSpec((None, 1, tn), lambda i, n: (i, 0, n)),
        ],
        out_specs=pl.BlockSpec((None, 8, tn), lambda i, n: (i, 0, n)),
        out_shape=jax.ShapeDtypeStruct((DEPTH, 8, 3 * D_MODEL), F32),
        compiler_params=pltpu.CompilerParams(vmem_limit_bytes=VMEM_LIMIT),
        name="ada_mods",
    )(cc, w_ada, b_ada.reshape(DEPTH, 1, 3 * D_MODEL))


def _mla_expand(ckv_bf, kpe_bf, wk_ref, rp_ref, wvt_ref):
    k = (jnp.dot(ckv_bf, wk_ref[...], preferred_element_type=F32)
         + jnp.dot(kpe_bf, rp_ref[...], preferred_element_type=F32))
    vt = lax.dot_general(wvt_ref[...], ckv_bf, NT_DIMS, preferred_element_type=F32)
    return k.astype(BF16), vt.astype(BF16)


def _proj_mla_kernel(*refs, rope, ctx):
    refs = list(refs)
    x_ref, mod_ref, g_ref, win_ref, qn_ref, wuq_ref, kvn_ref, wk_ref, rp_ref, wvt_ref = refs[:10]
    pos = 10
    if rope:
        cq_ref, sq_ref, ck_ref, sk_ref = refs[pos:pos + 4]
        pos += 4
    q_out, k_out, vt_out, gate_out = refs[pos:pos + 4]
    pos += 4
    if ctx:
        ckv_out, kpe_out = refs[pos:pos + 2]

    h = _mod_norm(x_ref[...], g_ref[...], mod_ref[...]).astype(BF16)
    z = jnp.dot(h, win_ref[...], preferred_element_type=F32)
    cq = z[:, :MLA_Q_LORA]
    ckv = z[:, MLA_Q_LORA:MLA_Q_LORA + MLA_KV_LORA]
    gate = z[:, 640:640 + D_MODEL]
    kpe = z[:, 640 + D_MODEL:]
    gate_out[...] = gate.astype(BF16)

    qf = jnp.dot(_rms(cq, qn_ref[...]).astype(BF16), wuq_ref[...], preferred_element_type=F32)
    ckvn = _rms(ckv, kvn_ref[...])
    if ctx:
        ckv_out[...] = ckvn
        kpe_out[...] = kpe[:, :MLA_ROPE]
    if rope:
        kpe = _rope_tile(kpe, ck_ref[...], sk_ref[...])
    k, vt = _mla_expand(ckvn.astype(BF16), kpe.astype(BF16), wk_ref, rp_ref, wvt_ref)
    k_out[...] = k
    vt_out[...] = vt
    for hd in range(N_HEADS):
        blk = qf[:, MLA_HEAD_PAD * hd:MLA_HEAD_PAD * (hd + 1)]
        if rope:
            blk = _rope_tile(blk, cq_ref[...], sq_ref[...])
        q_out[:, MLA_HEAD_PAD * hd:MLA_HEAD_PAD * (hd + 1)] = (blk * MLA_Q_SCALE).astype(BF16)


def _proj_gqa_kernel(*refs, qk_norm, rope, ctx):
    refs = list(refs)
    x_ref, mod_ref, g_ref, win_ref = refs[:4]
    pos = 4
    if qk_norm:
        qn_ref, kn_ref = refs[pos:pos + 2]
        pos += 2
    if rope:
        cos_ref, sin_ref = refs[pos:pos + 2]
        pos += 2
    q_out, k_out, vt_out, gate_out = refs[pos:pos + 4]
    pos += 4
    if ctx:
        kc_out, vc_out = refs[pos:pos + 2]

    nq = N_HEADS * HEAD_DIM
    nk = KV_HEADS * HEAD_DIM
    h = _mod_norm(x_ref[...], g_ref[...], mod_ref[...]).astype(BF16)
    z = jnp.dot(h, win_ref[...], preferred_element_type=F32)
    tm = z.shape[0]
    lo = lax.broadcasted_iota(jnp.int32, (tm, LANES), 1) < HEAD_DIM
    n_q_tiles = nq // LANES
    for c in range((nq + nk) // LANES):
        blk = z[:, LANES * c:LANES * (c + 1)]
        is_q = c < n_q_tiles
        if qk_norm:
            sq = blk * blk
            s_lo = jnp.sum(jnp.where(lo, sq, 0.0), axis=-1, keepdims=True)
            s_hi = jnp.sum(jnp.where(lo, 0.0, sq), axis=-1, keepdims=True)
            inv = jnp.where(lo, lax.rsqrt(s_lo * (1.0 / HEAD_DIM) + EPS),
                            lax.rsqrt(s_hi * (1.0 / HEAD_DIM) + EPS))
            blk = blk * inv * (qn_ref[...] if is_q else kn_ref[...])
        if ctx and not is_q:
            kc_out[:, LANES * (c - n_q_tiles):LANES * (c - n_q_tiles + 1)] = blk
        if rope:
            blk = _rope_tile(blk, cos_ref[...], sin_ref[...])
        if is_q:
            q_out[:, LANES * c:LANES * (c + 1)] = (blk * GQA_Q_SCALE).astype(BF16)
        else:
            k_out[:, LANES * (c - n_q_tiles):LANES * (c - n_q_tiles + 1)] = blk.astype(BF16)
    v = z[:, nq + nk:nq + 2 * nk]
    if ctx:
        vc_out[...] = v
    vt_out[...] = v.T.astype(BF16)
    gate_out[...] = z[:, nq + 2 * nk:].astype(BF16)


def _const_spec(shape):
    return pl.BlockSpec(shape, lambda i: (0,) * len(shape))


def _project(kind, x, mods_l, norm_g, weights, tables, *, batch, seq, ctx):
    n_tok = batch * seq
    tm = TOKEN_TILE
    tiles_per_seq = seq // tm
    rope = not ctx
    mod_row = (lambda i: (4, 0, 0)) if ctx else (lambda i: (i // tiles_per_seq, 0, 0))
    in_specs = [
        pl.BlockSpec((tm, D_MODEL), lambda i: (i, 0)),
        pl.BlockSpec((None, 1, 3 * D_MODEL), mod_row),
        _const_spec((1, D_MODEL)),
    ]
    args = [x, mods_l, norm_g.reshape(1, D_MODEL)]
    for w in weights:
        in_specs.append(_const_spec(w.shape))
        args.append(w)
    if rope:
        for t in tables:
            in_specs.append(pl.BlockSpec((tm, LANES), lambda i: (i % tiles_per_seq, 0)))
            args.append(t)
    if kind == 0:
        qw, kw, vw = N_HEADS * MLA_HEAD_PAD, N_HEADS * MLA_HEAD_PAD, N_HEADS * MLA_V
        body = functools.partial(_proj_mla_kernel, rope=rope, ctx=ctx)
        cache_shapes = [(n_tok, MLA_KV_LORA), (n_tok, MLA_ROPE)]
    else:
        qw, kw, vw = N_HEADS * HEAD_DIM, KV_HEADS * HEAD_DIM, KV_HEADS * HEAD_DIM
        body = functools.partial(_proj_gqa_kernel, qk_norm=(kind == 1), rope=rope, ctx=ctx)
        cache_shapes = [(n_tok, kw), (n_tok, vw)]
    out_shape = [
        jax.ShapeDtypeStruct((n_tok, qw), BF16),
        jax.ShapeDtypeStruct((n_tok, kw), BF16),
        jax.ShapeDtypeStruct((batch, vw, seq), BF16),
        jax.ShapeDtypeStruct((n_tok, D_MODEL), BF16),
    ]
    out_specs = [
        pl.BlockSpec((tm, qw), lambda i: (i, 0)),
        pl.BlockSpec((tm, kw), lambda i: (i, 0)),
        pl.BlockSpec((None, vw, tm), lambda i: (i // tiles_per_seq, 0, i % tiles_per_seq)),
        pl.BlockSpec((tm, D_MODEL), lambda i: (i, 0)),
    ]
    if ctx:
        for s in cache_shapes:
            out_shape.append(jax.ShapeDtypeStruct(s, F32))
            out_specs.append(pl.BlockSpec((tm, s[1]), lambda i: (i, 0)))
    return pl.pallas_call(
        body,
        grid=(n_tok // tm,),
        in_specs=in_specs,
        out_specs=out_specs,
        out_shape=out_shape,
        compiler_params=pltpu.CompilerParams(
            dimension_semantics=("arbitrary",), vmem_limit_bytes=VMEM_LIMIT),
        name=f"proj_k{kind}_{'ctx' if ctx else 'lat'}",
    )(*args)


def _mla_cache_kernel(ckv_ref, kpe_ref, wk_ref, rp_ref, wvt_ref, k_out, vt_out):
    k, vt = _mla_expand(ckv_ref[...].astype(BF16), kpe_ref[...].astype(BF16), wk_ref, rp_ref, wvt_ref)
    k_out[...] = k
    vt_out[...] = vt


def _mla_cache(ckv, kpe_pad, wk, rp, wvt):
    b, n, _ = ckv.shape
    return pl.pallas_call(
        _mla_cache_kernel,
        grid=(b,),
        in_specs=[
            pl.BlockSpec((None, n, MLA_KV_LORA), lambda i: (i, 0, 0)),
            pl.BlockSpec((None, n, LANES), lambda i: (i, 0, 0)),
            _const_spec(wk.shape), _const_spec(rp.shape), _const_spec(wvt.shape),
        ],
        out_specs=[
            pl.BlockSpec((None, n, N_HEADS * MLA_HEAD_PAD), lambda i: (i, 0, 0)),
            pl.BlockSpec((None, N_HEADS * MLA_V, n), lambda i: (i, 0, 0)),
        ],
        out_shape=[
            jax.ShapeDtypeStruct((b, n, N_HEADS * MLA_HEAD_PAD), BF16),
            jax.ShapeDtypeStruct((b, N_HEADS * MLA_V, n), BF16),
        ],
        compiler_params=pltpu.CompilerParams(
            dimension_semantics=("arbitrary",), vmem_limit_bytes=VMEM_LIMIT),
        name="mla_cache",
    )(ckv, kpe_pad, wk, rp, wvt)


def _gqa_cache_kernel(k_ref, v_ref, k_out, vt_out):
    k_out[...] = k_ref[...].astype(BF16)
    vt_out[...] = v_ref[...].T.astype(BF16)


def _gqa_cache(k, v):
    b, n, w = k.shape
    return pl.pallas_call(
        _gqa_cache_kernel,
        grid=(b,),
        in_specs=[pl.BlockSpec((None, n, w), lambda i: (i, 0, 0))] * 2,
        out_specs=[
            pl.BlockSpec((None, n, w), lambda i: (i, 0, 0)),
            pl.BlockSpec((None, w, n), lambda i: (i, 0, 0)),
        ],
        out_shape=[
            jax.ShapeDtypeStruct((b, n, w), BF16),
            jax.ShapeDtypeStruct((b, w, n), BF16),
        ],
        compiler_params=pltpu.CompilerParams(dimension_semantics=("arbitrary",)),
        name="gqa_cache",
    )(k, v)


def _attn_kernel(*refs, head_w, group, has_cache, has_sink, band, final, lag_b, lag_c, interleave):
    refs = list(refs)
    q_ref, kn_ref, vtn_ref = refs[:3]
    pos = 3
    segs = []
    if has_cache:
        segs.append((refs[pos], refs[pos + 1]))
        pos += 2
    segs.append((kn_ref, vtn_ref))
    gate_ref, x_ref, mod_ref, wo_ref = refs[pos:pos + 4]
    pos += 4
    if has_sink:
        sink_ref = refs[pos]
        pos += 1
    if final:
        fg_ref = refs[pos]
        pos += 1
    o_ref, ot_scr, s_scr, p_scr = refs[pos:pos + 4]
    if band:
        bias_scr = refs[pos + 4]

    tq = q_ref.shape[0]
    ch = KEY_CHUNK
    slots = s_scr.shape[0]
    seg_rows = []
    chunks = []
    off = 0
    for si, (k_ref, _) in enumerate(segs):
        seg_rows.append((off, k_ref.shape[0]))
        chunks += [(si, r, off + r) for r in range(0, k_ref.shape[0], ch)]
        off += k_ref.shape[0]
    if interleave:
        score_chunks = [c + (ch,) for c in chunks]
    else:
        score_chunks = [(si, 0, r0, n) for si, (r0, n) in enumerate(seg_rows)]
    own = len(segs) - 1
    if band:
        n_own = kn_ref.shape[0]
        kpos = lax.broadcasted_iota(jnp.int32, (n_own, tq), 0)
        qpos = pl.program_id(1) * tq + lax.broadcasted_iota(jnp.int32, (n_own, tq), 1)
        bias_scr[...] = jnp.where(jnp.abs(kpos - qpos) <= WINDOW, 0.0, NEG_INF)

    assert lag_b >= 1 and lag_c >= 1 and slots > max(lag_b, lag_c)
    row_max = {}
    row_sum = {}

    def scores_and_max(hd):
        kh = hd // group
        qh = q_ref[:, head_w * hd:head_w * (hd + 1)]
        mx = None
        for si, r, g, n in score_chunks:
            kk = segs[si][0][r:r + n, head_w * kh:head_w * (kh + 1)]
            s = lax.dot_general(kk, qh, NT_DIMS, preferred_element_type=F32)
            if band and si == own:
                s = s + bias_scr[r:r + n, :]
            s_scr[hd % slots, g:g + n, :] = s
            part = jnp.max(s.reshape(n // 8, 8, tq), axis=0)
            mx = part if mx is None else jnp.maximum(mx, part)
            yield
        m = jnp.max(mx, axis=0, keepdims=True)
        if has_sink:
            m = jnp.maximum(m, sink_ref[hd] * LOG2E)
        row_max[hd] = m

    def exponentials(hd):
        m = row_max.pop(hd)
        mb = jnp.broadcast_to(m, (8, tq))
        tot = None
        for _, _, g in chunks:
            p = jnp.exp2(s_scr[hd % slots, g:g + ch, :].reshape(ch // 8, 8, tq) - mb)
            part = jnp.sum(p, axis=0)
            tot = part if tot is None else tot + part
            p_scr[hd % slots, g:g + ch, :] = p.reshape(ch, tq).astype(BF16)
            yield
        den = jnp.sum(tot, axis=0, keepdims=True)
        if has_sink:
            den = den + jnp.exp2(sink_ref[hd] * LOG2E - m)
        row_sum[hd] = den

    def weighted_values(hd):
        kh = hd // group
        acc = None
        for si, (_, vt_ref) in enumerate(segs):
            r0, n = seg_rows[si]
            vt = vt_ref[HEAD_DIM * kh:HEAD_DIM * (kh + 1), :]
            part = jnp.dot(vt, p_scr[hd % slots, r0:r0 + n, :], preferred_element_type=F32)
            acc = part if acc is None else acc + part
            yield
        ot_scr[HEAD_DIM * hd:HEAD_DIM * (hd + 1), :] = acc / row_sum.pop(hd)

    for step in range(N_HEADS + lag_b + lag_c):
        live = []
        if step < N_HEADS:
            live.append(scores_and_max(step))
        if 0 <= step - lag_b < N_HEADS:
            live.append(exponentials(step - lag_b))
        if 0 <= step - lag_b - lag_c < N_HEADS:
            live.append(weighted_values(step - lag_b - lag_c))
        if not interleave:
            for g in live:
                for _ in g:
                    pass
            live = []
        while live:
            live = [g for g in live if next(g, True) is None]

    o = ot_scr[...].T
    gated = (o * _silu(gate_ref[...].astype(F32))).astype(BF16)
    y = jnp.dot(gated, wo_ref[...], preferred_element_type=F32)
    xn = x_ref[...] + mod_ref[...][:, 2 * D_MODEL:] * y
    if final:
        xn = _rms(xn, fg_ref[...])
    o_ref[...] = xn


def _attend(kind, q, k_new, vt_new, cache, gate, x, mods_l, w_out, sink, final_g, *, batch, seq, ctx,
            lags):
    lag_b, lag_c, interleave = lags
    tq = TOKEN_TILE
    n_tok = batch * seq
    tiles = seq // tq
    band = kind == 2 and not ctx
    head_w = MLA_HEAD_PAD if kind == 0 else HEAD_DIM
    kw = k_new.shape[1]
    vw = vt_new.shape[1]
    tok_spec = lambda w: pl.BlockSpec((tq, w), lambda b, t: (b * tiles + t, 0))
    in_specs = [
        tok_spec(q.shape[1]),
        pl.BlockSpec((seq, kw), lambda b, t: (b, 0)),
        pl.BlockSpec((None, vw, seq), lambda b, t: (b, 0, 0)),
    ]
    args = [q, k_new, vt_new]
    n_keys = seq
    if cache is not None:
        kc, vtc = cache
        n_c = kc.shape[1]
        n_keys += n_c
        in_specs += [
            pl.BlockSpec((None, n_c, kw), lambda b, t: (b, 0, 0)),
            pl.BlockSpec((None, vw, n_c), lambda b, t: (b, 0, 0)),
        ]
        args += [kc, vtc]
    mod_row = (lambda b, t: (4, 0, 0)) if ctx else (lambda b, t: (b, 0, 0))
    in_specs += [
        tok_spec(D_MODEL),
        tok_spec(D_MODEL),
        pl.BlockSpec((None, 1, 3 * D_MODEL), mod_row),
        pl.BlockSpec((D_MODEL, D_MODEL), lambda b, t: (0, 0)),
    ]
    args += [gate, x, mods_l, w_out]
    if sink is not None:
        in_specs.append(pl.BlockSpec(memory_space=pltpu.SMEM))
        args.append(sink)
    if final_g is not None:
        in_specs.append(pl.BlockSpec((1, D_MODEL), lambda b, t: (0, 0)))
        args.append(final_g.reshape(1, D_MODEL))
    body = functools.partial(
        _attn_kernel,
        head_w=head_w,
        group=1 if kind == 0 else GROUPS,
        has_cache=cache is not None,
        has_sink=sink is not None,
        band=band,
        final=final_g is not None,
        lag_b=lag_b,
        lag_c=lag_c,
        interleave=interleave,
    )
    slots = max(lag_b, lag_c) + 1
    scratch = [pltpu.VMEM((N_HEADS * HEAD_DIM, tq), F32),
               pltpu.VMEM((slots, n_keys, tq), F32),
               pltpu.VMEM((slots, n_keys, tq), BF16)]
    if band:
        scratch.append(pltpu.VMEM((seq, tq), F32))
    return pl.pallas_call(
        body,
        grid=(batch, tiles),
        in_specs=in_specs,
        out_specs=tok_spec(D_MODEL),
        out_shape=jax.ShapeDtypeStruct((n_tok, D_MODEL), F32),
        scratch_shapes=scratch,
        compiler_params=pltpu.CompilerParams(
            dimension_semantics=("arbitrary", "arbitrary"), vmem_limit_bytes=VMEM_LIMIT),
        name=f"attn_k{kind}_{'ctx' if ctx else 'lat'}",
    )(*args)


def _rope_lane_tables(seq, rot_dim):
    rows = seq // GRID_W
    row = jnp.repeat(jnp.arange(rows), GRID_W).astype(F32)
    col = jnp.tile(jnp.arange(GRID_W), rows).astype(F32)
    nf = rot_dim // 4
    freqs = ROPE_THETA ** (-jnp.arange(nf, dtype=F32) / nf)
    ang = jnp.concatenate([row[:, None] * freqs, col[:, None] * freqs], axis=-1)
    cos, sin = jnp.cos(ang), jnp.sin(ang)
    cos_l = jnp.repeat(cos, 2, axis=-1)
    sin_l = jnp.stack([-sin, sin], axis=-1).reshape(seq, rot_dim)
    return cos_l, sin_l


def _embed_lanes(table, start, fill):
    seq, w = table.shape
    return jnp.concatenate(
        [jnp.full((seq, start), fill, F32), table, jnp.full((seq, LANES - start - w), fill, F32)], axis=1)


def _mla_weights(w_in, q_norm, w_uq, kv_norm, w_ukv):
    c0 = MLA_Q_LORA + MLA_KV_LORA
    win = jnp.concatenate(
        [w_in[:, :c0], w_in[:, c0 + MLA_ROPE:], w_in[:, c0:c0 + MLA_ROPE],
         jnp.zeros((D_MODEL, MLA_IN_PAD - w_in.shape[1]), F32)], axis=1).astype(BF16)
    hq = MLA_NOPE + MLA_ROPE
    wuq = jnp.pad(w_uq.reshape(MLA_Q_LORA, N_HEADS, hq),
                  ((0, 0), (0, 0), (0, MLA_HEAD_PAD - hq))).reshape(MLA_Q_LORA, -1).astype(BF16)
    wkv = w_ukv.reshape(MLA_KV_LORA, N_HEADS, MLA_NOPE + MLA_V)
    wk = jnp.pad(wkv[:, :, :MLA_NOPE],
                 ((0, 0), (0, 0), (0, MLA_HEAD_PAD - MLA_NOPE))).reshape(MLA_KV_LORA, -1).astype(BF16)
    wvt = wkv[:, :, MLA_NOPE:].reshape(MLA_KV_LORA, -1).T.astype(BF16)
    src = lax.broadcasted_iota(jnp.int32, (LANES, N_HEADS * MLA_HEAD_PAD), 0)
    dst = lax.broadcasted_iota(jnp.int32, (LANES, N_HEADS * MLA_HEAD_PAD), 1)
    rp = ((dst % MLA_HEAD_PAD == src + MLA_NOPE) & (src < MLA_ROPE)).astype(BF16)
    return [win, q_norm.reshape(1, -1), wuq, kv_norm.reshape(1, -1), wk, rp, wvt]


def kernel(x_prompt, x_sample, cache_mla_ckv, cache_mla_kpe, cache_gqa_k, cache_gqa_v,
           cache_swa_k, cache_swa_v, c, c_ctx, norm_g, w_ada, b_ada, w_out,
           mla_w_in, mla_q_norm, mla_w_uq, mla_kv_norm, mla_w_ukv,
           gqa_w_in, gqa_q_norm, gqa_k_norm, swa_w_in, swa_sink, final_norm_g):
    bc, sc, _ = x_prompt.shape
    bl, sl, _ = x_sample.shape
    n_past = cache_mla_ckv.shape[2]
    kvw = KV_HEADS * HEAD_DIM

    cc = jnp.concatenate([c, c_ctx[None, :], jnp.zeros((8 - bl - 1, D_MODEL), F32)], axis=0)
    mods = _ada_mods(cc, w_ada, b_ada).reshape(DEPTH, 8, 1, 3 * D_MODEL)

    cos_h, sin_h = _rope_lane_tables(sl, HEAD_DIM)
    gqa_tables = [jnp.tile(cos_h, (1, 2)), jnp.tile(sin_h, (1, 2))]
    cos_r, sin_r = _rope_lane_tables(sl, MLA_ROPE)
    mla_tables = [_embed_lanes(cos_r, MLA_NOPE, 1.0), _embed_lanes(sin_r, MLA_NOPE, 0.0),
                  _embed_lanes(cos_r, 0, 1.0), _embed_lanes(sin_r, 0, 0.0)]

    xc = x_prompt.reshape(bc * sc, D_MODEL)
    xl = x_sample.reshape(bl * sl, D_MODEL)
    new_caches = {0: ([], []), 1: ([], []), 2: ([], [])}
    for i in range(DEPTH):
        kind, j = i % 3, i // 3
        mods_l = mods[i]
        wo = w_out[i].astype(BF16)
        final_g = final_norm_g if i == DEPTH - 1 else None
        sink = None
        if kind == 0:
            weights = _mla_weights(mla_w_in[j], mla_q_norm[j], mla_w_uq[j], mla_kv_norm[j], mla_w_ukv[j])
            tables = mla_tables
            kpe_pad = jnp.pad(cache_mla_kpe[:, j], ((0, 0), (0, 0), (0, LANES - MLA_ROPE)))
            cache = _mla_cache(cache_mla_ckv[:, j], kpe_pad, weights[4], weights[5], weights[6])
        else:
            tables = gqa_tables
            if kind == 1:
                weights = [gqa_w_in[j].astype(BF16),
                           jnp.tile(gqa_q_norm[j], 2).reshape(1, LANES),
                           jnp.tile(gqa_k_norm[j], 2).reshape(1, LANES)]
                cache = _gqa_cache(cache_gqa_k[:, j].reshape(bl, n_past, kvw),
                                   cache_gqa_v[:, j].reshape(bl, n_past, kvw))
            else:
                weights = [swa_w_in[j].astype(BF16)]
                sink = swa_sink[j]
                cache = _gqa_cache(cache_swa_k[:, j].reshape(bl, n_past, kvw),
                                   cache_swa_v[:, j].reshape(bl, n_past, kvw))

        q, k, vt, gate, c_a, c_b = _project(kind, xc, mods_l, norm_g[i], weights, tables,
                                            batch=bc, seq=sc, ctx=True)
        new_caches[kind][0].append(c_a)
        new_caches[kind][1].append(c_b)
        xc = _attend(kind, q, k, vt, None, gate, xc, mods_l, wo, sink, final_g,
                     batch=bc, seq=sc, ctx=True, lags=CTX_STAGE_LAGS)

        q, k, vt, gate = _project(kind, xl, mods_l, norm_g[i], weights, tables,
                                  batch=bl, seq=sl, ctx=False)
        xl = _attend(kind, q, k, vt, cache, gate, xl, mods_l, wo, sink, final_g,
                     batch=bl, seq=sl, ctx=False, lags=LAT_STAGE_LAGS)

    def stack(parts, tail):
        return jnp.stack([p.reshape((bc, sc) + tail) for p in parts], axis=1)

    return (xc.reshape(bc, sc, D_MODEL), xl.reshape(bl, sl, D_MODEL),
            stack(new_caches[0][0], (MLA_KV_LORA,)), stack(new_caches[0][1], (MLA_ROPE,)),
            stack(new_caches[1][0], (KV_HEADS, HEAD_DIM)), stack(new_caches[1][1], (KV_HEADS, HEAD_DIM)),
            stack(new_caches[2][0], (KV_HEADS, HEAD_DIM)), stack(new_caches[2][1], (KV_HEADS, HEAD_DIM)))
```

```python
import functools

import jax
import jax.numpy as jnp
from jax import lax
from jax.experimental import pallas as pl
from jax.experimental.pallas import tpu as pltpu

F32 = jnp.float32
BF16 = jnp.bfloat16

D_MODEL = 1024
DEPTH = 4
GRID_W = 64
N_HEADS = 16
HEAD_DIM = 64
KV_HEADS = 4
GROUPS = N_HEADS // KV_HEADS
MLA_Q_LORA = 384
MLA_KV_LORA = 256
MLA_NOPE = 64
MLA_ROPE = 32
MLA_V = 64
WINDOW = 128
ROPE_THETA = 10000.0
EPS = 1e-6
NEG_INF = -1e30

LOG2E = 1.4426950408889634
GQA_Q_SCALE = HEAD_DIM ** -0.5 * LOG2E
MLA_Q_SCALE = (MLA_NOPE + MLA_ROPE) ** -0.5 * LOG2E

LANES = 128
TOKEN_TILE = 256
PROJ_TILE = 512
KEY_CHUNK = 256
OUT_BLOCK = 256
CTX_STAGE_LAGS = (3, 3, True, 2)
LAT_STAGE_LAGS = (2, 2, False, 2)
MLA_HEAD_PAD = 128
MLA_IN_PAD = 1792
VMEM_LIMIT = 56 * 1024 * 1024
NT_DIMS = (((1,), (1,)), ((), ()))


def _silu(v):
    return v * jax.nn.sigmoid(v)


def _rms(v, g):
    return v * lax.rsqrt(jnp.mean(v * v, axis=-1, keepdims=True) + EPS) * g


def _mod_norm(x, g, mod):
    shift = mod[:, :D_MODEL]
    scale = mod[:, D_MODEL:2 * D_MODEL]
    return _rms(x, g) * (1.0 + scale) + shift


def _rope_tile(v, cos, sin_signed):
    lane = lax.broadcasted_iota(jnp.int32, v.shape, 1)
    nxt = pltpu.roll(v, LANES - 1, axis=1)
    prv = pltpu.roll(v, 1, axis=1)
    swapped = jnp.where((lane & 1) == 0, nxt, prv)
    return v * cos + swapped * sin_signed


def _ada_kernel(c_ref, w_ref, b_ref, o_ref):
    s = _silu(c_ref[...]).astype(BF16)
    o_ref[...] = jnp.dot(s, w_ref[...].astype(BF16), preferred_element_type=F32) + b_ref[...]


def _ada_mods(cc, w_ada, b_ada):
    tn = D_MODEL
    return pl.pallas_call(
        _ada_kernel,
        grid=(DEPTH, 3 * D_MODEL // tn),
        in_specs=[
            pl.BlockSpec((8, D_MODEL), lambda i, n: (0, 0)),
            pl.BlockSpec((None, D_MODEL, tn), lambda i, n: (i, 0, n)),
            pl.BlockSpec((None, 1, tn), lambda i, n: (i, 0, n)),
        ],
        out_specs=pl.BlockSpec((None, 8, tn), lambda i, n: (i, 0, n)),
        out_shape=jax.ShapeDtypeStruct((DEPTH, 8, 3 * D_MODEL), F32),
        compiler_params=pltpu.CompilerParams(vmem_limit_bytes=VMEM_LIMIT),
        name="ada_mods",
    )(cc, w_ada, b_ada.reshape(DEPTH, 1, 3 * D_MODEL))


def _mla_expand(ckv_bf, kpe_bf, wk_ref, rp_ref, wvt_ref):
    k = (jnp.dot(ckv_bf, wk_ref[...], preferred_element_type=F32)
         + jnp.dot(kpe_bf, rp_ref[...], preferred_element_type=F32))
    vt = lax.dot_general(wvt_ref[...], ckv_bf, NT_DIMS, preferred_element_type=F32)
    return k.astype(BF16), vt.astype(BF16)


def _proj_mla_kernel(*refs, rope, ctx):
    refs = list(refs)
    x_ref, mod_ref, g_ref, win_ref, qn_ref, wuq_ref, kvn_ref, wk_ref, rp_ref, wvt_ref = refs[:10]
    pos = 10
    if rope:
        cq_ref, sq_ref, ck_ref, sk_ref = refs[pos:pos + 4]
        pos += 4
    q_out, k_out, vt_out, gate_out = refs[pos:pos + 4]
    pos += 4
    if ctx:
        ckv_out, kpe_out = refs[pos:pos + 2]

    h = _mod_norm(x_ref[...], g_ref[...], mod_ref[...]).astype(BF16)
    z = jnp.dot(h, win_ref[...], preferred_element_type=F32)
    cq = z[:, :MLA_Q_LORA]
    ckv = z[:, MLA_Q_LORA:MLA_Q_LORA + MLA_KV_LORA]
    gate = z[:, 640:640 + D_MODEL]
    kpe = z[:, 640 + D_MODEL:]
    gate_out[...] = gate.astype(BF16)

    qf = jnp.dot(_rms(cq, qn_ref[...]).astype(BF16), wuq_ref[...], preferred_element_type=F32)
    ckvn = _rms(ckv, kvn_ref[...])
    if ctx:
        ckv_out[...] = ckvn
        kpe_out[...] = kpe[:, :MLA_ROPE]
    if rope:
        kpe = _rope_tile(kpe, ck_ref[...], sk_ref[...])
    k, vt = _mla_expand(ckvn.astype(BF16), kpe.astype(BF16), wk_ref, rp_ref, wvt_ref)
    k_out[...] = k
    vt_out[...] = vt
    for hd in range(N_HEADS):
        blk = qf[:, MLA_HEAD_PAD * hd:MLA_HEAD_PAD * (hd + 1)]
        if rope:
            blk = _rope_tile(blk, cq_ref[...], sq_ref[...])
        q_out[:, MLA_HEAD_PAD * hd:MLA_HEAD_PAD * (hd + 1)] = (blk * MLA_Q_SCALE).astype(BF16)


def _proj_gqa_kernel(*refs, qk_norm, rope, ctx):
    refs = list(refs)
    x_ref, mod_ref, g_ref, win_ref = refs[:4]
    pos = 4
    if qk_norm:
        qn_ref, kn_ref = refs[pos:pos + 2]
        pos += 2
    if rope:
        cos_ref, sin_ref = refs[pos:pos + 2]
        pos += 2
    q_out, k_out, vt_out, gate_out = refs[pos:pos + 4]
    pos += 4
    if ctx:
        kc_out, vc_out = refs[pos:pos + 2]

    nq = N_HEADS * HEAD_DIM
    nk = KV_HEADS * HEAD_DIM
    h = _mod_norm(x_ref[...], g_ref[...], mod_ref[...]).astype(BF16)
    z = jnp.dot(h, win_ref[...], preferred_element_type=F32)
    tm = z.shape[0]
    lo = lax.broadcasted_iota(jnp.int32, (tm, LANES), 1) < HEAD_DIM
    n_q_tiles = nq // LANES
    for c in range((nq + nk) // LANES):
        blk = z[:, LANES * c:LANES * (c + 1)]
        is_q = c < n_q_tiles
        if qk_norm:
            sq = blk * blk
            s_lo = jnp.sum(jnp.where(lo, sq, 0.0), axis=-1, keepdims=True)
            s_hi = jnp.sum(jnp.where(lo, 0.0, sq), axis=-1, keepdims=True)
            inv = jnp.where(lo, lax.rsqrt(s_lo * (1.0 / HEAD_DIM) + EPS),
                            lax.rsqrt(s_hi * (1.0 / HEAD_DIM) + EPS))
            blk = blk * inv * (qn_ref[...] if is_q else kn_ref[...])
        if ctx and not is_q:
            kc_out[:, LANES * (c - n_q_tiles):LANES * (c - n_q_tiles + 1)] = blk
        if rope:
            blk = _rope_tile(blk, cos_ref[...], sin_ref[...])
        if is_q:
            q_out[:, LANES * c:LANES * (c + 1)] = (blk * GQA_Q_SCALE).astype(BF16)
        else:
            k_out[:, LANES * (c - n_q_tiles):LANES * (c - n_q_tiles + 1)] = blk.astype(BF16)
    v = z[:, nq + nk:nq + 2 * nk]
    if ctx:
        vc_out[...] = v
    vt_out[...] = v.T.astype(BF16)
    gate_out[...] = z[:, nq + 2 * nk:].astype(BF16)


def _const_spec(shape):
    return pl.BlockSpec(shape, lambda i: (0,) * len(shape))


def _project(kind, x, mods_l, norm_g, weights, tables, *, batch, seq, ctx):
    n_tok = batch * seq
    tm = min(seq, PROJ_TILE)
    tiles_per_seq = seq // tm
    rope = not ctx
    mod_row = (lambda i: (4, 0, 0)) if ctx else (lambda i: (i // tiles_per_seq, 0, 0))
    in_specs = [
        pl.BlockSpec((tm, D_MODEL), lambda i: (i, 0)),
        pl.BlockSpec((None, 1, 3 * D_MODEL), mod_row),
        _const_spec((1, D_MODEL)),
    ]
    args = [x, mods_l, norm_g.reshape(1, D_MODEL)]
    for w in weights:
        in_specs.append(_const_spec(w.shape))
        args.append(w)
    if rope:
        for t in tables:
            in_specs.append(pl.BlockSpec((tm, LANES), lambda i: (i % tiles_per_seq, 0)))
            args.append(t)
    if kind == 0:
        qw, kw, vw = N_HEADS * MLA_HEAD_PAD, N_HEADS * MLA_HEAD_PAD, N_HEADS * MLA_V
        body = functools.partial(_proj_mla_kernel, rope=rope, ctx=ctx)
        cache_shapes = [(n_tok, MLA_KV_LORA), (n_tok, MLA_ROPE)]
    else:
        qw, kw, vw = N_HEADS * HEAD_DIM, KV_HEADS * HEAD_DIM, KV_HEADS * HEAD_DIM
        body = functools.partial(_proj_gqa_kernel, qk_norm=(kind == 1), rope=rope, ctx=ctx)
        cache_shapes = [(n_tok, kw), (n_tok, vw)]
    out_shape = [
        jax.ShapeDtypeStruct((n_tok, qw), BF16),
        jax.ShapeDtypeStruct((n_tok, kw), BF16),
        jax.ShapeDtypeStruct((batch, vw, seq), BF16),
        jax.ShapeDtypeStruct((n_tok, D_MODEL), BF16),
    ]
    out_specs = [
        pl.BlockSpec((tm, qw), lambda i: (i, 0)),
        pl.BlockSpec((tm, kw), lambda i: (i, 0)),
        pl.BlockSpec((None, vw, tm), lambda i: (i // tiles_per_seq, 0, i % tiles_per_seq)),
        pl.BlockSpec((tm, D_MODEL), lambda i: (i, 0)),
    ]
    if ctx:
        for s in cache_shapes:
            out_shape.append(jax.ShapeDtypeStruct(s, F32))
            out_specs.append(pl.BlockSpec((tm, s[1]), lambda i: (i, 0)))
    return pl.pallas_call(
        body,
        grid=(n_tok // tm,),
        in_specs=in_specs,
        out_specs=out_specs,
        out_shape=out_shape,
        compiler_params=pltpu.CompilerParams(
            dimension_semantics=("arbitrary",), vmem_limit_bytes=VMEM_LIMIT),
        name=f"proj_k{kind}_{'ctx' if ctx else 'lat'}",
    )(*args)


def _mla_cache_kernel(ckv_ref, kpe_ref, wk_ref, rp_ref, wvt_ref, k_out, vt_out):
    k, vt = _mla_expand(ckv_ref[...].astype(BF16), kpe_ref[...].astype(BF16), wk_ref, rp_ref, wvt_ref)
    k_out[...] = k
    vt_out[...] = vt


def _mla_cache(ckv, kpe_pad, wk, rp, wvt):
    b, n, _ = ckv.shape
    return pl.pallas_call(
        _mla_cache_kernel,
        grid=(b,),
        in_specs=[
            pl.BlockSpec((None, n, MLA_KV_LORA), lambda i: (i, 0, 0)),
            pl.BlockSpec((None, n, LANES), lambda i: (i, 0, 0)),
            _const_spec(wk.shape), _const_spec(rp.shape), _const_spec(wvt.shape),
        ],
        out_specs=[
            pl.BlockSpec((None, n, N_HEADS * MLA_HEAD_PAD), lambda i: (i, 0, 0)),
            pl.BlockSpec((None, N_HEADS * MLA_V, n), lambda i: (i, 0, 0)),
        ],
        out_shape=[
            jax.ShapeDtypeStruct((b, n, N_HEADS * MLA_HEAD_PAD), BF16),
            jax.ShapeDtypeStruct((b, N_HEADS * MLA_V, n), BF16),
        ],
        compiler_params=pltpu.CompilerParams(
            dimension_semantics=("arbitrary",), vmem_limit_bytes=VMEM_LIMIT),
        name="mla_cache",
    )(ckv, kpe_pad, wk, rp, wvt)


def _gqa_cache_kernel(k_ref, v_ref, k_out, vt_out):
    k_out[...] = k_ref[...].astype(BF16)
    vt_out[...] = v_ref[...].T.astype(BF16)


def _gqa_cache(k, v):
    b, n, w = k.shape
    return pl.pallas_call(
        _gqa_cache_kernel,
        grid=(b,),
        in_specs=[pl.BlockSpec((None, n, w), lambda i: (i, 0, 0))] * 2,
        out_specs=[
            pl.BlockSpec((None, n, w), lambda i: (i, 0, 0)),
            pl.BlockSpec((None, w, n), lambda i: (i, 0, 0)),
        ],
        out_shape=[
            jax.ShapeDtypeStruct((b, n, w), BF16),
            jax.ShapeDtypeStruct((b, w, n), BF16),
        ],
        compiler_params=pltpu.CompilerParams(dimension_semantics=("arbitrary",)),
        name="gqa_cache",
    )(k, v)


def _attn_kernel(*refs, head_w, group, has_cache, has_sink, band, final, lag_b, lag_c, interleave,
                 subs, own_keys, seq):
    refs = list(refs)
    q_ref, kn_ref, vtn_ref = refs[:3]
    pos = 3
    if has_cache:
        kc_ref, vtc_ref = refs[pos:pos + 2]
        pos += 2
    gate_ref, x_ref, mod_ref, wo_ref = refs[pos:pos + 4]
    pos += 4
    if has_sink:
        sink_ref = refs[pos]
        pos += 1
    if final:
        fg_ref = refs[pos]
        pos += 1
    o_ref, ot_scr, s_scr, p_scr, g_scr, y_scr = refs[pos:pos + 6]
    if band:
        bias_scr = refs[pos + 6]

    tq = TOKEN_TILE
    ch = KEY_CHUNK
    slots = s_scr.shape[0]
    segs = []
    if has_cache:
        segs.append((kc_ref.shape[0],
                     lambda e, r, n, cols: kc_ref[r:r + n, cols],
                     lambda e, rows: vtc_ref[rows, :]))
    if own_keys:
        segs.append((seq,
                     lambda e, r, n, cols: kn_ref[e * seq + r:e * seq + r + n, cols],
                     lambda e, rows: vtn_ref[e, rows, :]))
    else:
        segs.append((seq,
                     lambda e, r, n, cols: kn_ref[r:r + n, cols],
                     lambda e, rows: vtn_ref[rows, :]))
    seg_rows = []
    chunks = []
    off = 0
    for si, (n_seg, _, _) in enumerate(segs):
        seg_rows.append((off, n_seg))
        chunks += [(si, r, off + r) for r in range(0, n_seg, ch)]
        off += n_seg
    if interleave:
        score_chunks = [c + (ch,) for c in chunks]
    else:
        score_chunks = [(si, 0, r0, n) for si, (r0, n) in enumerate(seg_rows)]
    own = len(segs) - 1
    if band:
        kpos = lax.broadcasted_iota(jnp.int32, (seq, tq), 0)
        for e in range(subs):
            tile = pl.program_id(1) * subs + e
            qpos = tile * tq + lax.broadcasted_iota(jnp.int32, (seq, tq), 1)
            bias_scr[e] = jnp.where(jnp.abs(kpos - qpos) <= WINDOW, 0.0, NEG_INF)

    assert lag_b >= 1 and lag_c >= 1 and slots > max(lag_b, lag_c)
    row_max = {}
    row_sum = {}

    def scores_and_max(i):
        e, hd = divmod(i, N_HEADS)
        kh = hd // group
        qh = q_ref[e * tq:(e + 1) * tq, head_w * hd:head_w * (hd + 1)]
        mx = None
        for si, r, g, n in score_chunks:
            kk = segs[si][1](e, r, n, slice(head_w * kh, head_w * (kh + 1)))
            s = lax.dot_general(kk, qh, NT_DIMS, preferred_element_type=F32)
            if band and si == own:
                s = s + bias_scr[e, r:r + n, :]
            s_scr[i % slots, g:g + n, :] = s
            part = jnp.max(s.reshape(n // 8, 8, tq), axis=0)
            mx = part if mx is None else jnp.maximum(mx, part)
            yield
        m = jnp.max(mx, axis=0, keepdims=True)
        if has_sink:
            m = jnp.maximum(m, sink_ref[hd] * LOG2E)
        row_max[i] = m

    def exponentials(i):
        hd = i % N_HEADS
        m = row_max.pop(i)
        mb = jnp.broadcast_to(m, (8, tq))
        tot = None
        for _, _, g in chunks:
            p = jnp.exp2(s_scr[i % slots, g:g + ch, :].reshape(ch // 8, 8, tq) - mb)
            part = jnp.sum(p, axis=0)
            tot = part if tot is None else tot + part
            p_scr[i % slots, g:g + ch, :] = p.reshape(ch, tq).astype(BF16)
            yield
        den = jnp.sum(tot, axis=0, keepdims=True)
        if has_sink:
            den = den + jnp.exp2(sink_ref[hd] * LOG2E - m)
        row_sum[i] = den

    def weighted_values(i):
        e, hd = divmod(i, N_HEADS)
        kh = hd // group
        acc = None
        for si, (_, _, values_t) in enumerate(segs):
            r0, n = seg_rows[si]
            vt = values_t(e, slice(HEAD_DIM * kh, HEAD_DIM * (kh + 1)))
            part = jnp.dot(vt, p_scr[i % slots, r0:r0 + n, :], preferred_element_type=F32)
            acc = part if acc is None else acc + part
            yield
        ot_scr[e, HEAD_DIM * hd:HEAD_DIM * (hd + 1), :] = acc / row_sum.pop(i)

    def finish(e):
        rows = slice(e * tq, (e + 1) * tq)
        for c0 in range(0, D_MODEL, OUT_BLOCK):
            cols = slice(c0, c0 + OUT_BLOCK)
            o = ot_scr[e, cols, :].T
            g_scr[:, cols] = (o * _silu(gate_ref[rows, cols].astype(F32))).astype(BF16)
            yield
        for c0 in range(0, D_MODEL, OUT_BLOCK):
            cols = slice(c0, c0 + OUT_BLOCK)
            y = jnp.dot(g_scr[...], wo_ref[:, cols], preferred_element_type=F32)
            xn = x_ref[rows, cols] + mod_ref[:, 2 * D_MODEL + c0:2 * D_MODEL + c0 + OUT_BLOCK] * y
            if final:
                y_scr[:, cols] = xn
            else:
                o_ref[rows, cols] = xn
            yield
        if final:
            o_ref[rows, :] = _rms(y_scr[...], fg_ref[...])

    n_items = subs * N_HEADS
    for step in range(n_items + lag_b + lag_c + 1):
        live = []
        if step < n_items:
            live.append(scores_and_max(step))
        if 0 <= step - lag_b < n_items:
            live.append(exponentials(step - lag_b))
        done = step - lag_b - lag_c
        if 0 <= done < n_items:
            live.append(weighted_values(done))
        if done >= N_HEADS and done % N_HEADS == 0:
            live.append(finish(done // N_HEADS - 1))
        if not interleave:
            for g in live:
                for _ in g:
                    pass
            live = []
        while live:
            live = [g for g in live if next(g, True) is None]


def _attend(kind, q, k_new, vt_new, cache, gate, x, mods_l, w_out, sink, final_g, *, batch, seq, ctx,
            lags):
    lag_b, lag_c, interleave, subs = lags
    tq = TOKEN_TILE
    n_tok = batch * seq
    band = kind == 2 and not ctx
    head_w = MLA_HEAD_PAD if kind == 0 else HEAD_DIM
    kw = k_new.shape[1]
    vw = vt_new.shape[1]
    own_keys = cache is None
    assert seq == tq if own_keys else seq % (subs * tq) == 0
    grid = (batch // subs, 1) if own_keys else (batch, seq // (subs * tq))
    tok_spec = lambda w: pl.BlockSpec((subs * tq, w), lambda b, t: (b * grid[1] + t, 0))
    if own_keys:
        in_specs = [
            tok_spec(q.shape[1]),
            pl.BlockSpec((subs * seq, kw), lambda b, t: (b, 0)),
            pl.BlockSpec((subs, vw, seq), lambda b, t: (b, 0, 0)),
        ]
    else:
        in_specs = [
            tok_spec(q.shape[1]),
            pl.BlockSpec((seq, kw), lambda b, t: (b, 0)),
            pl.BlockSpec((None, vw, seq), lambda b, t: (b, 0, 0)),
        ]
    args = [q, k_new, vt_new]
    n_keys = seq
    if cache is not None:
        kc, vtc = cache
        n_c = kc.shape[1]
        n_keys += n_c
        in_specs += [
            pl.BlockSpec((None, n_c, kw), lambda b, t: (b, 0, 0)),
            pl.BlockSpec((None, vw, n_c), lambda b, t: (b, 0, 0)),
        ]
        args += [kc, vtc]
    mod_row = (lambda b, t: (4, 0, 0)) if ctx else (lambda b, t: (b, 0, 0))
    in_specs += [
        tok_spec(D_MODEL),
        tok_spec(D_MODEL),
        pl.BlockSpec((None, 1, 3 * D_MODEL), mod_row),
        pl.BlockSpec((D_MODEL, D_MODEL), lambda b, t: (0, 0)),
    ]
    args += [gate, x, mods_l, w_out]
    if sink is not None:
        in_specs.append(pl.BlockSpec(memory_space=pltpu.SMEM))
        args.append(sink)
    if final_g is not None:
        in_specs.append(pl.BlockSpec((1, D_MODEL), lambda b, t: (0, 0)))
        args.append(final_g.reshape(1, D_MODEL))
    body = functools.partial(
        _attn_kernel,
        head_w=head_w,
        group=1 if kind == 0 else GROUPS,
        has_cache=cache is not None,
        has_sink=sink is not None,
        band=band,
        final=final_g is not None,
        lag_b=lag_b,
        lag_c=lag_c,
        interleave=interleave,
        subs=subs,
        own_keys=own_keys,
        seq=seq,
    )
    slots = max(lag_b, lag_c) + 1
    scratch = [pltpu.VMEM((subs, N_HEADS * HEAD_DIM, tq), F32),
               pltpu.VMEM((slots, n_keys, tq), F32),
               pltpu.VMEM((slots, n_keys, tq), BF16),
               pltpu.VMEM((tq, D_MODEL), BF16),
               pltpu.VMEM((tq, D_MODEL), F32)]
    if band:
        scratch.append(pltpu.VMEM((subs, seq, tq), F32))
    return pl.pallas_call(
        body,
        grid=grid,
        in_specs=in_specs,
        out_specs=tok_spec(D_MODEL),
        out_shape=jax.ShapeDtypeStruct((n_tok, D_MODEL), F32),
        scratch_shapes=scratch,
        compiler_params=pltpu.CompilerParams(
            dimension_semantics=("arbitrary", "arbitrary"), vmem_limit_bytes=VMEM_LIMIT),
        name=f"attn_k{kind}_{'ctx' if ctx else 'lat'}",
    )(*args)


def _rope_lane_tables(seq, rot_dim):
    rows = seq // GRID_W
    row = jnp.repeat(jnp.arange(rows), GRID_W).astype(F32)
    col = jnp.tile(jnp.arange(GRID_W), rows).astype(F32)
    nf = rot_dim // 4
    freqs = ROPE_THETA ** (-jnp.arange(nf, dtype=F32) / nf)
    ang = jnp.concatenate([row[:, None] * freqs, col[:, None] * freqs], axis=-1)
    cos, sin = jnp.cos(ang), jnp.sin(ang)
    cos_l = jnp.repeat(cos, 2, axis=-1)
    sin_l = jnp.stack([-sin, sin], axis=-1).reshape(seq, rot_dim)
    return cos_l, sin_l


def _embed_lanes(table, start, fill):
    seq, w = table.shape
    return jnp.concatenate(
        [jnp.full((seq, start), fill, F32), table, jnp.full((seq, LANES - start - w), fill, F32)], axis=1)


def _mla_weights(w_in, q_norm, w_uq, kv_norm, w_ukv):
    c0 = MLA_Q_LORA + MLA_KV_LORA
    win = jnp.concatenate(
        [w_in[:, :c0], w_in[:, c0 + MLA_ROPE:], w_in[:, c0:c0 + MLA_ROPE],
         jnp.zeros((D_MODEL, MLA_IN_PAD - w_in.shape[1]), F32)], axis=1).astype(BF16)
    hq = MLA_NOPE + MLA_ROPE
    wuq = jnp.pad(w_uq.reshape(MLA_Q_LORA, N_HEADS, hq),
                  ((0, 0), (0, 0), (0, MLA_HEAD_PAD - hq))).reshape(MLA_Q_LORA, -1).astype(BF16)
    wkv = w_ukv.reshape(MLA_KV_LORA, N_HEADS, MLA_NOPE + MLA_V)
    wk = jnp.pad(wkv[:, :, :MLA_NOPE],
                 ((0, 0), (0, 0), (0, MLA_HEAD_PAD - MLA_NOPE))).reshape(MLA_KV_LORA, -1).astype(BF16)
    wvt = wkv[:, :, MLA_NOPE:].reshape(MLA_KV_LORA, -1).T.astype(BF16)
    src = lax.broadcasted_iota(jnp.int32, (LANES, N_HEADS * MLA_HEAD_PAD), 0)
    dst = lax.broadcasted_iota(jnp.int32, (LANES, N_HEADS * MLA_HEAD_PAD), 1)
    rp = ((dst % MLA_HEAD_PAD == src + MLA_NOPE) & (src < MLA_ROPE)).astype(BF16)
    return [win, q_norm.reshape(1, -1), wuq, kv_norm.reshape(1, -1), wk, rp, wvt]


def kernel(x_prompt, x_sample, cache_mla_ckv, cache_mla_kpe, cache_gqa_k, cache_gqa_v,
           cache_swa_k, cache_swa_v, c, c_ctx, norm_g, w_ada, b_ada, w_out,
           mla_w_in, mla_q_norm, mla_w_uq, mla_kv_norm, mla_w_ukv,
           gqa_w_in, gqa_q_norm, gqa_k_norm, swa_w_in, swa_sink, final_norm_g):
    bc, sc, _ = x_prompt.shape
    bl, sl, _ = x_sample.shape
    n_past = cache_mla_ckv.shape[2]
    kvw = KV_HEADS * HEAD_DIM

    cc = jnp.concatenate([c, c_ctx[None, :], jnp.zeros((8 - bl - 1, D_MODEL), F32)], axis=0)
    mods = _ada_mods(cc, w_ada, b_ada).reshape(DEPTH, 8, 1, 3 * D_MODEL)

    cos_h, sin_h = _rope_lane_tables(sl, HEAD_DIM)
    gqa_tables = [jnp.tile(cos_h, (1, 2)), jnp.tile(sin_h, (1, 2))]
    cos_r, sin_r = _rope_lane_tables(sl, MLA_ROPE)
    mla_tables = [_embed_lanes(cos_r, MLA_NOPE, 1.0), _embed_lanes(sin_r, MLA_NOPE, 0.0),
                  _embed_lanes(cos_r, 0, 1.0), _embed_lanes(sin_r, 0, 0.0)]

    xc = x_prompt.reshape(bc * sc, D_MODEL)
    xl = x_sample.reshape(bl * sl, D_MODEL)
    new_caches = {0: ([], []), 1: ([], []), 2: ([], [])}
    for i in range(DEPTH):
        kind, j = i % 3, i // 3
        mods_l = mods[i]
        wo = w_out[i].astype(BF16)
        final_g = final_norm_g if i == DEPTH - 1 else None
        sink = None
        if kind == 0:
            weights = _mla_weights(mla_w_in[j], mla_q_norm[j], mla_w_uq[j], mla_kv_norm[j], mla_w_ukv[j])
            tables = mla_tables
            kpe_pad = jnp.pad(cache_mla_kpe[:, j], ((0, 0), (0, 0), (0, LANES - MLA_ROPE)))
            cache = _mla_cache(cache_mla_ckv[:, j], kpe_pad, weights[4], weights[5], weights[6])
        else:
            tables = gqa_tables
            if kind == 1:
                weights = [gqa_w_in[j].astype(BF16),
                           jnp.tile(gqa_q_norm[j], 2).reshape(1, LANES),
                           jnp.tile(gqa_k_norm[j], 2).reshape(1, LANES)]
                cache = _gqa_cache(cache_gqa_k[:, j].reshape(bl, n_past, kvw),
                                   cache_gqa_v[:, j].reshape(bl, n_past, kvw))
            else:
                weights = [swa_w_in[j].astype(BF16)]
                sink = swa_sink[j]
                cache = _gqa_cache(cache_swa_k[:, j].reshape(bl, n_past, kvw),
                                   cache_swa_v[:, j].reshape(bl, n_past, kvw))

        q, k, vt, gate, c_a, c_b = _project(kind, xc, mods_l, norm_g[i], weights, tables,
                                            batch=bc, seq=sc, ctx=True)
        new_caches[kind][0].append(c_a)
        new_caches[kind][1].append(c_b)
        xc = _attend(kind, q, k, vt, None, gate, xc, mods_l, wo, sink, final_g,
                     batch=bc, seq=sc, ctx=True, lags=CTX_STAGE_LAGS)

        q, k, vt, gate = _project(kind, xl, mods_l, norm_g[i], weights, tables,
                                  batch=bl, seq=sl, ctx=False)
        xl = _attend(kind, q, k, vt, cache, gate, xl, mods_l, wo, sink, final_g,
                     batch=bl, seq=sl, ctx=False, lags=LAT_STAGE_LAGS)

    def stack(parts, tail):
        return jnp.stack([p.reshape((bc, sc) + tail) for p in parts], axis=1)

    return (xc.reshape(bc, sc, D_MODEL), xl.reshape(bl, sl, D_MODEL),
            stack(new_caches[0][0], (MLA_KV_LORA,)), stack(new_caches[0][1], (MLA_ROPE,)),
            stack(new_caches[1][0], (KV_HEADS, HEAD_DIM)), stack(new_caches[1][1], (KV_HEADS, HEAD_DIM)),
            stack(new_caches[2][0], (KV_HEADS, HEAD_DIM)), stack(new_caches[2][1], (KV_HEADS, HEAD_DIM)))
```

```python
import functools

import jax
import jax.numpy as jnp
from jax import lax
from jax.experimental import pallas as pl
from jax.experimental.pallas import tpu as pltpu

F32 = jnp.float32
BF16 = jnp.bfloat16

D_MODEL = 1024
DEPTH = 4
GRID_W = 64
N_HEADS = 16
HEAD_DIM = 64
KV_HEADS = 4
GROUPS = N_HEADS // KV_HEADS
MLA_Q_LORA = 384
MLA_KV_LORA = 256
MLA_NOPE = 64
MLA_ROPE = 32
MLA_V = 64
WINDOW = 128
ROPE_THETA = 10000.0
EPS = 1e-6
NEG_INF = -1e30

LOG2E = 1.4426950408889634
GQA_Q_SCALE = HEAD_DIM ** -0.5 * LOG2E
MLA_Q_SCALE = (MLA_NOPE + MLA_ROPE) ** -0.5 * LOG2E

LANES = 128
TOKEN_TILE = 256
PROJ_TILE = 512
KEY_CHUNK = 256
OUT_BLOCK = 256
CTX_STAGE_LAGS = (3, 3, True, 2)
LAT_STAGE_LAGS = (2, 2, False, 2)
MLA_HEAD_PAD = 128
V_ROWS = HEAD_DIM + 16
MLA_IN_PAD = 1792
VMEM_LIMIT = 56 * 1024 * 1024
NT_DIMS = (((1,), (1,)), ((), ()))


def _silu(v):
    return v * jax.nn.sigmoid(v)


def _rms(v, g):
    return v * lax.rsqrt(jnp.mean(v * v, axis=-1, keepdims=True) + EPS) * g


def _mod_norm(x, g, mod):
    shift = mod[:, :D_MODEL]
    scale = mod[:, D_MODEL:2 * D_MODEL]
    return _rms(x, g) * (1.0 + scale) + shift


def _rope_tile(v, cos, sin_signed):
    lane = lax.broadcasted_iota(jnp.int32, v.shape, 1)
    nxt = pltpu.roll(v, LANES - 1, axis=1)
    prv = pltpu.roll(v, 1, axis=1)
    swapped = jnp.where((lane & 1) == 0, nxt, prv)
    return v * cos + swapped * sin_signed


def _ada_kernel(c_ref, w_ref, b_ref, o_ref):
    s = _silu(c_ref[...]).astype(BF16)
    o_ref[...] = jnp.dot(s, w_ref[...].astype(BF16), preferred_element_type=F32) + b_ref[...]


def _ada_mods(cc, w_ada, b_ada):
    tn = D_MODEL
    return pl.pallas_call(
        _ada_kernel,
        grid=(DEPTH, 3 * D_MODEL // tn),
        in_specs=[
            pl.BlockSpec((8, D_MODEL), lambda i, n: (0, 0)),
            pl.BlockSpec((None, D_MODEL, tn), lambda i, n: (i, 0, n)),
            pl.BlockSpec((None, 1, tn), lambda i, n: (i, 0, n)),
        ],
        out_specs=pl.BlockSpec((None, 8, tn), lambda i, n: (i, 0, n)),
        out_shape=jax.ShapeDtypeStruct((DEPTH, 8, 3 * D_MODEL), F32),
        compiler_params=pltpu.CompilerParams(vmem_limit_bytes=VMEM_LIMIT),
        name="ada_mods",
    )(cc, w_ada, b_ada.reshape(DEPTH, 1, 3 * D_MODEL))


def _store_values_t(vt_out, vt, n_heads):
    ones = jnp.ones((V_ROWS - HEAD_DIM, vt.shape[1]), BF16)
    for h in range(n_heads):
        vt_out[V_ROWS * h:V_ROWS * h + HEAD_DIM, :] = vt[HEAD_DIM * h:HEAD_DIM * (h + 1), :].astype(BF16)
        vt_out[V_ROWS * h + HEAD_DIM:V_ROWS * (h + 1), :] = ones


def _mla_expand(ckv_bf, kpe_bf, wk_ref, rp_ref, wvt_ref):
    k = (jnp.dot(ckv_bf, wk_ref[...], preferred_element_type=F32)
         + jnp.dot(kpe_bf, rp_ref[...], preferred_element_type=F32))
    vt = lax.dot_general(wvt_ref[...], ckv_bf, NT_DIMS, preferred_element_type=F32)
    return k.astype(BF16), vt


def _proj_mla_kernel(*refs, rope, ctx):
    refs = list(refs)
    x_ref, mod_ref, g_ref, win_ref, qn_ref, wuq_ref, kvn_ref, wk_ref, rp_ref, wvt_ref = refs[:10]
    pos = 10
    if rope:
        cq_ref, sq_ref, ck_ref, sk_ref = refs[pos:pos + 4]
        pos += 4
    q_out, k_out, vt_out, gate_out = refs[pos:pos + 4]
    pos += 4
    if ctx:
        ckv_out, kpe_out = refs[pos:pos + 2]

    h = _mod_norm(x_ref[...], g_ref[...], mod_ref[...]).astype(BF16)
    z = jnp.dot(h, win_ref[...], preferred_element_type=F32)
    cq = z[:, :MLA_Q_LORA]
    ckv = z[:, MLA_Q_LORA:MLA_Q_LORA + MLA_KV_LORA]
    gate = z[:, 640:640 + D_MODEL]
    kpe = z[:, 640 + D_MODEL:]
    gate_out[...] = _silu(gate).astype(BF16)

    qf = jnp.dot(_rms(cq, qn_ref[...]).astype(BF16), wuq_ref[...], preferred_element_type=F32)
    ckvn = _rms(ckv, kvn_ref[...])
    if ctx:
        ckv_out[...] = ckvn
        kpe_out[...] = kpe[:, :MLA_ROPE]
    if rope:
        kpe = _rope_tile(kpe, ck_ref[...], sk_ref[...])
    k, vt = _mla_expand(ckvn.astype(BF16), kpe.astype(BF16), wk_ref, rp_ref, wvt_ref)
    k_out[...] = k
    _store_values_t(vt_out, vt, N_HEADS)
    for hd in range(N_HEADS):
        blk = qf[:, MLA_HEAD_PAD * hd:MLA_HEAD_PAD * (hd + 1)]
        if rope:
            blk = _rope_tile(blk, cq_ref[...], sq_ref[...])
        q_out[:, MLA_HEAD_PAD * hd:MLA_HEAD_PAD * (hd + 1)] = (blk * MLA_Q_SCALE).astype(BF16)


def _proj_gqa_kernel(*refs, qk_norm, rope, ctx):
    refs = list(refs)
    x_ref, mod_ref, g_ref, win_ref = refs[:4]
    pos = 4
    if qk_norm:
        qn_ref, kn_ref = refs[pos:pos + 2]
        pos += 2
    if rope:
        cos_ref, sin_ref = refs[pos:pos + 2]
        pos += 2
    q_out, k_out, vt_out, gate_out = refs[pos:pos + 4]
    pos += 4
    if ctx:
        kc_out, vc_out = refs[pos:pos + 2]

    nq = N_HEADS * HEAD_DIM
    nk = KV_HEADS * HEAD_DIM
    h = _mod_norm(x_ref[...], g_ref[...], mod_ref[...]).astype(BF16)
    z = jnp.dot(h, win_ref[...], preferred_element_type=F32)
    tm = z.shape[0]
    lo = lax.broadcasted_iota(jnp.int32, (tm, LANES), 1) < HEAD_DIM
    n_q_tiles = nq // LANES
    for c in range((nq + nk) // LANES):
        blk = z[:, LANES * c:LANES * (c + 1)]
        is_q = c < n_q_tiles
        if qk_norm:
            sq = blk * blk
            s_lo = jnp.sum(jnp.where(lo, sq, 0.0), axis=-1, keepdims=True)
            s_hi = jnp.sum(jnp.where(lo, 0.0, sq), axis=-1, keepdims=True)
            inv = jnp.where(lo, lax.rsqrt(s_lo * (1.0 / HEAD_DIM) + EPS),
                            lax.rsqrt(s_hi * (1.0 / HEAD_DIM) + EPS))
            blk = blk * inv * (qn_ref[...] if is_q else kn_ref[...])
        if ctx and not is_q:
            kc_out[:, LANES * (c - n_q_tiles):LANES * (c - n_q_tiles + 1)] = blk
        if rope:
            blk = _rope_tile(blk, cos_ref[...], sin_ref[...])
        if is_q:
            q_out[:, LANES * c:LANES * (c + 1)] = (blk * GQA_Q_SCALE).astype(BF16)
        else:
            k_out[:, LANES * (c - n_q_tiles):LANES * (c - n_q_tiles + 1)] = blk.astype(BF16)
    v = z[:, nq + nk:nq + 2 * nk]
    if ctx:
        vc_out[...] = v
    _store_values_t(vt_out, v.T, KV_HEADS)
    gate_out[...] = _silu(z[:, nq + 2 * nk:]).astype(BF16)


def _const_spec(shape):
    return pl.BlockSpec(shape, lambda i: (0,) * len(shape))


def _project(kind, x, mods_l, norm_g, weights, tables, *, batch, seq, ctx):
    n_tok = batch * seq
    tm = min(seq, PROJ_TILE)
    tiles_per_seq = seq // tm
    rope = not ctx
    mod_row = (lambda i: (4, 0, 0)) if ctx else (lambda i: (i // tiles_per_seq, 0, 0))
    in_specs = [
        pl.BlockSpec((tm, D_MODEL), lambda i: (i, 0)),
        pl.BlockSpec((None, 1, 3 * D_MODEL), mod_row),
        _const_spec((1, D_MODEL)),
    ]
    args = [x, mods_l, norm_g.reshape(1, D_MODEL)]
    for w in weights:
        in_specs.append(_const_spec(w.shape))
        args.append(w)
    if rope:
        for t in tables:
            in_specs.append(pl.BlockSpec((tm, LANES), lambda i: (i % tiles_per_seq, 0)))
            args.append(t)
    if kind == 0:
        qw, kw, vw = N_HEADS * MLA_HEAD_PAD, N_HEADS * MLA_HEAD_PAD, N_HEADS * V_ROWS
        body = functools.partial(_proj_mla_kernel, rope=rope, ctx=ctx)
        cache_shapes = [(n_tok, MLA_KV_LORA), (n_tok, MLA_ROPE)]
    else:
        qw, kw, vw = N_HEADS * HEAD_DIM, KV_HEADS * HEAD_DIM, KV_HEADS * V_ROWS
        body = functools.partial(_proj_gqa_kernel, qk_norm=(kind == 1), rope=rope, ctx=ctx)
        cache_shapes = [(n_tok, kw), (n_tok, kw)]
    out_shape = [
        jax.ShapeDtypeStruct((n_tok, qw), BF16),
        jax.ShapeDtypeStruct((n_tok, kw), BF16),
        jax.ShapeDtypeStruct((batch, vw, seq), BF16),
        jax.ShapeDtypeStruct((n_tok, D_MODEL), BF16),
    ]
    out_specs = [
        pl.BlockSpec((tm, qw), lambda i: (i, 0)),
        pl.BlockSpec((tm, kw), lambda i: (i, 0)),
        pl.BlockSpec((None, vw, tm), lambda i: (i // tiles_per_seq, 0, i % tiles_per_seq)),
        pl.BlockSpec((tm, D_MODEL), lambda i: (i, 0)),
    ]
    if ctx:
        for s in cache_shapes:
            out_shape.append(jax.ShapeDtypeStruct(s, F32))
            out_specs.append(pl.BlockSpec((tm, s[1]), lambda i: (i, 0)))
    return pl.pallas_call(
        body,
        grid=(n_tok // tm,),
        in_specs=in_specs,
        out_specs=out_specs,
        out_shape=out_shape,
        compiler_params=pltpu.CompilerParams(
            dimension_semantics=("arbitrary",), vmem_limit_bytes=VMEM_LIMIT),
        name=f"proj_k{kind}_{'ctx' if ctx else 'lat'}",
    )(*args)


def _mla_cache_kernel(ckv_ref, kpe_ref, wk_ref, rp_ref, wvt_ref, k_out, vt_out):
    k, vt = _mla_expand(ckv_ref[...].astype(BF16), kpe_ref[...].astype(BF16), wk_ref, rp_ref, wvt_ref)
    k_out[...] = k
    _store_values_t(vt_out, vt, N_HEADS)


def _mla_cache(ckv, kpe_pad, wk, rp, wvt):
    b, n, _ = ckv.shape
    return pl.pallas_call(
        _mla_cache_kernel,
        grid=(b,),
        in_specs=[
            pl.BlockSpec((None, n, MLA_KV_LORA), lambda i: (i, 0, 0)),
            pl.BlockSpec((None, n, LANES), lambda i: (i, 0, 0)),
            _const_spec(wk.shape), _const_spec(rp.shape), _const_spec(wvt.shape),
        ],
        out_specs=[
            pl.BlockSpec((None, n, N_HEADS * MLA_HEAD_PAD), lambda i: (i, 0, 0)),
            pl.BlockSpec((None, N_HEADS * V_ROWS, n), lambda i: (i, 0, 0)),
        ],
        out_shape=[
            jax.ShapeDtypeStruct((b, n, N_HEADS * MLA_HEAD_PAD), BF16),
            jax.ShapeDtypeStruct((b, N_HEADS * V_ROWS, n), BF16),
        ],
        compiler_params=pltpu.CompilerParams(
            dimension_semantics=("arbitrary",), vmem_limit_bytes=VMEM_LIMIT),
        name="mla_cache",
    )(ckv, kpe_pad, wk, rp, wvt)


def _gqa_cache_kernel(k_ref, v_ref, k_out, vt_out):
    k_out[...] = k_ref[...].astype(BF16)
    _store_values_t(vt_out, v_ref[...].T, KV_HEADS)


def _gqa_cache(k, v):
    b, n, w = k.shape
    return pl.pallas_call(
        _gqa_cache_kernel,
        grid=(b,),
        in_specs=[pl.BlockSpec((None, n, w), lambda i: (i, 0, 0))] * 2,
        out_specs=[
            pl.BlockSpec((None, n, w), lambda i: (i, 0, 0)),
            pl.BlockSpec((None, KV_HEADS * V_ROWS, n), lambda i: (i, 0, 0)),
        ],
        out_shape=[
            jax.ShapeDtypeStruct((b, n, w), BF16),
            jax.ShapeDtypeStruct((b, KV_HEADS * V_ROWS, n), BF16),
        ],
        compiler_params=pltpu.CompilerParams(dimension_semantics=("arbitrary",)),
        name="gqa_cache",
    )(k, v)


def _attn_kernel(*refs, head_w, group, has_cache, has_sink, band, final, lag_b, lag_c, interleave,
                 subs, own_keys, seq):
    refs = list(refs)
    q_ref, kn_ref, vtn_ref = refs[:3]
    pos = 3
    if has_cache:
        kc_ref, vtc_ref = refs[pos:pos + 2]
        pos += 2
    gate_ref, x_ref, mod_ref, wo_ref = refs[pos:pos + 4]
    pos += 4
    if has_sink:
        sink_ref = refs[pos]
        pos += 1
    if final:
        fg_ref = refs[pos]
        pos += 1
    o_ref, ot_scr, s_scr, p_scr, g_scr, y_scr = refs[pos:pos + 6]
    if band:
        bias_scr = refs[pos + 6]

    tq = TOKEN_TILE
    ch = KEY_CHUNK
    slots = s_scr.shape[0]
    segs = []
    if has_cache:
        segs.append((kc_ref.shape[0],
                     lambda e, r, n, cols: kc_ref[r:r + n, cols],
                     lambda e, rows: vtc_ref[rows, :]))
    if own_keys:
        segs.append((seq,
                     lambda e, r, n, cols: kn_ref[e * seq + r:e * seq + r + n, cols],
                     lambda e, rows: vtn_ref[e, rows, :]))
    else:
        segs.append((seq,
                     lambda e, r, n, cols: kn_ref[r:r + n, cols],
                     lambda e, rows: vtn_ref[rows, :]))
    seg_rows = []
    chunks = []
    off = 0
    for si, (n_seg, _, _) in enumerate(segs):
        seg_rows.append((off, n_seg))
        chunks += [(si, r, off + r) for r in range(0, n_seg, ch)]
        off += n_seg
    if interleave:
        score_chunks = [c + (ch,) for c in chunks]
    else:
        score_chunks = [(si, 0, r0, n) for si, (r0, n) in enumerate(seg_rows)]
    own = len(segs) - 1
    if band:
        kpos = lax.broadcasted_iota(jnp.int32, (seq, tq), 0)
        for e in range(subs):
            tile = pl.program_id(1) * subs + e
            qpos = tile * tq + lax.broadcasted_iota(jnp.int32, (seq, tq), 1)
            bias_scr[e] = jnp.where(jnp.abs(kpos - qpos) <= WINDOW, 0.0, NEG_INF)

    assert lag_b >= 1 and lag_c >= 1 and slots > max(lag_b, lag_c)
    row_max = {}
    sink_term = {}

    def scores_and_max(i):
        e, hd = divmod(i, N_HEADS)
        kh = hd // group
        qh = q_ref[e * tq:(e + 1) * tq, head_w * hd:head_w * (hd + 1)]
        mx = None
        for si, r, g, n in score_chunks:
            kk = segs[si][1](e, r, n, slice(head_w * kh, head_w * (kh + 1)))
            s = lax.dot_general(kk, qh, NT_DIMS, preferred_element_type=F32)
            if band and si == own:
                s = s + bias_scr[e, r:r + n, :]
            s_scr[i % slots, g:g + n, :] = s
            part = jnp.max(s.reshape(n // 8, 8, tq), axis=0)
            mx = part if mx is None else jnp.maximum(mx, part)
            yield
        m = jnp.max(mx, axis=0, keepdims=True)
        if has_sink:
            m = jnp.maximum(m, sink_ref[hd] * LOG2E)
        row_max[i] = m

    def exponentials(i):
        hd = i % N_HEADS
        m = row_max.pop(i)
        mb = jnp.broadcast_to(m, (8, tq))
        if has_sink:
            sink_term[i] = jnp.exp2(sink_ref[hd] * LOG2E - m)
        for _, _, g in chunks:
            p = jnp.exp2(s_scr[i % slots, g:g + ch, :].reshape(ch // 8, 8, tq) - mb)
            p_scr[i % slots, g:g + ch, :] = p.reshape(ch, tq).astype(BF16)
            yield

    def weighted_values(i):
        e, hd = divmod(i, N_HEADS)
        kh = hd // group
        acc = None
        for si, (_, _, values_t) in enumerate(segs):
            r0, n = seg_rows[si]
            vt = values_t(e, slice(V_ROWS * kh, V_ROWS * (kh + 1)))
            part = jnp.dot(vt, p_scr[i % slots, r0:r0 + n, :], preferred_element_type=F32)
            acc = part if acc is None else acc + part
            yield
        den = acc[HEAD_DIM:HEAD_DIM + 1, :]
        if has_sink:
            den = den + sink_term.pop(i)
        ot_scr[e, HEAD_DIM * hd:HEAD_DIM * (hd + 1), :] = acc[:HEAD_DIM, :] / den

    def finish(e):
        rows = slice(e * tq, (e + 1) * tq)
        for c0 in range(0, D_MODEL, OUT_BLOCK):
            cols = slice(c0, c0 + OUT_BLOCK)
            o = ot_scr[e, cols, :].T
            g_scr[:, cols] = (o * gate_ref[rows, cols].astype(F32)).astype(BF16)
            yield
        for c0 in range(0, D_MODEL, OUT_BLOCK):
            cols = slice(c0, c0 + OUT_BLOCK)
            y = jnp.dot(g_scr[...], wo_ref[:, cols], preferred_element_type=F32)
            xn = x_ref[rows, cols] + mod_ref[:, 2 * D_MODEL + c0:2 * D_MODEL + c0 + OUT_BLOCK] * y
            if final:
                y_scr[:, cols] = xn
            else:
                o_ref[rows, cols] = xn
            yield
        if final:
            o_ref[rows, :] = _rms(y_scr[...], fg_ref[...])

    n_items = subs * N_HEADS
    for step in range(n_items + lag_b + lag_c + 1):
        live = []
        if step < n_items:
            live.append(scores_and_max(step))
        if 0 <= step - lag_b < n_items:
            live.append(exponentials(step - lag_b))
        done = step - lag_b - lag_c
        if 0 <= done < n_items:
            live.append(weighted_values(done))
        if done >= N_HEADS and done % N_HEADS == 0:
            live.append(finish(done // N_HEADS - 1))
        if not interleave:
            for g in live:
                for _ in g:
                    pass
            live = []
        while live:
            live = [g for g in live if next(g, True) is None]


def _attend(kind, q, k_new, vt_new, cache, gate, x, mods_l, w_out, sink, final_g, *, batch, seq, ctx,
            lags):
    lag_b, lag_c, interleave, subs = lags
    tq = TOKEN_TILE
    n_tok = batch * seq
    band = kind == 2 and not ctx
    head_w = MLA_HEAD_PAD if kind == 0 else HEAD_DIM
    kw = k_new.shape[1]
    vw = vt_new.shape[1]
    own_keys = cache is None
    assert seq == tq if own_keys else seq % (subs * tq) == 0
    grid = (batch // subs, 1) if own_keys else (batch, seq // (subs * tq))
    tok_spec = lambda w: pl.BlockSpec((subs * tq, w), lambda b, t: (b * grid[1] + t, 0))
    if own_keys:
        in_specs = [
            tok_spec(q.shape[1]),
            pl.BlockSpec((subs * seq, kw), lambda b, t: (b, 0)),
            pl.BlockSpec((subs, vw, seq), lambda b, t: (b, 0, 0)),
        ]
    else:
        in_specs = [
            tok_spec(q.shape[1]),
            pl.BlockSpec((seq, kw), lambda b, t: (b, 0)),
            pl.BlockSpec((None, vw, seq), lambda b, t: (b, 0, 0)),
        ]
    args = [q, k_new, vt_new]
    n_keys = seq
    if cache is not None:
        kc, vtc = cache
        n_c = kc.shape[1]
        n_keys += n_c
        in_specs += [
            pl.BlockSpec((None, n_c, kw), lambda b, t: (b, 0, 0)),
            pl.BlockSpec((None, vw, n_c), lambda b, t: (b, 0, 0)),
        ]
        args += [kc, vtc]
    mod_row = (lambda b, t: (4, 0, 0)) if ctx else (lambda b, t: (b, 0, 0))
    in_specs += [
        tok_spec(D_MODEL),
        tok_spec(D_MODEL),
        pl.BlockSpec((None, 1, 3 * D_MODEL), mod_row),
        pl.BlockSpec((D_MODEL, D_MODEL), lambda b, t: (0, 0)),
    ]
    args += [gate, x, mods_l, w_out]
    if sink is not None:
        in_specs.append(pl.BlockSpec(memory_space=pltpu.SMEM))
        args.append(sink)
    if final_g is not None:
        in_specs.append(pl.BlockSpec((1, D_MODEL), lambda b, t: (0, 0)))
        args.append(final_g.reshape(1, D_MODEL))
    body = functools.partial(
        _attn_kernel,
        head_w=head_w,
        group=1 if kind == 0 else GROUPS,
        has_cache=cache is not None,
        has_sink=sink is not None,
        band=band,
        final=final_g is not None,
        lag_b=lag_b,
        lag_c=lag_c,
        interleave=interleave,
        subs=subs,
        own_keys=own_keys,
        seq=seq,
    )
    slots = max(lag_b, lag_c) + 1
    scratch = [pltpu.VMEM((subs, N_HEADS * HEAD_DIM, tq), F32),
               pltpu.VMEM((slots, n_keys, tq), F32),
               pltpu.VMEM((slots, n_keys, tq), BF16),
               pltpu.VMEM((tq, D_MODEL), BF16),
               pltpu.VMEM((tq, D_MODEL), F32)]
    if band:
        scratch.append(pltpu.VMEM((subs, seq, tq), F32))
    return pl.pallas_call(
        body,
        grid=grid,
        in_specs=in_specs,
        out_specs=tok_spec(D_MODEL),
        out_shape=jax.ShapeDtypeStruct((n_tok, D_MODEL), F32),
        scratch_shapes=scratch,
        compiler_params=pltpu.CompilerParams(
            dimension_semantics=("arbitrary", "arbitrary"), vmem_limit_bytes=VMEM_LIMIT),
        name=f"attn_k{kind}_{'ctx' if ctx else 'lat'}",
    )(*args)


def _rope_lane_tables(seq, rot_dim):
    rows = seq // GRID_W
    row = jnp.repeat(jnp.arange(rows), GRID_W).astype(F32)
    col = jnp.tile(jnp.arange(GRID_W), rows).astype(F32)
    nf = rot_dim // 4
    freqs = ROPE_THETA ** (-jnp.arange(nf, dtype=F32) / nf)
    ang = jnp.concatenate([row[:, None] * freqs, col[:, None] * freqs], axis=-1)
    cos, sin = jnp.cos(ang), jnp.sin(ang)
    cos_l = jnp.repeat(cos, 2, axis=-1)
    sin_l = jnp.stack([-sin, sin], axis=-1).reshape(seq, rot_dim)
    return cos_l, sin_l


def _embed_lanes(table, start, fill):
    seq, w = table.shape
    return jnp.concatenate(
        [jnp.full((seq, start), fill, F32), table, jnp.full((seq, LANES - start - w), fill, F32)], axis=1)


def _mla_weights(w_in, q_norm, w_uq, kv_norm, w_ukv):
    c0 = MLA_Q_LORA + MLA_KV_LORA
    win = jnp.concatenate(
        [w_in[:, :c0], w_in[:, c0 + MLA_ROPE:], w_in[:, c0:c0 + MLA_ROPE],
         jnp.zeros((D_MODEL, MLA_IN_PAD - w_in.shape[1]), F32)], axis=1).astype(BF16)
    hq = MLA_NOPE + MLA_ROPE
    wuq = jnp.pad(w_uq.reshape(MLA_Q_LORA, N_HEADS, hq),
                  ((0, 0), (0, 0), (0, MLA_HEAD_PAD - hq))).reshape(MLA_Q_LORA, -1).astype(BF16)
    wkv = w_ukv.reshape(MLA_KV_LORA, N_HEADS, MLA_NOPE + MLA_V)
    wk = jnp.pad(wkv[:, :, :MLA_NOPE],
                 ((0, 0), (0, 0), (0, MLA_HEAD_PAD - MLA_NOPE))).reshape(MLA_KV_LORA, -1).astype(BF16)
    wvt = wkv[:, :, MLA_NOPE:].reshape(MLA_KV_LORA, -1).T.astype(BF16)
    src = lax.broadcasted_iota(jnp.int32, (LANES, N_HEADS * MLA_HEAD_PAD), 0)
    dst = lax.broadcasted_iota(jnp.int32, (LANES, N_HEADS * MLA_HEAD_PAD), 1)
    rp = ((dst % MLA_HEAD_PAD == src + MLA_NOPE) & (src < MLA_ROPE)).astype(BF16)
    return [win, q_norm.reshape(1, -1), wuq, kv_norm.reshape(1, -1), wk, rp, wvt]


def kernel(x_prompt, x_sample, cache_mla_ckv, cache_mla_kpe, cache_gqa_k, cache_gqa_v,
           cache_swa_k, cache_swa_v, c, c_ctx, norm_g, w_ada, b_ada, w_out,
           mla_w_in, mla_q_norm, mla_w_uq, mla_kv_norm, mla_w_ukv,
           gqa_w_in, gqa_q_norm, gqa_k_norm, swa_w_in, swa_sink, final_norm_g):
    bc, sc, _ = x_prompt.shape
    bl, sl, _ = x_sample.shape
    n_past = cache_mla_ckv.shape[2]
    kvw = KV_HEADS * HEAD_DIM

    cc = jnp.concatenate([c, c_ctx[None, :], jnp.zeros((8 - bl - 1, D_MODEL), F32)], axis=0)
    mods = _ada_mods(cc, w_ada, b_ada).reshape(DEPTH, 8, 1, 3 * D_MODEL)

    cos_h, sin_h = _rope_lane_tables(sl, HEAD_DIM)
    gqa_tables = [jnp.tile(cos_h, (1, 2)), jnp.tile(sin_h, (1, 2))]
    cos_r, sin_r = _rope_lane_tables(sl, MLA_ROPE)
    mla_tables = [_embed_lanes(cos_r, MLA_NOPE, 1.0), _embed_lanes(sin_r, MLA_NOPE, 0.0),
                  _embed_lanes(cos_r, 0, 1.0), _embed_lanes(sin_r, 0, 0.0)]

    xc = x_prompt.reshape(bc * sc, D_MODEL)
    xl = x_sample.reshape(bl * sl, D_MODEL)
    new_caches = {0: ([], []), 1: ([], []), 2: ([], [])}
    for i in range(DEPTH):
        kind, j = i % 3, i // 3
        mods_l = mods[i]
        wo = w_out[i].astype(BF16)
        final_g = final_norm_g if i == DEPTH - 1 else None
        sink = None
        if kind == 0:
            weights = _mla_weights(mla_w_in[j], mla_q_norm[j], mla_w_uq[j], mla_kv_norm[j], mla_w_ukv[j])
            tables = mla_tables
            kpe_pad = jnp.pad(cache_mla_kpe[:, j], ((0, 0), (0, 0), (0, LANES - MLA_ROPE)))
            cache = _mla_cache(cache_mla_ckv[:, j], kpe_pad, weights[4], weights[5], weights[6])
        else:
            tables = gqa_tables
            if kind == 1:
                weights = [gqa_w_in[j].astype(BF16),
                           jnp.tile(gqa_q_norm[j], 2).reshape(1, LANES),
                           jnp.tile(gqa_k_norm[j], 2).reshape(1, LANES)]
                cache = _gqa_cache(cache_gqa_k[:, j].reshape(bl, n_past, kvw),
                                   cache_gqa_v[:, j].reshape(bl, n_past, kvw))
            else:
                weights = [swa_w_in[j].astype(BF16)]
                sink = swa_sink[j]
                cache = _gqa_cache(cache_swa_k[:, j].reshape(bl, n_past, kvw),
                                   cache_swa_v[:, j].reshape(bl, n_past, kvw))

        q, k, vt, gate, c_a, c_b = _project(kind, xc, mods_l, norm_g[i], weights, tables,
                                            batch=bc, seq=sc, ctx=True)
        new_caches[kind][0].append(c_a)
        new_caches[kind][1].append(c_b)
        xc = _attend(kind, q, k, vt, None, gate, xc, mods_l, wo, sink, final_g,
                     batch=bc, seq=sc, ctx=True, lags=CTX_STAGE_LAGS)

        q, k, vt, gate = _project(kind, xl, mods_l, norm_g[i], weights, tables,
                                  batch=bl, seq=sl, ctx=False)
        xl = _attend(kind, q, k, vt, cache, gate, xl, mods_l, wo, sink, final_g,
                     batch=bl, seq=sl, ctx=False, lags=LAT_STAGE_LAGS)

    def stack(parts, tail):
        return jnp.stack([p.reshape((bc, sc) + tail) for p in parts], axis=1)

    return (xc.reshape(bc, sc, D_MODEL), xl.reshape(bl, sl, D_MODEL),
            stack(new_caches[0][0], (MLA_KV_LORA,)), stack(new_caches[0][1], (MLA_ROPE,)),
            stack(new_caches[1][0], (KV_HEADS, HEAD_DIM)), stack(new_caches[1][1], (KV_HEADS, HEAD_DIM)),
            stack(new_caches[2][0], (KV_HEADS, HEAD_DIM)), stack(new_caches[2][1], (KV_HEADS, HEAD_DIM)))
```

```python
import functools

import jax
import jax.numpy as jnp
from jax import lax
from jax.experimental import pallas as pl
from jax.experimental.pallas import tpu as pltpu

F32 = jnp.float32
BF16 = jnp.bfloat16

D_MODEL = 1024
DEPTH = 4
GRID_W = 64
N_HEADS = 16
HEAD_DIM = 64
KV_HEADS = 4
GROUPS = N_HEADS // KV_HEADS
MLA_Q_LORA = 384
MLA_KV_LORA = 256
MLA_NOPE = 64
MLA_ROPE = 32
MLA_V = 64
WINDOW = 128
ROPE_THETA = 10000.0
EPS = 1e-6
NEG_INF = -1e30

LOG2E = 1.4426950408889634
GQA_Q_SCALE = HEAD_DIM ** -0.5 * LOG2E
MLA_Q_SCALE = (MLA_NOPE + MLA_ROPE) ** -0.5 * LOG2E

LANES = 128
TOKEN_TILE = 256
PROJ_TILE = 512
KEY_CHUNK = 256
OUT_BLOCK = 256
CTX_STAGE_LAGS = (3, 3, True, 2)
LAT_STAGE_LAGS = (2, 2, False, 2)
MLA_HEAD_PAD = 128
V_ROWS = HEAD_DIM + 16
MLA_IN_PAD = 1792
VMEM_LIMIT = 56 * 1024 * 1024
NT_DIMS = (((1,), (1,)), ((), ()))


def _silu(v):
    return v * jax.nn.sigmoid(v)


def _rms(v, g):
    return v * lax.rsqrt(jnp.mean(v * v, axis=-1, keepdims=True) + EPS) * g


def _mod_norm(x, g, mod):
    shift = mod[:, :D_MODEL]
    scale = mod[:, D_MODEL:2 * D_MODEL]
    return _rms(x, g) * (1.0 + scale) + shift


def _rope_tile(v, cos, sin_signed):
    lane = lax.broadcasted_iota(jnp.int32, v.shape, 1)
    nxt = pltpu.roll(v, LANES - 1, axis=1)
    prv = pltpu.roll(v, 1, axis=1)
    swapped = jnp.where((lane & 1) == 0, nxt, prv)
    return v * cos + swapped * sin_signed


def _ada_kernel(c_ref, w_ref, b_ref, o_ref):
    s = _silu(c_ref[...]).astype(BF16)
    o_ref[...] = jnp.dot(s, w_ref[...].astype(BF16), preferred_element_type=F32) + b_ref[...]


def _ada_mods(cc, w_ada, b_ada):
    tn = D_MODEL
    return pl.pallas_call(
        _ada_kernel,
        grid=(DEPTH, 3 * D_MODEL // tn),
        in_specs=[
            pl.BlockSpec((8, D_MODEL), lambda i, n: (0, 0)),
            pl.BlockSpec((None, D_MODEL, tn), lambda i, n: (i, 0, n)),
            pl.BlockSpec((None, 1, tn), lambda i, n: (i, 0, n)),
        ],
        out_specs=pl.BlockSpec((None, 8, tn), lambda i, n: (i, 0, n)),
        out_shape=jax.ShapeDtypeStruct((DEPTH, 8, 3 * D_MODEL), F32),
        compiler_params=pltpu.CompilerParams(vmem_limit_bytes=VMEM_LIMIT),
        name="ada_mods",
    )(cc, w_ada, b_ada.reshape(DEPTH, 1, 3 * D_MODEL))


def _store_values_t(vt_out, vt, n_heads):
    v_rows = vt_out.shape[0] // n_heads
    if v_rows == HEAD_DIM:
        vt_out[...] = vt.astype(BF16)
        return
    ones = jnp.ones((v_rows - HEAD_DIM, vt.shape[1]), BF16)
    for h in range(n_heads):
        vt_out[v_rows * h:v_rows * h + HEAD_DIM, :] = vt[HEAD_DIM * h:HEAD_DIM * (h + 1), :].astype(BF16)
        vt_out[v_rows * h + HEAD_DIM:v_rows * (h + 1), :] = ones


def _mla_expand(ckv_bf, kpe_bf, wk_ref, rp_ref, wvt_ref):
    k = (jnp.dot(ckv_bf, wk_ref[...], preferred_element_type=F32)
         + jnp.dot(kpe_bf, rp_ref[...], preferred_element_type=F32))
    vt = lax.dot_general(wvt_ref[...], ckv_bf, NT_DIMS, preferred_element_type=F32)
    return k.astype(BF16), vt


def _proj_mla_kernel(*refs, rope, ctx):
    refs = list(refs)
    x_ref, mod_ref, g_ref, win_ref, qn_ref, wuq_ref, kvn_ref, wk_ref, rp_ref, wvt_ref = refs[:10]
    pos = 10
    if rope:
        cq_ref, sq_ref, ck_ref, sk_ref = refs[pos:pos + 4]
        pos += 4
    q_out, k_out, vt_out, gate_out = refs[pos:pos + 4]
    pos += 4
    if ctx:
        ckv_out, kpe_out = refs[pos:pos + 2]

    h = _mod_norm(x_ref[...], g_ref[...], mod_ref[...]).astype(BF16)
    z = jnp.dot(h, win_ref[...], preferred_element_type=F32)
    cq = z[:, :MLA_Q_LORA]
    ckv = z[:, MLA_Q_LORA:MLA_Q_LORA + MLA_KV_LORA]
    gate = z[:, 640:640 + D_MODEL]
    kpe = z[:, 640 + D_MODEL:]
    gate_out[...] = _silu(gate).astype(BF16)

    qf = jnp.dot(_rms(cq, qn_ref[...]).astype(BF16), wuq_ref[...], preferred_element_type=F32)
    ckvn = _rms(ckv, kvn_ref[...])
    if ctx:
        ckv_out[...] = ckvn
        kpe_out[...] = kpe[:, :MLA_ROPE]
    if rope:
        kpe = _rope_tile(kpe, ck_ref[...], sk_ref[...])
    k, vt = _mla_expand(ckvn.astype(BF16), kpe.astype(BF16), wk_ref, rp_ref, wvt_ref)
    k_out[...] = k
    _store_values_t(vt_out, vt, N_HEADS)
    for hd in range(N_HEADS):
        blk = qf[:, MLA_HEAD_PAD * hd:MLA_HEAD_PAD * (hd + 1)]
        if rope:
            blk = _rope_tile(blk, cq_ref[...], sq_ref[...])
        q_out[:, MLA_HEAD_PAD * hd:MLA_HEAD_PAD * (hd + 1)] = (blk * MLA_Q_SCALE).astype(BF16)


def _proj_gqa_kernel(*refs, qk_norm, rope, ctx):
    refs = list(refs)
    x_ref, mod_ref, g_ref, win_ref = refs[:4]
    pos = 4
    if qk_norm:
        qn_ref, kn_ref = refs[pos:pos + 2]
        pos += 2
    if rope:
        cos_ref, sin_ref = refs[pos:pos + 2]
        pos += 2
    q_out, k_out, vt_out, gate_out = refs[pos:pos + 4]
    pos += 4
    if ctx:
        kc_out, vc_out = refs[pos:pos + 2]

    nq = N_HEADS * HEAD_DIM
    nk = KV_HEADS * HEAD_DIM
    h = _mod_norm(x_ref[...], g_ref[...], mod_ref[...]).astype(BF16)
    z = jnp.dot(h, win_ref[...], preferred_element_type=F32)
    tm = z.shape[0]
    lo = lax.broadcasted_iota(jnp.int32, (tm, LANES), 1) < HEAD_DIM
    n_q_tiles = nq // LANES
    for c in range((nq + nk) // LANES):
        blk = z[:, LANES * c:LANES * (c + 1)]
        is_q = c < n_q_tiles
        if qk_norm:
            sq = blk * blk
            s_lo = jnp.sum(jnp.where(lo, sq, 0.0), axis=-1, keepdims=True)
            s_hi = jnp.sum(jnp.where(lo, 0.0, sq), axis=-1, keepdims=True)
            inv = jnp.where(lo, lax.rsqrt(s_lo * (1.0 / HEAD_DIM) + EPS),
                            lax.rsqrt(s_hi * (1.0 / HEAD_DIM) + EPS))
            blk = blk * inv * (qn_ref[...] if is_q else kn_ref[...])
        if ctx and not is_q:
            kc_out[:, LANES * (c - n_q_tiles):LANES * (c - n_q_tiles + 1)] = blk
        if rope:
            blk = _rope_tile(blk, cos_ref[...], sin_ref[...])
        if is_q:
            q_out[:, LANES * c:LANES * (c + 1)] = (blk * GQA_Q_SCALE).astype(BF16)
        else:
            k_out[:, LANES * (c - n_q_tiles):LANES * (c - n_q_tiles + 1)] = blk.astype(BF16)
    v = z[:, nq + nk:nq + 2 * nk]
    if ctx:
        vc_out[...] = v
    _store_values_t(vt_out, v.T, KV_HEADS)
    gate_out[...] = _silu(z[:, nq + 2 * nk:]).astype(BF16)


def _const_spec(shape):
    return pl.BlockSpec(shape, lambda i: (0,) * len(shape))


def _project(kind, x, mods_l, norm_g, weights, tables, *, batch, seq, ctx):
    n_tok = batch * seq
    tm = min(seq, PROJ_TILE)
    tiles_per_seq = seq // tm
    v_rows = V_ROWS if ctx else HEAD_DIM
    rope = not ctx
    mod_row = (lambda i: (4, 0, 0)) if ctx else (lambda i: (i // tiles_per_seq, 0, 0))
    in_specs = [
        pl.BlockSpec((tm, D_MODEL), lambda i: (i, 0)),
        pl.BlockSpec((None, 1, 3 * D_MODEL), mod_row),
        _const_spec((1, D_MODEL)),
    ]
    args = [x, mods_l, norm_g.reshape(1, D_MODEL)]
    for w in weights:
        in_specs.append(_const_spec(w.shape))
        args.append(w)
    if rope:
        for t in tables:
            in_specs.append(pl.BlockSpec((tm, LANES), lambda i: (i % tiles_per_seq, 0)))
            args.append(t)
    if kind == 0:
        qw, kw, vw = N_HEADS * MLA_HEAD_PAD, N_HEADS * MLA_HEAD_PAD, N_HEADS * v_rows
        body = functools.partial(_proj_mla_kernel, rope=rope, ctx=ctx)
        cache_shapes = [(n_tok, MLA_KV_LORA), (n_tok, MLA_ROPE)]
    else:
        qw, kw, vw = N_HEADS * HEAD_DIM, KV_HEADS * HEAD_DIM, KV_HEADS * v_rows
        body = functools.partial(_proj_gqa_kernel, qk_norm=(kind == 1), rope=rope, ctx=ctx)
        cache_shapes = [(n_tok, kw), (n_tok, kw)]
    out_shape = [
        jax.ShapeDtypeStruct((n_tok, qw), BF16),
        jax.ShapeDtypeStruct((n_tok, kw), BF16),
        jax.ShapeDtypeStruct((batch, vw, seq), BF16),
        jax.ShapeDtypeStruct((n_tok, D_MODEL), BF16),
    ]
    out_specs = [
        pl.BlockSpec((tm, qw), lambda i: (i, 0)),
        pl.BlockSpec((tm, kw), lambda i: (i, 0)),
        pl.BlockSpec((None, vw, tm), lambda i: (i // tiles_per_seq, 0, i % tiles_per_seq)),
        pl.BlockSpec((tm, D_MODEL), lambda i: (i, 0)),
    ]
    if ctx:
        for s in cache_shapes:
            out_shape.append(jax.ShapeDtypeStruct(s, F32))
            out_specs.append(pl.BlockSpec((tm, s[1]), lambda i: (i, 0)))
    return pl.pallas_call(
        body,
        grid=(n_tok // tm,),
        in_specs=in_specs,
        out_specs=out_specs,
        out_shape=out_shape,
        compiler_params=pltpu.CompilerParams(
            dimension_semantics=("arbitrary",), vmem_limit_bytes=VMEM_LIMIT),
        name=f"proj_k{kind}_{'ctx' if ctx else 'lat'}",
    )(*args)


def _mla_cache_kernel(ckv_ref, kpe_ref, wk_ref, rp_ref, wvt_ref, k_out, vt_out):
    k, vt = _mla_expand(ckv_ref[...].astype(BF16), kpe_ref[...].astype(BF16), wk_ref, rp_ref, wvt_ref)
    k_out[...] = k
    _store_values_t(vt_out, vt, N_HEADS)


def _mla_cache(ckv, kpe_pad, wk, rp, wvt):
    b, n, _ = ckv.shape
    return pl.pallas_call(
        _mla_cache_kernel,
        grid=(b,),
        in_specs=[
            pl.BlockSpec((None, n, MLA_KV_LORA), lambda i: (i, 0, 0)),
            pl.BlockSpec((None, n, LANES), lambda i: (i, 0, 0)),
            _const_spec(wk.shape), _const_spec(rp.shape), _const_spec(wvt.shape),
        ],
        out_specs=[
            pl.BlockSpec((None, n, N_HEADS * MLA_HEAD_PAD), lambda i: (i, 0, 0)),
            pl.BlockSpec((None, N_HEADS * HEAD_DIM, n), lambda i: (i, 0, 0)),
        ],
        out_shape=[
            jax.ShapeDtypeStruct((b, n, N_HEADS * MLA_HEAD_PAD), BF16),
            jax.ShapeDtypeStruct((b, N_HEADS * HEAD_DIM, n), BF16),
        ],
        compiler_params=pltpu.CompilerParams(
            dimension_semantics=("arbitrary",), vmem_limit_bytes=VMEM_LIMIT),
        name="mla_cache",
    )(ckv, kpe_pad, wk, rp, wvt)


def _gqa_cache_kernel(k_ref, v_ref, k_out, vt_out):
    k_out[...] = k_ref[...].astype(BF16)
    _store_values_t(vt_out, v_ref[...].T, KV_HEADS)


def _gqa_cache(k, v):
    b, n, w = k.shape
    return pl.pallas_call(
        _gqa_cache_kernel,
        grid=(b,),
        in_specs=[pl.BlockSpec((None, n, w), lambda i: (i, 0, 0))] * 2,
        out_specs=[
            pl.BlockSpec((None, n, w), lambda i: (i, 0, 0)),
            pl.BlockSpec((None, KV_HEADS * HEAD_DIM, n), lambda i: (i, 0, 0)),
        ],
        out_shape=[
            jax.ShapeDtypeStruct((b, n, w), BF16),
            jax.ShapeDtypeStruct((b, KV_HEADS * HEAD_DIM, n), BF16),
        ],
        compiler_params=pltpu.CompilerParams(dimension_semantics=("arbitrary",)),
        name="gqa_cache",
    )(k, v)


def _attn_kernel(*refs, head_w, group, has_cache, has_sink, band, final, lag_b, lag_c, interleave,
                 subs, own_keys, seq):
    refs = list(refs)
    q_ref, kn_ref, vtn_ref = refs[:3]
    pos = 3
    if has_cache:
        kc_ref, vtc_ref = refs[pos:pos + 2]
        pos += 2
    gate_ref, x_ref, mod_ref, wo_ref = refs[pos:pos + 4]
    pos += 4
    if has_sink:
        sink_ref = refs[pos]
        pos += 1
    if final:
        fg_ref = refs[pos]
        pos += 1
    o_ref, ot_scr, s_scr, p_scr, g_scr, y_scr = refs[pos:pos + 6]
    if band:
        bias_scr = refs[pos + 6]

    tq = TOKEN_TILE
    ch = KEY_CHUNK
    slots = s_scr.shape[0]
    segs = []
    if has_cache:
        segs.append((kc_ref.shape[0],
                     lambda e, r, n, cols: kc_ref[r:r + n, cols],
                     lambda e, rows: vtc_ref[rows, :]))
    if own_keys:
        segs.append((seq,
                     lambda e, r, n, cols: kn_ref[e * seq + r:e * seq + r + n, cols],
                     lambda e, rows: vtn_ref[e, rows, :]))
    else:
        segs.append((seq,
                     lambda e, r, n, cols: kn_ref[r:r + n, cols],
                     lambda e, rows: vtn_ref[rows, :]))
    seg_rows = []
    chunks = []
    off = 0
    for si, (n_seg, _, _) in enumerate(segs):
        seg_rows.append((off, n_seg))
        chunks += [(si, r, off + r) for r in range(0, n_seg, ch)]
        off += n_seg
    if interleave:
        score_chunks = [c + (ch,) for c in chunks]
    else:
        score_chunks = [(si, 0, r0, n) for si, (r0, n) in enumerate(seg_rows)]
    own = len(segs) - 1
    if band:
        kpos = lax.broadcasted_iota(jnp.int32, (seq, tq), 0)
        for e in range(subs):
            tile = pl.program_id(1) * subs + e
            qpos = tile * tq + lax.broadcasted_iota(jnp.int32, (seq, tq), 1)
            bias_scr[e] = jnp.where(jnp.abs(kpos - qpos) <= WINDOW, 0.0, NEG_INF)

    assert lag_b >= 1 and lag_c >= 1 and slots > max(lag_b, lag_c)
    row_max = {}
    den_part = {}
    v_rows = vtn_ref.shape[-2] * group // N_HEADS
    den_on_mxu = v_rows > HEAD_DIM

    def scores_and_max(i):
        e, hd = divmod(i, N_HEADS)
        kh = hd // group
        qh = q_ref[e * tq:(e + 1) * tq, head_w * hd:head_w * (hd + 1)]
        mx = None
        for si, r, g, n in score_chunks:
            kk = segs[si][1](e, r, n, slice(head_w * kh, head_w * (kh + 1)))
            s = lax.dot_general(kk, qh, NT_DIMS, preferred_element_type=F32)
            if band and si == own:
                s = s + bias_scr[e, r:r + n, :]
            s_scr[i % slots, g:g + n, :] = s
            part = jnp.max(s.reshape(n // 8, 8, tq), axis=0)
            mx = part if mx is None else jnp.maximum(mx, part)
            yield
        m = jnp.max(mx, axis=0, keepdims=True)
        if has_sink:
            m = jnp.maximum(m, sink_ref[hd] * LOG2E)
        row_max[i] = m

    def exponentials(i):
        hd = i % N_HEADS
        m = row_max.pop(i)
        mb = jnp.broadcast_to(m, (8, tq))
        tot = None
        for _, _, g in chunks:
            p = jnp.exp2(s_scr[i % slots, g:g + ch, :].reshape(ch // 8, 8, tq) - mb)
            if not den_on_mxu:
                part = jnp.sum(p, axis=0)
                tot = part if tot is None else tot + part
            p_scr[i % slots, g:g + ch, :] = p.reshape(ch, tq).astype(BF16)
            yield
        den = None if den_on_mxu else jnp.sum(tot, axis=0, keepdims=True)
        if has_sink:
            sink = jnp.exp2(sink_ref[hd] * LOG2E - m)
            den = sink if den is None else den + sink
        den_part[i] = den

    def weighted_values(i):
        e, hd = divmod(i, N_HEADS)
        kh = hd // group
        acc = None
        for si, (_, _, values_t) in enumerate(segs):
            r0, n = seg_rows[si]
            vt = values_t(e, slice(v_rows * kh, v_rows * (kh + 1)))
            part = jnp.dot(vt, p_scr[i % slots, r0:r0 + n, :], preferred_element_type=F32)
            acc = part if acc is None else acc + part
            yield
        den = den_part.pop(i)
        if den_on_mxu:
            mxu_sum = acc[HEAD_DIM:HEAD_DIM + 1, :]
            den = mxu_sum if den is None else den + mxu_sum
        ot_scr[e, HEAD_DIM * hd:HEAD_DIM * (hd + 1), :] = acc[:HEAD_DIM, :] / den

    def finish(e):
        rows = slice(e * tq, (e + 1) * tq)
        for c0 in range(0, D_MODEL, OUT_BLOCK):
            cols = slice(c0, c0 + OUT_BLOCK)
            o = ot_scr[e, cols, :].T
            g_scr[:, cols] = (o * gate_ref[rows, cols].astype(F32)).astype(BF16)
            yield
        for c0 in range(0, D_MODEL, OUT_BLOCK):
            cols = slice(c0, c0 + OUT_BLOCK)
            y = jnp.dot(g_scr[...], wo_ref[:, cols], preferred_element_type=F32)
            xn = x_ref[rows, cols] + mod_ref[:, 2 * D_MODEL + c0:2 * D_MODEL + c0 + OUT_BLOCK] * y
            if final:
                y_scr[:, cols] = xn
            else:
                o_ref[rows, cols] = xn
            yield
        if final:
            o_ref[rows, :] = _rms(y_scr[...], fg_ref[...])

    n_items = subs * N_HEADS
    for step in range(n_items + lag_b + lag_c + 1):
        live = []
        if step < n_items:
            live.append(scores_and_max(step))
        if 0 <= step - lag_b < n_items:
            live.append(exponentials(step - lag_b))
        done = step - lag_b - lag_c
        if 0 <= done < n_items:
            live.append(weighted_values(done))
        if done >= N_HEADS and done % N_HEADS == 0:
            live.append(finish(done // N_HEADS - 1))
        if not interleave:
            for g in live:
                for _ in g:
                    pass
            live = []
        while live:
            live = [g for g in live if next(g, True) is None]


def _attend(kind, q, k_new, vt_new, cache, gate, x, mods_l, w_out, sink, final_g, *, batch, seq, ctx,
            lags):
    lag_b, lag_c, interleave, subs = lags
    tq = TOKEN_TILE
    n_tok = batch * seq
    band = kind == 2 and not ctx
    head_w = MLA_HEAD_PAD if kind == 0 else HEAD_DIM
    kw = k_new.shape[1]
    vw = vt_new.shape[1]
    own_keys = cache is None
    assert seq == tq if own_keys else seq % (subs * tq) == 0
    grid = (batch // subs, 1) if own_keys else (batch, seq // (subs * tq))
    tok_spec = lambda w: pl.BlockSpec((subs * tq, w), lambda b, t: (b * grid[1] + t, 0))
    if own_keys:
        in_specs = [
            tok_spec(q.shape[1]),
            pl.BlockSpec((subs * seq, kw), lambda b, t: (b, 0)),
            pl.BlockSpec((subs, vw, seq), lambda b, t: (b, 0, 0)),
        ]
    else:
        in_specs = [
            tok_spec(q.shape[1]),
            pl.BlockSpec((seq, kw), lambda b, t: (b, 0)),
            pl.BlockSpec((None, vw, seq), lambda b, t: (b, 0, 0)),
        ]
    args = [q, k_new, vt_new]
    n_keys = seq
    if cache is not None:
        kc, vtc = cache
        n_c = kc.shape[1]
        n_keys += n_c
        in_specs += [
            pl.BlockSpec((None, n_c, kw), lambda b, t: (b, 0, 0)),
            pl.BlockSpec((None, vw, n_c), lambda b, t: (b, 0, 0)),
        ]
        args += [kc, vtc]
    mod_row = (lambda b, t: (4, 0, 0)) if ctx else (lambda b, t: (b, 0, 0))
    in_specs += [
        tok_spec(D_MODEL),
        tok_spec(D_MODEL),
        pl.BlockSpec((None, 1, 3 * D_MODEL), mod_row),
        pl.BlockSpec((D_MODEL, D_MODEL), lambda b, t: (0, 0)),
    ]
    args += [gate, x, mods_l, w_out]
    if sink is not None:
        in_specs.append(pl.BlockSpec(memory_space=pltpu.SMEM))
        args.append(sink)
    if final_g is not None:
        in_specs.append(pl.BlockSpec((1, D_MODEL), lambda b, t: (0, 0)))
        args.append(final_g.reshape(1, D_MODEL))
    body = functools.partial(
        _attn_kernel,
        head_w=head_w,
        group=1 if kind == 0 else GROUPS,
        has_cache=cache is not None,
        has_sink=sink is not None,
        band=band,
        final=final_g is not None,
        lag_b=lag_b,
        lag_c=lag_c,
        interleave=interleave,
        subs=subs,
        own_keys=own_keys,
        seq=seq,
    )
    slots = max(lag_b, lag_c) + 1
    scratch = [pltpu.VMEM((subs, N_HEADS * HEAD_DIM, tq), F32),
               pltpu.VMEM((slots, n_keys, tq), F32),
               pltpu.VMEM((slots, n_keys, tq), BF16),
               pltpu.VMEM((tq, D_MODEL), BF16),
               pltpu.VMEM((tq, D_MODEL), F32)]
    if band:
        scratch.append(pltpu.VMEM((subs, seq, tq), F32))
    return pl.pallas_call(
        body,
        grid=grid,
        in_specs=in_specs,
        out_specs=tok_spec(D_MODEL),
        out_shape=jax.ShapeDtypeStruct((n_tok, D_MODEL), F32),
        scratch_shapes=scratch,
        compiler_params=pltpu.CompilerParams(
            dimension_semantics=("arbitrary", "arbitrary"), vmem_limit_bytes=VMEM_LIMIT),
        name=f"attn_k{kind}_{'ctx' if ctx else 'lat'}",
    )(*args)


def _rope_lane_tables(seq, rot_dim):
    rows = seq // GRID_W
    row = jnp.repeat(jnp.arange(rows), GRID_W).astype(F32)
    col = jnp.tile(jnp.arange(GRID_W), rows).astype(F32)
    nf = rot_dim // 4
    freqs = ROPE_THETA ** (-jnp.arange(nf, dtype=F32) / nf)
    ang = jnp.concatenate([row[:, None] * freqs, col[:, None] * freqs], axis=-1)
    cos, sin = jnp.cos(ang), jnp.sin(ang)
    cos_l = jnp.repeat(cos, 2, axis=-1)
    sin_l = jnp.stack([-sin, sin], axis=-1).reshape(seq, rot_dim)
    return cos_l, sin_l


def _embed_lanes(table, start, fill):
    seq, w = table.shape
    return jnp.concatenate(
        [jnp.full((seq, start), fill, F32), table, jnp.full((seq, LANES - start - w), fill, F32)], axis=1)


def _mla_weights(w_in, q_norm, w_uq, kv_norm, w_ukv):
    c0 = MLA_Q_LORA + MLA_KV_LORA
    win = jnp.concatenate(
        [w_in[:, :c0], w_in[:, c0 + MLA_ROPE:], w_in[:, c0:c0 + MLA_ROPE],
         jnp.zeros((D_MODEL, MLA_IN_PAD - w_in.shape[1]), F32)], axis=1).astype(BF16)
    hq = MLA_NOPE + MLA_ROPE
    wuq = jnp.pad(w_uq.reshape(MLA_Q_LORA, N_HEADS, hq),
                  ((0, 0), (0, 0), (0, MLA_HEAD_PAD - hq))).reshape(MLA_Q_LORA, -1).astype(BF16)
    wkv = w_ukv.reshape(MLA_KV_LORA, N_HEADS, MLA_NOPE + MLA_V)
    wk = jnp.pad(wkv[:, :, :MLA_NOPE],
                 ((0, 0), (0, 0), (0, MLA_HEAD_PAD - MLA_NOPE))).reshape(MLA_KV_LORA, -1).astype(BF16)
    wvt = wkv[:, :, MLA_NOPE:].reshape(MLA_KV_LORA, -1).T.astype(BF16)
    src = lax.broadcasted_iota(jnp.int32, (LANES, N_HEADS * MLA_HEAD_PAD), 0)
    dst = lax.broadcasted_iota(jnp.int32, (LANES, N_HEADS * MLA_HEAD_PAD), 1)
    rp = ((dst % MLA_HEAD_PAD == src + MLA_NOPE) & (src < MLA_ROPE)).astype(BF16)
    return [win, q_norm.reshape(1, -1), wuq, kv_norm.reshape(1, -1), wk, rp, wvt]


def kernel(x_prompt, x_sample, cache_mla_ckv, cache_mla_kpe, cache_gqa_k, cache_gqa_v,
           cache_swa_k, cache_swa_v, c, c_ctx, norm_g, w_ada, b_ada, w_out,
           mla_w_in, mla_q_norm, mla_w_uq, mla_kv_norm, mla_w_ukv,
           gqa_w_in, gqa_q_norm, gqa_k_norm, swa_w_in, swa_sink, final_norm_g):
    bc, sc, _ = x_prompt.shape
    bl, sl, _ = x_sample.shape
    n_past = cache_mla_ckv.shape[2]
    kvw = KV_HEADS * HEAD_DIM

    cc = jnp.concatenate([c, c_ctx[None, :], jnp.zeros((8 - bl - 1, D_MODEL), F32)], axis=0)
    mods = _ada_mods(cc, w_ada, b_ada).reshape(DEPTH, 8, 1, 3 * D_MODEL)

    cos_h, sin_h = _rope_lane_tables(sl, HEAD_DIM)
    gqa_tables = [jnp.tile(cos_h, (1, 2)), jnp.tile(sin_h, (1, 2))]
    cos_r, sin_r = _rope_lane_tables(sl, MLA_ROPE)
    mla_tables = [_embed_lanes(cos_r, MLA_NOPE, 1.0), _embed_lanes(sin_r, MLA_NOPE, 0.0),
                  _embed_lanes(cos_r, 0, 1.0), _embed_lanes(sin_r, 0, 0.0)]

    xc = x_prompt.reshape(bc * sc, D_MODEL)
    xl = x_sample.reshape(bl * sl, D_MODEL)
    new_caches = {0: ([], []), 1: ([], []), 2: ([], [])}
    for i in range(DEPTH):
        kind, j = i % 3, i // 3
        mods_l = mods[i]
        wo = w_out[i].astype(BF16)
        final_g = final_norm_g if i == DEPTH - 1 else None
        sink = None
        if kind == 0:
            weights = _mla_weights(mla_w_in[j], mla_q_norm[j], mla_w_uq[j], mla_kv_norm[j], mla_w_ukv[j])
            tables = mla_tables
            kpe_pad = jnp.pad(cache_mla_kpe[:, j], ((0, 0), (0, 0), (0, LANES - MLA_ROPE)))
            cache = _mla_cache(cache_mla_ckv[:, j], kpe_pad, weights[4], weights[5], weights[6])
        else:
            tables = gqa_tables
            if kind == 1:
                weights = [gqa_w_in[j].astype(BF16),
                           jnp.tile(gqa_q_norm[j], 2).reshape(1, LANES),
                           jnp.tile(gqa_k_norm[j], 2).reshape(1, LANES)]
                cache = _gqa_cache(cache_gqa_k[:, j].reshape(bl, n_past, kvw),
                                   cache_gqa_v[:, j].reshape(bl, n_past, kvw))
            else:
                weights = [swa_w_in[j].astype(BF16)]
                sink = swa_sink[j]
                cache = _gqa_cache(cache_swa_k[:, j].reshape(bl, n_past, kvw),
                                   cache_swa_v[:, j].reshape(bl, n_past, kvw))

        q, k, vt, gate, c_a, c_b = _project(kind, xc, mods_l, norm_g[i], weights, tables,
                                            batch=bc, seq=sc, ctx=True)
        new_caches[kind][0].append(c_a)
        new_caches[kind][1].append(c_b)
        xc = _attend(kind, q, k, vt, None, gate, xc, mods_l, wo, sink, final_g,
                     batch=bc, seq=sc, ctx=True, lags=CTX_STAGE_LAGS)

        q, k, vt, gate = _project(kind, xl, mods_l, norm_g[i], weights, tables,
                                  batch=bl, seq=sl, ctx=False)
        xl = _attend(kind, q, k, vt, cache, gate, xl, mods_l, wo, sink, final_g,
                     batch=bl, seq=sl, ctx=False, lags=LAT_STAGE_LAGS)

    def stack(parts, tail):
        return jnp.stack([p.reshape((bc, sc) + tail) for p in parts], axis=1)

    return (xc.reshape(bc, sc, D_MODEL), xl.reshape(bl, sl, D_MODEL),
            stack(new_caches[0][0], (MLA_KV_LORA,)), stack(new_caches[0][1], (MLA_ROPE,)),
            stack(new_caches[1][0], (KV_HEADS, HEAD_DIM)), stack(new_caches[1][1], (KV_HEADS, HEAD_DIM)),
            stack(new_caches[2][0], (KV_HEADS, HEAD_DIM)), stack(new_caches[2][1], (KV_HEADS, HEAD_DIM)))
```

```python
import functools

import jax
import jax.numpy as jnp
from jax import lax
from jax.experimental import pallas as pl
from jax.experimental.pallas import tpu as pltpu

F32 = jnp.float32
BF16 = jnp.bfloat16

D_MODEL = 1024
DEPTH = 4
GRID_W = 64
N_HEADS = 16
HEAD_DIM = 64
KV_HEADS = 4
GROUPS = N_HEADS // KV_HEADS
MLA_Q_LORA = 384
MLA_KV_LORA = 256
MLA_NOPE = 64
MLA_ROPE = 32
MLA_V = 64
WINDOW = 128
ROPE_THETA = 10000.0
EPS = 1e-6
NEG_INF = -1e30

LOG2E = 1.4426950408889634
GQA_Q_SCALE = HEAD_DIM ** -0.5 * LOG2E
MLA_Q_SCALE = (MLA_NOPE + MLA_ROPE) ** -0.5 * LOG2E

LANES = 128
TOKEN_TILE = 256
PROJ_TILE = 512
KEY_CHUNK = 256
OUT_BLOCK = 256
CTX_STAGE_LAGS = (3, 3, True, 2)
LAT_STAGE_LAGS = (2, 2, False, 2)
MLA_HEAD_PAD = 128
V_ROWS = HEAD_DIM + 16
MLA_IN_PAD = 1792
VMEM_LIMIT = 56 * 1024 * 1024
NT_DIMS = (((1,), (1,)), ((), ()))


def _silu(v):
    return v * jax.nn.sigmoid(v)


def _rms(v, g):
    return v * lax.rsqrt(jnp.mean(v * v, axis=-1, keepdims=True) + EPS) * g


def _mod_norm(x, g, mod):
    shift = mod[:, :D_MODEL]
    scale = mod[:, D_MODEL:2 * D_MODEL]
    return _rms(x, g) * (1.0 + scale) + shift


def _rope_tile(v, cos, sin_signed):
    lane = lax.broadcasted_iota(jnp.int32, v.shape, 1)
    nxt = pltpu.roll(v, LANES - 1, axis=1)
    prv = pltpu.roll(v, 1, axis=1)
    swapped = jnp.where((lane & 1) == 0, nxt, prv)
    return v * cos + swapped * sin_signed


def _ada_kernel(c_ref, w_ref, b_ref, o_ref):
    s = _silu(c_ref[...]).astype(BF16)
    o_ref[...] = jnp.dot(s, w_ref[...].astype(BF16), preferred_element_type=F32) + b_ref[...]


def _ada_mods(cc, w_ada, b_ada):
    tn = D_MODEL
    return pl.pallas_call(
        _ada_kernel,
        grid=(DEPTH, 3 * D_MODEL // tn),
        in_specs=[
            pl.BlockSpec((8, D_MODEL), lambda i, n: (0, 0)),
            pl.BlockSpec((None, D_MODEL, tn), lambda i, n: (i, 0, n)),
            pl.BlockSpec((None, 1, tn), lambda i, n: (i, 0, n)),
        ],
        out_specs=pl.BlockSpec((None, 8, tn), lambda i, n: (i, 0, n)),
        out_shape=jax.ShapeDtypeStruct((DEPTH, 8, 3 * D_MODEL), F32),
        compiler_params=pltpu.CompilerParams(vmem_limit_bytes=VMEM_LIMIT),
        name="ada_mods",
    )(cc, w_ada, b_ada.reshape(DEPTH, 1, 3 * D_MODEL))


def _store_values_t(vt_out, vt, n_heads):
    if len(vt_out.shape) == 3:
        seq = vt_out.shape[2]
        for e in range(vt_out.shape[0]):
            _store_values_t(vt_out.at[e], vt[:, e * seq:(e + 1) * seq], n_heads)
        return
    v_rows = vt_out.shape[0] // n_heads
    if v_rows == HEAD_DIM:
        vt_out[...] = vt.astype(BF16)
        return
    ones = jnp.ones((v_rows - HEAD_DIM, vt.shape[1]), BF16)
    for h in range(n_heads):
        vt_out[v_rows * h:v_rows * h + HEAD_DIM, :] = vt[HEAD_DIM * h:HEAD_DIM * (h + 1), :].astype(BF16)
        vt_out[v_rows * h + HEAD_DIM:v_rows * (h + 1), :] = ones


def _mla_expand(ckv_bf, kpe, wk_ref, wvt_ref, k_out, vt_out):
    k = jnp.dot(ckv_bf, wk_ref[...], preferred_element_type=F32)
    lane = lax.broadcasted_iota(jnp.int32, kpe.shape, 1)
    rope_lanes = (lane >= MLA_NOPE) & (lane < MLA_NOPE + MLA_ROPE)
    kpe_at_rope = pltpu.roll(kpe, MLA_NOPE, axis=1)
    for hd in range(N_HEADS):
        cols = slice(MLA_HEAD_PAD * hd, MLA_HEAD_PAD * (hd + 1))
        k_out[:, cols] = jnp.where(rope_lanes, kpe_at_rope, k[:, cols]).astype(BF16)
    vt = lax.dot_general(wvt_ref[...], ckv_bf, NT_DIMS, preferred_element_type=F32)
    _store_values_t(vt_out, vt, N_HEADS)


def _proj_mla_kernel(*refs, rope, ctx):
    refs = list(refs)
    x_ref, mod_ref, g_ref, win_ref, qn_ref, wuq_ref, kvn_ref, wk_ref, wvt_ref = refs[:9]
    pos = 9
    if rope:
        cq_ref, sq_ref, ck_ref, sk_ref = refs[pos:pos + 4]
        pos += 4
    q_out, k_out, vt_out, gate_out = refs[pos:pos + 4]
    pos += 4
    if ctx:
        ckv_out, kpe_out = refs[pos:pos + 2]

    h = _mod_norm(x_ref[...], g_ref[...], mod_ref[...]).astype(BF16)
    z = jnp.dot(h, win_ref[...], preferred_element_type=F32)
    cq = z[:, :MLA_Q_LORA]
    ckv = z[:, MLA_Q_LORA:MLA_Q_LORA + MLA_KV_LORA]
    gate = z[:, 640:640 + D_MODEL]
    kpe = z[:, 640 + D_MODEL:]
    gate_out[...] = _silu(gate).astype(BF16)

    qf = jnp.dot(_rms(cq, qn_ref[...]).astype(BF16), wuq_ref[...], preferred_element_type=F32)
    ckvn = _rms(ckv, kvn_ref[...])
    if ctx:
        ckv_out[...] = ckvn
        kpe_out[...] = kpe[:, :MLA_ROPE]
    if rope:
        kpe = _rope_tile(kpe, ck_ref[...], sk_ref[...])
    _mla_expand(ckvn.astype(BF16), kpe, wk_ref, wvt_ref, k_out, vt_out)
    for hd in range(N_HEADS):
        blk = qf[:, MLA_HEAD_PAD * hd:MLA_HEAD_PAD * (hd + 1)]
        if rope:
            blk = _rope_tile(blk, cq_ref[...], sq_ref[...])
        q_out[:, MLA_HEAD_PAD * hd:MLA_HEAD_PAD * (hd + 1)] = (blk * MLA_Q_SCALE).astype(BF16)


def _proj_gqa_kernel(*refs, qk_norm, rope, ctx):
    refs = list(refs)
    x_ref, mod_ref, g_ref, win_ref = refs[:4]
    pos = 4
    if qk_norm:
        qn_ref, kn_ref = refs[pos:pos + 2]
        pos += 2
    if rope:
        cos_ref, sin_ref = refs[pos:pos + 2]
        pos += 2
    q_out, k_out, vt_out, gate_out = refs[pos:pos + 4]
    pos += 4
    if ctx:
        kc_out, vc_out = refs[pos:pos + 2]

    nq = N_HEADS * HEAD_DIM
    nk = KV_HEADS * HEAD_DIM
    h = _mod_norm(x_ref[...], g_ref[...], mod_ref[...]).astype(BF16)
    z = jnp.dot(h, win_ref[...], preferred_element_type=F32)
    tm = z.shape[0]
    lo = lax.broadcasted_iota(jnp.int32, (tm, LANES), 1) < HEAD_DIM
    n_q_tiles = nq // LANES
    for c in range((nq + nk) // LANES):
        blk = z[:, LANES * c:LANES * (c + 1)]
        is_q = c < n_q_tiles
        if qk_norm:
            sq = blk * blk
            s_lo = jnp.sum(jnp.where(lo, sq, 0.0), axis=-1, keepdims=True)
            s_hi = jnp.sum(jnp.where(lo, 0.0, sq), axis=-1, keepdims=True)
            inv = jnp.where(lo, lax.rsqrt(s_lo * (1.0 / HEAD_DIM) + EPS),
                            lax.rsqrt(s_hi * (1.0 / HEAD_DIM) + EPS))
            blk = blk * inv * (qn_ref[...] if is_q else kn_ref[...])
        if ctx and not is_q:
            kc_out[:, LANES * (c - n_q_tiles):LANES * (c - n_q_tiles + 1)] = blk
        if rope:
            blk = _rope_tile(blk, cos_ref[...], sin_ref[...])
        if is_q:
            q_out[:, LANES * c:LANES * (c + 1)] = (blk * GQA_Q_SCALE).astype(BF16)
        else:
            k_out[:, LANES * (c - n_q_tiles):LANES * (c - n_q_tiles + 1)] = blk.astype(BF16)
    v = z[:, nq + nk:nq + 2 * nk]
    if ctx:
        vc_out[...] = v
    _store_values_t(vt_out, v.T, KV_HEADS)
    gate_out[...] = _silu(z[:, nq + 2 * nk:]).astype(BF16)


def _const_spec(shape):
    return pl.BlockSpec(shape, lambda i: (0,) * len(shape))


def _project(kind, x, mods_l, norm_g, weights, tables, *, batch, seq, ctx):
    n_tok = batch * seq
    tm = PROJ_TILE
    tiles_per_seq = max(seq // tm, 1)
    seqs_per_tile = max(tm // seq, 1)
    assert tm == seqs_per_tile * seq // tiles_per_seq and (ctx or seqs_per_tile == 1)
    v_rows = V_ROWS if ctx else HEAD_DIM
    rope = not ctx
    mod_row = (lambda i: (4, 0, 0)) if ctx else (lambda i: (i // tiles_per_seq, 0, 0))
    in_specs = [
        pl.BlockSpec((tm, D_MODEL), lambda i: (i, 0)),
        pl.BlockSpec((None, 1, 3 * D_MODEL), mod_row),
        _const_spec((1, D_MODEL)),
    ]
    args = [x, mods_l, norm_g.reshape(1, D_MODEL)]
    for w in weights:
        in_specs.append(_const_spec(w.shape))
        args.append(w)
    if rope:
        for t in tables:
            in_specs.append(pl.BlockSpec((tm, LANES), lambda i: (i % tiles_per_seq, 0)))
            args.append(t)
    if kind == 0:
        qw, kw, vw = N_HEADS * MLA_HEAD_PAD, N_HEADS * MLA_HEAD_PAD, N_HEADS * v_rows
        body = functools.partial(_proj_mla_kernel, rope=rope, ctx=ctx)
        cache_shapes = [(n_tok, MLA_KV_LORA), (n_tok, MLA_ROPE)]
    else:
        qw, kw, vw = N_HEADS * HEAD_DIM, KV_HEADS * HEAD_DIM, KV_HEADS * v_rows
        body = functools.partial(_proj_gqa_kernel, qk_norm=(kind == 1), rope=rope, ctx=ctx)
        cache_shapes = [(n_tok, kw), (n_tok, kw)]
    out_shape = [
        jax.ShapeDtypeStruct((n_tok, qw), BF16),
        jax.ShapeDtypeStruct((n_tok, kw), BF16),
        jax.ShapeDtypeStruct((batch, vw, seq), BF16),
        jax.ShapeDtypeStruct((n_tok, D_MODEL), BF16),
    ]
    out_specs = [
        pl.BlockSpec((tm, qw), lambda i: (i, 0)),
        pl.BlockSpec((tm, kw), lambda i: (i, 0)),
        (pl.BlockSpec((seqs_per_tile, vw, seq), lambda i: (i, 0, 0)) if seqs_per_tile > 1 else
         pl.BlockSpec((None, vw, tm), lambda i: (i // tiles_per_seq, 0, i % tiles_per_seq))),
        pl.BlockSpec((tm, D_MODEL), lambda i: (i, 0)),
    ]
    if ctx:
        for s in cache_shapes:
            out_shape.append(jax.ShapeDtypeStruct(s, F32))
            out_specs.append(pl.BlockSpec((tm, s[1]), lambda i: (i, 0)))
    return pl.pallas_call(
        body,
        grid=(n_tok // tm,),
        in_specs=in_specs,
        out_specs=out_specs,
        out_shape=out_shape,
        compiler_params=pltpu.CompilerParams(
            dimension_semantics=("arbitrary",), vmem_limit_bytes=VMEM_LIMIT),
        name=f"proj_k{kind}_{'ctx' if ctx else 'lat'}",
    )(*args)


def _mla_cache_kernel(ckv_ref, kpe_ref, wk_ref, wvt_ref, k_out, vt_out):
    _mla_expand(ckv_ref[...].astype(BF16), kpe_ref[...], wk_ref, wvt_ref, k_out, vt_out)


def _mla_cache(ckv, kpe_pad, wk, wvt):
    b, n, _ = ckv.shape
    return pl.pallas_call(
        _mla_cache_kernel,
        grid=(b,),
        in_specs=[
            pl.BlockSpec((None, n, MLA_KV_LORA), lambda i: (i, 0, 0)),
            pl.BlockSpec((None, n, LANES), lambda i: (i, 0, 0)),
            _const_spec(wk.shape), _const_spec(wvt.shape),
        ],
        out_specs=[
            pl.BlockSpec((None, n, N_HEADS * MLA_HEAD_PAD), lambda i: (i, 0, 0)),
            pl.BlockSpec((None, N_HEADS * HEAD_DIM, n), lambda i: (i, 0, 0)),
        ],
        out_shape=[
            jax.ShapeDtypeStruct((b, n, N_HEADS * MLA_HEAD_PAD), BF16),
            jax.ShapeDtypeStruct((b, N_HEADS * HEAD_DIM, n), BF16),
        ],
        compiler_params=pltpu.CompilerParams(
            dimension_semantics=("arbitrary",), vmem_limit_bytes=VMEM_LIMIT),
        name="mla_cache",
    )(ckv, kpe_pad, wk, wvt)


def _gqa_cache_kernel(k_ref, v_ref, k_out, vt_out):
    k_out[...] = k_ref[...].astype(BF16)
    _store_values_t(vt_out, v_ref[...].T, KV_HEADS)


def _gqa_cache(k, v):
    b, n, w = k.shape
    return pl.pallas_call(
        _gqa_cache_kernel,
        grid=(b,),
        in_specs=[pl.BlockSpec((None, n, w), lambda i: (i, 0, 0))] * 2,
        out_specs=[
            pl.BlockSpec((None, n, w), lambda i: (i, 0, 0)),
            pl.BlockSpec((None, KV_HEADS * HEAD_DIM, n), lambda i: (i, 0, 0)),
        ],
        out_shape=[
            jax.ShapeDtypeStruct((b, n, w), BF16),
            jax.ShapeDtypeStruct((b, KV_HEADS * HEAD_DIM, n), BF16),
        ],
        compiler_params=pltpu.CompilerParams(dimension_semantics=("arbitrary",)),
        name="gqa_cache",
    )(k, v)


def _attn_kernel(*refs, head_w, group, has_cache, has_sink, band, final, lag_b, lag_c, interleave,
                 subs, own_keys, seq):
    refs = list(refs)
    q_ref, kn_ref, vtn_ref = refs[:3]
    pos = 3
    if has_cache:
        kc_ref, vtc_ref = refs[pos:pos + 2]
        pos += 2
    gate_ref, x_ref, mod_ref, wo_ref = refs[pos:pos + 4]
    pos += 4
    if has_sink:
        sink_ref = refs[pos]
        pos += 1
    if final:
        fg_ref = refs[pos]
        pos += 1
    o_ref, ot_scr, s_scr, p_scr, g_scr, y_scr = refs[pos:pos + 6]
    if band:
        bias_scr = refs[pos + 6]

    tq = TOKEN_TILE
    ch = KEY_CHUNK
    slots = s_scr.shape[0]
    segs = []
    if has_cache:
        segs.append((kc_ref.shape[0],
                     lambda e, r, n, cols: kc_ref[r:r + n, cols],
                     lambda e, rows: vtc_ref[rows, :]))
    if own_keys:
        segs.append((seq,
                     lambda e, r, n, cols: kn_ref[e * seq + r:e * seq + r + n, cols],
                     lambda e, rows: vtn_ref[e, rows, :]))
    else:
        segs.append((seq,
                     lambda e, r, n, cols: kn_ref[r:r + n, cols],
                     lambda e, rows: vtn_ref[rows, :]))
    seg_rows = []
    chunks = []
    off = 0
    for si, (n_seg, _, _) in enumerate(segs):
        seg_rows.append((off, n_seg))
        chunks += [(si, r, off + r) for r in range(0, n_seg, ch)]
        off += n_seg
    if interleave:
        score_chunks = [c + (ch,) for c in chunks]
    else:
        score_chunks = [(si, 0, r0, n) for si, (r0, n) in enumerate(seg_rows)]
    own = len(segs) - 1
    if band:
        kpos = lax.broadcasted_iota(jnp.int32, (seq, tq), 0)
        for e in range(subs):
            tile = pl.program_id(1) * subs + e
            qpos = tile * tq + lax.broadcasted_iota(jnp.int32, (seq, tq), 1)
            bias_scr[e] = jnp.where(jnp.abs(kpos - qpos) <= WINDOW, 0.0, NEG_INF)

    assert lag_b >= 1 and lag_c >= 1 and slots > max(lag_b, lag_c)
    row_max = {}
    den_part = {}
    v_rows = vtn_ref.shape[-2] * group // N_HEADS
    den_on_mxu = v_rows > HEAD_DIM

    def scores_and_max(i):
        e, hd = divmod(i, N_HEADS)
        kh = hd // group
        qh = q_ref[e * tq:(e + 1) * tq, head_w * hd:head_w * (hd + 1)]
        mx = None
        for si, r, g, n in score_chunks:
            kk = segs[si][1](e, r, n, slice(head_w * kh, head_w * (kh + 1)))
            s = lax.dot_general(kk, qh, NT_DIMS, preferred_element_type=F32)
            if band and si == own:
                s = s + bias_scr[e, r:r + n, :]
            s_scr[i % slots, g:g + n, :] = s
            part = jnp.max(s.reshape(n // 8, 8, tq), axis=0)
            mx = part if mx is None else jnp.maximum(mx, part)
            yield
        m = jnp.max(mx, axis=0, keepdims=True)
        if has_sink:
            m = jnp.maximum(m, sink_ref[hd] * LOG2E)
        row_max[i] = m

    def exponentials(i):
        hd = i % N_HEADS
        m = row_max.pop(i)
        mb = jnp.broadcast_to(m, (8, tq))
        tot = None
        for _, _, g in chunks:
            p = jnp.exp2(s_scr[i % slots, g:g + ch, :].reshape(ch // 8, 8, tq) - mb)
            if not den_on_mxu:
                part = jnp.sum(p, axis=0)
                tot = part if tot is None else tot + part
            p_scr[i % slots, g:g + ch, :] = p.reshape(ch, tq).astype(BF16)
            yield
        den = None if den_on_mxu else jnp.sum(tot, axis=0, keepdims=True)
        if has_sink:
            sink = jnp.exp2(sink_ref[hd] * LOG2E - m)
            den = sink if den is None else den + sink
        den_part[i] = den

    def weighted_values(i):
        e, hd = divmod(i, N_HEADS)
        kh = hd // group
        acc = None
        for si, (_, _, values_t) in enumerate(segs):
            r0, n = seg_rows[si]
            vt = values_t(e, slice(v_rows * kh, v_rows * (kh + 1)))
            part = jnp.dot(vt, p_scr[i % slots, r0:r0 + n, :], preferred_element_type=F32)
            acc = part if acc is None else acc + part
            yield
        den = den_part.pop(i)
        if den_on_mxu:
            mxu_sum = acc[HEAD_DIM:HEAD_DIM + 1, :]
            den = mxu_sum if den is None else den + mxu_sum
        ot_scr[e, HEAD_DIM * hd:HEAD_DIM * (hd + 1), :] = acc[:HEAD_DIM, :] / den

    def finish(e):
        rows = slice(e * tq, (e + 1) * tq)
        for c0 in range(0, D_MODEL, OUT_BLOCK):
            cols = slice(c0, c0 + OUT_BLOCK)
            o = ot_scr[e, cols, :].T
            g_scr[:, cols] = (o * gate_ref[rows, cols].astype(F32)).astype(BF16)
            yield
        for c0 in range(0, D_MODEL, OUT_BLOCK):
            cols = slice(c0, c0 + OUT_BLOCK)
            y = jnp.dot(g_scr[...], wo_ref[:, cols], preferred_element_type=F32)
            xn = x_ref[rows, cols] + mod_ref[:, 2 * D_MODEL + c0:2 * D_MODEL + c0 + OUT_BLOCK] * y
            if final:
                y_scr[:, cols] = xn
            else:
                o_ref[rows, cols] = xn
            yield
        if final:
            o_ref[rows, :] = _rms(y_scr[...], fg_ref[...])

    n_items = subs * N_HEADS
    for step in range(n_items + lag_b + lag_c + 1):
        live = []
        if step < n_items:
            live.append(scores_and_max(step))
        if 0 <= step - lag_b < n_items:
            live.append(exponentials(step - lag_b))
        done = step - lag_b - lag_c
        if 0 <= done < n_items:
            live.append(weighted_values(done))
        if done >= N_HEADS and done % N_HEADS == 0:
            live.append(finish(done // N_HEADS - 1))
        if not interleave:
            for g in live:
                for _ in g:
                    pass
            live = []
        while live:
            live = [g for g in live if next(g, True) is None]


def _attend(kind, q, k_new, vt_new, cache, gate, x, mods_l, w_out, sink, final_g, *, batch, seq, ctx,
            lags):
    lag_b, lag_c, interleave, subs = lags
    tq = TOKEN_TILE
    n_tok = batch * seq
    band = kind == 2 and not ctx
    head_w = MLA_HEAD_PAD if kind == 0 else HEAD_DIM
    kw = k_new.shape[1]
    vw = vt_new.shape[1]
    own_keys = cache is None
    assert seq == tq if own_keys else seq % (subs * tq) == 0
    grid = (batch // subs, 1) if own_keys else (batch, seq // (subs * tq))
    tok_spec = lambda w: pl.BlockSpec((subs * tq, w), lambda b, t: (b * grid[1] + t, 0))
    if own_keys:
        in_specs = [
            tok_spec(q.shape[1]),
            pl.BlockSpec((subs * seq, kw), lambda b, t: (b, 0)),
            pl.BlockSpec((subs, vw, seq), lambda b, t: (b, 0, 0)),
        ]
    else:
        in_specs = [
            tok_spec(q.shape[1]),
            pl.BlockSpec((seq, kw), lambda b, t: (b, 0)),
            pl.BlockSpec((None, vw, seq), lambda b, t: (b, 0, 0)),
        ]
    args = [q, k_new, vt_new]
    n_keys = seq
    if cache is not None:
        kc, vtc = cache
        n_c = kc.shape[1]
        n_keys += n_c
        in_specs += [
            pl.BlockSpec((None, n_c, kw), lambda b, t: (b, 0, 0)),
            pl.BlockSpec((None, vw, n_c), lambda b, t: (b, 0, 0)),
        ]
        args += [kc, vtc]
    mod_row = (lambda b, t: (4, 0, 0)) if ctx else (lambda b, t: (b, 0, 0))
    in_specs += [
        tok_spec(D_MODEL),
        tok_spec(D_MODEL),
        pl.BlockSpec((None, 1, 3 * D_MODEL), mod_row),
        pl.BlockSpec((D_MODEL, D_MODEL), lambda b, t: (0, 0)),
    ]
    args += [gate, x, mods_l, w_out]
    if sink is not None:
        in_specs.append(pl.BlockSpec(memory_space=pltpu.SMEM))
        args.append(sink)
    if final_g is not None:
        in_specs.append(pl.BlockSpec((1, D_MODEL), lambda b, t: (0, 0)))
        args.append(final_g.reshape(1, D_MODEL))
    body = functools.partial(
        _attn_kernel,
        head_w=head_w,
        group=1 if kind == 0 else GROUPS,
        has_cache=cache is not None,
        has_sink=sink is not None,
        band=band,
        final=final_g is not None,
        lag_b=lag_b,
        lag_c=lag_c,
        interleave=interleave,
        subs=subs,
        own_keys=own_keys,
        seq=seq,
    )
    slots = max(lag_b, lag_c) + 1
    scratch = [pltpu.VMEM((subs, N_HEADS * HEAD_DIM, tq), F32),
               pltpu.VMEM((slots, n_keys, tq), F32),
               pltpu.VMEM((slots, n_keys, tq), BF16),
               pltpu.VMEM((tq, D_MODEL), BF16),
               pltpu.VMEM((tq, D_MODEL), F32)]
    if band:
        scratch.append(pltpu.VMEM((subs, seq, tq), F32))
    return pl.pallas_call(
        body,
        grid=grid,
        in_specs=in_specs,
        out_specs=tok_spec(D_MODEL),
        out_shape=jax.ShapeDtypeStruct((n_tok, D_MODEL), F32),
        scratch_shapes=scratch,
        compiler_params=pltpu.CompilerParams(
            dimension_semantics=("arbitrary", "arbitrary"), vmem_limit_bytes=VMEM_LIMIT),
        name=f"attn_k{kind}_{'ctx' if ctx else 'lat'}",
    )(*args)


def _rope_lane_tables(seq, rot_dim):
    rows = seq // GRID_W
    row = jnp.repeat(jnp.arange(rows), GRID_W).astype(F32)
    col = jnp.tile(jnp.arange(GRID_W), rows).astype(F32)
    nf = rot_dim // 4
    freqs = ROPE_THETA ** (-jnp.arange(nf, dtype=F32) / nf)
    ang = jnp.concatenate([row[:, None] * freqs, col[:, None] * freqs], axis=-1)
    cos, sin = jnp.cos(ang), jnp.sin(ang)
    cos_l = jnp.repeat(cos, 2, axis=-1)
    sin_l = jnp.stack([-sin, sin], axis=-1).reshape(seq, rot_dim)
    return cos_l, sin_l


def _embed_lanes(table, start, fill):
    seq, w = table.shape
    return jnp.concatenate(
        [jnp.full((seq, start), fill, F32), table, jnp.full((seq, LANES - start - w), fill, F32)], axis=1)


def _mla_weights(w_in, q_norm, w_uq, kv_norm, w_ukv):
    c0 = MLA_Q_LORA + MLA_KV_LORA
    win = jnp.concatenate(
        [w_in[:, :c0], w_in[:, c0 + MLA_ROPE:], w_in[:, c0:c0 + MLA_ROPE],
         jnp.zeros((D_MODEL, MLA_IN_PAD - w_in.shape[1]), F32)], axis=1).astype(BF16)
    hq = MLA_NOPE + MLA_ROPE
    wuq = jnp.pad(w_uq.reshape(MLA_Q_LORA, N_HEADS, hq),
                  ((0, 0), (0, 0), (0, MLA_HEAD_PAD - hq))).reshape(MLA_Q_LORA, -1).astype(BF16)
    wkv = w_ukv.reshape(MLA_KV_LORA, N_HEADS, MLA_NOPE + MLA_V)
    wk = jnp.pad(wkv[:, :, :MLA_NOPE],
                 ((0, 0), (0, 0), (0, MLA_HEAD_PAD - MLA_NOPE))).reshape(MLA_KV_LORA, -1).astype(BF16)
    wvt = wkv[:, :, MLA_NOPE:].reshape(MLA_KV_LORA, -1).T.astype(BF16)
    return [win, q_norm.reshape(1, -1), wuq, kv_norm.reshape(1, -1), wk, wvt]


def kernel(x_prompt, x_sample, cache_mla_ckv, cache_mla_kpe, cache_gqa_k, cache_gqa_v,
           cache_swa_k, cache_swa_v, c, c_ctx, norm_g, w_ada, b_ada, w_out,
           mla_w_in, mla_q_norm, mla_w_uq, mla_kv_norm, mla_w_ukv,
           gqa_w_in, gqa_q_norm, gqa_k_norm, swa_w_in, swa_sink, final_norm_g):
    bc, sc, _ = x_prompt.shape
    bl, sl, _ = x_sample.shape
    n_past = cache_mla_ckv.shape[2]
    kvw = KV_HEADS * HEAD_DIM

    cc = jnp.concatenate([c, c_ctx[None, :], jnp.zeros((8 - bl - 1, D_MODEL), F32)], axis=0)
    mods = _ada_mods(cc, w_ada, b_ada).reshape(DEPTH, 8, 1, 3 * D_MODEL)

    cos_h, sin_h = _rope_lane_tables(sl, HEAD_DIM)
    gqa_tables = [jnp.tile(cos_h, (1, 2)), jnp.tile(sin_h, (1, 2))]
    cos_r, sin_r = _rope_lane_tables(sl, MLA_ROPE)
    mla_tables = [_embed_lanes(cos_r, MLA_NOPE, 1.0), _embed_lanes(sin_r, MLA_NOPE, 0.0),
                  _embed_lanes(cos_r, 0, 1.0), _embed_lanes(sin_r, 0, 0.0)]

    xc = x_prompt.reshape(bc * sc, D_MODEL)
    xl = x_sample.reshape(bl * sl, D_MODEL)
    new_caches = {0: ([], []), 1: ([], []), 2: ([], [])}
    for i in range(DEPTH):
        kind, j = i % 3, i // 3
        mods_l = mods[i]
        wo = w_out[i].astype(BF16)
        final_g = final_norm_g if i == DEPTH - 1 else None
        sink = None
        if kind == 0:
            weights = _mla_weights(mla_w_in[j], mla_q_norm[j], mla_w_uq[j], mla_kv_norm[j], mla_w_ukv[j])
            tables = mla_tables
            kpe_pad = jnp.pad(cache_mla_kpe[:, j], ((0, 0), (0, 0), (0, LANES - MLA_ROPE)))
            cache = _mla_cache(cache_mla_ckv[:, j], kpe_pad, weights[4], weights[5])
        else:
            tables = gqa_tables
            if kind == 1:
                weights = [gqa_w_in[j].astype(BF16),
                           jnp.tile(gqa_q_norm[j], 2).reshape(1, LANES),
                           jnp.tile(gqa_k_norm[j], 2).reshape(1, LANES)]
                cache = _gqa_cache(cache_gqa_k[:, j].reshape(bl, n_past, kvw),
                                   cache_gqa_v[:, j].reshape(bl, n_past, kvw))
            else:
                weights = [swa_w_in[j].astype(BF16)]
                sink = swa_sink[j]
                cache = _gqa_cache(cache_swa_k[:, j].reshape(bl, n_past, kvw),
                                   cache_swa_v[:, j].reshape(bl, n_past, kvw))

        q, k, vt, gate, c_a, c_b = _project(kind, xc, mods_l, norm_g[i], weights, tables,
                                            batch=bc, seq=sc, ctx=True)
        new_caches[kind][0].append(c_a)
        new_caches[kind][1].append(c_b)
        xc = _attend(kind, q, k, vt, None, gate, xc, mods_l, wo, sink, final_g,
                     batch=bc, seq=sc, ctx=True, lags=CTX_STAGE_LAGS)

        q, k, vt, gate = _project(kind, xl, mods_l, norm_g[i], weights, tables,
                                  batch=bl, seq=sl, ctx=False)
        xl = _attend(kind, q, k, vt, cache, gate, xl, mods_l, wo, sink, final_g,
                     batch=bl, seq=sl, ctx=False, lags=LAT_STAGE_LAGS)

    def stack(parts, tail):
        return jnp.stack([p.reshape((bc, sc) + tail) for p in parts], axis=1)

    return (xc.reshape(bc, sc, D_MODEL), xl.reshape(bl, sl, D_MODEL),
            stack(new_caches[0][0], (MLA_KV_LORA,)), stack(new_caches[0][1], (MLA_ROPE,)),
            stack(new_caches[1][0], (KV_HEADS, HEAD_DIM)), stack(new_caches[1][1], (KV_HEADS, HEAD_DIM)),
            stack(new_caches[2][0], (KV_HEADS, HEAD_DIM)), stack(new_caches[2][1], (KV_HEADS, HEAD_DIM)))
```

```python
import functools

import jax
import jax.numpy as jnp
from jax import lax
from jax.experimental import pallas as pl
from jax.experimental.pallas import tpu as pltpu

F32 = jnp.float32
BF16 = jnp.bfloat16

D_MODEL = 1024
DEPTH = 4
GRID_W = 64
N_HEADS = 16
HEAD_DIM = 64
KV_HEADS = 4
GROUPS = N_HEADS // KV_HEADS
MLA_Q_LORA = 384
MLA_KV_LORA = 256
MLA_NOPE = 64
MLA_ROPE = 32
MLA_V = 64
WINDOW = 128
ROPE_THETA = 10000.0
EPS = 1e-6
NEG_INF = -1e30

LOG2E = 1.4426950408889634
GQA_Q_SCALE = HEAD_DIM ** -0.5 * LOG2E
MLA_Q_SCALE = (MLA_NOPE + MLA_ROPE) ** -0.5 * LOG2E

LANES = 128
TOKEN_TILE = 256
PROJ_TILE = 512
KEY_CHUNK = 256
OUT_BLOCK = 256
CTX_STAGE_LAGS = (3, 3, True, 2)
LAT_STAGE_LAGS = (2, 2, False, 2)
MLA_HEAD_PAD = 128
V_ROWS = HEAD_DIM + 16
MLA_IN_PAD = 1792
VMEM_LIMIT = 56 * 1024 * 1024
NT_DIMS = (((1,), (1,)), ((), ()))


def _silu(v):
    return v * jax.nn.sigmoid(v)


def _rms(v, g):
    return v * lax.rsqrt(jnp.mean(v * v, axis=-1, keepdims=True) + EPS) * g


def _mod_norm(x, g, mod):
    shift = mod[:, :D_MODEL]
    scale = mod[:, D_MODEL:2 * D_MODEL]
    return _rms(x, g) * (1.0 + scale) + shift


def _rope_tile(v, cos, sin_signed):
    lane = lax.broadcasted_iota(jnp.int32, v.shape, 1)
    nxt = pltpu.roll(v, LANES - 1, axis=1)
    prv = pltpu.roll(v, 1, axis=1)
    swapped = jnp.where((lane & 1) == 0, nxt, prv)
    return v * cos + swapped * sin_signed


def _ada_kernel(c_ref, w_ref, b_ref, o_ref):
    s = _silu(c_ref[...]).astype(BF16)
    o_ref[...] = jnp.dot(s, w_ref[...].astype(BF16), preferred_element_type=F32) + b_ref[...]


def _ada_mods(cc, w_ada, b_ada):
    tn = D_MODEL
    return pl.pallas_call(
        _ada_kernel,
        grid=(DEPTH, 3 * D_MODEL // tn),
        in_specs=[
            pl.BlockSpec((8, D_MODEL), lambda i, n: (0, 0)),
            pl.BlockSpec((None, D_MODEL, tn), lambda i, n: (i, 0, n)),
            pl.BlockSpec((None, 1, tn), lambda i, n: (i, 0, n)),
        ],
        out_specs=pl.BlockSpec((None, 8, tn), lambda i, n: (i, 0, n)),
        out_shape=jax.ShapeDtypeStruct((DEPTH, 8, 3 * D_MODEL), F32),
        compiler_params=pltpu.CompilerParams(vmem_limit_bytes=VMEM_LIMIT),
        name="ada_mods",
    )(cc, w_ada, b_ada.reshape(DEPTH, 1, 3 * D_MODEL))


def _store_queries_t(qt_out, row0, blk):
    blk_t = blk.T.astype(BF16)
    w = blk.shape[1]
    if len(qt_out.shape) == 3:
        seq = qt_out.shape[2]
        for e in range(qt_out.shape[0]):
            qt_out[e, row0:row0 + w, :] = blk_t[:, e * seq:(e + 1) * seq]
    else:
        qt_out[row0:row0 + w, :] = blk_t


def _store_values_t(vt_out, vt, n_heads):
    if len(vt_out.shape) == 3:
        seq = vt_out.shape[2]
        for e in range(vt_out.shape[0]):
            _store_values_t(vt_out.at[e], vt[:, e * seq:(e + 1) * seq], n_heads)
        return
    v_rows = vt_out.shape[0] // n_heads
    if v_rows == HEAD_DIM:
        vt_out[...] = vt.astype(BF16)
        return
    ones = jnp.ones((v_rows - HEAD_DIM, vt.shape[1]), BF16)
    for h in range(n_heads):
        vt_out[v_rows * h:v_rows * h + HEAD_DIM, :] = vt[HEAD_DIM * h:HEAD_DIM * (h + 1), :].astype(BF16)
        vt_out[v_rows * h + HEAD_DIM:v_rows * (h + 1), :] = ones


def _mla_expand(ckv_bf, kpe, wk_ref, wvt_ref, k_out, vt_out):
    k = jnp.dot(ckv_bf, wk_ref[...], preferred_element_type=F32)
    lane = lax.broadcasted_iota(jnp.int32, kpe.shape, 1)
    rope_lanes = (lane >= MLA_NOPE) & (lane < MLA_NOPE + MLA_ROPE)
    kpe_at_rope = pltpu.roll(kpe, MLA_NOPE, axis=1)
    for hd in range(N_HEADS):
        cols = slice(MLA_HEAD_PAD * hd, MLA_HEAD_PAD * (hd + 1))
        k_out[:, cols] = jnp.where(rope_lanes, kpe_at_rope, k[:, cols]).astype(BF16)
    vt = lax.dot_general(wvt_ref[...], ckv_bf, NT_DIMS, preferred_element_type=F32)
    _store_values_t(vt_out, vt, N_HEADS)


def _proj_mla_kernel(*refs, rope, ctx):
    refs = list(refs)
    x_ref, mod_ref, g_ref, win_ref, qn_ref, wuq_ref, kvn_ref, wk_ref, wvt_ref = refs[:9]
    pos = 9
    if rope:
        cq_ref, sq_ref, ck_ref, sk_ref = refs[pos:pos + 4]
        pos += 4
    q_out, k_out, vt_out, gate_out = refs[pos:pos + 4]
    pos += 4
    if ctx:
        ckv_out, kpe_out = refs[pos:pos + 2]

    h = _mod_norm(x_ref[...], g_ref[...], mod_ref[...]).astype(BF16)
    z = jnp.dot(h, win_ref[...], preferred_element_type=F32)
    cq = z[:, :MLA_Q_LORA]
    ckv = z[:, MLA_Q_LORA:MLA_Q_LORA + MLA_KV_LORA]
    gate = z[:, 640:640 + D_MODEL]
    kpe = z[:, 640 + D_MODEL:]
    gate_out[...] = _silu(gate).astype(BF16)

    qf = jnp.dot(_rms(cq, qn_ref[...]).astype(BF16), wuq_ref[...], preferred_element_type=F32)
    ckvn = _rms(ckv, kvn_ref[...])
    if ctx:
        ckv_out[...] = ckvn
        kpe_out[...] = kpe[:, :MLA_ROPE]
    if rope:
        kpe = _rope_tile(kpe, ck_ref[...], sk_ref[...])
    _mla_expand(ckvn.astype(BF16), kpe, wk_ref, wvt_ref, k_out, vt_out)
    for hd in range(N_HEADS):
        blk = qf[:, MLA_HEAD_PAD * hd:MLA_HEAD_PAD * (hd + 1)]
        if rope:
            blk = _rope_tile(blk, cq_ref[...], sq_ref[...])
        _store_queries_t(q_out, MLA_HEAD_PAD * hd, blk * MLA_Q_SCALE)


def _proj_gqa_kernel(*refs, qk_norm, rope, ctx):
    refs = list(refs)
    x_ref, mod_ref, g_ref, win_ref = refs[:4]
    pos = 4
    if qk_norm:
        qn_ref, kn_ref = refs[pos:pos + 2]
        pos += 2
    if rope:
        cos_ref, sin_ref = refs[pos:pos + 2]
        pos += 2
    q_out, k_out, vt_out, gate_out = refs[pos:pos + 4]
    pos += 4
    if ctx:
        kc_out, vc_out = refs[pos:pos + 2]

    nq = N_HEADS * HEAD_DIM
    nk = KV_HEADS * HEAD_DIM
    h = _mod_norm(x_ref[...], g_ref[...], mod_ref[...]).astype(BF16)
    z = jnp.dot(h, win_ref[...], preferred_element_type=F32)
    tm = z.shape[0]
    lo = lax.broadcasted_iota(jnp.int32, (tm, LANES), 1) < HEAD_DIM
    n_q_tiles = nq // LANES
    for c in range((nq + nk) // LANES):
        blk = z[:, LANES * c:LANES * (c + 1)]
        is_q = c < n_q_tiles
        if qk_norm:
            sq = blk * blk
            s_lo = jnp.sum(jnp.where(lo, sq, 0.0), axis=-1, keepdims=True)
            s_hi = jnp.sum(jnp.where(lo, 0.0, sq), axis=-1, keepdims=True)
            inv = jnp.where(lo, lax.rsqrt(s_lo * (1.0 / HEAD_DIM) + EPS),
                            lax.rsqrt(s_hi * (1.0 / HEAD_DIM) + EPS))
            blk = blk * inv * (qn_ref[...] if is_q else kn_ref[...])
        if ctx and not is_q:
            kc_out[:, LANES * (c - n_q_tiles):LANES * (c - n_q_tiles + 1)] = blk
        if rope:
            blk = _rope_tile(blk, cos_ref[...], sin_ref[...])
        if is_q:
            _store_queries_t(q_out, LANES * c, blk * GQA_Q_SCALE)
        else:
            k_out[:, LANES * (c - n_q_tiles):LANES * (c - n_q_tiles + 1)] = blk.astype(BF16)
    v = z[:, nq + nk:nq + 2 * nk]
    if ctx:
        vc_out[...] = v
    _store_values_t(vt_out, v.T, KV_HEADS)
    gate_out[...] = _silu(z[:, nq + 2 * nk:]).astype(BF16)


def _const_spec(shape):
    return pl.BlockSpec(shape, lambda i: (0,) * len(shape))


def _project(kind, x, mods_l, norm_g, weights, tables, *, batch, seq, ctx):
    n_tok = batch * seq
    tm = PROJ_TILE
    tiles_per_seq = max(seq // tm, 1)
    seqs_per_tile = max(tm // seq, 1)
    assert tm == seqs_per_tile * seq // tiles_per_seq and (ctx or seqs_per_tile == 1)
    v_rows = V_ROWS if ctx else HEAD_DIM
    rope = not ctx
    mod_row = (lambda i: (4, 0, 0)) if ctx else (lambda i: (i // tiles_per_seq, 0, 0))
    in_specs = [
        pl.BlockSpec((tm, D_MODEL), lambda i: (i, 0)),
        pl.BlockSpec((None, 1, 3 * D_MODEL), mod_row),
        _const_spec((1, D_MODEL)),
    ]
    args = [x, mods_l, norm_g.reshape(1, D_MODEL)]
    for w in weights:
        in_specs.append(_const_spec(w.shape))
        args.append(w)
    if rope:
        for t in tables:
            in_specs.append(pl.BlockSpec((tm, LANES), lambda i: (i % tiles_per_seq, 0)))
            args.append(t)
    if kind == 0:
        qw, kw, vw = N_HEADS * MLA_HEAD_PAD, N_HEADS * MLA_HEAD_PAD, N_HEADS * v_rows
        body = functools.partial(_proj_mla_kernel, rope=rope, ctx=ctx)
        cache_shapes = [(n_tok, MLA_KV_LORA), (n_tok, MLA_ROPE)]
    else:
        qw, kw, vw = N_HEADS * HEAD_DIM, KV_HEADS * HEAD_DIM, KV_HEADS * v_rows
        body = functools.partial(_proj_gqa_kernel, qk_norm=(kind == 1), rope=rope, ctx=ctx)
        cache_shapes = [(n_tok, kw), (n_tok, kw)]
    def feature_major_spec(w):
        if seqs_per_tile > 1:
            return pl.BlockSpec((seqs_per_tile, w, seq), lambda i: (i, 0, 0))
        return pl.BlockSpec((None, w, tm), lambda i: (i // tiles_per_seq, 0, i % tiles_per_seq))

    out_shape = [
        jax.ShapeDtypeStruct((batch, qw, seq), BF16),
        jax.ShapeDtypeStruct((n_tok, kw), BF16),
        jax.ShapeDtypeStruct((batch, vw, seq), BF16),
        jax.ShapeDtypeStruct((n_tok, D_MODEL), BF16),
    ]
    out_specs = [
        feature_major_spec(qw),
        pl.BlockSpec((tm, kw), lambda i: (i, 0)),
        feature_major_spec(vw),
        pl.BlockSpec((tm, D_MODEL), lambda i: (i, 0)),
    ]
    if ctx:
        for s in cache_shapes:
            out_shape.append(jax.ShapeDtypeStruct(s, F32))
            out_specs.append(pl.BlockSpec((tm, s[1]), lambda i: (i, 0)))
    return pl.pallas_call(
        body,
        grid=(n_tok // tm,),
        in_specs=in_specs,
        out_specs=out_specs,
        out_shape=out_shape,
        compiler_params=pltpu.CompilerParams(
            dimension_semantics=("arbitrary",), vmem_limit_bytes=VMEM_LIMIT),
        name=f"proj_k{kind}_{'ctx' if ctx else 'lat'}",
    )(*args)


def _mla_cache_kernel(ckv_ref, kpe_ref, wk_ref, wvt_ref, k_out, vt_out):
    _mla_expand(ckv_ref[...].astype(BF16), kpe_ref[...], wk_ref, wvt_ref, k_out, vt_out)


def _mla_cache(ckv, kpe_pad, wk, wvt):
    b, n, _ = ckv.shape
    return pl.pallas_call(
        _mla_cache_kernel,
        grid=(b,),
        in_specs=[
            pl.BlockSpec((None, n, MLA_KV_LORA), lambda i: (i, 0, 0)),
            pl.BlockSpec((None, n, LANES), lambda i: (i, 0, 0)),
            _const_spec(wk.shape), _const_spec(wvt.shape),
        ],
        out_specs=[
            pl.BlockSpec((None, n, N_HEADS * MLA_HEAD_PAD), lambda i: (i, 0, 0)),
            pl.BlockSpec((None, N_HEADS * HEAD_DIM, n), lambda i: (i, 0, 0)),
        ],
        out_shape=[
            jax.ShapeDtypeStruct((b, n, N_HEADS * MLA_HEAD_PAD), BF16),
            jax.ShapeDtypeStruct((b, N_HEADS * HEAD_DIM, n), BF16),
        ],
        compiler_params=pltpu.CompilerParams(
            dimension_semantics=("arbitrary",), vmem_limit_bytes=VMEM_LIMIT),
        name="mla_cache",
    )(ckv, kpe_pad, wk, wvt)


def _gqa_cache_kernel(k_ref, v_ref, k_out, vt_out):
    k_out[...] = k_ref[...].astype(BF16)
    _store_values_t(vt_out, v_ref[...].T, KV_HEADS)


def _gqa_cache(k, v):
    b, n, w = k.shape
    return pl.pallas_call(
        _gqa_cache_kernel,
        grid=(b,),
        in_specs=[pl.BlockSpec((None, n, w), lambda i: (i, 0, 0))] * 2,
        out_specs=[
            pl.BlockSpec((None, n, w), lambda i: (i, 0, 0)),
            pl.BlockSpec((None, KV_HEADS * HEAD_DIM, n), lambda i: (i, 0, 0)),
        ],
        out_shape=[
            jax.ShapeDtypeStruct((b, n, w), BF16),
            jax.ShapeDtypeStruct((b, KV_HEADS * HEAD_DIM, n), BF16),
        ],
        compiler_params=pltpu.CompilerParams(dimension_semantics=("arbitrary",)),
        name="gqa_cache",
    )(k, v)


def _attn_kernel(*refs, head_w, group, has_cache, has_sink, band, final, lag_b, lag_c, interleave,
                 subs, own_keys, seq):
    refs = list(refs)
    q_ref, kn_ref, vtn_ref = refs[:3]
    pos = 3
    if has_cache:
        kc_ref, vtc_ref = refs[pos:pos + 2]
        pos += 2
    gate_ref, x_ref, mod_ref, wo_ref = refs[pos:pos + 4]
    pos += 4
    if has_sink:
        sink_ref = refs[pos]
        pos += 1
    if final:
        fg_ref = refs[pos]
        pos += 1
    o_ref, ot_scr, s_scr, p_scr, g_scr, y_scr = refs[pos:pos + 6]
    if band:
        bias_scr = refs[pos + 6]

    tq = TOKEN_TILE
    ch = KEY_CHUNK
    slots = s_scr.shape[0]
    segs = []
    if has_cache:
        segs.append((kc_ref.shape[0],
                     lambda e, r, n, cols: kc_ref[r:r + n, cols],
                     lambda e, rows: vtc_ref[rows, :]))
    if own_keys:
        segs.append((seq,
                     lambda e, r, n, cols: kn_ref[e * seq + r:e * seq + r + n, cols],
                     lambda e, rows: vtn_ref[e, rows, :]))
    else:
        segs.append((seq,
                     lambda e, r, n, cols: kn_ref[r:r + n, cols],
                     lambda e, rows: vtn_ref[rows, :]))
    seg_rows = []
    chunks = []
    off = 0
    for si, (n_seg, _, _) in enumerate(segs):
        seg_rows.append((off, n_seg))
        chunks += [(si, r, off + r) for r in range(0, n_seg, ch)]
        off += n_seg
    if interleave:
        score_chunks = [c + (ch,) for c in chunks]
    else:
        score_chunks = [(si, 0, r0, n) for si, (r0, n) in enumerate(seg_rows)]
    own = len(segs) - 1
    if band:
        kpos = lax.broadcasted_iota(jnp.int32, (seq, tq), 0)
        for e in range(subs):
            tile = pl.program_id(1) * subs + e
            qpos = tile * tq + lax.broadcasted_iota(jnp.int32, (seq, tq), 1)
            bias_scr[e] = jnp.where(jnp.abs(kpos - qpos) <= WINDOW, 0.0, NEG_INF)

    assert lag_b >= 1 and lag_c >= 1 and slots > max(lag_b, lag_c)
    row_max = {}
    den_part = {}
    v_rows = vtn_ref.shape[-2] * group // N_HEADS
    den_on_mxu = v_rows > HEAD_DIM

    def scores_and_max(i):
        e, hd = divmod(i, N_HEADS)
        kh = hd // group
        q_rows = slice(head_w * hd, head_w * (hd + 1))
        qh_t = q_ref[e, q_rows, :] if own_keys else q_ref[q_rows, e * tq:(e + 1) * tq]
        mx = None
        for si, r, g, n in score_chunks:
            kk = segs[si][1](e, r, n, slice(head_w * kh, head_w * (kh + 1)))
            s = jnp.dot(kk, qh_t, preferred_element_type=F32)
            if band and si == own:
                s = s + bias_scr[e, r:r + n, :]
            s_scr[i % slots, g:g + n, :] = s
            part = jnp.max(s.reshape(n // 8, 8, tq), axis=0)
            mx = part if mx is None else jnp.maximum(mx, part)
            yield
        m = jnp.max(mx, axis=0, keepdims=True)
        if has_sink:
            m = jnp.maximum(m, sink_ref[hd] * LOG2E)
        row_max[i] = m

    def exponentials(i):
        hd = i % N_HEADS
        m = row_max.pop(i)
        mb = jnp.broadcast_to(m, (8, tq))
        tot = None
        for _, _, g in chunks:
            p = jnp.exp2(s_scr[i % slots, g:g + ch, :].reshape(ch // 8, 8, tq) - mb)
            if not den_on_mxu:
                part = jnp.sum(p, axis=0)
                tot = part if tot is None else tot + part
            p_scr[i % slots, g:g + ch, :] = p.reshape(ch, tq).astype(BF16)
            yield
        den = None if den_on_mxu else jnp.sum(tot, axis=0, keepdims=True)
        if has_sink:
            sink = jnp.exp2(sink_ref[hd] * LOG2E - m)
            den = sink if den is None else den + sink
        den_part[i] = den

    def weighted_values(i):
        e, hd = divmod(i, N_HEADS)
        kh = hd // group
        acc = None
        for si, (_, _, values_t) in enumerate(segs):
            r0, n = seg_rows[si]
            vt = values_t(e, slice(v_rows * kh, v_rows * (kh + 1)))
            part = jnp.dot(vt, p_scr[i % slots, r0:r0 + n, :], preferred_element_type=F32)
            acc = part if acc is None else acc + part
            yield
        den = den_part.pop(i)
        if den_on_mxu:
            mxu_sum = acc[HEAD_DIM:HEAD_DIM + 1, :]
            den = mxu_sum if den is None else den + mxu_sum
        ot_scr[e, HEAD_DIM * hd:HEAD_DIM * (hd + 1), :] = acc[:HEAD_DIM, :] / den

    def finish(e):
        rows = slice(e * tq, (e + 1) * tq)
        for c0 in range(0, D_MODEL, OUT_BLOCK):
            cols = slice(c0, c0 + OUT_BLOCK)
            o = ot_scr[e, cols, :].T
            g_scr[:, cols] = (o * gate_ref[rows, cols].astype(F32)).astype(BF16)
            yield
        for c0 in range(0, D_MODEL, OUT_BLOCK):
            cols = slice(c0, c0 + OUT_BLOCK)
            y = jnp.dot(g_scr[...], wo_ref[:, cols], preferred_element_type=F32)
            xn = x_ref[rows, cols] + mod_ref[:, 2 * D_MODEL + c0:2 * D_MODEL + c0 + OUT_BLOCK] * y
            if final:
                y_scr[:, cols] = xn
            else:
                o_ref[rows, cols] = xn
            yield
        if final:
            o_ref[rows, :] = _rms(y_scr[...], fg_ref[...])

    n_items = subs * N_HEADS
    for step in range(n_items + lag_b + lag_c + 1):
        live = []
        if step < n_items:
            live.append(scores_and_max(step))
        if 0 <= step - lag_b < n_items:
            live.append(exponentials(step - lag_b))
        done = step - lag_b - lag_c
        if 0 <= done < n_items:
            live.append(weighted_values(done))
        if done >= N_HEADS and done % N_HEADS == 0:
            live.append(finish(done // N_HEADS - 1))
        if not interleave:
            for g in live:
                for _ in g:
                    pass
            live = []
        while live:
            live = [g for g in live if next(g, True) is None]


def _attend(kind, q, k_new, vt_new, cache, gate, x, mods_l, w_out, sink, final_g, *, batch, seq, ctx,
            lags):
    lag_b, lag_c, interleave, subs = lags
    tq = TOKEN_TILE
    n_tok = batch * seq
    band = kind == 2 and not ctx
    head_w = MLA_HEAD_PAD if kind == 0 else HEAD_DIM
    kw = k_new.shape[1]
    vw = vt_new.shape[1]
    own_keys = cache is None
    assert seq == tq if own_keys else seq % (subs * tq) == 0
    grid = (batch // subs, 1) if own_keys else (batch, seq // (subs * tq))
    tok_spec = lambda w: pl.BlockSpec((subs * tq, w), lambda b, t: (b * grid[1] + t, 0))
    qw = q.shape[1]
    if own_keys:
        in_specs = [
            pl.BlockSpec((subs, qw, seq), lambda b, t: (b, 0, 0)),
            pl.BlockSpec((subs * seq, kw), lambda b, t: (b, 0)),
            pl.BlockSpec((subs, vw, seq), lambda b, t: (b, 0, 0)),
        ]
    else:
        in_specs = [
            pl.BlockSpec((None, qw, subs * tq), lambda b, t: (b, 0, t)),
            pl.BlockSpec((seq, kw), lambda b, t: (b, 0)),
            pl.BlockSpec((None, vw, seq), lambda b, t: (b, 0, 0)),
        ]
    args = [q, k_new, vt_new]
    n_keys = seq
    if cache is not None:
        kc, vtc = cache
        n_c = kc.shape[1]
        n_keys += n_c
        in_specs += [
            pl.BlockSpec((None, n_c, kw), lambda b, t: (b, 0, 0)),
            pl.BlockSpec((None, vw, n_c), lambda b, t: (b, 0, 0)),
        ]
        args += [kc, vtc]
    mod_row = (lambda b, t: (4, 0, 0)) if ctx else (lambda b, t: (b, 0, 0))
    in_specs += [
        tok_spec(D_MODEL),
        tok_spec(D_MODEL),
        pl.BlockSpec((None, 1, 3 * D_MODEL), mod_row),
        pl.BlockSpec((D_MODEL, D_MODEL), lambda b, t: (0, 0)),
    ]
    args += [gate, x, mods_l, w_out]
    if sink is not None:
        in_specs.append(pl.BlockSpec(memory_space=pltpu.SMEM))
        args.append(sink)
    if final_g is not None:
        in_specs.append(pl.BlockSpec((1, D_MODEL), lambda b, t: (0, 0)))
        args.append(final_g.reshape(1, D_MODEL))
    body = functools.partial(
        _attn_kernel,
        head_w=head_w,
        group=1 if kind == 0 else GROUPS,
        has_cache=cache is not None,
        has_sink=sink is not None,
        band=band,
        final=final_g is not None,
        lag_b=lag_b,
        lag_c=lag_c,
        interleave=interleave,
        subs=subs,
        own_keys=own_keys,
        seq=seq,
    )
    slots = max(lag_b, lag_c) + 1
    scratch = [pltpu.VMEM((subs, N_HEADS * HEAD_DIM, tq), F32),
               pltpu.VMEM((slots, n_keys, tq), F32),
               pltpu.VMEM((slots, n_keys, tq), BF16),
               pltpu.VMEM((tq, D_MODEL), BF16),
               pltpu.VMEM((tq, D_MODEL), F32)]
    if band:
        scratch.append(pltpu.VMEM((subs, seq, tq), F32))
    return pl.pallas_call(
        body,
        grid=grid,
        in_specs=in_specs,
        out_specs=tok_spec(D_MODEL),
        out_shape=jax.ShapeDtypeStruct((n_tok, D_MODEL), F32),
        scratch_shapes=scratch,
        compiler_params=pltpu.CompilerParams(
            dimension_semantics=("arbitrary", "arbitrary"), vmem_limit_bytes=VMEM_LIMIT),
        name=f"attn_k{kind}_{'ctx' if ctx else 'lat'}",
    )(*args)


def _rope_lane_tables(seq, rot_dim):
    rows = seq // GRID_W
    row = jnp.repeat(jnp.arange(rows), GRID_W).astype(F32)
    col = jnp.tile(jnp.arange(GRID_W), rows).astype(F32)
    nf = rot_dim // 4
    freqs = ROPE_THETA ** (-jnp.arange(nf, dtype=F32) / nf)
    ang = jnp.concatenate([row[:, None] * freqs, col[:, None] * freqs], axis=-1)
    cos, sin = jnp.cos(ang), jnp.sin(ang)
    cos_l = jnp.repeat(cos, 2, axis=-1)
    sin_l = jnp.stack([-sin, sin], axis=-1).reshape(seq, rot_dim)
    return cos_l, sin_l


def _embed_lanes(table, start, fill):
    seq, w = table.shape
    return jnp.concatenate(
        [jnp.full((seq, start), fill, F32), table, jnp.full((seq, LANES - start - w), fill, F32)], axis=1)


def _mla_weights(w_in, q_norm, w_uq, kv_norm, w_ukv):
    c0 = MLA_Q_LORA + MLA_KV_LORA
    win = jnp.concatenate(
        [w_in[:, :c0], w_in[:, c0 + MLA_ROPE:], w_in[:, c0:c0 + MLA_ROPE],
         jnp.zeros((D_MODEL, MLA_IN_PAD - w_in.shape[1]), F32)], axis=1).astype(BF16)
    hq = MLA_NOPE + MLA_ROPE
    wuq = jnp.pad(w_uq.reshape(MLA_Q_LORA, N_HEADS, hq),
                  ((0, 0), (0, 0), (0, MLA_HEAD_PAD - hq))).reshape(MLA_Q_LORA, -1).astype(BF16)
    wkv = w_ukv.reshape(MLA_KV_LORA, N_HEADS, MLA_NOPE + MLA_V)
    wk = jnp.pad(wkv[:, :, :MLA_NOPE],
                 ((0, 0), (0, 0), (0, MLA_HEAD_PAD - MLA_NOPE))).reshape(MLA_KV_LORA, -1).astype(BF16)
    wvt = wkv[:, :, MLA_NOPE:].reshape(MLA_KV_LORA, -1).T.astype(BF16)
    return [win, q_norm.reshape(1, -1), wuq, kv_norm.reshape(1, -1), wk, wvt]


def kernel(x_prompt, x_sample, cache_mla_ckv, cache_mla_kpe, cache_gqa_k, cache_gqa_v,
           cache_swa_k, cache_swa_v, c, c_ctx, norm_g, w_ada, b_ada, w_out,
           mla_w_in, mla_q_norm, mla_w_uq, mla_kv_norm, mla_w_ukv,
           gqa_w_in, gqa_q_norm, gqa_k_norm, swa_w_in, swa_sink, final_norm_g):
    bc, sc, _ = x_prompt.shape
    bl, sl, _ = x_sample.shape
    n_past = cache_mla_ckv.shape[2]
    kvw = KV_HEADS * HEAD_DIM

    cc = jnp.concatenate([c, c_ctx[None, :], jnp.zeros((8 - bl - 1, D_MODEL), F32)], axis=0)
    mods = _ada_mods(cc, w_ada, b_ada).reshape(DEPTH, 8, 1, 3 * D_MODEL)

    cos_h, sin_h = _rope_lane_tables(sl, HEAD_DIM)
    gqa_tables = [jnp.tile(cos_h, (1, 2)), jnp.tile(sin_h, (1, 2))]
    cos_r, sin_r = _rope_lane_tables(sl, MLA_ROPE)
    mla_tables = [_embed_lanes(cos_r, MLA_NOPE, 1.0), _embed_lanes(sin_r, MLA_NOPE, 0.0),
                  _embed_lanes(cos_r, 0, 1.0), _embed_lanes(sin_r, 0, 0.0)]

    xc = x_prompt.reshape(bc * sc, D_MODEL)
    xl = x_sample.reshape(bl * sl, D_MODEL)
    new_caches = {0: ([], []), 1: ([], []), 2: ([], [])}
    for i in range(DEPTH):
        kind, j = i % 3, i // 3
        mods_l = mods[i]
        wo = w_out[i].astype(BF16)
        final_g = final_norm_g if i == DEPTH - 1 else None
        sink = None
        if kind == 0:
            weights = _mla_weights(mla_w_in[j], mla_q_norm[j], mla_w_uq[j], mla_kv_norm[j], mla_w_ukv[j])
            tables = mla_tables
            kpe_pad = jnp.pad(cache_mla_kpe[:, j], ((0, 0), (0, 0), (0, LANES - MLA_ROPE)))
            cache = _mla_cache(cache_mla_ckv[:, j], kpe_pad, weights[4], weights[5])
        else:
            tables = gqa_tables
            if kind == 1:
                weights = [gqa_w_in[j].astype(BF16),
                           jnp.tile(gqa_q_norm[j], 2).reshape(1, LANES),
                           jnp.tile(gqa_k_norm[j], 2).reshape(1, LANES)]
                cache = _gqa_cache(cache_gqa_k[:, j].reshape(bl, n_past, kvw),
                                   cache_gqa_v[:, j].reshape(bl, n_past, kvw))
            else:
                weights = [swa_w_in[j].astype(BF16)]
                sink = swa_sink[j]
                cache = _gqa_cache(cache_swa_k[:, j].reshape(bl, n_past, kvw),
                                   cache_swa_v[:, j].reshape(bl, n_past, kvw))

        q, k, vt, gate, c_a, c_b = _project(kind, xc, mods_l, norm_g[i], weights, tables,
                                            batch=bc, seq=sc, ctx=True)
        new_caches[kind][0].append(c_a)
        new_caches[kind][1].append(c_b)
        xc = _attend(kind, q, k, vt, None, gate, xc, mods_l, wo, sink, final_g,
                     batch=bc, seq=sc, ctx=True, lags=CTX_STAGE_LAGS)

        q, k, vt, gate = _project(kind, xl, mods_l, norm_g[i], weights, tables,
                                  batch=bl, seq=sl, ctx=False)
        xl = _attend(kind, q, k, vt, cache, gate, xl, mods_l, wo, sink, final_g,
                     batch=bl, seq=sl, ctx=False, lags=LAT_STAGE_LAGS)

    def stack(parts, tail):
        return jnp.stack([p.reshape((bc, sc) + tail) for p in parts], axis=1)

    return (xc.reshape(bc, sc, D_MODEL), xl.reshape(bl, sl, D_MODEL),
            stack(new_caches[0][0], (MLA_KV_LORA,)), stack(new_caches[0][1], (MLA_ROPE,)),
            stack(new_caches[1][0], (KV_HEADS, HEAD_DIM)), stack(new_caches[1][1], (KV_HEADS, HEAD_DIM)),
            stack(new_caches[2][0], (KV_HEADS, HEAD_DIM)), stack(new_caches[2][1], (KV_HEADS, HEAD_DIM)))
```

```python
import functools

import jax
import jax.numpy as jnp
from jax import lax
from jax.experimental import pallas as pl
from jax.experimental.pallas import tpu as pltpu

F32 = jnp.float32
BF16 = jnp.bfloat16

D_MODEL = 1024
DEPTH = 4
GRID_W = 64
N_HEADS = 16
HEAD_DIM = 64
KV_HEADS = 4
GROUPS = N_HEADS // KV_HEADS
MLA_Q_LORA = 384
MLA_KV_LORA = 256
MLA_NOPE = 64
MLA_ROPE = 32
MLA_V = 64
WINDOW = 128
ROPE_THETA = 10000.0
EPS = 1e-6
NEG_INF = -1e30

LOG2E = 1.4426950408889634
GQA_Q_SCALE = HEAD_DIM ** -0.5 * LOG2E
MLA_Q_SCALE = (MLA_NOPE + MLA_ROPE) ** -0.5 * LOG2E

LANES = 128
TOKEN_TILE = 256
PROJ_TILE = 512
KEY_CHUNK = 256
OUT_BLOCK = 256
CTX_STAGE_LAGS = (3, 3, True, 2)
LAT_STAGE_LAGS = (2, 2, False, 2)
MLA_HEAD_PAD = 128
V_ROWS = HEAD_DIM + 16
MLA_IN_PAD = 1792
VMEM_LIMIT = 56 * 1024 * 1024
NT_DIMS = (((1,), (1,)), ((), ()))


def _silu(v):
    return v * jax.nn.sigmoid(v)


def _rms(v, g):
    return v * lax.rsqrt(jnp.mean(v * v, axis=-1, keepdims=True) + EPS) * g


def _mod_norm(x, g, mod):
    shift = mod[:, :D_MODEL]
    scale = mod[:, D_MODEL:2 * D_MODEL]
    return _rms(x, g) * (1.0 + scale) + shift


def _rope_tile(v, cos, sin_signed):
    lane = lax.broadcasted_iota(jnp.int32, v.shape, 1)
    nxt = pltpu.roll(v, LANES - 1, axis=1)
    prv = pltpu.roll(v, 1, axis=1)
    swapped = jnp.where((lane & 1) == 0, nxt, prv)
    return v * cos + swapped * sin_signed


def _ada_kernel(c_ref, w_ref, b_ref, o_ref):
    s = _silu(c_ref[...]).astype(BF16)
    o_ref[...] = jnp.dot(s, w_ref[...].astype(BF16), preferred_element_type=F32) + b_ref[...]


def _ada_mods(cc, w_ada, b_ada):
    tn = D_MODEL
    return pl.pallas_call(
        _ada_kernel,
        grid=(DEPTH, 3 * D_MODEL // tn),
        in_specs=[
            pl.BlockSpec((8, D_MODEL), lambda i, n: (0, 0)),
            pl.BlockSpec((None, D_MODEL, tn), lambda i, n: (i, 0, n)),
            pl.BlockSpec((None, 1, tn), lambda i, n: (i, 0, n)),
        ],
        out_specs=pl.BlockSpec((None, 8, tn), lambda i, n: (i, 0, n)),
        out_shape=jax.ShapeDtypeStruct((DEPTH, 8, 3 * D_MODEL), F32),
        compiler_params=pltpu.CompilerParams(vmem_limit_bytes=VMEM_LIMIT),
        name="ada_mods",
    )(cc, w_ada, b_ada.reshape(DEPTH, 1, 3 * D_MODEL))


def _store_queries(q_out, col0, blk):
    w = blk.shape[1]
    if len(q_out.shape) == 2:
        q_out[:, col0:col0 + w] = blk.astype(BF16)
        return
    blk_t = blk.T.astype(BF16)
    seq = q_out.shape[2]
    for e in range(q_out.shape[0]):
        q_out[e, col0:col0 + w, :] = blk_t[:, e * seq:(e + 1) * seq]


def _store_values_t(vt_out, vt, n_heads):
    if len(vt_out.shape) == 3:
        seq = vt_out.shape[2]
        for e in range(vt_out.shape[0]):
            _store_values_t(vt_out.at[e], vt[:, e * seq:(e + 1) * seq], n_heads)
        return
    v_rows = vt_out.shape[0] // n_heads
    if v_rows == HEAD_DIM:
        vt_out[...] = vt.astype(BF16)
        return
    ones = jnp.ones((v_rows - HEAD_DIM, vt.shape[1]), BF16)
    for h in range(n_heads):
        vt_out[v_rows * h:v_rows * h + HEAD_DIM, :] = vt[HEAD_DIM * h:HEAD_DIM * (h + 1), :].astype(BF16)
        vt_out[v_rows * h + HEAD_DIM:v_rows * (h + 1), :] = ones


def _mla_expand(ckv_bf, kpe, wk_ref, wvt_ref, k_out, vt_out):
    k = jnp.dot(ckv_bf, wk_ref[...], preferred_element_type=F32)
    lane = lax.broadcasted_iota(jnp.int32, kpe.shape, 1)
    rope_lanes = (lane >= MLA_NOPE) & (lane < MLA_NOPE + MLA_ROPE)
    kpe_at_rope = pltpu.roll(kpe, MLA_NOPE, axis=1)
    for hd in range(N_HEADS):
        cols = slice(MLA_HEAD_PAD * hd, MLA_HEAD_PAD * (hd + 1))
        k_out[:, cols] = jnp.where(rope_lanes, kpe_at_rope, k[:, cols]).astype(BF16)
    vt = lax.dot_general(wvt_ref[...], ckv_bf, NT_DIMS, preferred_element_type=F32)
    _store_values_t(vt_out, vt, N_HEADS)


def _proj_mla_kernel(*refs, rope, ctx):
    refs = list(refs)
    x_ref, mod_ref, g_ref, win_ref, qn_ref, wuq_ref, kvn_ref, wk_ref, wvt_ref = refs[:9]
    pos = 9
    if rope:
        cq_ref, sq_ref, ck_ref, sk_ref = refs[pos:pos + 4]
        pos += 4
    q_out, k_out, vt_out, gate_out = refs[pos:pos + 4]
    pos += 4
    if ctx:
        ckv_out, kpe_out = refs[pos:pos + 2]

    h = _mod_norm(x_ref[...], g_ref[...], mod_ref[...]).astype(BF16)
    z = jnp.dot(h, win_ref[...], preferred_element_type=F32)
    cq = z[:, :MLA_Q_LORA]
    ckv = z[:, MLA_Q_LORA:MLA_Q_LORA + MLA_KV_LORA]
    gate = z[:, 640:640 + D_MODEL]
    kpe = z[:, 640 + D_MODEL:]
    gate_out[...] = _silu(gate).astype(BF16)

    qf = jnp.dot(_rms(cq, qn_ref[...]).astype(BF16), wuq_ref[...], preferred_element_type=F32)
    ckvn = _rms(ckv, kvn_ref[...])
    if ctx:
        ckv_out[...] = ckvn
        kpe_out[...] = kpe[:, :MLA_ROPE]
    if rope:
        kpe = _rope_tile(kpe, ck_ref[...], sk_ref[...])
    _mla_expand(ckvn.astype(BF16), kpe, wk_ref, wvt_ref, k_out, vt_out)
    for hd in range(N_HEADS):
        blk = qf[:, MLA_HEAD_PAD * hd:MLA_HEAD_PAD * (hd + 1)]
        if rope:
            blk = _rope_tile(blk, cq_ref[...], sq_ref[...])
        _store_queries(q_out, MLA_HEAD_PAD * hd, blk * MLA_Q_SCALE)


def _proj_gqa_kernel(*refs, qk_norm, rope, ctx):
    refs = list(refs)
    x_ref, mod_ref, g_ref, win_ref = refs[:4]
    pos = 4
    if qk_norm:
        qn_ref, kn_ref = refs[pos:pos + 2]
        pos += 2
    if rope:
        cos_ref, sin_ref = refs[pos:pos + 2]
        pos += 2
    q_out, k_out, vt_out, gate_out = refs[pos:pos + 4]
    pos += 4
    if ctx:
        kc_out, vc_out = refs[pos:pos + 2]

    nq = N_HEADS * HEAD_DIM
    nk = KV_HEADS * HEAD_DIM
    h = _mod_norm(x_ref[...], g_ref[...], mod_ref[...]).astype(BF16)
    z = jnp.dot(h, win_ref[...], preferred_element_type=F32)
    tm = z.shape[0]
    lo = lax.broadcasted_iota(jnp.int32, (tm, LANES), 1) < HEAD_DIM
    n_q_tiles = nq // LANES
    for c in range((nq + nk) // LANES):
        blk = z[:, LANES * c:LANES * (c + 1)]
        is_q = c < n_q_tiles
        if qk_norm:
            sq = blk * blk
            s_lo = jnp.sum(jnp.where(lo, sq, 0.0), axis=-1, keepdims=True)
            s_hi = jnp.sum(jnp.where(lo, 0.0, sq), axis=-1, keepdims=True)
            inv = jnp.where(lo, lax.rsqrt(s_lo * (1.0 / HEAD_DIM) + EPS),
                            lax.rsqrt(s_hi * (1.0 / HEAD_DIM) + EPS))
            blk = blk * inv * (qn_ref[...] if is_q else kn_ref[...])
        if ctx and not is_q:
            kc_out[:, LANES * (c - n_q_tiles):LANES * (c - n_q_tiles + 1)] = blk
        if rope:
            blk = _rope_tile(blk, cos_ref[...], sin_ref[...])
        if is_q:
            _store_queries(q_out, LANES * c, blk * GQA_Q_SCALE)
        else:
            k_out[:, LANES * (c - n_q_tiles):LANES * (c - n_q_tiles + 1)] = blk.astype(BF16)
    v = z[:, nq + nk:nq + 2 * nk]
    if ctx:
        vc_out[...] = v
    _store_values_t(vt_out, v.T, KV_HEADS)
    gate_out[...] = _silu(z[:, nq + 2 * nk:]).astype(BF16)


def _const_spec(shape):
    return pl.BlockSpec(shape, lambda i: (0,) * len(shape))


def _project(kind, x, mods_l, norm_g, weights, tables, *, batch, seq, ctx):
    n_tok = batch * seq
    tm = PROJ_TILE
    tiles_per_seq = max(seq // tm, 1)
    seqs_per_tile = max(tm // seq, 1)
    assert tm == seqs_per_tile * seq // tiles_per_seq and (ctx or seqs_per_tile == 1)
    v_rows = V_ROWS if ctx else HEAD_DIM
    rope = not ctx
    mod_row = (lambda i: (4, 0, 0)) if ctx else (lambda i: (i // tiles_per_seq, 0, 0))
    in_specs = [
        pl.BlockSpec((tm, D_MODEL), lambda i: (i, 0)),
        pl.BlockSpec((None, 1, 3 * D_MODEL), mod_row),
        _const_spec((1, D_MODEL)),
    ]
    args = [x, mods_l, norm_g.reshape(1, D_MODEL)]
    for w in weights:
        in_specs.append(_const_spec(w.shape))
        args.append(w)
    if rope:
        for t in tables:
            in_specs.append(pl.BlockSpec((tm, LANES), lambda i: (i % tiles_per_seq, 0)))
            args.append(t)
    if kind == 0:
        qw, kw, vw = N_HEADS * MLA_HEAD_PAD, N_HEADS * MLA_HEAD_PAD, N_HEADS * v_rows
        body = functools.partial(_proj_mla_kernel, rope=rope, ctx=ctx)
        cache_shapes = [(n_tok, MLA_KV_LORA), (n_tok, MLA_ROPE)]
    else:
        qw, kw, vw = N_HEADS * HEAD_DIM, KV_HEADS * HEAD_DIM, KV_HEADS * v_rows
        body = functools.partial(_proj_gqa_kernel, qk_norm=(kind == 1), rope=rope, ctx=ctx)
        cache_shapes = [(n_tok, kw), (n_tok, kw)]
    def feature_major_spec(w):
        if seqs_per_tile > 1:
            return pl.BlockSpec((seqs_per_tile, w, seq), lambda i: (i, 0, 0))
        return pl.BlockSpec((None, w, tm), lambda i: (i // tiles_per_seq, 0, i % tiles_per_seq))

    out_shape = [
        jax.ShapeDtypeStruct((batch, qw, seq) if ctx else (n_tok, qw), BF16),
        jax.ShapeDtypeStruct((n_tok, kw), BF16),
        jax.ShapeDtypeStruct((batch, vw, seq), BF16),
        jax.ShapeDtypeStruct((n_tok, D_MODEL), BF16),
    ]
    out_specs = [
        feature_major_spec(qw) if ctx else pl.BlockSpec((tm, qw), lambda i: (i, 0)),
        pl.BlockSpec((tm, kw), lambda i: (i, 0)),
        feature_major_spec(vw),
        pl.BlockSpec((tm, D_MODEL), lambda i: (i, 0)),
    ]
    if ctx:
        for s in cache_shapes:
            out_shape.append(jax.ShapeDtypeStruct(s, F32))
            out_specs.append(pl.BlockSpec((tm, s[1]), lambda i: (i, 0)))
    return pl.pallas_call(
        body,
        grid=(n_tok // tm,),
        in_specs=in_specs,
        out_specs=out_specs,
        out_shape=out_shape,
        compiler_params=pltpu.CompilerParams(
            dimension_semantics=("arbitrary",), vmem_limit_bytes=VMEM_LIMIT),
        name=f"proj_k{kind}_{'ctx' if ctx else 'lat'}",
    )(*args)


def _mla_cache_kernel(ckv_ref, kpe_ref, wk_ref, wvt_ref, k_out, vt_out):
    _mla_expand(ckv_ref[...].astype(BF16), kpe_ref[...], wk_ref, wvt_ref, k_out, vt_out)


def _mla_cache(ckv, kpe_pad, wk, wvt):
    b, n, _ = ckv.shape
    return pl.pallas_call(
        _mla_cache_kernel,
        grid=(b,),
        in_specs=[
            pl.BlockSpec((None, n, MLA_KV_LORA), lambda i: (i, 0, 0)),
            pl.BlockSpec((None, n, LANES), lambda i: (i, 0, 0)),
            _const_spec(wk.shape), _const_spec(wvt.shape),
        ],
        out_specs=[
            pl.BlockSpec((None, n, N_HEADS * MLA_HEAD_PAD), lambda i: (i, 0, 0)),
            pl.BlockSpec((None, N_HEADS * HEAD_DIM, n), lambda i: (i, 0, 0)),
        ],
        out_shape=[
            jax.ShapeDtypeStruct((b, n, N_HEADS * MLA_HEAD_PAD), BF16),
            jax.ShapeDtypeStruct((b, N_HEADS * HEAD_DIM, n), BF16),
        ],
        compiler_params=pltpu.CompilerParams(
            dimension_semantics=("arbitrary",), vmem_limit_bytes=VMEM_LIMIT),
        name="mla_cache",
    )(ckv, kpe_pad, wk, wvt)


def _gqa_cache_kernel(k_ref, v_ref, k_out, vt_out):
    k_out[...] = k_ref[...].astype(BF16)
    _store_values_t(vt_out, v_ref[...].T, KV_HEADS)


def _gqa_cache(k, v):
    b, n, w = k.shape
    return pl.pallas_call(
        _gqa_cache_kernel,
        grid=(b,),
        in_specs=[pl.BlockSpec((None, n, w), lambda i: (i, 0, 0))] * 2,
        out_specs=[
            pl.BlockSpec((None, n, w), lambda i: (i, 0, 0)),
            pl.BlockSpec((None, KV_HEADS * HEAD_DIM, n), lambda i: (i, 0, 0)),
        ],
        out_shape=[
            jax.ShapeDtypeStruct((b, n, w), BF16),
            jax.ShapeDtypeStruct((b, KV_HEADS * HEAD_DIM, n), BF16),
        ],
        compiler_params=pltpu.CompilerParams(dimension_semantics=("arbitrary",)),
        name="gqa_cache",
    )(k, v)


def _attn_kernel(*refs, head_w, group, has_cache, has_sink, band, final, lag_b, lag_c, interleave,
                 subs, own_keys, seq):
    refs = list(refs)
    q_ref, kn_ref, vtn_ref = refs[:3]
    pos = 3
    if has_cache:
        kc_ref, vtc_ref = refs[pos:pos + 2]
        pos += 2
    gate_ref, x_ref, mod_ref, wo_ref = refs[pos:pos + 4]
    pos += 4
    if has_sink:
        sink_ref = refs[pos]
        pos += 1
    if final:
        fg_ref = refs[pos]
        pos += 1
    o_ref, ot_scr, s_scr, p_scr, g_scr, y_scr = refs[pos:pos + 6]
    if band:
        bias_scr = refs[pos + 6]

    tq = TOKEN_TILE
    ch = KEY_CHUNK
    slots = s_scr.shape[0]
    segs = []
    if has_cache:
        segs.append((kc_ref.shape[0],
                     lambda e, r, n, cols: kc_ref[r:r + n, cols],
                     lambda e, rows: vtc_ref[rows, :]))
    if own_keys:
        segs.append((seq,
                     lambda e, r, n, cols: kn_ref[e * seq + r:e * seq + r + n, cols],
                     lambda e, rows: vtn_ref[e, rows, :]))
    else:
        segs.append((seq,
                     lambda e, r, n, cols: kn_ref[r:r + n, cols],
                     lambda e, rows: vtn_ref[rows, :]))
    seg_rows = []
    chunks = []
    off = 0
    for si, (n_seg, _, _) in enumerate(segs):
        seg_rows.append((off, n_seg))
        chunks += [(si, r, off + r) for r in range(0, n_seg, ch)]
        off += n_seg
    if interleave:
        score_chunks = [c + (ch,) for c in chunks]
    else:
        score_chunks = [(si, 0, r0, n) for si, (r0, n) in enumerate(seg_rows)]
    own = len(segs) - 1
    if band:
        kpos = lax.broadcasted_iota(jnp.int32, (seq, tq), 0)
        for e in range(subs):
            tile = pl.program_id(1) * subs + e
            qpos = tile * tq + lax.broadcasted_iota(jnp.int32, (seq, tq), 1)
            bias_scr[e] = jnp.where(jnp.abs(kpos - qpos) <= WINDOW, 0.0, NEG_INF)

    assert lag_b >= 1 and lag_c >= 1 and slots > max(lag_b, lag_c)
    row_max = {}
    den_part = {}
    v_rows = vtn_ref.shape[-2] * group // N_HEADS
    den_on_mxu = v_rows > HEAD_DIM

    def scores_and_max(i):
        e, hd = divmod(i, N_HEADS)
        kh = hd // group
        q_cols = slice(head_w * hd, head_w * (hd + 1))
        qh = q_ref[e, q_cols, :] if own_keys else q_ref[e * tq:(e + 1) * tq, q_cols]
        mx = None
        for si, r, g, n in score_chunks:
            kk = segs[si][1](e, r, n, slice(head_w * kh, head_w * (kh + 1)))
            if own_keys:
                s = jnp.dot(kk, qh, preferred_element_type=F32)
            else:
                s = lax.dot_general(kk, qh, NT_DIMS, preferred_element_type=F32)
            if band and si == own:
                s = s + bias_scr[e, r:r + n, :]
            s_scr[i % slots, g:g + n, :] = s
            part = jnp.max(s.reshape(n // 8, 8, tq), axis=0)
            mx = part if mx is None else jnp.maximum(mx, part)
            yield
        m = jnp.max(mx, axis=0, keepdims=True)
        if has_sink:
            m = jnp.maximum(m, sink_ref[hd] * LOG2E)
        row_max[i] = m

    def exponentials(i):
        hd = i % N_HEADS
        m = row_max.pop(i)
        mb = jnp.broadcast_to(m, (8, tq))
        tot = None
        for _, _, g in chunks:
            p = jnp.exp2(s_scr[i % slots, g:g + ch, :].reshape(ch // 8, 8, tq) - mb)
            if not den_on_mxu:
                part = jnp.sum(p, axis=0)
                tot = part if tot is None else tot + part
            p_scr[i % slots, g:g + ch, :] = p.reshape(ch, tq).astype(BF16)
            yield
        den = None if den_on_mxu else jnp.sum(tot, axis=0, keepdims=True)
        if has_sink:
            sink = jnp.exp2(sink_ref[hd] * LOG2E - m)
            den = sink if den is None else den + sink
        den_part[i] = den

    def weighted_values(i):
        e, hd = divmod(i, N_HEADS)
        kh = hd // group
        acc = None
        for si, (_, _, values_t) in enumerate(segs):
            r0, n = seg_rows[si]
            vt = values_t(e, slice(v_rows * kh, v_rows * (kh + 1)))
            part = jnp.dot(vt, p_scr[i % slots, r0:r0 + n, :], preferred_element_type=F32)
            acc = part if acc is None else acc + part
            yield
        den = den_part.pop(i)
        if den_on_mxu:
            mxu_sum = acc[HEAD_DIM:HEAD_DIM + 1, :]
            den = mxu_sum if den is None else den + mxu_sum
        ot_scr[e, HEAD_DIM * hd:HEAD_DIM * (hd + 1), :] = acc[:HEAD_DIM, :] / den

    def finish(e):
        rows = slice(e * tq, (e + 1) * tq)
        for c0 in range(0, D_MODEL, OUT_BLOCK):
            cols = slice(c0, c0 + OUT_BLOCK)
            o = ot_scr[e, cols, :].T
            g_scr[:, cols] = (o * gate_ref[rows, cols].astype(F32)).astype(BF16)
            yield
        for c0 in range(0, D_MODEL, OUT_BLOCK):
            cols = slice(c0, c0 + OUT_BLOCK)
            y = jnp.dot(g_scr[...], wo_ref[:, cols], preferred_element_type=F32)
            xn = x_ref[rows, cols] + mod_ref[:, 2 * D_MODEL + c0:2 * D_MODEL + c0 + OUT_BLOCK] * y
            if final:
                y_scr[:, cols] = xn
            else:
                o_ref[rows, cols] = xn
            yield
        if final:
            o_ref[rows, :] = _rms(y_scr[...], fg_ref[...])

    n_items = subs * N_HEADS
    for step in range(n_items + lag_b + lag_c + 1):
        live = []
        if step < n_items:
            live.append(scores_and_max(step))
        if 0 <= step - lag_b < n_items:
            live.append(exponentials(step - lag_b))
        done = step - lag_b - lag_c
        if 0 <= done < n_items:
            live.append(weighted_values(done))
        if done >= N_HEADS and done % N_HEADS == 0:
            live.append(finish(done // N_HEADS - 1))
        if not interleave:
            for g in live:
                for _ in g:
                    pass
            live = []
        while live:
            live = [g for g in live if next(g, True) is None]


def _attend(kind, q, k_new, vt_new, cache, gate, x, mods_l, w_out, sink, final_g, *, batch, seq, ctx,
            lags):
    lag_b, lag_c, interleave, subs = lags
    tq = TOKEN_TILE
    n_tok = batch * seq
    band = kind == 2 and not ctx
    head_w = MLA_HEAD_PAD if kind == 0 else HEAD_DIM
    kw = k_new.shape[1]
    vw = vt_new.shape[1]
    own_keys = cache is None
    assert seq == tq if own_keys else seq % (subs * tq) == 0
    grid = (batch // subs, 1) if own_keys else (batch, seq // (subs * tq))
    tok_spec = lambda w: pl.BlockSpec((subs * tq, w), lambda b, t: (b * grid[1] + t, 0))
    qw = q.shape[1]
    if own_keys:
        in_specs = [
            pl.BlockSpec((subs, qw, seq), lambda b, t: (b, 0, 0)),
            pl.BlockSpec((subs * seq, kw), lambda b, t: (b, 0)),
            pl.BlockSpec((subs, vw, seq), lambda b, t: (b, 0, 0)),
        ]
    else:
        in_specs = [
            tok_spec(qw),
            pl.BlockSpec((seq, kw), lambda b, t: (b, 0)),
            pl.BlockSpec((None, vw, seq), lambda b, t: (b, 0, 0)),
        ]
    args = [q, k_new, vt_new]
    n_keys = seq
    if cache is not None:
        kc, vtc = cache
        n_c = kc.shape[1]
        n_keys += n_c
        in_specs += [
            pl.BlockSpec((None, n_c, kw), lambda b, t: (b, 0, 0)),
            pl.BlockSpec((None, vw, n_c), lambda b, t: (b, 0, 0)),
        ]
        args += [kc, vtc]
    mod_row = (lambda b, t: (4, 0, 0)) if ctx else (lambda b, t: (b, 0, 0))
    in_specs += [
        tok_spec(D_MODEL),
        tok_spec(D_MODEL),
        pl.BlockSpec((None, 1, 3 * D_MODEL), mod_row),
        pl.BlockSpec((D_MODEL, D_MODEL), lambda b, t: (0, 0)),
    ]
    args += [gate, x, mods_l, w_out]
    if sink is not None:
        in_specs.append(pl.BlockSpec(memory_space=pltpu.SMEM))
        args.append(sink)
    if final_g is not None:
        in_specs.append(pl.BlockSpec((1, D_MODEL), lambda b, t: (0, 0)))
        args.append(final_g.reshape(1, D_MODEL))
    body = functools.partial(
        _attn_kernel,
        head_w=head_w,
        group=1 if kind == 0 else GROUPS,
        has_cache=cache is not None,
        has_sink=sink is not None,
        band=band,
        final=final_g is not None,
        lag_b=lag_b,
        lag_c=lag_c,
        interleave=interleave,
        subs=subs,
        own_keys=own_keys,
        seq=seq,
    )
    slots = max(lag_b, lag_c) + 1
    scratch = [pltpu.VMEM((subs, N_HEADS * HEAD_DIM, tq), F32),
               pltpu.VMEM((slots, n_keys, tq), F32),
               pltpu.VMEM((slots, n_keys, tq), BF16),
               pltpu.VMEM((tq, D_MODEL), BF16),
               pltpu.VMEM((tq, D_MODEL), F32)]
    if band:
        scratch.append(pltpu.VMEM((subs, seq, tq), F32))
    return pl.pallas_call(
        body,
        grid=grid,
        in_specs=in_specs,
        out_specs=tok_spec(D_MODEL),
        out_shape=jax.ShapeDtypeStruct((n_tok, D_MODEL), F32),
        scratch_shapes=scratch,
        compiler_params=pltpu.CompilerParams(
            dimension_semantics=("arbitrary", "arbitrary"), vmem_limit_bytes=VMEM_LIMIT),
        name=f"attn_k{kind}_{'ctx' if ctx else 'lat'}",
    )(*args)


def _rope_lane_tables(seq, rot_dim):
    rows = seq // GRID_W
    row = jnp.repeat(jnp.arange(rows), GRID_W).astype(F32)
    col = jnp.tile(jnp.arange(GRID_W), rows).astype(F32)
    nf = rot_dim // 4
    freqs = ROPE_THETA ** (-jnp.arange(nf, dtype=F32) / nf)
    ang = jnp.concatenate([row[:, None] * freqs, col[:, None] * freqs], axis=-1)
    cos, sin = jnp.cos(ang), jnp.sin(ang)
    cos_l = jnp.repeat(cos, 2, axis=-1)
    sin_l = jnp.stack([-sin, sin], axis=-1).reshape(seq, rot_dim)
    return cos_l, sin_l


def _embed_lanes(table, start, fill):
    seq, w = table.shape
    return jnp.concatenate(
        [jnp.full((seq, start), fill, F32), table, jnp.full((seq, LANES - start - w), fill, F32)], axis=1)


def _mla_weights(w_in, q_norm, w_uq, kv_norm, w_ukv):
    n = w_in.shape[0]
    c0 = MLA_Q_LORA + MLA_KV_LORA
    w_in = w_in.astype(BF16)
    win = jnp.concatenate(
        [w_in[:, :, :c0], w_in[:, :, c0 + MLA_ROPE:], w_in[:, :, c0:c0 + MLA_ROPE],
         jnp.zeros((n, D_MODEL, MLA_IN_PAD - w_in.shape[2]), BF16)], axis=2)
    hq = MLA_NOPE + MLA_ROPE
    wuq = jnp.pad(w_uq.astype(BF16).reshape(n, MLA_Q_LORA, N_HEADS, hq),
                  ((0, 0), (0, 0), (0, 0), (0, MLA_HEAD_PAD - hq))).reshape(n, MLA_Q_LORA, -1)
    wkv = w_ukv.astype(BF16).reshape(n, MLA_KV_LORA, N_HEADS, MLA_NOPE + MLA_V)
    wk = jnp.pad(wkv[..., :MLA_NOPE],
                 ((0, 0), (0, 0), (0, 0), (0, MLA_HEAD_PAD - MLA_NOPE))).reshape(n, MLA_KV_LORA, -1)
    wvt = jnp.swapaxes(wkv[..., MLA_NOPE:].reshape(n, MLA_KV_LORA, -1), 1, 2)
    return [[win[j], q_norm[j].reshape(1, -1), wuq[j], kv_norm[j].reshape(1, -1), wk[j], wvt[j]]
            for j in range(n)]


def kernel(x_prompt, x_sample, cache_mla_ckv, cache_mla_kpe, cache_gqa_k, cache_gqa_v,
           cache_swa_k, cache_swa_v, c, c_ctx, norm_g, w_ada, b_ada, w_out,
           mla_w_in, mla_q_norm, mla_w_uq, mla_kv_norm, mla_w_ukv,
           gqa_w_in, gqa_q_norm, gqa_k_norm, swa_w_in, swa_sink, final_norm_g):
    bc, sc, _ = x_prompt.shape
    bl, sl, _ = x_sample.shape
    n_past = cache_mla_ckv.shape[2]
    kvw = KV_HEADS * HEAD_DIM

    cc = jnp.concatenate([c, c_ctx[None, :], jnp.zeros((8 - bl - 1, D_MODEL), F32)], axis=0)
    mods = _ada_mods(cc, w_ada, b_ada).reshape(DEPTH, 8, 1, 3 * D_MODEL)

    cos_h, sin_h = _rope_lane_tables(sl, HEAD_DIM)
    gqa_tables = [jnp.tile(cos_h, (1, 2)), jnp.tile(sin_h, (1, 2))]
    cos_r, sin_r = _rope_lane_tables(sl, MLA_ROPE)
    mla_tables = [_embed_lanes(cos_r, MLA_NOPE, 1.0), _embed_lanes(sin_r, MLA_NOPE, 0.0),
                  _embed_lanes(cos_r, 0, 1.0), _embed_lanes(sin_r, 0, 0.0)]

    mla_weights = _mla_weights(mla_w_in, mla_q_norm, mla_w_uq, mla_kv_norm, mla_w_ukv)
    w_out_bf = w_out.astype(BF16)

    xc = x_prompt.reshape(bc * sc, D_MODEL)
    xl = x_sample.reshape(bl * sl, D_MODEL)
    new_caches = {0: ([], []), 1: ([], []), 2: ([], [])}
    for i in range(DEPTH):
        kind, j = i % 3, i // 3
        mods_l = mods[i]
        wo = w_out_bf[i]
        final_g = final_norm_g if i == DEPTH - 1 else None
        sink = None
        if kind == 0:
            weights = mla_weights[j]
            tables = mla_tables
            kpe_pad = jnp.pad(cache_mla_kpe[:, j], ((0, 0), (0, 0), (0, LANES - MLA_ROPE)))
            cache = _mla_cache(cache_mla_ckv[:, j], kpe_pad, weights[4], weights[5])
        else:
            tables = gqa_tables
            if kind == 1:
                weights = [gqa_w_in[j].astype(BF16),
                           jnp.tile(gqa_q_norm[j], 2).reshape(1, LANES),
                           jnp.tile(gqa_k_norm[j], 2).reshape(1, LANES)]
                cache = _gqa_cache(cache_gqa_k[:, j].reshape(bl, n_past, kvw),
                                   cache_gqa_v[:, j].reshape(bl, n_past, kvw))
            else:
                weights = [swa_w_in[j].astype(BF16)]
                sink = swa_sink[j]
                cache = _gqa_cache(cache_swa_k[:, j].reshape(bl, n_past, kvw),
                                   cache_swa_v[:, j].reshape(bl, n_past, kvw))

        q, k, vt, gate, c_a, c_b = _project(kind, xc, mods_l, norm_g[i], weights, tables,
                                            batch=bc, seq=sc, ctx=True)
        new_caches[kind][0].append(c_a)
        new_caches[kind][1].append(c_b)
        xc = _attend(kind, q, k, vt, None, gate, xc, mods_l, wo, sink, final_g,
                     batch=bc, seq=sc, ctx=True, lags=CTX_STAGE_LAGS)

        q, k, vt, gate = _project(kind, xl, mods_l, norm_g[i], weights, tables,
                                  batch=bl, seq=sl, ctx=False)
        xl = _attend(kind, q, k, vt, cache, gate, xl, mods_l, wo, sink, final_g,
                     batch=bl, seq=sl, ctx=False, lags=LAT_STAGE_LAGS)

    def stack(parts, tail):
        return jnp.stack([p.reshape((bc, sc) + tail) for p in parts], axis=1)

    return (xc.reshape(bc, sc, D_MODEL), xl.reshape(bl, sl, D_MODEL),
            stack(new_caches[0][0], (MLA_KV_LORA,)), stack(new_caches[0][1], (MLA_ROPE,)),
            stack(new_caches[1][0], (KV_HEADS, HEAD_DIM)), stack(new_caches[1][1], (KV_HEADS, HEAD_DIM)),
            stack(new_caches[2][0], (KV_HEADS, HEAD_DIM)), stack(new_caches[2][1], (KV_HEADS, HEAD_DIM)))
```

```python
import functools

import jax
import jax.numpy as jnp
from jax import lax
from jax.experimental import pallas as pl
from jax.experimental.pallas import tpu as pltpu

F32 = jnp.float32
BF16 = jnp.bfloat16

D_MODEL = 1024
DEPTH = 4
GRID_W = 64
N_HEADS = 16
HEAD_DIM = 64
KV_HEADS = 4
GROUPS = N_HEADS // KV_HEADS
MLA_Q_LORA = 384
MLA_KV_LORA = 256
MLA_NOPE = 64
MLA_ROPE = 32
MLA_V = 64
WINDOW = 128
ROPE_THETA = 10000.0
EPS = 1e-6
NEG_INF = -1e30

LOG2E = 1.4426950408889634
GQA_Q_SCALE = HEAD_DIM ** -0.5 * LOG2E
MLA_Q_SCALE = (MLA_NOPE + MLA_ROPE) ** -0.5 * LOG2E

LANES = 128
TOKEN_TILE = 256
PROJ_TILE = 512
KEY_CHUNK = 256
OUT_BLOCK = 256
CTX_STAGE_LAGS = (3, 3, True, 2)
LAT_STAGE_LAGS = (2, 2, False, 2)
MLA_HEAD_PAD = 128
V_ROWS = HEAD_DIM + 16
MLA_IN_PAD = 1792
VMEM_LIMIT = 56 * 1024 * 1024
NT_DIMS = (((1,), (1,)), ((), ()))


def _silu(v):
    return v * jax.nn.sigmoid(v)


def _rms(v, g):
    return v * lax.rsqrt(jnp.mean(v * v, axis=-1, keepdims=True) + EPS) * g


def _mod_norm(x, g, mod):
    shift = mod[:, :D_MODEL]
    scale = mod[:, D_MODEL:2 * D_MODEL]
    return _rms(x, g) * (1.0 + scale) + shift


def _rope_tile(v, cos, sin_signed):
    lane = lax.broadcasted_iota(jnp.int32, v.shape, 1)
    nxt = pltpu.roll(v, LANES - 1, axis=1)
    prv = pltpu.roll(v, 1, axis=1)
    swapped = jnp.where((lane & 1) == 0, nxt, prv)
    return v * cos + swapped * sin_signed


def _ada_kernel(c_ref, w_ref, b_ref, o_ref):
    s = _silu(c_ref[...]).astype(BF16)
    o_ref[...] = jnp.dot(s, w_ref[...].astype(BF16), preferred_element_type=F32) + b_ref[...]


def _ada_mods(cc, w_ada, b_ada):
    tn = D_MODEL
    return pl.pallas_call(
        _ada_kernel,
        grid=(DEPTH, 3 * D_MODEL // tn),
        in_specs=[
            pl.BlockSpec((8, D_MODEL), lambda i, n: (0, 0)),
            pl.BlockSpec((None, D_MODEL, tn), lambda i, n: (i, 0, n)),
            pl.BlockSpec((None, 1, tn), lambda i, n: (i, 0, n)),
        ],
        out_specs=pl.BlockSpec((None, 8, tn), lambda i, n: (i, 0, n)),
        out_shape=jax.ShapeDtypeStruct((DEPTH, 8, 3 * D_MODEL), F32),
        compiler_params=pltpu.CompilerParams(vmem_limit_bytes=VMEM_LIMIT),
        name="ada_mods",
    )(cc, w_ada, b_ada.reshape(DEPTH, 1, 3 * D_MODEL))


def _store_queries(q_out, col0, blk):
    w = blk.shape[1]
    if len(q_out.shape) == 2:
        q_out[:, col0:col0 + w] = blk.astype(BF16)
        return
    blk_t = blk.T.astype(BF16)
    seq = q_out.shape[2]
    for e in range(q_out.shape[0]):
        q_out[e, col0:col0 + w, :] = blk_t[:, e * seq:(e + 1) * seq]


def _store_values_t(vt_out, vt, n_heads):
    if len(vt_out.shape) == 3:
        seq = vt_out.shape[2]
        for e in range(vt_out.shape[0]):
            _store_values_t(vt_out.at[e], vt[:, e * seq:(e + 1) * seq], n_heads)
        return
    v_rows = vt_out.shape[0] // n_heads
    if v_rows == HEAD_DIM:
        vt_out[...] = vt.astype(BF16)
        return
    ones = jnp.ones((v_rows - HEAD_DIM, vt.shape[1]), BF16)
    for h in range(n_heads):
        vt_out[v_rows * h:v_rows * h + HEAD_DIM, :] = vt[HEAD_DIM * h:HEAD_DIM * (h + 1), :].astype(BF16)
        vt_out[v_rows * h + HEAD_DIM:v_rows * (h + 1), :] = ones


def _mla_expand(ckv_bf, kpe, wk_ref, wvt_ref, k_out, vt_out):
    k = jnp.dot(ckv_bf, wk_ref[...], preferred_element_type=F32)
    lane = lax.broadcasted_iota(jnp.int32, kpe.shape, 1)
    rope_lanes = (lane >= MLA_NOPE) & (lane < MLA_NOPE + MLA_ROPE)
    kpe_at_rope = pltpu.roll(kpe, MLA_NOPE, axis=1)
    for hd in range(N_HEADS):
        cols = slice(MLA_HEAD_PAD * hd, MLA_HEAD_PAD * (hd + 1))
        k_out[:, cols] = jnp.where(rope_lanes, kpe_at_rope, k[:, cols]).astype(BF16)
    vt = lax.dot_general(wvt_ref[...], ckv_bf, NT_DIMS, preferred_element_type=F32)
    _store_values_t(vt_out, vt, N_HEADS)


def _proj_mla_kernel(*refs, rope, ctx):
    refs = list(refs)
    x_ref, mod_ref, g_ref, win_ref, qn_ref, wuq_ref, kvn_ref, wk_ref, wvt_ref = refs[:9]
    pos = 9
    if rope:
        cq_ref, sq_ref, ck_ref, sk_ref = refs[pos:pos + 4]
        pos += 4
    q_out, k_out, vt_out, gate_out = refs[pos:pos + 4]
    pos += 4
    if ctx:
        ckv_out, kpe_out = refs[pos:pos + 2]

    h = _mod_norm(x_ref[...], g_ref[...], mod_ref[...]).astype(BF16)
    z = jnp.dot(h, win_ref[...], preferred_element_type=F32)
    cq = z[:, :MLA_Q_LORA]
    ckv = z[:, MLA_Q_LORA:MLA_Q_LORA + MLA_KV_LORA]
    gate = z[:, 640:640 + D_MODEL]
    kpe = z[:, 640 + D_MODEL:]
    gate_out[...] = _silu(gate).astype(BF16)

    qf = jnp.dot(_rms(cq, qn_ref[...]).astype(BF16), wuq_ref[...], preferred_element_type=F32)
    ckvn = _rms(ckv, kvn_ref[...])
    if ctx:
        ckv_out[...] = ckvn
        kpe_out[...] = kpe[:, :MLA_ROPE]
    if rope:
        kpe = _rope_tile(kpe, ck_ref[...], sk_ref[...])
    _mla_expand(ckvn.astype(BF16), kpe, wk_ref, wvt_ref, k_out, vt_out)
    for hd in range(N_HEADS):
        blk = qf[:, MLA_HEAD_PAD * hd:MLA_HEAD_PAD * (hd + 1)]
        if rope:
            blk = _rope_tile(blk, cq_ref[...], sq_ref[...])
        _store_queries(q_out, MLA_HEAD_PAD * hd, blk * MLA_Q_SCALE)


def _proj_gqa_kernel(*refs, qk_norm, rope, ctx):
    refs = list(refs)
    x_ref, mod_ref, g_ref, win_ref = refs[:4]
    pos = 4
    if qk_norm:
        qn_ref, kn_ref = refs[pos:pos + 2]
        pos += 2
    if rope:
        cos_ref, sin_ref = refs[pos:pos + 2]
        pos += 2
    q_out, k_out, vt_out, gate_out = refs[pos:pos + 4]
    pos += 4
    if ctx:
        kc_out, vc_out = refs[pos:pos + 2]

    nq = N_HEADS * HEAD_DIM
    nk = KV_HEADS * HEAD_DIM
    h = _mod_norm(x_ref[...], g_ref[...], mod_ref[...]).astype(BF16)
    z = jnp.dot(h, win_ref[...], preferred_element_type=F32)
    tm = z.shape[0]
    lo = lax.broadcasted_iota(jnp.int32, (tm, LANES), 1) < HEAD_DIM
    n_q_tiles = nq // LANES
    for c in range((nq + nk) // LANES):
        blk = z[:, LANES * c:LANES * (c + 1)]
        is_q = c < n_q_tiles
        if qk_norm:
            sq = blk * blk
            s_lo = jnp.sum(jnp.where(lo, sq, 0.0), axis=-1, keepdims=True)
            s_hi = jnp.sum(jnp.where(lo, 0.0, sq), axis=-1, keepdims=True)
            inv = jnp.where(lo, lax.rsqrt(s_lo * (1.0 / HEAD_DIM) + EPS),
                            lax.rsqrt(s_hi * (1.0 / HEAD_DIM) + EPS))
            blk = blk * inv * (qn_ref[...] if is_q else kn_ref[...])
        if ctx and not is_q:
            kc_out[:, LANES * (c - n_q_tiles):LANES * (c - n_q_tiles + 1)] = blk
        if rope:
            blk = _rope_tile(blk, cos_ref[...], sin_ref[...])
        if is_q:
            _store_queries(q_out, LANES * c, blk * GQA_Q_SCALE)
        else:
            k_out[:, LANES * (c - n_q_tiles):LANES * (c - n_q_tiles + 1)] = blk.astype(BF16)
    v = z[:, nq + nk:nq + 2 * nk]
    if ctx:
        vc_out[...] = v
    _store_values_t(vt_out, v.T, KV_HEADS)
    gate_out[...] = _silu(z[:, nq + 2 * nk:]).astype(BF16)


def _const_spec(shape):
    return pl.BlockSpec(shape, lambda i: (0,) * len(shape))


def _project(kind, x, mods_l, norm_g, weights, tables, *, batch, seq, ctx):
    n_tok = batch * seq
    tm = PROJ_TILE
    tiles_per_seq = max(seq // tm, 1)
    seqs_per_tile = max(tm // seq, 1)
    assert tm == seqs_per_tile * seq // tiles_per_seq and (ctx or seqs_per_tile == 1)
    v_rows = V_ROWS if ctx else HEAD_DIM
    rope = not ctx
    mod_row = (lambda i: (4, 0, 0)) if ctx else (lambda i: (i // tiles_per_seq, 0, 0))
    in_specs = [
        pl.BlockSpec((tm, D_MODEL), lambda i: (i, 0)),
        pl.BlockSpec((None, 1, 3 * D_MODEL), mod_row),
        _const_spec((1, D_MODEL)),
    ]
    args = [x, mods_l, norm_g.reshape(1, D_MODEL)]
    for w in weights:
        in_specs.append(_const_spec(w.shape))
        args.append(w)
    if rope:
        for t in tables:
            in_specs.append(pl.BlockSpec((tm, LANES), lambda i: (i % tiles_per_seq, 0)))
            args.append(t)
    if kind == 0:
        qw, kw, vw = N_HEADS * MLA_HEAD_PAD, N_HEADS * MLA_HEAD_PAD, N_HEADS * v_rows
        body = functools.partial(_proj_mla_kernel, rope=rope, ctx=ctx)
        cache_shapes = [(n_tok, MLA_KV_LORA), (n_tok, MLA_ROPE)]
    else:
        qw, kw, vw = N_HEADS * HEAD_DIM, KV_HEADS * HEAD_DIM, KV_HEADS * v_rows
        body = functools.partial(_proj_gqa_kernel, qk_norm=(kind == 1), rope=rope, ctx=ctx)
        cache_shapes = [(n_tok, kw), (n_tok, kw)]
    def feature_major_spec(w):
        if seqs_per_tile > 1:
            return pl.BlockSpec((seqs_per_tile, w, seq), lambda i: (i, 0, 0))
        return pl.BlockSpec((None, w, tm), lambda i: (i // tiles_per_seq, 0, i % tiles_per_seq))

    out_shape = [
        jax.ShapeDtypeStruct((batch, qw, seq) if ctx else (n_tok, qw), BF16),
        jax.ShapeDtypeStruct((n_tok, kw), BF16),
        jax.ShapeDtypeStruct((batch, vw, seq), BF16),
        jax.ShapeDtypeStruct((n_tok, D_MODEL), BF16),
    ]
    out_specs = [
        feature_major_spec(qw) if ctx else pl.BlockSpec((tm, qw), lambda i: (i, 0)),
        pl.BlockSpec((tm, kw), lambda i: (i, 0)),
        feature_major_spec(vw),
        pl.BlockSpec((tm, D_MODEL), lambda i: (i, 0)),
    ]
    if ctx:
        for s in cache_shapes:
            out_shape.append(jax.ShapeDtypeStruct(s, F32))
            out_specs.append(pl.BlockSpec((tm, s[1]), lambda i: (i, 0)))
    return pl.pallas_call(
        body,
        grid=(n_tok // tm,),
        in_specs=in_specs,
        out_specs=out_specs,
        out_shape=out_shape,
        compiler_params=pltpu.CompilerParams(
            dimension_semantics=("arbitrary",), vmem_limit_bytes=VMEM_LIMIT),
        name=f"proj_k{kind}_{'ctx' if ctx else 'lat'}",
    )(*args)


def _mla_cache_kernel(ckv_ref, kpe_ref, wk_ref, wvt_ref, k_out, vt_out):
    _mla_expand(ckv_ref[...].astype(BF16), kpe_ref[...], wk_ref, wvt_ref, k_out, vt_out)


def _mla_cache(ckv, kpe_pad, wk, wvt):
    b, n, _ = ckv.shape
    return pl.pallas_call(
        _mla_cache_kernel,
        grid=(b,),
        in_specs=[
            pl.BlockSpec((None, n, MLA_KV_LORA), lambda i: (i, 0, 0)),
            pl.BlockSpec((None, n, LANES), lambda i: (i, 0, 0)),
            _const_spec(wk.shape), _const_spec(wvt.shape),
        ],
        out_specs=[
            pl.BlockSpec((None, n, N_HEADS * MLA_HEAD_PAD), lambda i: (i, 0, 0)),
            pl.BlockSpec((None, N_HEADS * HEAD_DIM, n), lambda i: (i, 0, 0)),
        ],
        out_shape=[
            jax.ShapeDtypeStruct((b, n, N_HEADS * MLA_HEAD_PAD), BF16),
            jax.ShapeDtypeStruct((b, N_HEADS * HEAD_DIM, n), BF16),
        ],
        compiler_params=pltpu.CompilerParams(
            dimension_semantics=("arbitrary",), vmem_limit_bytes=VMEM_LIMIT),
        name="mla_cache",
    )(ckv, kpe_pad, wk, wvt)


def _gqa_cache_kernel(k_ref, v_ref, k_out, vt_out):
    k_out[...] = k_ref[...].astype(BF16)
    _store_values_t(vt_out, v_ref[...].T, KV_HEADS)


def _gqa_cache(k, v):
    b, n, w = k.shape
    return pl.pallas_call(
        _gqa_cache_kernel,
        grid=(b,),
        in_specs=[pl.BlockSpec((None, n, w), lambda i: (i, 0, 0))] * 2,
        out_specs=[
            pl.BlockSpec((None, n, w), lambda i: (i, 0, 0)),
            pl.BlockSpec((None, KV_HEADS * HEAD_DIM, n), lambda i: (i, 0, 0)),
        ],
        out_shape=[
            jax.ShapeDtypeStruct((b, n, w), BF16),
            jax.ShapeDtypeStruct((b, KV_HEADS * HEAD_DIM, n), BF16),
        ],
        compiler_params=pltpu.CompilerParams(dimension_semantics=("arbitrary",)),
        name="gqa_cache",
    )(k, v)


def _attn_kernel(*refs, head_w, group, has_cache, has_sink, band, final, lag_b, lag_c, interleave,
                 subs, own_keys, seq):
    refs = list(refs)
    q_ref = refs[0]
    n_own = subs + 2 if band else 1
    kn_refs = refs[1:1 + n_own]
    vtn_refs = refs[1 + n_own:1 + 2 * n_own]
    kn_ref, vtn_ref = kn_refs[0], vtn_refs[0]
    pos = 1 + 2 * n_own
    if has_cache:
        kc_ref, vtc_ref = refs[pos:pos + 2]
        pos += 2
    gate_ref, x_ref, mod_ref, wo_ref = refs[pos:pos + 4]
    pos += 4
    if has_sink:
        sink_ref = refs[pos]
        pos += 1
    if final:
        fg_ref = refs[pos]
        pos += 1
    o_ref, ot_scr, s_scr, p_scr, g_scr, y_scr = refs[pos:pos + 6]
    if band:
        bias_scr = refs[pos + 6]

    tq = TOKEN_TILE
    ch = KEY_CHUNK
    slots = s_scr.shape[0]
    segs = []
    if has_cache:
        segs.append((kc_ref.shape[0],
                     lambda e, r, n, cols: kc_ref[r:r + n, cols],
                     lambda e, rows: vtc_ref[rows, :]))
    band_of = {}
    if own_keys:
        segs.append((seq,
                     lambda e, r, n, cols: kn_ref[e * seq + r:e * seq + r + n, cols],
                     lambda e, rows: vtn_ref[e, rows, :]))
    elif band:
        for a in range(3):
            band_of[len(segs)] = a
            segs.append((tq,
                         lambda e, r, n, cols, a=a: kn_refs[e + a][r:r + n, cols],
                         lambda e, rows, a=a: vtn_refs[e + a][rows, :]))
    else:
        segs.append((seq,
                     lambda e, r, n, cols: kn_ref[r:r + n, cols],
                     lambda e, rows: vtn_ref[rows, :]))
    seg_rows = []
    chunks = []
    off = 0
    for si, (n_seg, _, _) in enumerate(segs):
        seg_rows.append((off, n_seg))
        chunks += [(si, r, off + r) for r in range(0, n_seg, ch)]
        off += n_seg
    if interleave:
        score_chunks = [c + (ch,) for c in chunks]
    else:
        score_chunks = [(si, 0, r0, n) for si, (r0, n) in enumerate(seg_rows)]
    if band:
        row = lax.broadcasted_iota(jnp.int32, (tq, tq), 0)
        col = lax.broadcasted_iota(jnp.int32, (tq, tq), 1)
        for e in range(subs):
            tile = pl.program_id(1) * subs + e
            for a in range(3):
                key_block = tile - 1 + a
                ok = ((jnp.abs((a - 1) * tq + row - col) <= WINDOW)
                      & (key_block >= 0) & (key_block < seq // tq))
                bias_scr[e, a] = jnp.where(ok, 0.0, NEG_INF)

    assert lag_b >= 1 and lag_c >= 1 and slots > max(lag_b, lag_c)
    row_max = {}
    den_part = {}
    v_rows = vtn_ref.shape[-2] * group // N_HEADS
    den_on_mxu = v_rows > HEAD_DIM

    def scores_and_max(i):
        e, hd = divmod(i, N_HEADS)
        kh = hd // group
        q_cols = slice(head_w * hd, head_w * (hd + 1))
        qh = q_ref[e, q_cols, :] if own_keys else q_ref[e * tq:(e + 1) * tq, q_cols]
        mx = None
        for si, r, g, n in score_chunks:
            kk = segs[si][1](e, r, n, slice(head_w * kh, head_w * (kh + 1)))
            if own_keys:
                s = jnp.dot(kk, qh, preferred_element_type=F32)
            else:
                s = lax.dot_general(kk, qh, NT_DIMS, preferred_element_type=F32)
            if si in band_of:
                s = s + bias_scr[e, band_of[si], r:r + n, :]
            s_scr[i % slots, g:g + n, :] = s
            part = jnp.max(s.reshape(n // 8, 8, tq), axis=0)
            mx = part if mx is None else jnp.maximum(mx, part)
            yield
        m = jnp.max(mx, axis=0, keepdims=True)
        if has_sink:
            m = jnp.maximum(m, sink_ref[hd] * LOG2E)
        row_max[i] = m

    def exponentials(i):
        hd = i % N_HEADS
        m = row_max.pop(i)
        mb = jnp.broadcast_to(m, (8, tq))
        tot = None
        for _, _, g in chunks:
            p = jnp.exp2(s_scr[i % slots, g:g + ch, :].reshape(ch // 8, 8, tq) - mb)
            if not den_on_mxu:
                part = jnp.sum(p, axis=0)
                tot = part if tot is None else tot + part
            p_scr[i % slots, g:g + ch, :] = p.reshape(ch, tq).astype(BF16)
            yield
        den = None if den_on_mxu else jnp.sum(tot, axis=0, keepdims=True)
        if has_sink:
            sink = jnp.exp2(sink_ref[hd] * LOG2E - m)
            den = sink if den is None else den + sink
        den_part[i] = den

    def weighted_values(i):
        e, hd = divmod(i, N_HEADS)
        kh = hd // group
        acc = None
        for si, (_, _, values_t) in enumerate(segs):
            r0, n = seg_rows[si]
            vt = values_t(e, slice(v_rows * kh, v_rows * (kh + 1)))
            part = jnp.dot(vt, p_scr[i % slots, r0:r0 + n, :], preferred_element_type=F32)
            acc = part if acc is None else acc + part
            yield
        den = den_part.pop(i)
        if den_on_mxu:
            mxu_sum = acc[HEAD_DIM:HEAD_DIM + 1, :]
            den = mxu_sum if den is None else den + mxu_sum
        ot_scr[e, HEAD_DIM * hd:HEAD_DIM * (hd + 1), :] = acc[:HEAD_DIM, :] / den

    def finish(e):
        rows = slice(e * tq, (e + 1) * tq)
        for c0 in range(0, D_MODEL, OUT_BLOCK):
            cols = slice(c0, c0 + OUT_BLOCK)
            o = ot_scr[e, cols, :].T
            g_scr[:, cols] = (o * gate_ref[rows, cols].astype(F32)).astype(BF16)
            yield
        for c0 in range(0, D_MODEL, OUT_BLOCK):
            cols = slice(c0, c0 + OUT_BLOCK)
            y = jnp.dot(g_scr[...], wo_ref[:, cols], preferred_element_type=F32)
            xn = x_ref[rows, cols] + mod_ref[:, 2 * D_MODEL + c0:2 * D_MODEL + c0 + OUT_BLOCK] * y
            if final:
                y_scr[:, cols] = xn
            else:
                o_ref[rows, cols] = xn
            yield
        if final:
            o_ref[rows, :] = _rms(y_scr[...], fg_ref[...])

    n_items = subs * N_HEADS
    for step in range(n_items + lag_b + lag_c + 1):
        live = []
        if step < n_items:
            live.append(scores_and_max(step))
        if 0 <= step - lag_b < n_items:
            live.append(exponentials(step - lag_b))
        done = step - lag_b - lag_c
        if 0 <= done < n_items:
            live.append(weighted_values(done))
        if done >= N_HEADS and done % N_HEADS == 0:
            live.append(finish(done // N_HEADS - 1))
        if not interleave:
            for g in live:
                for _ in g:
                    pass
            live = []
        while live:
            live = [g for g in live if next(g, True) is None]


def _attend(kind, q, k_new, vt_new, cache, gate, x, mods_l, w_out, sink, final_g, *, batch, seq, ctx,
            lags):
    lag_b, lag_c, interleave, subs = lags
    tq = TOKEN_TILE
    n_tok = batch * seq
    band = kind == 2 and not ctx
    head_w = MLA_HEAD_PAD if kind == 0 else HEAD_DIM
    kw = k_new.shape[1]
    vw = vt_new.shape[1]
    own_keys = cache is None
    assert seq == tq if own_keys else seq % (subs * tq) == 0
    grid = (batch // subs, 1) if own_keys else (batch, seq // (subs * tq))
    tok_spec = lambda w: pl.BlockSpec((subs * tq, w), lambda b, t: (b * grid[1] + t, 0))
    qw = q.shape[1]
    if own_keys:
        in_specs = [
            pl.BlockSpec((subs, qw, seq), lambda b, t: (b, 0, 0)),
            pl.BlockSpec((subs * seq, kw), lambda b, t: (b, 0)),
            pl.BlockSpec((subs, vw, seq), lambda b, t: (b, 0, 0)),
        ]
    elif band:
        n_blk = seq // tq
        block = lambda j: (lambda b, t: jnp.clip(subs * t - 1 + j, 0, n_blk - 1))
        in_specs = [tok_spec(qw)]
        in_specs += [pl.BlockSpec((tq, kw), lambda b, t, f=block(j): (b * n_blk + f(b, t), 0))
                     for j in range(subs + 2)]
        in_specs += [pl.BlockSpec((None, vw, tq), lambda b, t, f=block(j): (b, 0, f(b, t)))
                     for j in range(subs + 2)]
    else:
        in_specs = [
            tok_spec(qw),
            pl.BlockSpec((seq, kw), lambda b, t: (b, 0)),
            pl.BlockSpec((None, vw, seq), lambda b, t: (b, 0, 0)),
        ]
    n_own = subs + 2 if band else 1
    args = [q] + [k_new] * n_own + [vt_new] * n_own
    n_keys = 3 * tq if band else seq
    if cache is not None:
        kc, vtc = cache
        n_c = kc.shape[1]
        n_keys += n_c
        in_specs += [
            pl.BlockSpec((None, n_c, kw), lambda b, t: (b, 0, 0)),
            pl.BlockSpec((None, vw, n_c), lambda b, t: (b, 0, 0)),
        ]
        args += [kc, vtc]
    mod_row = (lambda b, t: (4, 0, 0)) if ctx else (lambda b, t: (b, 0, 0))
    in_specs += [
        tok_spec(D_MODEL),
        tok_spec(D_MODEL),
        pl.BlockSpec((None, 1, 3 * D_MODEL), mod_row),
        pl.BlockSpec((D_MODEL, D_MODEL), lambda b, t: (0, 0)),
    ]
    args += [gate, x, mods_l, w_out]
    if sink is not None:
        in_specs.append(pl.BlockSpec(memory_space=pltpu.SMEM))
        args.append(sink)
    if final_g is not None:
        in_specs.append(pl.BlockSpec((1, D_MODEL), lambda b, t: (0, 0)))
        args.append(final_g.reshape(1, D_MODEL))
    body = functools.partial(
        _attn_kernel,
        head_w=head_w,
        group=1 if kind == 0 else GROUPS,
        has_cache=cache is not None,
        has_sink=sink is not None,
        band=band,
        final=final_g is not None,
        lag_b=lag_b,
        lag_c=lag_c,
        interleave=interleave,
        subs=subs,
        own_keys=own_keys,
        seq=seq,
    )
    slots = max(lag_b, lag_c) + 1
    scratch = [pltpu.VMEM((subs, N_HEADS * HEAD_DIM, tq), F32),
               pltpu.VMEM((slots, n_keys, tq), F32),
               pltpu.VMEM((slots, n_keys, tq), BF16),
               pltpu.VMEM((tq, D_MODEL), BF16),
               pltpu.VMEM((tq, D_MODEL), F32)]
    if band:
        scratch.append(pltpu.VMEM((subs, 3, tq, tq), F32))
    return pl.pallas_call(
        body,
        grid=grid,
        in_specs=in_specs,
        out_specs=tok_spec(D_MODEL),
        out_shape=jax.ShapeDtypeStruct((n_tok, D_MODEL), F32),
        scratch_shapes=scratch,
        compiler_params=pltpu.CompilerParams(
            dimension_semantics=("arbitrary", "arbitrary"), vmem_limit_bytes=VMEM_LIMIT),
        name=f"attn_k{kind}_{'ctx' if ctx else 'lat'}",
    )(*args)


def _rope_lane_tables(seq, rot_dim):
    rows = seq // GRID_W
    row = jnp.repeat(jnp.arange(rows), GRID_W).astype(F32)
    col = jnp.tile(jnp.arange(GRID_W), rows).astype(F32)
    nf = rot_dim // 4
    freqs = ROPE_THETA ** (-jnp.arange(nf, dtype=F32) / nf)
    ang = jnp.concatenate([row[:, None] * freqs, col[:, None] * freqs], axis=-1)
    cos, sin = jnp.cos(ang), jnp.sin(ang)
    cos_l = jnp.repeat(cos, 2, axis=-1)
    sin_l = jnp.stack([-sin, sin], axis=-1).reshape(seq, rot_dim)
    return cos_l, sin_l


def _embed_lanes(table, start, fill):
    seq, w = table.shape
    return jnp.concatenate(
        [jnp.full((seq, start), fill, F32), table, jnp.full((seq, LANES - start - w), fill, F32)], axis=1)


def _mla_weights(w_in, q_norm, w_uq, kv_norm, w_ukv):
    c0 = MLA_Q_LORA + MLA_KV_LORA
    win = jnp.concatenate(
        [w_in[:, :c0], w_in[:, c0 + MLA_ROPE:], w_in[:, c0:c0 + MLA_ROPE],
         jnp.zeros((D_MODEL, MLA_IN_PAD - w_in.shape[1]), F32)], axis=1).astype(BF16)
    hq = MLA_NOPE + MLA_ROPE
    wuq = jnp.pad(w_uq.reshape(MLA_Q_LORA, N_HEADS, hq),
                  ((0, 0), (0, 0), (0, MLA_HEAD_PAD - hq))).reshape(MLA_Q_LORA, -1).astype(BF16)
    wkv = w_ukv.reshape(MLA_KV_LORA, N_HEADS, MLA_NOPE + MLA_V)
    wk = jnp.pad(wkv[:, :, :MLA_NOPE],
                 ((0, 0), (0, 0), (0, MLA_HEAD_PAD - MLA_NOPE))).reshape(MLA_KV_LORA, -1).astype(BF16)
    wvt = wkv[:, :, MLA_NOPE:].reshape(MLA_KV_LORA, -1).T.astype(BF16)
    return [win, q_norm.reshape(1, -1), wuq, kv_norm.reshape(1, -1), wk, wvt]


def kernel(x_prompt, x_sample, cache_mla_ckv, cache_mla_kpe, cache_gqa_k, cache_gqa_v,
           cache_swa_k, cache_swa_v, c, c_ctx, norm_g, w_ada, b_ada, w_out,
           mla_w_in, mla_q_norm, mla_w_uq, mla_kv_norm, mla_w_ukv,
           gqa_w_in, gqa_q_norm, gqa_k_norm, swa_w_in, swa_sink, final_norm_g):
    bc, sc, _ = x_prompt.shape
    bl, sl, _ = x_sample.shape
    n_past = cache_mla_ckv.shape[2]
    kvw = KV_HEADS * HEAD_DIM

    cc = jnp.concatenate([c, c_ctx[None, :], jnp.zeros((8 - bl - 1, D_MODEL), F32)], axis=0)
    mods = _ada_mods(cc, w_ada, b_ada).reshape(DEPTH, 8, 1, 3 * D_MODEL)

    cos_h, sin_h = _rope_lane_tables(sl, HEAD_DIM)
    gqa_tables = [jnp.tile(cos_h, (1, 2)), jnp.tile(sin_h, (1, 2))]
    cos_r, sin_r = _rope_lane_tables(sl, MLA_ROPE)
    mla_tables = [_embed_lanes(cos_r, MLA_NOPE, 1.0), _embed_lanes(sin_r, MLA_NOPE, 0.0),
                  _embed_lanes(cos_r, 0, 1.0), _embed_lanes(sin_r, 0, 0.0)]

    w_out_bf = w_out.astype(BF16)

    xc = x_prompt.reshape(bc * sc, D_MODEL)
    xl = x_sample.reshape(bl * sl, D_MODEL)
    new_caches = {0: ([], []), 1: ([], []), 2: ([], [])}
    for i in range(DEPTH):
        kind, j = i % 3, i // 3
        mods_l = mods[i]
        wo = w_out_bf[i]
        final_g = final_norm_g if i == DEPTH - 1 else None
        sink = None
        if kind == 0:
            weights = _mla_weights(mla_w_in[j], mla_q_norm[j], mla_w_uq[j], mla_kv_norm[j], mla_w_ukv[j])
            tables = mla_tables
            kpe_pad = jnp.pad(cache_mla_kpe[:, j], ((0, 0), (0, 0), (0, LANES - MLA_ROPE)))
            cache = _mla_cache(cache_mla_ckv[:, j], kpe_pad, weights[4], weights[5])
        else:
            tables = gqa_tables
            if kind == 1:
                weights = [gqa_w_in[j].astype(BF16),
                           jnp.tile(gqa_q_norm[j], 2).reshape(1, LANES),
                           jnp.tile(gqa_k_norm[j], 2).reshape(1, LANES)]
                cache = _gqa_cache(cache_gqa_k[:, j].reshape(bl, n_past, kvw),
                                   cache_gqa_v[:, j].reshape(bl, n_past, kvw))
            else:
                weights = [swa_w_in[j].astype(BF16)]
                sink = swa_sink[j]
                cache = _gqa_cache(cache_swa_k[:, j].reshape(bl, n_past, kvw),
                                   cache_swa_v[:, j].reshape(bl, n_past, kvw))

        q, k, vt, gate, c_a, c_b = _project(kind, xc, mods_l, norm_g[i], weights, tables,
                                            batch=bc, seq=sc, ctx=True)
        new_caches[kind][0].append(c_a)
        new_caches[kind][1].append(c_b)
        xc = _attend(kind, q, k, vt, None, gate, xc, mods_l, wo, sink, final_g,
                     batch=bc, seq=sc, ctx=True, lags=CTX_STAGE_LAGS)

        q, k, vt, gate = _project(kind, xl, mods_l, norm_g[i], weights, tables,
                                  batch=bl, seq=sl, ctx=False)
        xl = _attend(kind, q, k, vt, cache, gate, xl, mods_l, wo, sink, final_g,
                     batch=bl, seq=sl, ctx=False, lags=LAT_STAGE_LAGS)

    def stack(parts, tail):
        return jnp.stack([p.reshape((bc, sc) + tail) for p in parts], axis=1)

    return (xc.reshape(bc, sc, D_MODEL), xl.reshape(bl, sl, D_MODEL),
            stack(new_caches[0][0], (MLA_KV_LORA,)), stack(new_caches[0][1], (MLA_ROPE,)),
            stack(new_caches[1][0], (KV_HEADS, HEAD_DIM)), stack(new_caches[1][1], (KV_HEADS, HEAD_DIM)),
            stack(new_caches[2][0], (KV_HEADS, HEAD_DIM)), stack(new_caches[2][1], (KV_HEADS, HEAD_DIM)))
```

```python
import functools

import jax
import jax.numpy as jnp
from jax import lax
from jax.experimental import pallas as pl
from jax.experimental.pallas import tpu as pltpu

F32 = jnp.float32
BF16 = jnp.bfloat16

D_MODEL = 1024
DEPTH = 4
GRID_W = 64
N_HEADS = 16
HEAD_DIM = 64
KV_HEADS = 4
GROUPS = N_HEADS // KV_HEADS
MLA_Q_LORA = 384
MLA_KV_LORA = 256
MLA_NOPE = 64
MLA_ROPE = 32
MLA_V = 64
WINDOW = 128
ROPE_THETA = 10000.0
EPS = 1e-6
NEG_INF = -1e30

LOG2E = 1.4426950408889634
GQA_Q_SCALE = HEAD_DIM ** -0.5 * LOG2E
MLA_Q_SCALE = (MLA_NOPE + MLA_ROPE) ** -0.5 * LOG2E

LANES = 128
TOKEN_TILE = 256
PROJ_TILE = 512
KEY_CHUNK = 256
OUT_BLOCK = 256
CTX_STAGE_LAGS = (3, 3, True, 4)
LAT_STAGE_LAGS = (2, 2, False, 2)
MLA_HEAD_PAD = 128
V_ROWS = HEAD_DIM + 16
MLA_IN_PAD = 1792
VMEM_LIMIT = 56 * 1024 * 1024
NT_DIMS = (((1,), (1,)), ((), ()))


def _silu(v):
    return v * jax.nn.sigmoid(v)


def _rms(v, g):
    return v * lax.rsqrt(jnp.mean(v * v, axis=-1, keepdims=True) + EPS) * g


def _mod_norm(x, g, mod):
    shift = mod[:, :D_MODEL]
    scale = mod[:, D_MODEL:2 * D_MODEL]
    return _rms(x, g) * (1.0 + scale) + shift


def _rope_tile(v, cos, sin_signed):
    lane = lax.broadcasted_iota(jnp.int32, v.shape, 1)
    nxt = pltpu.roll(v, LANES - 1, axis=1)
    prv = pltpu.roll(v, 1, axis=1)
    swapped = jnp.where((lane & 1) == 0, nxt, prv)
    return v * cos + swapped * sin_signed


def _ada_kernel(c_ref, w_ref, b_ref, o_ref):
    s = _silu(c_ref[...]).astype(BF16)
    o_ref[...] = jnp.dot(s, w_ref[...].astype(BF16), preferred_element_type=F32) + b_ref[...]


def _ada_mods(cc, w_ada, b_ada):
    tn = D_MODEL
    return pl.pallas_call(
        _ada_kernel,
        grid=(DEPTH, 3 * D_MODEL // tn),
        in_specs=[
            pl.BlockSpec((8, D_MODEL), lambda i, n: (0, 0)),
            pl.BlockSpec((None, D_MODEL, tn), lambda i, n: (i, 0, n)),
            pl.BlockSpec((None, 1, tn), lambda i, n: (i, 0, n)),
        ],
        out_specs=pl.BlockSpec((None, 8, tn), lambda i, n: (i, 0, n)),
        out_shape=jax.ShapeDtypeStruct((DEPTH, 8, 3 * D_MODEL), F32),
        compiler_params=pltpu.CompilerParams(vmem_limit_bytes=VMEM_LIMIT),
        name="ada_mods",
    )(cc, w_ada, b_ada.reshape(DEPTH, 1, 3 * D_MODEL))


def _store_queries(q_out, col0, blk):
    w = blk.shape[1]
    if len(q_out.shape) == 2:
        q_out[:, col0:col0 + w] = blk.astype(BF16)
        return
    blk_t = blk.T.astype(BF16)
    seq = q_out.shape[2]
    for e in range(q_out.shape[0]):
        q_out[e, col0:col0 + w, :] = blk_t[:, e * seq:(e + 1) * seq]


def _store_values_t(vt_out, vt, n_heads):
    if len(vt_out.shape) == 3:
        seq = vt_out.shape[2]
        for e in range(vt_out.shape[0]):
            _store_values_t(vt_out.at[e], vt[:, e * seq:(e + 1) * seq], n_heads)
        return
    v_rows = vt_out.shape[0] // n_heads
    if v_rows == HEAD_DIM:
        vt_out[...] = vt.astype(BF16)
        return
    ones = jnp.ones((v_rows - HEAD_DIM, vt.shape[1]), BF16)
    for h in range(n_heads):
        vt_out[v_rows * h:v_rows * h + HEAD_DIM, :] = vt[HEAD_DIM * h:HEAD_DIM * (h + 1), :].astype(BF16)
        vt_out[v_rows * h + HEAD_DIM:v_rows * (h + 1), :] = ones


def _mla_expand(ckv_bf, kpe, wk_ref, wvt_ref, k_out, vt_out):
    k = jnp.dot(ckv_bf, wk_ref[...], preferred_element_type=F32)
    lane = lax.broadcasted_iota(jnp.int32, kpe.shape, 1)
    rope_lanes = (lane >= MLA_NOPE) & (lane < MLA_NOPE + MLA_ROPE)
    kpe_at_rope = pltpu.roll(kpe, MLA_NOPE, axis=1)
    for hd in range(N_HEADS):
        cols = slice(MLA_HEAD_PAD * hd, MLA_HEAD_PAD * (hd + 1))
        k_out[:, cols] = jnp.where(rope_lanes, kpe_at_rope, k[:, cols]).astype(BF16)
    vt = lax.dot_general(wvt_ref[...], ckv_bf, NT_DIMS, preferred_element_type=F32)
    _store_values_t(vt_out, vt, N_HEADS)


def _proj_mla_kernel(*refs, rope, ctx):
    refs = list(refs)
    x_ref, mod_ref, g_ref, win_ref, qn_ref, wuq_ref, kvn_ref, wk_ref, wvt_ref = refs[:9]
    pos = 9
    if rope:
        cq_ref, sq_ref, ck_ref, sk_ref = refs[pos:pos + 4]
        pos += 4
    q_out, k_out, vt_out, gate_out = refs[pos:pos + 4]
    pos += 4
    if ctx:
        ckv_out, kpe_out = refs[pos:pos + 2]

    h = _mod_norm(x_ref[...], g_ref[...], mod_ref[...]).astype(BF16)
    z = jnp.dot(h, win_ref[...], preferred_element_type=F32)
    cq = z[:, :MLA_Q_LORA]
    ckv = z[:, MLA_Q_LORA:MLA_Q_LORA + MLA_KV_LORA]
    gate = z[:, 640:640 + D_MODEL]
    kpe = z[:, 640 + D_MODEL:]
    gate_out[...] = _silu(gate).astype(BF16)

    qf = jnp.dot(_rms(cq, qn_ref[...]).astype(BF16), wuq_ref[...], preferred_element_type=F32)
    ckvn = _rms(ckv, kvn_ref[...])
    if ctx:
        ckv_out[...] = ckvn
        kpe_out[...] = kpe[:, :MLA_ROPE]
    if rope:
        kpe = _rope_tile(kpe, ck_ref[...], sk_ref[...])
    _mla_expand(ckvn.astype(BF16), kpe, wk_ref, wvt_ref, k_out, vt_out)
    for hd in range(N_HEADS):
        blk = qf[:, MLA_HEAD_PAD * hd:MLA_HEAD_PAD * (hd + 1)]
        if rope:
            blk = _rope_tile(blk, cq_ref[...], sq_ref[...])
        _store_queries(q_out, MLA_HEAD_PAD * hd, blk * MLA_Q_SCALE)


def _proj_gqa_kernel(*refs, qk_norm, rope, ctx):
    refs = list(refs)
    x_ref, mod_ref, g_ref, win_ref = refs[:4]
    pos = 4
    if qk_norm:
        qn_ref, kn_ref = refs[pos:pos + 2]
        pos += 2
    if rope:
        cos_ref, sin_ref = refs[pos:pos + 2]
        pos += 2
    q_out, k_out, vt_out, gate_out = refs[pos:pos + 4]
    pos += 4
    if ctx:
        kc_out, vc_out = refs[pos:pos + 2]

    nq = N_HEADS * HEAD_DIM
    nk = KV_HEADS * HEAD_DIM
    h = _mod_norm(x_ref[...], g_ref[...], mod_ref[...]).astype(BF16)
    z = jnp.dot(h, win_ref[...], preferred_element_type=F32)
    tm = z.shape[0]
    lo = lax.broadcasted_iota(jnp.int32, (tm, LANES), 1) < HEAD_DIM
    n_q_tiles = nq // LANES
    for c in range((nq + nk) // LANES):
        blk = z[:, LANES * c:LANES * (c + 1)]
        is_q = c < n_q_tiles
        if qk_norm:
            sq = blk * blk
            s_lo = jnp.sum(jnp.where(lo, sq, 0.0), axis=-1, keepdims=True)
            s_hi = jnp.sum(jnp.where(lo, 0.0, sq), axis=-1, keepdims=True)
            inv = jnp.where(lo, lax.rsqrt(s_lo * (1.0 / HEAD_DIM) + EPS),
                            lax.rsqrt(s_hi * (1.0 / HEAD_DIM) + EPS))
            blk = blk * inv * (qn_ref[...] if is_q else kn_ref[...])
        if ctx and not is_q:
            kc_out[:, LANES * (c - n_q_tiles):LANES * (c - n_q_tiles + 1)] = blk
        if rope:
            blk = _rope_tile(blk, cos_ref[...], sin_ref[...])
        if is_q:
            _store_queries(q_out, LANES * c, blk * GQA_Q_SCALE)
        else:
            k_out[:, LANES * (c - n_q_tiles):LANES * (c - n_q_tiles + 1)] = blk.astype(BF16)
    v = z[:, nq + nk:nq + 2 * nk]
    if ctx:
        vc_out[...] = v
    _store_values_t(vt_out, v.T, KV_HEADS)
    gate_out[...] = _silu(z[:, nq + 2 * nk:]).astype(BF16)


def _const_spec(shape):
    return pl.BlockSpec(shape, lambda i: (0,) * len(shape))


def _project(kind, x, mods_l, norm_g, weights, tables, *, batch, seq, ctx):
    n_tok = batch * seq
    tm = PROJ_TILE
    tiles_per_seq = max(seq // tm, 1)
    seqs_per_tile = max(tm // seq, 1)
    assert tm == seqs_per_tile * seq // tiles_per_seq and (ctx or seqs_per_tile == 1)
    v_rows = V_ROWS if ctx else HEAD_DIM
    rope = not ctx
    mod_row = (lambda i: (4, 0, 0)) if ctx else (lambda i: (i // tiles_per_seq, 0, 0))
    in_specs = [
        pl.BlockSpec((tm, D_MODEL), lambda i: (i, 0)),
        pl.BlockSpec((None, 1, 3 * D_MODEL), mod_row),
        _const_spec((1, D_MODEL)),
    ]
    args = [x, mods_l, norm_g.reshape(1, D_MODEL)]
    for w in weights:
        in_specs.append(_const_spec(w.shape))
        args.append(w)
    if rope:
        for t in tables:
            in_specs.append(pl.BlockSpec((tm, LANES), lambda i: (i % tiles_per_seq, 0)))
            args.append(t)
    if kind == 0:
        qw, kw, vw = N_HEADS * MLA_HEAD_PAD, N_HEADS * MLA_HEAD_PAD, N_HEADS * v_rows
        body = functools.partial(_proj_mla_kernel, rope=rope, ctx=ctx)
        cache_shapes = [(n_tok, MLA_KV_LORA), (n_tok, MLA_ROPE)]
    else:
        qw, kw, vw = N_HEADS * HEAD_DIM, KV_HEADS * HEAD_DIM, KV_HEADS * v_rows
        body = functools.partial(_proj_gqa_kernel, qk_norm=(kind == 1), rope=rope, ctx=ctx)
        cache_shapes = [(n_tok, kw), (n_tok, kw)]
    def feature_major_spec(w):
        if seqs_per_tile > 1:
            return pl.BlockSpec((seqs_per_tile, w, seq), lambda i: (i, 0, 0))
        return pl.BlockSpec((None, w, tm), lambda i: (i // tiles_per_seq, 0, i % tiles_per_seq))

    out_shape = [
        jax.ShapeDtypeStruct((batch, qw, seq) if ctx else (n_tok, qw), BF16),
        jax.ShapeDtypeStruct((n_tok, kw), BF16),
        jax.ShapeDtypeStruct((batch, vw, seq), BF16),
        jax.ShapeDtypeStruct((n_tok, D_MODEL), BF16),
    ]
    out_specs = [
        feature_major_spec(qw) if ctx else pl.BlockSpec((tm, qw), lambda i: (i, 0)),
        pl.BlockSpec((tm, kw), lambda i: (i, 0)),
        feature_major_spec(vw),
        pl.BlockSpec((tm, D_MODEL), lambda i: (i, 0)),
    ]
    if ctx:
        for s in cache_shapes:
            out_shape.append(jax.ShapeDtypeStruct(s, F32))
            out_specs.append(pl.BlockSpec((tm, s[1]), lambda i: (i, 0)))
    return pl.pallas_call(
        body,
        grid=(n_tok // tm,),
        in_specs=in_specs,
        out_specs=out_specs,
        out_shape=out_shape,
        compiler_params=pltpu.CompilerParams(
            dimension_semantics=("arbitrary",), vmem_limit_bytes=VMEM_LIMIT),
        name=f"proj_k{kind}_{'ctx' if ctx else 'lat'}",
    )(*args)


def _mla_cache_kernel(ckv_ref, kpe_ref, wk_ref, wvt_ref, k_out, vt_out):
    _mla_expand(ckv_ref[...].astype(BF16), kpe_ref[...], wk_ref, wvt_ref, k_out, vt_out)


def _mla_cache(ckv, kpe_pad, wk, wvt):
    b, n, _ = ckv.shape
    return pl.pallas_call(
        _mla_cache_kernel,
        grid=(b,),
        in_specs=[
            pl.BlockSpec((None, n, MLA_KV_LORA), lambda i: (i, 0, 0)),
            pl.BlockSpec((None, n, LANES), lambda i: (i, 0, 0)),
            _const_spec(wk.shape), _const_spec(wvt.shape),
        ],
        out_specs=[
            pl.BlockSpec((None, n, N_HEADS * MLA_HEAD_PAD), lambda i: (i, 0, 0)),
            pl.BlockSpec((None, N_HEADS * HEAD_DIM, n), lambda i: (i, 0, 0)),
        ],
        out_shape=[
            jax.ShapeDtypeStruct((b, n, N_HEADS * MLA_HEAD_PAD), BF16),
            jax.ShapeDtypeStruct((b, N_HEADS * HEAD_DIM, n), BF16),
        ],
        compiler_params=pltpu.CompilerParams(
            dimension_semantics=("arbitrary",), vmem_limit_bytes=VMEM_LIMIT),
        name="mla_cache",
    )(ckv, kpe_pad, wk, wvt)


def _gqa_cache_kernel(k_ref, v_ref, k_out, vt_out):
    k_out[...] = k_ref[...].astype(BF16)
    _store_values_t(vt_out, v_ref[...].T, KV_HEADS)


def _gqa_cache(k, v):
    b, n, w = k.shape
    return pl.pallas_call(
        _gqa_cache_kernel,
        grid=(b,),
        in_specs=[pl.BlockSpec((None, n, w), lambda i: (i, 0, 0))] * 2,
        out_specs=[
            pl.BlockSpec((None, n, w), lambda i: (i, 0, 0)),
            pl.BlockSpec((None, KV_HEADS * HEAD_DIM, n), lambda i: (i, 0, 0)),
        ],
        out_shape=[
            jax.ShapeDtypeStruct((b, n, w), BF16),
            jax.ShapeDtypeStruct((b, KV_HEADS * HEAD_DIM, n), BF16),
        ],
        compiler_params=pltpu.CompilerParams(dimension_semantics=("arbitrary",)),
        name="gqa_cache",
    )(k, v)


def _attn_kernel(*refs, head_w, group, has_cache, has_sink, band, final, lag_b, lag_c, interleave,
                 subs, own_keys, seq):
    refs = list(refs)
    q_ref = refs[0]
    n_own = subs + 2 if band else 1
    kn_refs = refs[1:1 + n_own]
    vtn_refs = refs[1 + n_own:1 + 2 * n_own]
    kn_ref, vtn_ref = kn_refs[0], vtn_refs[0]
    pos = 1 + 2 * n_own
    if has_cache:
        kc_ref, vtc_ref = refs[pos:pos + 2]
        pos += 2
    gate_ref, x_ref, mod_ref, wo_ref = refs[pos:pos + 4]
    pos += 4
    if has_sink:
        sink_ref = refs[pos]
        pos += 1
    if final:
        fg_ref = refs[pos]
        pos += 1
    o_ref, ot_scr, s_scr, p_scr, g_scr, y_scr = refs[pos:pos + 6]
    if band:
        bias_scr = refs[pos + 6]

    tq = TOKEN_TILE
    ch = KEY_CHUNK
    slots = s_scr.shape[0]
    segs = []
    if has_cache:
        segs.append((kc_ref.shape[0],
                     lambda e, r, n, cols: kc_ref[r:r + n, cols],
                     lambda e, rows: vtc_ref[rows, :]))
    band_of = {}
    if own_keys:
        segs.append((seq,
                     lambda e, r, n, cols: kn_ref[e * seq + r:e * seq + r + n, cols],
                     lambda e, rows: vtn_ref[e, rows, :]))
    elif band:
        for a in range(3):
            band_of[len(segs)] = a
            segs.append((tq,
                         lambda e, r, n, cols, a=a: kn_refs[e + a][r:r + n, cols],
                         lambda e, rows, a=a: vtn_refs[e + a][rows, :]))
    else:
        segs.append((seq,
                     lambda e, r, n, cols: kn_ref[r:r + n, cols],
                     lambda e, rows: vtn_ref[rows, :]))
    seg_rows = []
    chunks = []
    off = 0
    for si, (n_seg, _, _) in enumerate(segs):
        seg_rows.append((off, n_seg))
        chunks += [(si, r, off + r) for r in range(0, n_seg, ch)]
        off += n_seg
    if interleave:
        score_chunks = [c + (ch,) for c in chunks]
    else:
        score_chunks = [(si, 0, r0, n) for si, (r0, n) in enumerate(seg_rows)]
    if band:
        row = lax.broadcasted_iota(jnp.int32, (tq, tq), 0)
        col = lax.broadcasted_iota(jnp.int32, (tq, tq), 1)
        for e in range(subs):
            tile = pl.program_id(1) * subs + e
            for a in range(3):
                key_block = tile - 1 + a
                ok = ((jnp.abs((a - 1) * tq + row - col) <= WINDOW)
                      & (key_block >= 0) & (key_block < seq // tq))
                bias_scr[e, a] = jnp.where(ok, 0.0, NEG_INF)

    assert lag_b >= 1 and lag_c >= 1 and slots > max(lag_b, lag_c)
    row_max = {}
    den_part = {}
    v_rows = vtn_ref.shape[-2] * group // N_HEADS
    den_on_mxu = v_rows > HEAD_DIM

    def scores_and_max(i):
        e, hd = divmod(i, N_HEADS)
        kh = hd // group
        q_cols = slice(head_w * hd, head_w * (hd + 1))
        qh = q_ref[e, q_cols, :] if own_keys else q_ref[e * tq:(e + 1) * tq, q_cols]
        mx = None
        for si, r, g, n in score_chunks:
            kk = segs[si][1](e, r, n, slice(head_w * kh, head_w * (kh + 1)))
            if own_keys:
                s = jnp.dot(kk, qh, preferred_element_type=F32)
            else:
                s = lax.dot_general(kk, qh, NT_DIMS, preferred_element_type=F32)
            if si in band_of:
                s = s + bias_scr[e, band_of[si], r:r + n, :]
            s_scr[i % slots, g:g + n, :] = s
            part = jnp.max(s.reshape(n // 8, 8, tq), axis=0)
            mx = part if mx is None else jnp.maximum(mx, part)
            yield
        m = jnp.max(mx, axis=0, keepdims=True)
        if has_sink:
            m = jnp.maximum(m, sink_ref[hd] * LOG2E)
        row_max[i] = m

    def exponentials(i):
        hd = i % N_HEADS
        m = row_max.pop(i)
        mb = jnp.broadcast_to(m, (8, tq))
        tot = None
        for _, _, g in chunks:
            p = jnp.exp2(s_scr[i % slots, g:g + ch, :].reshape(ch // 8, 8, tq) - mb)
            if not den_on_mxu:
                part = jnp.sum(p, axis=0)
                tot = part if tot is None else tot + part
            p_scr[i % slots, g:g + ch, :] = p.reshape(ch, tq).astype(BF16)
            yield
        den = None if den_on_mxu else jnp.sum(tot, axis=0, keepdims=True)
        if has_sink:
            sink = jnp.exp2(sink_ref[hd] * LOG2E - m)
            den = sink if den is None else den + sink
        den_part[i] = den

    def weighted_values(i):
        e, hd = divmod(i, N_HEADS)
        kh = hd // group
        acc = None
        for si, (_, _, values_t) in enumerate(segs):
            r0, n = seg_rows[si]
            vt = values_t(e, slice(v_rows * kh, v_rows * (kh + 1)))
            part = jnp.dot(vt, p_scr[i % slots, r0:r0 + n, :], preferred_element_type=F32)
            acc = part if acc is None else acc + part
            yield
        den = den_part.pop(i)
        if den_on_mxu:
            mxu_sum = acc[HEAD_DIM:HEAD_DIM + 1, :]
            den = mxu_sum if den is None else den + mxu_sum
        ot_scr[e, HEAD_DIM * hd:HEAD_DIM * (hd + 1), :] = acc[:HEAD_DIM, :] / den

    def finish(e):
        rows = slice(e * tq, (e + 1) * tq)
        for c0 in range(0, D_MODEL, OUT_BLOCK):
            cols = slice(c0, c0 + OUT_BLOCK)
            o = ot_scr[e, cols, :].T
            g_scr[:, cols] = (o * gate_ref[rows, cols].astype(F32)).astype(BF16)
            yield
        for c0 in range(0, D_MODEL, OUT_BLOCK):
            cols = slice(c0, c0 + OUT_BLOCK)
            y = jnp.dot(g_scr[...], wo_ref[:, cols], preferred_element_type=F32)
            xn = x_ref[rows, cols] + mod_ref[:, 2 * D_MODEL + c0:2 * D_MODEL + c0 + OUT_BLOCK] * y
            if final:
                y_scr[:, cols] = xn
            else:
                o_ref[rows, cols] = xn
            yield
        if final:
            o_ref[rows, :] = _rms(y_scr[...], fg_ref[...])

    n_items = subs * N_HEADS
    for step in range(n_items + lag_b + lag_c + 1):
        live = []
        if step < n_items:
            live.append(scores_and_max(step))
        if 0 <= step - lag_b < n_items:
            live.append(exponentials(step - lag_b))
        done = step - lag_b - lag_c
        if 0 <= done < n_items:
            live.append(weighted_values(done))
        if done >= N_HEADS and done % N_HEADS == 0:
            live.append(finish(done // N_HEADS - 1))
        if not interleave:
            for g in live:
                for _ in g:
                    pass
            live = []
        while live:
            live = [g for g in live if next(g, True) is None]


def _attend(kind, q, k_new, vt_new, cache, gate, x, mods_l, w_out, sink, final_g, *, batch, seq, ctx,
            lags):
    lag_b, lag_c, interleave, subs = lags
    tq = TOKEN_TILE
    n_tok = batch * seq
    band = kind == 2 and not ctx
    head_w = MLA_HEAD_PAD if kind == 0 else HEAD_DIM
    kw = k_new.shape[1]
    vw = vt_new.shape[1]
    own_keys = cache is None
    assert seq == tq if own_keys else seq % (subs * tq) == 0
    grid = (batch // subs, 1) if own_keys else (batch, seq // (subs * tq))
    tok_spec = lambda w: pl.BlockSpec((subs * tq, w), lambda b, t: (b * grid[1] + t, 0))
    qw = q.shape[1]
    if own_keys:
        in_specs = [
            pl.BlockSpec((subs, qw, seq), lambda b, t: (b, 0, 0)),
            pl.BlockSpec((subs * seq, kw), lambda b, t: (b, 0)),
            pl.BlockSpec((subs, vw, seq), lambda b, t: (b, 0, 0)),
        ]
    elif band:
        n_blk = seq // tq
        block = lambda j: (lambda b, t: jnp.clip(subs * t - 1 + j, 0, n_blk - 1))
        in_specs = [tok_spec(qw)]
        in_specs += [pl.BlockSpec((tq, kw), lambda b, t, f=block(j): (b * n_blk + f(b, t), 0))
                     for j in range(subs + 2)]
        in_specs += [pl.BlockSpec((None, vw, tq), lambda b, t, f=block(j): (b, 0, f(b, t)))
                     for j in range(subs + 2)]
    else:
        in_specs = [
            tok_spec(qw),
            pl.BlockSpec((seq, kw), lambda b, t: (b, 0)),
            pl.BlockSpec((None, vw, seq), lambda b, t: (b, 0, 0)),
        ]
    n_own = subs + 2 if band else 1
    args = [q] + [k_new] * n_own + [vt_new] * n_own
    n_keys = 3 * tq if band else seq
    if cache is not None:
        kc, vtc = cache
        n_c = kc.shape[1]
        n_keys += n_c
        in_specs += [
            pl.BlockSpec((None, n_c, kw), lambda b, t: (b, 0, 0)),
            pl.BlockSpec((None, vw, n_c), lambda b, t: (b, 0, 0)),
        ]
        args += [kc, vtc]
    mod_row = (lambda b, t: (4, 0, 0)) if ctx else (lambda b, t: (b, 0, 0))
    in_specs += [
        tok_spec(D_MODEL),
        tok_spec(D_MODEL),
        pl.BlockSpec((None, 1, 3 * D_MODEL), mod_row),
        pl.BlockSpec((D_MODEL, D_MODEL), lambda b, t: (0, 0)),
    ]
    args += [gate, x, mods_l, w_out]
    if sink is not None:
        in_specs.append(pl.BlockSpec(memory_space=pltpu.SMEM))
        args.append(sink)
    if final_g is not None:
        in_specs.append(pl.BlockSpec((1, D_MODEL), lambda b, t: (0, 0)))
        args.append(final_g.reshape(1, D_MODEL))
    body = functools.partial(
        _attn_kernel,
        head_w=head_w,
        group=1 if kind == 0 else GROUPS,
        has_cache=cache is not None,
        has_sink=sink is not None,
        band=band,
        final=final_g is not None,
        lag_b=lag_b,
        lag_c=lag_c,
        interleave=interleave,
        subs=subs,
        own_keys=own_keys,
        seq=seq,
    )
    slots = max(lag_b, lag_c) + 1
    scratch = [pltpu.VMEM((subs, N_HEADS * HEAD_DIM, tq), F32),
               pltpu.VMEM((slots, n_keys, tq), F32),
               pltpu.VMEM((slots, n_keys, tq), BF16),
               pltpu.VMEM((tq, D_MODEL), BF16),
               pltpu.VMEM((tq, D_MODEL), F32)]
    if band:
        scratch.append(pltpu.VMEM((subs, 3, tq, tq), F32))
    return pl.pallas_call(
        body,
        grid=grid,
        in_specs=in_specs,
        out_specs=tok_spec(D_MODEL),
        out_shape=jax.ShapeDtypeStruct((n_tok, D_MODEL), F32),
        scratch_shapes=scratch,
        compiler_params=pltpu.CompilerParams(
            dimension_semantics=("arbitrary", "arbitrary"), vmem_limit_bytes=VMEM_LIMIT),
        name=f"attn_k{kind}_{'ctx' if ctx else 'lat'}",
    )(*args)


def _rope_lane_tables(seq, rot_dim):
    rows = seq // GRID_W
    row = jnp.repeat(jnp.arange(rows), GRID_W).astype(F32)
    col = jnp.tile(jnp.arange(GRID_W), rows).astype(F32)
    nf = rot_dim // 4
    freqs = ROPE_THETA ** (-jnp.arange(nf, dtype=F32) / nf)
    ang = jnp.concatenate([row[:, None] * freqs, col[:, None] * freqs], axis=-1)
    cos, sin = jnp.cos(ang), jnp.sin(ang)
    cos_l = jnp.repeat(cos, 2, axis=-1)
    sin_l = jnp.stack([-sin, sin], axis=-1).reshape(seq, rot_dim)
    return cos_l, sin_l


def _embed_lanes(table, start, fill):
    seq, w = table.shape
    return jnp.concatenate(
        [jnp.full((seq, start), fill, F32), table, jnp.full((seq, LANES - start - w), fill, F32)], axis=1)


def _mla_weights(w_in, q_norm, w_uq, kv_norm, w_ukv):
    c0 = MLA_Q_LORA + MLA_KV_LORA
    win = jnp.concatenate(
        [w_in[:, :c0], w_in[:, c0 + MLA_ROPE:], w_in[:, c0:c0 + MLA_ROPE],
         jnp.zeros((D_MODEL, MLA_IN_PAD - w_in.shape[1]), F32)], axis=1).astype(BF16)
    hq = MLA_NOPE + MLA_ROPE
    wuq = jnp.pad(w_uq.reshape(MLA_Q_LORA, N_HEADS, hq),
                  ((0, 0), (0, 0), (0, MLA_HEAD_PAD - hq))).reshape(MLA_Q_LORA, -1).astype(BF16)
    wkv = w_ukv.reshape(MLA_KV_LORA, N_HEADS, MLA_NOPE + MLA_V)
    wk = jnp.pad(wkv[:, :, :MLA_NOPE],
                 ((0, 0), (0, 0), (0, MLA_HEAD_PAD - MLA_NOPE))).reshape(MLA_KV_LORA, -1).astype(BF16)
    wvt = wkv[:, :, MLA_NOPE:].reshape(MLA_KV_LORA, -1).T.astype(BF16)
    return [win, q_norm.reshape(1, -1), wuq, kv_norm.reshape(1, -1), wk, wvt]


def kernel(x_prompt, x_sample, cache_mla_ckv, cache_mla_kpe, cache_gqa_k, cache_gqa_v,
           cache_swa_k, cache_swa_v, c, c_ctx, norm_g, w_ada, b_ada, w_out,
           mla_w_in, mla_q_norm, mla_w_uq, mla_kv_norm, mla_w_ukv,
           gqa_w_in, gqa_q_norm, gqa_k_norm, swa_w_in, swa_sink, final_norm_g):
    bc, sc, _ = x_prompt.shape
    bl, sl, _ = x_sample.shape
    n_past = cache_mla_ckv.shape[2]
    kvw = KV_HEADS * HEAD_DIM

    cc = jnp.concatenate([c, c_ctx[None, :], jnp.zeros((8 - bl - 1, D_MODEL), F32)], axis=0)
    mods = _ada_mods(cc, w_ada, b_ada).reshape(DEPTH, 8, 1, 3 * D_MODEL)

    cos_h, sin_h = _rope_lane_tables(sl, HEAD_DIM)
    gqa_tables = [jnp.tile(cos_h, (1, 2)), jnp.tile(sin_h, (1, 2))]
    cos_r, sin_r = _rope_lane_tables(sl, MLA_ROPE)
    mla_tables = [_embed_lanes(cos_r, MLA_NOPE, 1.0), _embed_lanes(sin_r, MLA_NOPE, 0.0),
                  _embed_lanes(cos_r, 0, 1.0), _embed_lanes(sin_r, 0, 0.0)]

    w_out_bf = w_out.astype(BF16)

    xc = x_prompt.reshape(bc * sc, D_MODEL)
    xl = x_sample.reshape(bl * sl, D_MODEL)
    new_caches = {0: ([], []), 1: ([], []), 2: ([], [])}
    for i in range(DEPTH):
        kind, j = i % 3, i // 3
        mods_l = mods[i]
        wo = w_out_bf[i]
        final_g = final_norm_g if i == DEPTH - 1 else None
        sink = None
        if kind == 0:
            weights = _mla_weights(mla_w_in[j], mla_q_norm[j], mla_w_uq[j], mla_kv_norm[j], mla_w_ukv[j])
            tables = mla_tables
            kpe_pad = jnp.pad(cache_mla_kpe[:, j], ((0, 0), (0, 0), (0, LANES - MLA_ROPE)))
            cache = _mla_cache(cache_mla_ckv[:, j], kpe_pad, weights[4], weights[5])
        else:
            tables = gqa_tables
            if kind == 1:
                weights = [gqa_w_in[j].astype(BF16),
                           jnp.tile(gqa_q_norm[j], 2).reshape(1, LANES),
                           jnp.tile(gqa_k_norm[j], 2).reshape(1, LANES)]
                cache = _gqa_cache(cache_gqa_k[:, j].reshape(bl, n_past, kvw),
                                   cache_gqa_v[:, j].reshape(bl, n_past, kvw))
            else:
                weights = [swa_w_in[j].astype(BF16)]
                sink = swa_sink[j]
                cache = _gqa_cache(cache_swa_k[:, j].reshape(bl, n_past, kvw),
                                   cache_swa_v[:, j].reshape(bl, n_past, kvw))

        q, k, vt, gate, c_a, c_b = _project(kind, xc, mods_l, norm_g[i], weights, tables,
                                            batch=bc, seq=sc, ctx=True)
        new_caches[kind][0].append(c_a)
        new_caches[kind][1].append(c_b)
        xc = _attend(kind, q, k, vt, None, gate, xc, mods_l, wo, sink, final_g,
                     batch=bc, seq=sc, ctx=True, lags=CTX_STAGE_LAGS)

        q, k, vt, gate = _project(kind, xl, mods_l, norm_g[i], weights, tables,
                                  batch=bl, seq=sl, ctx=False)
        xl = _attend(kind, q, k, vt, cache, gate, xl, mods_l, wo, sink, final_g,
                     batch=bl, seq=sl, ctx=False, lags=LAT_STAGE_LAGS)

    def stack(parts, tail):
        return jnp.stack([p.reshape((bc, sc) + tail) for p in parts], axis=1)

    return (xc.reshape(bc, sc, D_MODEL), xl.reshape(bl, sl, D_MODEL),
            stack(new_caches[0][0], (MLA_KV_LORA,)), stack(new_caches[0][1], (MLA_ROPE,)),
            stack(new_caches[1][0], (KV_HEADS, HEAD_DIM)), stack(new_caches[1][1], (KV_HEADS, HEAD_DIM)),
            stack(new_caches[2][0], (KV_HEADS, HEAD_DIM)), stack(new_caches[2][1], (KV_HEADS, HEAD_DIM)))
```

```python
import functools

import jax
import jax.numpy as jnp
from jax import lax
from jax.experimental import pallas as pl
from jax.experimental.pallas import tpu as pltpu

F32 = jnp.float32
BF16 = jnp.bfloat16

D_MODEL = 1024
DEPTH = 4
GRID_W = 64
N_HEADS = 16
HEAD_DIM = 64
KV_HEADS = 4
GROUPS = N_HEADS // KV_HEADS
MLA_Q_LORA = 384
MLA_KV_LORA = 256
MLA_NOPE = 64
MLA_ROPE = 32
MLA_V = 64
WINDOW = 128
ROPE_THETA = 10000.0
EPS = 1e-6
NEG_INF = -1e30

LOG2E = 1.4426950408889634
GQA_Q_SCALE = HEAD_DIM ** -0.5 * LOG2E
MLA_Q_SCALE = (MLA_NOPE + MLA_ROPE) ** -0.5 * LOG2E

LANES = 128
TOKEN_TILE = 256
PROJ_TILE = 512
KEY_CHUNK = 256
OUT_BLOCK = 256
CTX_STAGE_LAGS = (3, 3, True, 4)
LAT_STAGE_LAGS = (2, 2, False, 2)
MLA_HEAD_PAD = 128
V_ROWS = HEAD_DIM + 16
MLA_IN_PAD = 1792
VMEM_LIMIT = 56 * 1024 * 1024
NT_DIMS = (((1,), (1,)), ((), ()))


def _silu(v):
    return v * jax.nn.sigmoid(v)


def _rms(v, g):
    return v * lax.rsqrt(jnp.mean(v * v, axis=-1, keepdims=True) + EPS) * g


def _mod_norm(x, g, mod):
    shift = mod[:, :D_MODEL]
    scale = mod[:, D_MODEL:2 * D_MODEL]
    return _rms(x, g) * (1.0 + scale) + shift


def _rope_tile(v, cos, sin_signed):
    lane = lax.broadcasted_iota(jnp.int32, v.shape, 1)
    nxt = pltpu.roll(v, LANES - 1, axis=1)
    prv = pltpu.roll(v, 1, axis=1)
    swapped = jnp.where((lane & 1) == 0, nxt, prv)
    return v * cos + swapped * sin_signed


def _ada_kernel(c_ref, w_ref, b_ref, o_ref):
    s = _silu(c_ref[...]).astype(BF16)
    o_ref[...] = jnp.dot(s, w_ref[...].astype(BF16), preferred_element_type=F32) + b_ref[...]


def _ada_mods(cc, w_ada, b_ada):
    tn = 3 * D_MODEL
    return pl.pallas_call(
        _ada_kernel,
        grid=(DEPTH, 3 * D_MODEL // tn),
        in_specs=[
            pl.BlockSpec((8, D_MODEL), lambda i, n: (0, 0)),
            pl.BlockSpec((None, D_MODEL, tn), lambda i, n: (i, 0, n)),
            pl.BlockSpec((None, 1, tn), lambda i, n: (i, 0, n)),
        ],
        out_specs=pl.BlockSpec((None, 8, tn), lambda i, n: (i, 0, n)),
        out_shape=jax.ShapeDtypeStruct((DEPTH, 8, 3 * D_MODEL), F32),
        compiler_params=pltpu.CompilerParams(vmem_limit_bytes=VMEM_LIMIT),
        name="ada_mods",
    )(cc, w_ada, b_ada.reshape(DEPTH, 1, 3 * D_MODEL))


def _store_queries(q_out, col0, blk):
    w = blk.shape[1]
    if len(q_out.shape) == 2:
        q_out[:, col0:col0 + w] = blk.astype(BF16)
        return
    blk_t = blk.T.astype(BF16)
    seq = q_out.shape[2]
    for e in range(q_out.shape[0]):
        q_out[e, col0:col0 + w, :] = blk_t[:, e * seq:(e + 1) * seq]


def _store_values_t(vt_out, vt, n_heads):
    if len(vt_out.shape) == 3:
        seq = vt_out.shape[2]
        for e in range(vt_out.shape[0]):
            _store_values_t(vt_out.at[e], vt[:, e * seq:(e + 1) * seq], n_heads)
        return
    v_rows = vt_out.shape[0] // n_heads
    if v_rows == HEAD_DIM:
        vt_out[...] = vt.astype(BF16)
        return
    ones = jnp.ones((v_rows - HEAD_DIM, vt.shape[1]), BF16)
    for h in range(n_heads):
        vt_out[v_rows * h:v_rows * h + HEAD_DIM, :] = vt[HEAD_DIM * h:HEAD_DIM * (h + 1), :].astype(BF16)
        vt_out[v_rows * h + HEAD_DIM:v_rows * (h + 1), :] = ones


def _mla_expand(ckv_bf, kpe, wk_ref, wvt_ref, k_out, vt_out):
    k = jnp.dot(ckv_bf, wk_ref[...], preferred_element_type=F32)
    lane = lax.broadcasted_iota(jnp.int32, kpe.shape, 1)
    rope_lanes = (lane >= MLA_NOPE) & (lane < MLA_NOPE + MLA_ROPE)
    kpe_at_rope = pltpu.roll(kpe, MLA_NOPE, axis=1)
    for hd in range(N_HEADS):
        cols = slice(MLA_HEAD_PAD * hd, MLA_HEAD_PAD * (hd + 1))
        k_out[:, cols] = jnp.where(rope_lanes, kpe_at_rope, k[:, cols]).astype(BF16)
    vt = lax.dot_general(wvt_ref[...], ckv_bf, NT_DIMS, preferred_element_type=F32)
    _store_values_t(vt_out, vt, N_HEADS)


def _proj_mla_kernel(*refs, rope, ctx, n_aliased):
    refs = list(refs)
    x_ref, mod_ref, g_ref, win_ref, qn_ref, wuq_ref, kvn_ref, wk_ref, wvt_ref = refs[:9]
    pos = 9
    if rope:
        cq_ref, sq_ref, ck_ref, sk_ref = refs[pos:pos + 4]
        pos += 4
    pos += n_aliased
    q_out, k_out, vt_out, gate_out = refs[pos:pos + 4]
    pos += 4
    if ctx:
        ckv_out, kpe_out = refs[pos:pos + 2]

    h = _mod_norm(x_ref[...], g_ref[...], mod_ref[...]).astype(BF16)
    z = jnp.dot(h, win_ref[...], preferred_element_type=F32)
    cq = z[:, :MLA_Q_LORA]
    ckv = z[:, MLA_Q_LORA:MLA_Q_LORA + MLA_KV_LORA]
    gate = z[:, 640:640 + D_MODEL]
    kpe = z[:, 640 + D_MODEL:]
    gate_out[...] = _silu(gate).astype(BF16)

    qf = jnp.dot(_rms(cq, qn_ref[...]).astype(BF16), wuq_ref[...], preferred_element_type=F32)
    ckvn = _rms(ckv, kvn_ref[...])
    if ctx:
        seq = ckv_out.shape[1]
        for e in range(ckv_out.shape[0]):
            ckv_out[e] = ckvn[e * seq:(e + 1) * seq]
            kpe_out[e] = kpe[e * seq:(e + 1) * seq, :MLA_ROPE]
    if rope:
        kpe = _rope_tile(kpe, ck_ref[...], sk_ref[...])
    _mla_expand(ckvn.astype(BF16), kpe, wk_ref, wvt_ref, k_out, vt_out)
    for hd in range(N_HEADS):
        blk = qf[:, MLA_HEAD_PAD * hd:MLA_HEAD_PAD * (hd + 1)]
        if rope:
            blk = _rope_tile(blk, cq_ref[...], sq_ref[...])
        _store_queries(q_out, MLA_HEAD_PAD * hd, blk * MLA_Q_SCALE)


def _proj_gqa_kernel(*refs, qk_norm, rope, ctx):
    refs = list(refs)
    x_ref, mod_ref, g_ref, win_ref = refs[:4]
    pos = 4
    if qk_norm:
        qn_ref, kn_ref = refs[pos:pos + 2]
        pos += 2
    if rope:
        cos_ref, sin_ref = refs[pos:pos + 2]
        pos += 2
    q_out, k_out, vt_out, gate_out = refs[pos:pos + 4]
    pos += 4
    if ctx:
        kc_out, vc_out = refs[pos:pos + 2]

    nq = N_HEADS * HEAD_DIM
    nk = KV_HEADS * HEAD_DIM
    h = _mod_norm(x_ref[...], g_ref[...], mod_ref[...]).astype(BF16)
    z = jnp.dot(h, win_ref[...], preferred_element_type=F32)
    tm = z.shape[0]
    lo = lax.broadcasted_iota(jnp.int32, (tm, LANES), 1) < HEAD_DIM
    n_q_tiles = nq // LANES
    for c in range((nq + nk) // LANES):
        blk = z[:, LANES * c:LANES * (c + 1)]
        is_q = c < n_q_tiles
        if qk_norm:
            sq = blk * blk
            s_lo = jnp.sum(jnp.where(lo, sq, 0.0), axis=-1, keepdims=True)
            s_hi = jnp.sum(jnp.where(lo, 0.0, sq), axis=-1, keepdims=True)
            inv = jnp.where(lo, lax.rsqrt(s_lo * (1.0 / HEAD_DIM) + EPS),
                            lax.rsqrt(s_hi * (1.0 / HEAD_DIM) + EPS))
            blk = blk * inv * (qn_ref[...] if is_q else kn_ref[...])
        if ctx and not is_q:
            kc_out[:, LANES * (c - n_q_tiles):LANES * (c - n_q_tiles + 1)] = blk
        if rope:
            blk = _rope_tile(blk, cos_ref[...], sin_ref[...])
        if is_q:
            _store_queries(q_out, LANES * c, blk * GQA_Q_SCALE)
        else:
            k_out[:, LANES * (c - n_q_tiles):LANES * (c - n_q_tiles + 1)] = blk.astype(BF16)
    v = z[:, nq + nk:nq + 2 * nk]
    if ctx:
        vc_out[...] = v
    _store_values_t(vt_out, v.T, KV_HEADS)
    gate_out[...] = _silu(z[:, nq + 2 * nk:]).astype(BF16)


def _const_spec(shape):
    return pl.BlockSpec(shape, lambda i: (0,) * len(shape))


def _project(kind, x, mods_l, norm_g, weights, tables, *, batch, seq, ctx, layer_slot=(0, 1),
             stacked=None):
    n_tok = batch * seq
    tm = PROJ_TILE
    tiles_per_seq = max(seq // tm, 1)
    seqs_per_tile = max(tm // seq, 1)
    assert tm == seqs_per_tile * seq // tiles_per_seq and (ctx or seqs_per_tile == 1)
    v_rows = V_ROWS if ctx else HEAD_DIM
    rope = not ctx
    mod_row = (lambda i: (4, 0, 0)) if ctx else (lambda i: (i // tiles_per_seq, 0, 0))
    in_specs = [
        pl.BlockSpec((tm, D_MODEL), lambda i: (i, 0)),
        pl.BlockSpec((None, 1, 3 * D_MODEL), mod_row),
        _const_spec((1, D_MODEL)),
    ]
    args = [x, mods_l, norm_g.reshape(1, D_MODEL)]
    for w in weights:
        in_specs.append(_const_spec(w.shape))
        args.append(w)
    if rope:
        for t in tables:
            in_specs.append(pl.BlockSpec((tm, LANES), lambda i: (i % tiles_per_seq, 0)))
            args.append(t)
    aliases = {}
    if kind == 0:
        qw, kw, vw = N_HEADS * MLA_HEAD_PAD, N_HEADS * MLA_HEAD_PAD, N_HEADS * v_rows
        n_aliased = len(stacked) if stacked else 0
        for a in range(n_aliased):
            aliases[len(args)] = 4 + a
            in_specs.append(pl.BlockSpec(memory_space=pl.ANY))
            args.append(stacked[a])
        body = functools.partial(_proj_mla_kernel, rope=rope, ctx=ctx, n_aliased=n_aliased)
        cache_shapes = [(n_tok, MLA_KV_LORA), (n_tok, MLA_ROPE)]
    else:
        qw, kw, vw = N_HEADS * HEAD_DIM, KV_HEADS * HEAD_DIM, KV_HEADS * v_rows
        body = functools.partial(_proj_gqa_kernel, qk_norm=(kind == 1), rope=rope, ctx=ctx)
        cache_shapes = [(n_tok, kw), (n_tok, kw)]
    def feature_major_spec(w):
        if seqs_per_tile > 1:
            return pl.BlockSpec((seqs_per_tile, w, seq), lambda i: (i, 0, 0))
        return pl.BlockSpec((None, w, tm), lambda i: (i // tiles_per_seq, 0, i % tiles_per_seq))

    out_shape = [
        jax.ShapeDtypeStruct((batch, qw, seq) if ctx else (n_tok, qw), BF16),
        jax.ShapeDtypeStruct((n_tok, kw), BF16),
        jax.ShapeDtypeStruct((batch, vw, seq), BF16),
        jax.ShapeDtypeStruct((n_tok, D_MODEL), BF16),
    ]
    out_specs = [
        feature_major_spec(qw) if ctx else pl.BlockSpec((tm, qw), lambda i: (i, 0)),
        pl.BlockSpec((tm, kw), lambda i: (i, 0)),
        feature_major_spec(vw),
        pl.BlockSpec((tm, D_MODEL), lambda i: (i, 0)),
    ]
    if ctx and kind == 0:
        slot, n_slots = layer_slot
        for _, w in cache_shapes:
            out_shape.append(jax.ShapeDtypeStruct((batch, n_slots, seq, w), F32))
            out_specs.append(pl.BlockSpec((seqs_per_tile, None, seq, w), lambda i: (i, slot, 0, 0)))
    elif ctx:
        for s in cache_shapes:
            out_shape.append(jax.ShapeDtypeStruct(s, F32))
            out_specs.append(pl.BlockSpec((tm, s[1]), lambda i: (i, 0)))
    return pl.pallas_call(
        body,
        grid=(n_tok // tm,),
        in_specs=in_specs,
        out_specs=out_specs,
        out_shape=out_shape,
        input_output_aliases=aliases,
        compiler_params=pltpu.CompilerParams(
            dimension_semantics=("arbitrary",), vmem_limit_bytes=VMEM_LIMIT),
        name=f"proj_k{kind}_{'ctx' if ctx else 'lat'}",
    )(*args)


def _mla_cache_kernel(ckv_ref, kpe_ref, wk_ref, wvt_ref, k_out, vt_out):
    _mla_expand(ckv_ref[...].astype(BF16), kpe_ref[...], wk_ref, wvt_ref, k_out, vt_out)


def _mla_cache(ckv, kpe_pad, wk, wvt):
    b, n, _ = ckv.shape
    return pl.pallas_call(
        _mla_cache_kernel,
        grid=(b,),
        in_specs=[
            pl.BlockSpec((None, n, MLA_KV_LORA), lambda i: (i, 0, 0)),
            pl.BlockSpec((None, n, LANES), lambda i: (i, 0, 0)),
            _const_spec(wk.shape), _const_spec(wvt.shape),
        ],
        out_specs=[
            pl.BlockSpec((None, n, N_HEADS * MLA_HEAD_PAD), lambda i: (i, 0, 0)),
            pl.BlockSpec((None, N_HEADS * HEAD_DIM, n), lambda i: (i, 0, 0)),
        ],
        out_shape=[
            jax.ShapeDtypeStruct((b, n, N_HEADS * MLA_HEAD_PAD), BF16),
            jax.ShapeDtypeStruct((b, N_HEADS * HEAD_DIM, n), BF16),
        ],
        compiler_params=pltpu.CompilerParams(
            dimension_semantics=("arbitrary",), vmem_limit_bytes=VMEM_LIMIT),
        name="mla_cache",
    )(ckv, kpe_pad, wk, wvt)


def _gqa_cache_kernel(k_ref, v_ref, k_out, vt_out):
    k_out[...] = k_ref[...].astype(BF16)
    _store_values_t(vt_out, v_ref[...].T, KV_HEADS)


def _gqa_cache(k, v):
    b, n, w = k.shape
    return pl.pallas_call(
        _gqa_cache_kernel,
        grid=(b,),
        in_specs=[pl.BlockSpec((None, n, w), lambda i: (i, 0, 0))] * 2,
        out_specs=[
            pl.BlockSpec((None, n, w), lambda i: (i, 0, 0)),
            pl.BlockSpec((None, KV_HEADS * HEAD_DIM, n), lambda i: (i, 0, 0)),
        ],
        out_shape=[
            jax.ShapeDtypeStruct((b, n, w), BF16),
            jax.ShapeDtypeStruct((b, KV_HEADS * HEAD_DIM, n), BF16),
        ],
        compiler_params=pltpu.CompilerParams(dimension_semantics=("arbitrary",)),
        name="gqa_cache",
    )(k, v)


def _attn_kernel(*refs, head_w, group, has_cache, has_sink, band, final, lag_b, lag_c, interleave,
                 subs, own_keys, seq):
    refs = list(refs)
    q_ref = refs[0]
    n_own = subs + 2 if band else 1
    kn_refs = refs[1:1 + n_own]
    vtn_refs = refs[1 + n_own:1 + 2 * n_own]
    kn_ref, vtn_ref = kn_refs[0], vtn_refs[0]
    pos = 1 + 2 * n_own
    if has_cache:
        kc_ref, vtc_ref = refs[pos:pos + 2]
        pos += 2
    gate_ref, x_ref, mod_ref, wo_ref = refs[pos:pos + 4]
    pos += 4
    if has_sink:
        sink_ref = refs[pos]
        pos += 1
    if final:
        fg_ref = refs[pos]
        pos += 1
    o_ref, ot_scr, s_scr, p_scr, g_scr, y_scr = refs[pos:pos + 6]
    if band:
        bias_scr = refs[pos + 6]

    tq = TOKEN_TILE
    ch = KEY_CHUNK
    slots = s_scr.shape[0]
    segs = []
    if has_cache:
        segs.append((kc_ref.shape[0],
                     lambda e, r, n, cols: kc_ref[r:r + n, cols],
                     lambda e, rows: vtc_ref[rows, :]))
    band_of = {}
    if own_keys:
        segs.append((seq,
                     lambda e, r, n, cols: kn_ref[e * seq + r:e * seq + r + n, cols],
                     lambda e, rows: vtn_ref[e, rows, :]))
    elif band:
        for a in range(3):
            band_of[len(segs)] = a
            segs.append((tq,
                         lambda e, r, n, cols, a=a: kn_refs[e + a][r:r + n, cols],
                         lambda e, rows, a=a: vtn_refs[e + a][rows, :]))
    else:
        segs.append((seq,
                     lambda e, r, n, cols: kn_ref[r:r + n, cols],
                     lambda e, rows: vtn_ref[rows, :]))
    seg_rows = []
    chunks = []
    off = 0
    for si, (n_seg, _, _) in enumerate(segs):
        seg_rows.append((off, n_seg))
        chunks += [(si, r, off + r) for r in range(0, n_seg, ch)]
        off += n_seg
    if interleave:
        score_chunks = [c + (ch,) for c in chunks]
    else:
        score_chunks = [(si, 0, r0, n) for si, (r0, n) in enumerate(seg_rows)]
    if band:
        row = lax.broadcasted_iota(jnp.int32, (tq, tq), 0)
        col = lax.broadcasted_iota(jnp.int32, (tq, tq), 1)
        for e in range(subs):
            tile = pl.program_id(1) * subs + e
            for a in range(3):
                key_block = tile - 1 + a
                ok = ((jnp.abs((a - 1) * tq + row - col) <= WINDOW)
                      & (key_block >= 0) & (key_block < seq // tq))
                bias_scr[e, a] = jnp.where(ok, 0.0, NEG_INF)

    assert lag_b >= 1 and lag_c >= 1 and slots > max(lag_b, lag_c)
    row_max = {}
    den_part = {}
    v_rows = vtn_ref.shape[-2] * group // N_HEADS
    den_on_mxu = v_rows > HEAD_DIM

    def scores_and_max(i):
        e, hd = divmod(i, N_HEADS)
        kh = hd // group
        q_cols = slice(head_w * hd, head_w * (hd + 1))
        qh = q_ref[e, q_cols, :] if own_keys else q_ref[e * tq:(e + 1) * tq, q_cols]
        mx = None
        for si, r, g, n in score_chunks:
            kk = segs[si][1](e, r, n, slice(head_w * kh, head_w * (kh + 1)))
            if own_keys:
                s = jnp.dot(kk, qh, preferred_element_type=F32)
            else:
                s = lax.dot_general(kk, qh, NT_DIMS, preferred_element_type=F32)
            if si in band_of:
                s = s + bias_scr[e, band_of[si], r:r + n, :]
            s_scr[i % slots, g:g + n, :] = s
            part = jnp.max(s.reshape(n // 8, 8, tq), axis=0)
            mx = part if mx is None else jnp.maximum(mx, part)
            yield
        m = jnp.max(mx, axis=0, keepdims=True)
        if has_sink:
            m = jnp.maximum(m, sink_ref[hd] * LOG2E)
        row_max[i] = m

    def exponentials(i):
        hd = i % N_HEADS
        m = row_max.pop(i)
        mb = jnp.broadcast_to(m, (8, tq))
        tot = None
        for _, _, g in chunks:
            p = jnp.exp2(s_scr[i % slots, g:g + ch, :].reshape(ch // 8, 8, tq) - mb)
            if not den_on_mxu:
                part = jnp.sum(p, axis=0)
                tot = part if tot is None else tot + part
            p_scr[i % slots, g:g + ch, :] = p.reshape(ch, tq).astype(BF16)
            yield
        den = None if den_on_mxu else jnp.sum(tot, axis=0, keepdims=True)
        if has_sink:
            sink = jnp.exp2(sink_ref[hd] * LOG2E - m)
            den = sink if den is None else den + sink
        den_part[i] = den

    def weighted_values(i):
        e, hd = divmod(i, N_HEADS)
        kh = hd // group
        acc = None
        for si, (_, _, values_t) in enumerate(segs):
            r0, n = seg_rows[si]
            vt = values_t(e, slice(v_rows * kh, v_rows * (kh + 1)))
            part = jnp.dot(vt, p_scr[i % slots, r0:r0 + n, :], preferred_element_type=F32)
            acc = part if acc is None else acc + part
            yield
        den = den_part.pop(i)
        if den_on_mxu:
            mxu_sum = acc[HEAD_DIM:HEAD_DIM + 1, :]
            den = mxu_sum if den is None else den + mxu_sum
        ot_scr[e, HEAD_DIM * hd:HEAD_DIM * (hd + 1), :] = acc[:HEAD_DIM, :] / den

    def finish(e):
        rows = slice(e * tq, (e + 1) * tq)
        for c0 in range(0, D_MODEL, OUT_BLOCK):
            cols = slice(c0, c0 + OUT_BLOCK)
            o = ot_scr[e, cols, :].T
            g_scr[:, cols] = (o * gate_ref[rows, cols].astype(F32)).astype(BF16)
            yield
        for c0 in range(0, D_MODEL, OUT_BLOCK):
            cols = slice(c0, c0 + OUT_BLOCK)
            y = jnp.dot(g_scr[...], wo_ref[:, cols], preferred_element_type=F32)
            xn = x_ref[rows, cols] + mod_ref[:, 2 * D_MODEL + c0:2 * D_MODEL + c0 + OUT_BLOCK] * y
            if final:
                y_scr[:, cols] = xn
            else:
                o_ref[rows, cols] = xn
            yield
        if final:
            o_ref[rows, :] = _rms(y_scr[...], fg_ref[...])

    n_items = subs * N_HEADS
    for step in range(n_items + lag_b + lag_c + 1):
        live = []
        if step < n_items:
            live.append(scores_and_max(step))
        if 0 <= step - lag_b < n_items:
            live.append(exponentials(step - lag_b))
        done = step - lag_b - lag_c
        if 0 <= done < n_items:
            live.append(weighted_values(done))
        if done >= N_HEADS and done % N_HEADS == 0:
            live.append(finish(done // N_HEADS - 1))
        if not interleave:
            for g in live:
                for _ in g:
                    pass
            live = []
        while live:
            live = [g for g in live if next(g, True) is None]


def _attend(kind, q, k_new, vt_new, cache, gate, x, mods_l, w_out, sink, final_g, *, batch, seq, ctx,
            lags):
    lag_b, lag_c, interleave, subs = lags
    tq = TOKEN_TILE
    n_tok = batch * seq
    band = kind == 2 and not ctx
    head_w = MLA_HEAD_PAD if kind == 0 else HEAD_DIM
    kw = k_new.shape[1]
    vw = vt_new.shape[1]
    own_keys = cache is None
    assert seq == tq if own_keys else seq % (subs * tq) == 0
    grid = (batch // subs, 1) if own_keys else (batch, seq // (subs * tq))
    tok_spec = lambda w: pl.BlockSpec((subs * tq, w), lambda b, t: (b * grid[1] + t, 0))
    qw = q.shape[1]
    if own_keys:
        in_specs = [
            pl.BlockSpec((subs, qw, seq), lambda b, t: (b, 0, 0)),
            pl.BlockSpec((subs * seq, kw), lambda b, t: (b, 0)),
            pl.BlockSpec((subs, vw, seq), lambda b, t: (b, 0, 0)),
        ]
    elif band:
        n_blk = seq // tq
        block = lambda j: (lambda b, t: jnp.clip(subs * t - 1 + j, 0, n_blk - 1))
        in_specs = [tok_spec(qw)]
        in_specs += [pl.BlockSpec((tq, kw), lambda b, t, f=block(j): (b * n_blk + f(b, t), 0))
                     for j in range(subs + 2)]
        in_specs += [pl.BlockSpec((None, vw, tq), lambda b, t, f=block(j): (b, 0, f(b, t)))
                     for j in range(subs + 2)]
    else:
        in_specs = [
            tok_spec(qw),
            pl.BlockSpec((seq, kw), lambda b, t: (b, 0)),
            pl.BlockSpec((None, vw, seq), lambda b, t: (b, 0, 0)),
        ]
    n_own = subs + 2 if band else 1
    args = [q] + [k_new] * n_own + [vt_new] * n_own
    n_keys = 3 * tq if band else seq
    if cache is not None:
        kc, vtc = cache
        n_c = kc.shape[1]
        n_keys += n_c
        in_specs += [
            pl.BlockSpec((None, n_c, kw), lambda b, t: (b, 0, 0)),
            pl.BlockSpec((None, vw, n_c), lambda b, t: (b, 0, 0)),
        ]
        args += [kc, vtc]
    mod_row = (lambda b, t: (4, 0, 0)) if ctx else (lambda b, t: (b, 0, 0))
    in_specs += [
        tok_spec(D_MODEL),
        tok_spec(D_MODEL),
        pl.BlockSpec((None, 1, 3 * D_MODEL), mod_row),
        pl.BlockSpec((D_MODEL, D_MODEL), lambda b, t: (0, 0)),
    ]
    args += [gate, x, mods_l, w_out]
    if sink is not None:
        in_specs.append(pl.BlockSpec(memory_space=pltpu.SMEM))
        args.append(sink)
    if final_g is not None:
        in_specs.append(pl.BlockSpec((1, D_MODEL), lambda b, t: (0, 0)))
        args.append(final_g.reshape(1, D_MODEL))
    body = functools.partial(
        _attn_kernel,
        head_w=head_w,
        group=1 if kind == 0 else GROUPS,
        has_cache=cache is not None,
        has_sink=sink is not None,
        band=band,
        final=final_g is not None,
        lag_b=lag_b,
        lag_c=lag_c,
        interleave=interleave,
        subs=subs,
        own_keys=own_keys,
        seq=seq,
    )
    slots = max(lag_b, lag_c) + 1
    scratch = [pltpu.VMEM((subs, N_HEADS * HEAD_DIM, tq), F32),
               pltpu.VMEM((slots, n_keys, tq), F32),
               pltpu.VMEM((slots, n_keys, tq), BF16),
               pltpu.VMEM((tq, D_MODEL), BF16),
               pltpu.VMEM((tq, D_MODEL), F32)]
    if band:
        scratch.append(pltpu.VMEM((subs, 3, tq, tq), F32))
    return pl.pallas_call(
        body,
        grid=grid,
        in_specs=in_specs,
        out_specs=tok_spec(D_MODEL),
        out_shape=jax.ShapeDtypeStruct((n_tok, D_MODEL), F32),
        scratch_shapes=scratch,
        compiler_params=pltpu.CompilerParams(
            dimension_semantics=("arbitrary", "arbitrary"), vmem_limit_bytes=VMEM_LIMIT),
        name=f"attn_k{kind}_{'ctx' if ctx else 'lat'}",
    )(*args)


def _rope_lane_tables(seq, rot_dim):
    rows = seq // GRID_W
    row = jnp.repeat(jnp.arange(rows), GRID_W).astype(F32)
    col = jnp.tile(jnp.arange(GRID_W), rows).astype(F32)
    nf = rot_dim // 4
    freqs = ROPE_THETA ** (-jnp.arange(nf, dtype=F32) / nf)
    ang = jnp.concatenate([row[:, None] * freqs, col[:, None] * freqs], axis=-1)
    cos, sin = jnp.cos(ang), jnp.sin(ang)
    cos_l = jnp.repeat(cos, 2, axis=-1)
    sin_l = jnp.stack([-sin, sin], axis=-1).reshape(seq, rot_dim)
    return cos_l, sin_l


def _embed_lanes(table, start, fill):
    seq, w = table.shape
    return jnp.concatenate(
        [jnp.full((seq, start), fill, F32), table, jnp.full((seq, LANES - start - w), fill, F32)], axis=1)


def _mla_weights(w_in, q_norm, w_uq, kv_norm, w_ukv):
    c0 = MLA_Q_LORA + MLA_KV_LORA
    win = jnp.concatenate(
        [w_in[:, :c0], w_in[:, c0 + MLA_ROPE:], w_in[:, c0:c0 + MLA_ROPE],
         jnp.zeros((D_MODEL, MLA_IN_PAD - w_in.shape[1]), F32)], axis=1).astype(BF16)
    hq = MLA_NOPE + MLA_ROPE
    wuq = jnp.pad(w_uq.reshape(MLA_Q_LORA, N_HEADS, hq),
                  ((0, 0), (0, 0), (0, MLA_HEAD_PAD - hq))).reshape(MLA_Q_LORA, -1).astype(BF16)
    wkv = w_ukv.reshape(MLA_KV_LORA, N_HEADS, MLA_NOPE + MLA_V)
    wk = jnp.pad(wkv[:, :, :MLA_NOPE],
                 ((0, 0), (0, 0), (0, MLA_HEAD_PAD - MLA_NOPE))).reshape(MLA_KV_LORA, -1).astype(BF16)
    wvt = wkv[:, :, MLA_NOPE:].reshape(MLA_KV_LORA, -1).T.astype(BF16)
    return [win, q_norm.reshape(1, -1), wuq, kv_norm.reshape(1, -1), wk, wvt]


def kernel(x_prompt, x_sample, cache_mla_ckv, cache_mla_kpe, cache_gqa_k, cache_gqa_v,
           cache_swa_k, cache_swa_v, c, c_ctx, norm_g, w_ada, b_ada, w_out,
           mla_w_in, mla_q_norm, mla_w_uq, mla_kv_norm, mla_w_ukv,
           gqa_w_in, gqa_q_norm, gqa_k_norm, swa_w_in, swa_sink, final_norm_g):
    bc, sc, _ = x_prompt.shape
    bl, sl, _ = x_sample.shape
    n_past = cache_mla_ckv.shape[2]
    kvw = KV_HEADS * HEAD_DIM

    cc = jnp.concatenate([c, c_ctx[None, :], jnp.zeros((8 - bl - 1, D_MODEL), F32)], axis=0)
    mods = _ada_mods(cc, w_ada, b_ada).reshape(DEPTH, 8, 1, 3 * D_MODEL)

    cos_h, sin_h = _rope_lane_tables(sl, HEAD_DIM)
    gqa_tables = [jnp.tile(cos_h, (1, 2)), jnp.tile(sin_h, (1, 2))]
    cos_r, sin_r = _rope_lane_tables(sl, MLA_ROPE)
    mla_tables = [_embed_lanes(cos_r, MLA_NOPE, 1.0), _embed_lanes(sin_r, MLA_NOPE, 0.0),
                  _embed_lanes(cos_r, 0, 1.0), _embed_lanes(sin_r, 0, 0.0)]

    w_out_bf = w_out.astype(BF16)

    xc = x_prompt.reshape(bc * sc, D_MODEL)
    xl = x_sample.reshape(bl * sl, D_MODEL)
    new_caches = {1: ([], []), 2: ([], [])}
    n_mla = mla_w_in.shape[0]
    mla_stacked = None
    for i in range(DEPTH):
        kind, j = i % 3, i // 3
        mods_l = mods[i]
        wo = w_out_bf[i]
        final_g = final_norm_g if i == DEPTH - 1 else None
        sink = None
        if kind == 0:
            weights = _mla_weights(mla_w_in[j], mla_q_norm[j], mla_w_uq[j], mla_kv_norm[j], mla_w_ukv[j])
            tables = mla_tables
            kpe_pad = jnp.pad(cache_mla_kpe[:, j], ((0, 0), (0, 0), (0, LANES - MLA_ROPE)))
            cache = _mla_cache(cache_mla_ckv[:, j], kpe_pad, weights[4], weights[5])
        else:
            tables = gqa_tables
            if kind == 1:
                weights = [gqa_w_in[j].astype(BF16),
                           jnp.tile(gqa_q_norm[j], 2).reshape(1, LANES),
                           jnp.tile(gqa_k_norm[j], 2).reshape(1, LANES)]
                cache = _gqa_cache(cache_gqa_k[:, j].reshape(bl, n_past, kvw),
                                   cache_gqa_v[:, j].reshape(bl, n_past, kvw))
            else:
                weights = [swa_w_in[j].astype(BF16)]
                sink = swa_sink[j]
                cache = _gqa_cache(cache_swa_k[:, j].reshape(bl, n_past, kvw),
                                   cache_swa_v[:, j].reshape(bl, n_past, kvw))

        if kind == 0:
            q, k, vt, gate, c_a, c_b = _project(
                kind, xc, mods_l, norm_g[i], weights, tables, batch=bc, seq=sc, ctx=True,
                layer_slot=(j, n_mla), stacked=mla_stacked)
            mla_stacked = (c_a, c_b)
        else:
            q, k, vt, gate, c_a, c_b = _project(kind, xc, mods_l, norm_g[i], weights, tables,
                                                batch=bc, seq=sc, ctx=True)
            new_caches[kind][0].append(c_a)
            new_caches[kind][1].append(c_b)
        xc = _attend(kind, q, k, vt, None, gate, xc, mods_l, wo, sink, final_g,
                     batch=bc, seq=sc, ctx=True, lags=CTX_STAGE_LAGS)

        q, k, vt, gate = _project(kind, xl, mods_l, norm_g[i], weights, tables,
                                  batch=bl, seq=sl, ctx=False)
        xl = _attend(kind, q, k, vt, cache, gate, xl, mods_l, wo, sink, final_g,
                     batch=bl, seq=sl, ctx=False, lags=LAT_STAGE_LAGS)

    def stack(parts, tail):
        return jnp.stack([p.reshape((bc, sc) + tail) for p in parts], axis=1)

    return (xc.reshape(bc, sc, D_MODEL), xl.reshape(bl, sl, D_MODEL),
            mla_stacked[0], mla_stacked[1],
            stack(new_caches[1][0], (KV_HEADS, HEAD_DIM)), stack(new_caches[1][1], (KV_HEADS, HEAD_DIM)),
            stack(new_caches[2][0], (KV_HEADS, HEAD_DIM)), stack(new_caches[2][1], (KV_HEADS, HEAD_DIM)))
```

```python
import functools

import jax
import jax.numpy as jnp
from jax import lax
from jax.experimental import pallas as pl
from jax.experimental.pallas import tpu as pltpu

F32 = jnp.float32
BF16 = jnp.bfloat16

D_MODEL = 1024
DEPTH = 4
GRID_W = 64
N_HEADS = 16
HEAD_DIM = 64
KV_HEADS = 4
GROUPS = N_HEADS // KV_HEADS
MLA_Q_LORA = 384
MLA_KV_LORA = 256
MLA_NOPE = 64
MLA_ROPE = 32
MLA_V = 64
WINDOW = 128
ROPE_THETA = 10000.0
EPS = 1e-6
NEG_INF = -1e30

LOG2E = 1.4426950408889634
GQA_Q_SCALE = HEAD_DIM ** -0.5 * LOG2E
MLA_Q_SCALE = (MLA_NOPE + MLA_ROPE) ** -0.5 * LOG2E

LANES = 128
TOKEN_TILE = 256
PROJ_TILE = 512
KEY_CHUNK = 256
OUT_BLOCK = 256
CTX_STAGE_LAGS = (3, 3, True, 4)
LAT_STAGE_LAGS = (2, 2, False, 2)
MLA_HEAD_PAD = 128
V_ROWS = HEAD_DIM + 16
MLA_IN_PAD = 1792
VMEM_LIMIT = 56 * 1024 * 1024
NT_DIMS = (((1,), (1,)), ((), ()))


def _silu(v):
    return v * jax.nn.sigmoid(v)


def _rms(v, g):
    return v * lax.rsqrt(jnp.mean(v * v, axis=-1, keepdims=True) + EPS) * g


def _mod_norm(x, g, mod):
    shift = mod[:, :D_MODEL]
    scale = mod[:, D_MODEL:2 * D_MODEL]
    return _rms(x, g) * (1.0 + scale) + shift


def _rope_tile(v, cos, sin_signed):
    lane = lax.broadcasted_iota(jnp.int32, v.shape, 1)
    nxt = pltpu.roll(v, LANES - 1, axis=1)
    prv = pltpu.roll(v, 1, axis=1)
    swapped = jnp.where((lane & 1) == 0, nxt, prv)
    return v * cos + swapped * sin_signed


def _ada_kernel(c_ref, w_ref, b_ref, o_ref):
    s = _silu(c_ref[...]).astype(BF16)
    o_ref[...] = jnp.dot(s, w_ref[...].astype(BF16), preferred_element_type=F32) + b_ref[...]


def _ada_mods(cc, w_ada, b_ada):
    tn = D_MODEL
    return pl.pallas_call(
        _ada_kernel,
        grid=(DEPTH, 3 * D_MODEL // tn),
        in_specs=[
            pl.BlockSpec((8, D_MODEL), lambda i, n: (0, 0)),
            pl.BlockSpec((None, D_MODEL, tn), lambda i, n: (i, 0, n)),
            pl.BlockSpec((None, 1, tn), lambda i, n: (i, 0, n)),
        ],
        out_specs=pl.BlockSpec((None, 8, tn), lambda i, n: (i, 0, n)),
        out_shape=jax.ShapeDtypeStruct((DEPTH, 8, 3 * D_MODEL), F32),
        compiler_params=pltpu.CompilerParams(vmem_limit_bytes=VMEM_LIMIT),
        name="ada_mods",
    )(cc, w_ada, b_ada.reshape(DEPTH, 1, 3 * D_MODEL))


def _store_queries(q_out, col0, blk):
    w = blk.shape[1]
    if len(q_out.shape) == 2:
        q_out[:, col0:col0 + w] = blk.astype(BF16)
        return
    blk_t = blk.T.astype(BF16)
    seq = q_out.shape[2]
    for e in range(q_out.shape[0]):
        q_out[e, col0:col0 + w, :] = blk_t[:, e * seq:(e + 1) * seq]


def _store_values_t(vt_out, vt, n_heads):
    if len(vt_out.shape) == 3:
        seq = vt_out.shape[2]
        for e in range(vt_out.shape[0]):
            _store_values_t(vt_out.at[e], vt[:, e * seq:(e + 1) * seq], n_heads)
        return
    v_rows = vt_out.shape[0] // n_heads
    if v_rows == HEAD_DIM:
        vt_out[...] = vt.astype(BF16)
        return
    ones = jnp.ones((v_rows - HEAD_DIM, vt.shape[1]), BF16)
    for h in range(n_heads):
        vt_out[v_rows * h:v_rows * h + HEAD_DIM, :] = vt[HEAD_DIM * h:HEAD_DIM * (h + 1), :].astype(BF16)
        vt_out[v_rows * h + HEAD_DIM:v_rows * (h + 1), :] = ones


def _mla_expand(ckv_bf, kpe, wk_ref, wvt_ref, k_out, vt_out):
    k = jnp.dot(ckv_bf, wk_ref[...], preferred_element_type=F32)
    lane = lax.broadcasted_iota(jnp.int32, kpe.shape, 1)
    rope_lanes = (lane >= MLA_NOPE) & (lane < MLA_NOPE + MLA_ROPE)
    kpe_at_rope = pltpu.roll(kpe, MLA_NOPE, axis=1)
    for hd in range(N_HEADS):
        cols = slice(MLA_HEAD_PAD * hd, MLA_HEAD_PAD * (hd + 1))
        k_out[:, cols] = jnp.where(rope_lanes, kpe_at_rope, k[:, cols]).astype(BF16)
    vt = lax.dot_general(wvt_ref[...], ckv_bf, NT_DIMS, preferred_element_type=F32)
    _store_values_t(vt_out, vt, N_HEADS)


def _proj_mla_kernel(*refs, rope, ctx):
    refs = list(refs)
    x_ref, mod_ref, g_ref, win_ref, qn_ref, wuq_ref, kvn_ref, wk_ref, wvt_ref = refs[:9]
    pos = 9
    if rope:
        cq_ref, sq_ref, ck_ref, sk_ref = refs[pos:pos + 4]
        pos += 4
    q_out, k_out, vt_out, gate_out = refs[pos:pos + 4]
    pos += 4
    if ctx:
        ckv_out, kpe_out = refs[pos:pos + 2]

    h = _mod_norm(x_ref[...], g_ref[...], mod_ref[...]).astype(BF16)
    z = jnp.dot(h, win_ref[...], preferred_element_type=F32)
    cq = z[:, :MLA_Q_LORA]
    ckv = z[:, MLA_Q_LORA:MLA_Q_LORA + MLA_KV_LORA]
    gate = z[:, 640:640 + D_MODEL]
    kpe = z[:, 640 + D_MODEL:]
    gate_out[...] = _silu(gate).astype(BF16)

    qf = jnp.dot(_rms(cq, qn_ref[...]).astype(BF16), wuq_ref[...], preferred_element_type=F32)
    ckvn = _rms(ckv, kvn_ref[...])
    if ctx:
        ckv_out[...] = ckvn
        kpe_out[...] = kpe[:, :MLA_ROPE]
    if rope:
        kpe = _rope_tile(kpe, ck_ref[...], sk_ref[...])
    _mla_expand(ckvn.astype(BF16), kpe, wk_ref, wvt_ref, k_out, vt_out)
    for hd in range(N_HEADS):
        blk = qf[:, MLA_HEAD_PAD * hd:MLA_HEAD_PAD * (hd + 1)]
        if rope:
            blk = _rope_tile(blk, cq_ref[...], sq_ref[...])
        _store_queries(q_out, MLA_HEAD_PAD * hd, blk * MLA_Q_SCALE)


def _store_head_rows(out_ref, pair, blk):
    tokens = blk.shape[0]
    out_ref[pl.ds(2 * pair, tokens, stride=KV_HEADS), :] = blk
    out_ref[pl.ds(2 * pair + 1, tokens, stride=KV_HEADS), :] = pltpu.roll(blk, HEAD_DIM, axis=1)


def _proj_gqa_kernel(*refs, qk_norm, rope, ctx):
    refs = list(refs)
    x_ref, mod_ref, g_ref, win_ref = refs[:4]
    pos = 4
    if qk_norm:
        qn_ref, kn_ref = refs[pos:pos + 2]
        pos += 2
    if rope:
        cos_ref, sin_ref = refs[pos:pos + 2]
        pos += 2
    q_out, k_out, vt_out, gate_out = refs[pos:pos + 4]
    pos += 4
    if ctx:
        kc_out, vc_out = refs[pos:pos + 2]

    nq = N_HEADS * HEAD_DIM
    nk = KV_HEADS * HEAD_DIM
    h = _mod_norm(x_ref[...], g_ref[...], mod_ref[...]).astype(BF16)
    z = jnp.dot(h, win_ref[...], preferred_element_type=F32)
    tm = z.shape[0]
    lo = lax.broadcasted_iota(jnp.int32, (tm, LANES), 1) < HEAD_DIM
    n_q_tiles = nq // LANES
    for c in range((nq + nk) // LANES):
        blk = z[:, LANES * c:LANES * (c + 1)]
        is_q = c < n_q_tiles
        if qk_norm:
            sq = blk * blk
            s_lo = jnp.sum(jnp.where(lo, sq, 0.0), axis=-1, keepdims=True)
            s_hi = jnp.sum(jnp.where(lo, 0.0, sq), axis=-1, keepdims=True)
            inv = jnp.where(lo, lax.rsqrt(s_lo * (1.0 / HEAD_DIM) + EPS),
                            lax.rsqrt(s_hi * (1.0 / HEAD_DIM) + EPS))
            blk = blk * inv * (qn_ref[...] if is_q else kn_ref[...])
        if ctx and not is_q:
            _store_head_rows(kc_out, c - n_q_tiles, blk)
        if rope:
            blk = _rope_tile(blk, cos_ref[...], sin_ref[...])
        if is_q:
            _store_queries(q_out, LANES * c, blk * GQA_Q_SCALE)
        else:
            k_out[:, LANES * (c - n_q_tiles):LANES * (c - n_q_tiles + 1)] = blk.astype(BF16)
    v = z[:, nq + nk:nq + 2 * nk]
    if ctx:
        for c in range(nk // LANES):
            _store_head_rows(vc_out, c, v[:, LANES * c:LANES * (c + 1)])
    _store_values_t(vt_out, v.T, KV_HEADS)
    gate_out[...] = _silu(z[:, nq + 2 * nk:]).astype(BF16)


def _const_spec(shape):
    return pl.BlockSpec(shape, lambda i: (0,) * len(shape))


def _project(kind, x, mods_l, norm_g, weights, tables, *, batch, seq, ctx):
    n_tok = batch * seq
    tm = PROJ_TILE
    tiles_per_seq = max(seq // tm, 1)
    seqs_per_tile = max(tm // seq, 1)
    assert tm == seqs_per_tile * seq // tiles_per_seq and (ctx or seqs_per_tile == 1)
    v_rows = V_ROWS if ctx else HEAD_DIM
    rope = not ctx
    mod_row = (lambda i: (4, 0, 0)) if ctx else (lambda i: (i // tiles_per_seq, 0, 0))
    in_specs = [
        pl.BlockSpec((tm, D_MODEL), lambda i: (i, 0)),
        pl.BlockSpec((None, 1, 3 * D_MODEL), mod_row),
        _const_spec((1, D_MODEL)),
    ]
    args = [x, mods_l, norm_g.reshape(1, D_MODEL)]
    for w in weights:
        in_specs.append(_const_spec(w.shape))
        args.append(w)
    if rope:
        for t in tables:
            in_specs.append(pl.BlockSpec((tm, LANES), lambda i: (i % tiles_per_seq, 0)))
            args.append(t)
    if kind == 0:
        qw, kw, vw = N_HEADS * MLA_HEAD_PAD, N_HEADS * MLA_HEAD_PAD, N_HEADS * v_rows
        body = functools.partial(_proj_mla_kernel, rope=rope, ctx=ctx)
        cache_shapes = [(n_tok, MLA_KV_LORA), (n_tok, MLA_ROPE)]
    else:
        qw, kw, vw = N_HEADS * HEAD_DIM, KV_HEADS * HEAD_DIM, KV_HEADS * v_rows
        body = functools.partial(_proj_gqa_kernel, qk_norm=(kind == 1), rope=rope, ctx=ctx)
        cache_shapes = [(n_tok, kw), (n_tok, kw)]
    def feature_major_spec(w):
        if seqs_per_tile > 1:
            return pl.BlockSpec((seqs_per_tile, w, seq), lambda i: (i, 0, 0))
        return pl.BlockSpec((None, w, tm), lambda i: (i // tiles_per_seq, 0, i % tiles_per_seq))

    out_shape = [
        jax.ShapeDtypeStruct((batch, qw, seq) if ctx else (n_tok, qw), BF16),
        jax.ShapeDtypeStruct((n_tok, kw), BF16),
        jax.ShapeDtypeStruct((batch, vw, seq), BF16),
        jax.ShapeDtypeStruct((n_tok, D_MODEL), BF16),
    ]
    out_specs = [
        feature_major_spec(qw) if ctx else pl.BlockSpec((tm, qw), lambda i: (i, 0)),
        pl.BlockSpec((tm, kw), lambda i: (i, 0)),
        feature_major_spec(vw),
        pl.BlockSpec((tm, D_MODEL), lambda i: (i, 0)),
    ]
    if ctx and kind != 0:
        for _ in cache_shapes:
            out_shape.append(jax.ShapeDtypeStruct((n_tok * KV_HEADS, LANES), F32))
            out_specs.append(pl.BlockSpec((tm * KV_HEADS, LANES), lambda i: (i, 0)))
    elif ctx:
        for s in cache_shapes:
            out_shape.append(jax.ShapeDtypeStruct(s, F32))
            out_specs.append(pl.BlockSpec((tm, s[1]), lambda i: (i, 0)))
    return pl.pallas_call(
        body,
        grid=(n_tok // tm,),
        in_specs=in_specs,
        out_specs=out_specs,
        out_shape=out_shape,
        compiler_params=pltpu.CompilerParams(
            dimension_semantics=("arbitrary",), vmem_limit_bytes=VMEM_LIMIT),
        name=f"proj_k{kind}_{'ctx' if ctx else 'lat'}",
    )(*args)


def _mla_cache_kernel(ckv_ref, kpe_ref, wk_ref, wvt_ref, k_out, vt_out):
    _mla_expand(ckv_ref[...].astype(BF16), kpe_ref[...], wk_ref, wvt_ref, k_out, vt_out)


def _mla_cache(ckv, kpe_pad, wk, wvt):
    b, n, _ = ckv.shape
    return pl.pallas_call(
        _mla_cache_kernel,
        grid=(b,),
        in_specs=[
            pl.BlockSpec((None, n, MLA_KV_LORA), lambda i: (i, 0, 0)),
            pl.BlockSpec((None, n, LANES), lambda i: (i, 0, 0)),
            _const_spec(wk.shape), _const_spec(wvt.shape),
        ],
        out_specs=[
            pl.BlockSpec((None, n, N_HEADS * MLA_HEAD_PAD), lambda i: (i, 0, 0)),
            pl.BlockSpec((None, N_HEADS * HEAD_DIM, n), lambda i: (i, 0, 0)),
        ],
        out_shape=[
            jax.ShapeDtypeStruct((b, n, N_HEADS * MLA_HEAD_PAD), BF16),
            jax.ShapeDtypeStruct((b, N_HEADS * HEAD_DIM, n), BF16),
        ],
        compiler_params=pltpu.CompilerParams(
            dimension_semantics=("arbitrary",), vmem_limit_bytes=VMEM_LIMIT),
        name="mla_cache",
    )(ckv, kpe_pad, wk, wvt)


def _gqa_cache_kernel(k_ref, v_ref, k_out, vt_out):
    k_out[...] = k_ref[...].astype(BF16)
    _store_values_t(vt_out, v_ref[...].T, KV_HEADS)


def _gqa_cache(k, v):
    b, n, w = k.shape
    return pl.pallas_call(
        _gqa_cache_kernel,
        grid=(b,),
        in_specs=[pl.BlockSpec((None, n, w), lambda i: (i, 0, 0))] * 2,
        out_specs=[
            pl.BlockSpec((None, n, w), lambda i: (i, 0, 0)),
            pl.BlockSpec((None, KV_HEADS * HEAD_DIM, n), lambda i: (i, 0, 0)),
        ],
        out_shape=[
            jax.ShapeDtypeStruct((b, n, w), BF16),
            jax.ShapeDtypeStruct((b, KV_HEADS * HEAD_DIM, n), BF16),
        ],
        compiler_params=pltpu.CompilerParams(dimension_semantics=("arbitrary",)),
        name="gqa_cache",
    )(k, v)


def _attn_kernel(*refs, head_w, group, has_cache, has_sink, band, final, lag_b, lag_c, interleave,
                 subs, own_keys, seq):
    refs = list(refs)
    q_ref = refs[0]
    n_own = subs + 2 if band else 1
    kn_refs = refs[1:1 + n_own]
    vtn_refs = refs[1 + n_own:1 + 2 * n_own]
    kn_ref, vtn_ref = kn_refs[0], vtn_refs[0]
    pos = 1 + 2 * n_own
    if has_cache:
        kc_ref, vtc_ref = refs[pos:pos + 2]
        pos += 2
    gate_ref, x_ref, mod_ref, wo_ref = refs[pos:pos + 4]
    pos += 4
    if has_sink:
        sink_ref = refs[pos]
        pos += 1
    if final:
        fg_ref = refs[pos]
        pos += 1
    o_ref, ot_scr, s_scr, p_scr, g_scr, y_scr = refs[pos:pos + 6]
    if band:
        bias_scr = refs[pos + 6]

    tq = TOKEN_TILE
    ch = KEY_CHUNK
    slots = s_scr.shape[0]
    segs = []
    if has_cache:
        segs.append((kc_ref.shape[0],
                     lambda e, r, n, cols: kc_ref[r:r + n, cols],
                     lambda e, rows: vtc_ref[rows, :]))
    band_of = {}
    if own_keys:
        segs.append((seq,
                     lambda e, r, n, cols: kn_ref[e * seq + r:e * seq + r + n, cols],
                     lambda e, rows: vtn_ref[e, rows, :]))
    elif band:
        for a in range(3):
            band_of[len(segs)] = a
            segs.append((tq,
                         lambda e, r, n, cols, a=a: kn_refs[e + a][r:r + n, cols],
                         lambda e, rows, a=a: vtn_refs[e + a][rows, :]))
    else:
        segs.append((seq,
                     lambda e, r, n, cols: kn_ref[r:r + n, cols],
                     lambda e, rows: vtn_ref[rows, :]))
    seg_rows = []
    chunks = []
    off = 0
    for si, (n_seg, _, _) in enumerate(segs):
        seg_rows.append((off, n_seg))
        chunks += [(si, r, off + r) for r in range(0, n_seg, ch)]
        off += n_seg
    if interleave:
        score_chunks = [c + (ch,) for c in chunks]
    else:
        score_chunks = [(si, 0, r0, n) for si, (r0, n) in enumerate(seg_rows)]
    if band:
        row = lax.broadcasted_iota(jnp.int32, (tq, tq), 0)
        col = lax.broadcasted_iota(jnp.int32, (tq, tq), 1)
        for e in range(subs):
            tile = pl.program_id(1) * subs + e
            for a in range(3):
                key_block = tile - 1 + a
                ok = ((jnp.abs((a - 1) * tq + row - col) <= WINDOW)
                      & (key_block >= 0) & (key_block < seq // tq))
                bias_scr[e, a] = jnp.where(ok, 0.0, NEG_INF)

    assert lag_b >= 1 and lag_c >= 1 and slots > max(lag_b, lag_c)
    row_max = {}
    den_part = {}
    v_rows = vtn_ref.shape[-2] * group // N_HEADS
    den_on_mxu = v_rows > HEAD_DIM

    def scores_and_max(i):
        e, hd = divmod(i, N_HEADS)
        kh = hd // group
        q_cols = slice(head_w * hd, head_w * (hd + 1))
        qh = q_ref[e, q_cols, :] if own_keys else q_ref[e * tq:(e + 1) * tq, q_cols]
        mx = None
        for si, r, g, n in score_chunks:
            kk = segs[si][1](e, r, n, slice(head_w * kh, head_w * (kh + 1)))
            if own_keys:
                s = jnp.dot(kk, qh, preferred_element_type=F32)
            else:
                s = lax.dot_general(kk, qh, NT_DIMS, preferred_element_type=F32)
            if si in band_of:
                s = s + bias_scr[e, band_of[si], r:r + n, :]
            s_scr[i % slots, g:g + n, :] = s
            part = jnp.max(s.reshape(n // 8, 8, tq), axis=0)
            mx = part if mx is None else jnp.maximum(mx, part)
            yield
        m = jnp.max(mx, axis=0, keepdims=True)
        if has_sink:
            m = jnp.maximum(m, sink_ref[hd] * LOG2E)
        row_max[i] = m

    def exponentials(i):
        hd = i % N_HEADS
        m = row_max.pop(i)
        mb = jnp.broadcast_to(m, (8, tq))
        tot = None
        for _, _, g in chunks:
            p = jnp.exp2(s_scr[i % slots, g:g + ch, :].reshape(ch // 8, 8, tq) - mb)
            if not den_on_mxu:
                part = jnp.sum(p, axis=0)
                tot = part if tot is None else tot + part
            p_scr[i % slots, g:g + ch, :] = p.reshape(ch, tq).astype(BF16)
            yield
        den = None if den_on_mxu else jnp.sum(tot, axis=0, keepdims=True)
        if has_sink:
            sink = jnp.exp2(sink_ref[hd] * LOG2E - m)
            den = sink if den is None else den + sink
        den_part[i] = den

    def weighted_values(i):
        e, hd = divmod(i, N_HEADS)
        kh = hd // group
        acc = None
        for si, (_, _, values_t) in enumerate(segs):
            r0, n = seg_rows[si]
            vt = values_t(e, slice(v_rows * kh, v_rows * (kh + 1)))
            part = jnp.dot(vt, p_scr[i % slots, r0:r0 + n, :], preferred_element_type=F32)
            acc = part if acc is None else acc + part
            yield
        den = den_part.pop(i)
        if den_on_mxu:
            mxu_sum = acc[HEAD_DIM:HEAD_DIM + 1, :]
            den = mxu_sum if den is None else den + mxu_sum
        ot_scr[e, HEAD_DIM * hd:HEAD_DIM * (hd + 1), :] = acc[:HEAD_DIM, :] / den

    def finish(e):
        rows = slice(e * tq, (e + 1) * tq)
        for c0 in range(0, D_MODEL, OUT_BLOCK):
            cols = slice(c0, c0 + OUT_BLOCK)
            o = ot_scr[e, cols, :].T
            g_scr[:, cols] = (o * gate_ref[rows, cols].astype(F32)).astype(BF16)
            yield
        for c0 in range(0, D_MODEL, OUT_BLOCK):
            cols = slice(c0, c0 + OUT_BLOCK)
            y = jnp.dot(g_scr[...], wo_ref[:, cols], preferred_element_type=F32)
            xn = x_ref[rows, cols] + mod_ref[:, 2 * D_MODEL + c0:2 * D_MODEL + c0 + OUT_BLOCK] * y
            if final:
                y_scr[:, cols] = xn
            else:
                o_ref[rows, cols] = xn
            yield
        if final:
            o_ref[rows, :] = _rms(y_scr[...], fg_ref[...])

    n_items = subs * N_HEADS
    for step in range(n_items + lag_b + lag_c + 1):
        live = []
        if step < n_items:
            live.append(scores_and_max(step))
        if 0 <= step - lag_b < n_items:
            live.append(exponentials(step - lag_b))
        done = step - lag_b - lag_c
        if 0 <= done < n_items:
            live.append(weighted_values(done))
        if done >= N_HEADS and done % N_HEADS == 0:
            live.append(finish(done // N_HEADS - 1))
        if not interleave:
            for g in live:
                for _ in g:
                    pass
            live = []
        while live:
            live = [g for g in live if next(g, True) is None]


def _attend(kind, q, k_new, vt_new, cache, gate, x, mods_l, w_out, sink, final_g, *, batch, seq, ctx,
            lags):
    lag_b, lag_c, interleave, subs = lags
    tq = TOKEN_TILE
    n_tok = batch * seq
    band = kind == 2 and not ctx
    head_w = MLA_HEAD_PAD if kind == 0 else HEAD_DIM
    kw = k_new.shape[1]
    vw = vt_new.shape[1]
    own_keys = cache is None
    assert seq == tq if own_keys else seq % (subs * tq) == 0
    grid = (batch // subs, 1) if own_keys else (batch, seq // (subs * tq))
    tok_spec = lambda w: pl.BlockSpec((subs * tq, w), lambda b, t: (b * grid[1] + t, 0))
    qw = q.shape[1]
    if own_keys:
        in_specs = [
            pl.BlockSpec((subs, qw, seq), lambda b, t: (b, 0, 0)),
            pl.BlockSpec((subs * seq, kw), lambda b, t: (b, 0)),
            pl.BlockSpec((subs, vw, seq), lambda b, t: (b, 0, 0)),
        ]
    elif band:
        n_blk = seq // tq
        block = lambda j: (lambda b, t: jnp.clip(subs * t - 1 + j, 0, n_blk - 1))
        in_specs = [tok_spec(qw)]
        in_specs += [pl.BlockSpec((tq, kw), lambda b, t, f=block(j): (b * n_blk + f(b, t), 0))
                     for j in range(subs + 2)]
        in_specs += [pl.BlockSpec((None, vw, tq), lambda b, t, f=block(j): (b, 0, f(b, t)))
                     for j in range(subs + 2)]
    else:
        in_specs = [
            tok_spec(qw),
            pl.BlockSpec((seq, kw), lambda b, t: (b, 0)),
            pl.BlockSpec((None, vw, seq), lambda b, t: (b, 0, 0)),
        ]
    n_own = subs + 2 if band else 1
    args = [q] + [k_new] * n_own + [vt_new] * n_own
    n_keys = 3 * tq if band else seq
    if cache is not None:
        kc, vtc = cache
        n_c = kc.shape[1]
        n_keys += n_c
        in_specs += [
            pl.BlockSpec((None, n_c, kw), lambda b, t: (b, 0, 0)),
            pl.BlockSpec((None, vw, n_c), lambda b, t: (b, 0, 0)),
        ]
        args += [kc, vtc]
    mod_row = (lambda b, t: (4, 0, 0)) if ctx else (lambda b, t: (b, 0, 0))
    in_specs += [
        tok_spec(D_MODEL),
        tok_spec(D_MODEL),
        pl.BlockSpec((None, 1, 3 * D_MODEL), mod_row),
        pl.BlockSpec((D_MODEL, D_MODEL), lambda b, t: (0, 0)),
    ]
    args += [gate, x, mods_l, w_out]
    if sink is not None:
        in_specs.append(pl.BlockSpec(memory_space=pltpu.SMEM))
        args.append(sink)
    if final_g is not None:
        in_specs.append(pl.BlockSpec((1, D_MODEL), lambda b, t: (0, 0)))
        args.append(final_g.reshape(1, D_MODEL))
    body = functools.partial(
        _attn_kernel,
        head_w=head_w,
        group=1 if kind == 0 else GROUPS,
        has_cache=cache is not None,
        has_sink=sink is not None,
        band=band,
        final=final_g is not None,
        lag_b=lag_b,
        lag_c=lag_c,
        interleave=interleave,
        subs=subs,
        own_keys=own_keys,
        seq=seq,
    )
    slots = max(lag_b, lag_c) + 1
    scratch = [pltpu.VMEM((subs, N_HEADS * HEAD_DIM, tq), F32),
               pltpu.VMEM((slots, n_keys, tq), F32),
               pltpu.VMEM((slots, n_keys, tq), BF16),
               pltpu.VMEM((tq, D_MODEL), BF16),
               pltpu.VMEM((tq, D_MODEL), F32)]
    if band:
        scratch.append(pltpu.VMEM((subs, 3, tq, tq), F32))
    return pl.pallas_call(
        body,
        grid=grid,
        in_specs=in_specs,
        out_specs=tok_spec(D_MODEL),
        out_shape=jax.ShapeDtypeStruct((n_tok, D_MODEL), F32),
        scratch_shapes=scratch,
        compiler_params=pltpu.CompilerParams(
            dimension_semantics=("arbitrary", "arbitrary"), vmem_limit_bytes=VMEM_LIMIT),
        name=f"attn_k{kind}_{'ctx' if ctx else 'lat'}",
    )(*args)


def _rope_lane_tables(seq, rot_dim):
    rows = seq // GRID_W
    row = jnp.repeat(jnp.arange(rows), GRID_W).astype(F32)
    col = jnp.tile(jnp.arange(GRID_W), rows).astype(F32)
    nf = rot_dim // 4
    freqs = ROPE_THETA ** (-jnp.arange(nf, dtype=F32) / nf)
    ang = jnp.concatenate([row[:, None] * freqs, col[:, None] * freqs], axis=-1)
    cos, sin = jnp.cos(ang), jnp.sin(ang)
    cos_l = jnp.repeat(cos, 2, axis=-1)
    sin_l = jnp.stack([-sin, sin], axis=-1).reshape(seq, rot_dim)
    return cos_l, sin_l


def _embed_lanes(table, start, fill):
    seq, w = table.shape
    return jnp.concatenate(
        [jnp.full((seq, start), fill, F32), table, jnp.full((seq, LANES - start - w), fill, F32)], axis=1)


def _mla_weights(w_in, q_norm, w_uq, kv_norm, w_ukv):
    c0 = MLA_Q_LORA + MLA_KV_LORA
    win = jnp.concatenate(
        [w_in[:, :c0], w_in[:, c0 + MLA_ROPE:], w_in[:, c0:c0 + MLA_ROPE],
         jnp.zeros((D_MODEL, MLA_IN_PAD - w_in.shape[1]), F32)], axis=1).astype(BF16)
    hq = MLA_NOPE + MLA_ROPE
    wuq = jnp.pad(w_uq.reshape(MLA_Q_LORA, N_HEADS, hq),
                  ((0, 0), (0, 0), (0, MLA_HEAD_PAD - hq))).reshape(MLA_Q_LORA, -1).astype(BF16)
    wkv = w_ukv.reshape(MLA_KV_LORA, N_HEADS, MLA_NOPE + MLA_V)
    wk = jnp.pad(wkv[:, :, :MLA_NOPE],
                 ((0, 0), (0, 0), (0, MLA_HEAD_PAD - MLA_NOPE))).reshape(MLA_KV_LORA, -1).astype(BF16)
    wvt = wkv[:, :, MLA_NOPE:].reshape(MLA_KV_LORA, -1).T.astype(BF16)
    return [win, q_norm.reshape(1, -1), wuq, kv_norm.reshape(1, -1), wk, wvt]


def kernel(x_prompt, x_sample, cache_mla_ckv, cache_mla_kpe, cache_gqa_k, cache_gqa_v,
           cache_swa_k, cache_swa_v, c, c_ctx, norm_g, w_ada, b_ada, w_out,
           mla_w_in, mla_q_norm, mla_w_uq, mla_kv_norm, mla_w_ukv,
           gqa_w_in, gqa_q_norm, gqa_k_norm, swa_w_in, swa_sink, final_norm_g):
    bc, sc, _ = x_prompt.shape
    bl, sl, _ = x_sample.shape
    n_past = cache_mla_ckv.shape[2]
    kvw = KV_HEADS * HEAD_DIM

    cc = jnp.concatenate([c, c_ctx[None, :], jnp.zeros((8 - bl - 1, D_MODEL), F32)], axis=0)
    mods = _ada_mods(cc, w_ada, b_ada).reshape(DEPTH, 8, 1, 3 * D_MODEL)

    cos_h, sin_h = _rope_lane_tables(sl, HEAD_DIM)
    gqa_tables = [jnp.tile(cos_h, (1, 2)), jnp.tile(sin_h, (1, 2))]
    cos_r, sin_r = _rope_lane_tables(sl, MLA_ROPE)
    mla_tables = [_embed_lanes(cos_r, MLA_NOPE, 1.0), _embed_lanes(sin_r, MLA_NOPE, 0.0),
                  _embed_lanes(cos_r, 0, 1.0), _embed_lanes(sin_r, 0, 0.0)]

    w_out_bf = w_out.astype(BF16)

    xc = x_prompt.reshape(bc * sc, D_MODEL)
    xl = x_sample.reshape(bl * sl, D_MODEL)
    new_caches = {0: ([], []), 1: ([], []), 2: ([], [])}
    for i in range(DEPTH):
        kind, j = i % 3, i // 3
        mods_l = mods[i]
        wo = w_out_bf[i]
        final_g = final_norm_g if i == DEPTH - 1 else None
        sink = None
        if kind == 0:
            weights = _mla_weights(mla_w_in[j], mla_q_norm[j], mla_w_uq[j], mla_kv_norm[j], mla_w_ukv[j])
            tables = mla_tables
            kpe_pad = jnp.pad(cache_mla_kpe[:, j], ((0, 0), (0, 0), (0, LANES - MLA_ROPE)))
            cache = _mla_cache(cache_mla_ckv[:, j], kpe_pad, weights[4], weights[5])
        else:
            tables = gqa_tables
            if kind == 1:
                weights = [gqa_w_in[j].astype(BF16),
                           jnp.tile(gqa_q_norm[j], 2).reshape(1, LANES),
                           jnp.tile(gqa_k_norm[j], 2).reshape(1, LANES)]
                cache = _gqa_cache(cache_gqa_k[:, j].reshape(bl, n_past, kvw),
                                   cache_gqa_v[:, j].reshape(bl, n_past, kvw))
            else:
                weights = [swa_w_in[j].astype(BF16)]
                sink = swa_sink[j]
                cache = _gqa_cache(cache_swa_k[:, j].reshape(bl, n_past, kvw),
                                   cache_swa_v[:, j].reshape(bl, n_past, kvw))

        q, k, vt, gate, c_a, c_b = _project(kind, xc, mods_l, norm_g[i], weights, tables,
                                            batch=bc, seq=sc, ctx=True)
        new_caches[kind][0].append(c_a)
        new_caches[kind][1].append(c_b)
        xc = _attend(kind, q, k, vt, None, gate, xc, mods_l, wo, sink, final_g,
                     batch=bc, seq=sc, ctx=True, lags=CTX_STAGE_LAGS)

        q, k, vt, gate = _project(kind, xl, mods_l, norm_g[i], weights, tables,
                                  batch=bl, seq=sl, ctx=False)
        xl = _attend(kind, q, k, vt, cache, gate, xl, mods_l, wo, sink, final_g,
                     batch=bl, seq=sl, ctx=False, lags=LAT_STAGE_LAGS)

    def stack(parts, tail):
        if len(tail) == 2:
            parts = [p[:, :HEAD_DIM] for p in parts]
        return jnp.stack([p.reshape((bc, sc) + tail) for p in parts], axis=1)

    return (xc.reshape(bc, sc, D_MODEL), xl.reshape(bl, sl, D_MODEL),
            stack(new_caches[0][0], (MLA_KV_LORA,)), stack(new_caches[0][1], (MLA_ROPE,)),
            stack(new_caches[1][0], (KV_HEADS, HEAD_DIM)), stack(new_caches[1][1], (KV_HEADS, HEAD_DIM)),
            stack(new_caches[2][0], (KV_HEADS, HEAD_DIM)), stack(new_caches[2][1], (KV_HEADS, HEAD_DIM)))
```

```python
import functools

import jax
import jax.numpy as jnp
from jax import lax
from jax.experimental import pallas as pl
from jax.experimental.pallas import tpu as pltpu

F32 = jnp.float32
BF16 = jnp.bfloat16

D_MODEL = 1024
DEPTH = 4
GRID_W = 64
N_HEADS = 16
HEAD_DIM = 64
KV_HEADS = 4
GROUPS = N_HEADS // KV_HEADS
MLA_Q_LORA = 384
MLA_KV_LORA = 256
MLA_NOPE = 64
MLA_ROPE = 32
MLA_V = 64
WINDOW = 128
ROPE_THETA = 10000.0
EPS = 1e-6
NEG_INF = -1e30

LOG2E = 1.4426950408889634
GQA_Q_SCALE = HEAD_DIM ** -0.5 * LOG2E
MLA_Q_SCALE = (MLA_NOPE + MLA_ROPE) ** -0.5 * LOG2E

LANES = 128
TOKEN_TILE = 256
PROJ_TILE = 512
KEY_CHUNK = 256
OUT_BLOCK = 256
CTX_STAGE_LAGS = (3, 3, True, 4)
LAT_STAGE_LAGS = (1, 1, False, 2)
MLA_HEAD_PAD = 128
V_ROWS = HEAD_DIM + 16
MLA_IN_PAD = 1792
VMEM_LIMIT = 56 * 1024 * 1024
NT_DIMS = (((1,), (1,)), ((), ()))


def _silu(v):
    return v * jax.nn.sigmoid(v)


def _rms(v, g):
    return v * lax.rsqrt(jnp.mean(v * v, axis=-1, keepdims=True) + EPS) * g


def _mod_norm(x, g, mod):
    shift = mod[:, :D_MODEL]
    scale = mod[:, D_MODEL:2 * D_MODEL]
    return _rms(x, g) * (1.0 + scale) + shift


def _rope_tile(v, cos, sin_signed):
    lane = lax.broadcasted_iota(jnp.int32, v.shape, 1)
    nxt = pltpu.roll(v, LANES - 1, axis=1)
    prv = pltpu.roll(v, 1, axis=1)
    swapped = jnp.where((lane & 1) == 0, nxt, prv)
    return v * cos + swapped * sin_signed


def _ada_kernel(c_ref, w_ref, b_ref, o_ref):
    s = _silu(c_ref[...]).astype(BF16)
    o_ref[...] = jnp.dot(s, w_ref[...].astype(BF16), preferred_element_type=F32) + b_ref[...]


def _ada_mods(cc, w_ada, b_ada):
    tn = D_MODEL
    return pl.pallas_call(
        _ada_kernel,
        grid=(DEPTH, 3 * D_MODEL // tn),
        in_specs=[
            pl.BlockSpec((8, D_MODEL), lambda i, n: (0, 0)),
            pl.BlockSpec((None, D_MODEL, tn), lambda i, n: (i, 0, n)),
            pl.BlockSpec((None, 1, tn), lambda i, n: (i, 0, n)),
        ],
        out_specs=pl.BlockSpec((None, 8, tn), lambda i, n: (i, 0, n)),
        out_shape=jax.ShapeDtypeStruct((DEPTH, 8, 3 * D_MODEL), F32),
        compiler_params=pltpu.CompilerParams(vmem_limit_bytes=VMEM_LIMIT),
        name="ada_mods",
    )(cc, w_ada, b_ada.reshape(DEPTH, 1, 3 * D_MODEL))


def _store_queries(q_out, col0, blk):
    w = blk.shape[1]
    if len(q_out.shape) == 2:
        q_out[:, col0:col0 + w] = blk.astype(BF16)
        return
    blk_t = blk.T.astype(BF16)
    seq = q_out.shape[2]
    for e in range(q_out.shape[0]):
        q_out[e, col0:col0 + w, :] = blk_t[:, e * seq:(e + 1) * seq]


def _store_values_t(vt_out, vt, n_heads):
    if len(vt_out.shape) == 3:
        seq = vt_out.shape[2]
        for e in range(vt_out.shape[0]):
            _store_values_t(vt_out.at[e], vt[:, e * seq:(e + 1) * seq], n_heads)
        return
    v_rows = vt_out.shape[0] // n_heads
    if v_rows == HEAD_DIM:
        vt_out[...] = vt.astype(BF16)
        return
    ones = jnp.ones((v_rows - HEAD_DIM, vt.shape[1]), BF16)
    for h in range(n_heads):
        vt_out[v_rows * h:v_rows * h + HEAD_DIM, :] = vt[HEAD_DIM * h:HEAD_DIM * (h + 1), :].astype(BF16)
        vt_out[v_rows * h + HEAD_DIM:v_rows * (h + 1), :] = ones


def _mla_expand(ckv_bf, kpe, wk_ref, wvt_ref, k_out, vt_out):
    k = jnp.dot(ckv_bf, wk_ref[...], preferred_element_type=F32)
    lane = lax.broadcasted_iota(jnp.int32, kpe.shape, 1)
    rope_lanes = (lane >= MLA_NOPE) & (lane < MLA_NOPE + MLA_ROPE)
    kpe_at_rope = pltpu.roll(kpe, MLA_NOPE, axis=1)
    for hd in range(N_HEADS):
        cols = slice(MLA_HEAD_PAD * hd, MLA_HEAD_PAD * (hd + 1))
        k_out[:, cols] = jnp.where(rope_lanes, kpe_at_rope, k[:, cols]).astype(BF16)
    vt = lax.dot_general(wvt_ref[...], ckv_bf, NT_DIMS, preferred_element_type=F32)
    _store_values_t(vt_out, vt, N_HEADS)


def _proj_mla_kernel(*refs, rope, ctx):
    refs = list(refs)
    x_ref, mod_ref, g_ref, win_ref, qn_ref, wuq_ref, kvn_ref, wk_ref, wvt_ref = refs[:9]
    pos = 9
    if rope:
        cq_ref, sq_ref, ck_ref, sk_ref = refs[pos:pos + 4]
        pos += 4
    q_out, k_out, vt_out, gate_out = refs[pos:pos + 4]
    pos += 4
    if ctx:
        ckv_out, kpe_out = refs[pos:pos + 2]

    h = _mod_norm(x_ref[...], g_ref[...], mod_ref[...]).astype(BF16)
    z = jnp.dot(h, win_ref[...], preferred_element_type=F32)
    cq = z[:, :MLA_Q_LORA]
    ckv = z[:, MLA_Q_LORA:MLA_Q_LORA + MLA_KV_LORA]
    gate = z[:, 640:640 + D_MODEL]
    kpe = z[:, 640 + D_MODEL:]
    gate_out[...] = _silu(gate).astype(BF16)

    qf = jnp.dot(_rms(cq, qn_ref[...]).astype(BF16), wuq_ref[...], preferred_element_type=F32)
    ckvn = _rms(ckv, kvn_ref[...])
    if ctx:
        ckv_out[...] = ckvn
        kpe_out[...] = kpe[:, :MLA_ROPE]
    if rope:
        kpe = _rope_tile(kpe, ck_ref[...], sk_ref[...])
    _mla_expand(ckvn.astype(BF16), kpe, wk_ref, wvt_ref, k_out, vt_out)
    for hd in range(N_HEADS):
        blk = qf[:, MLA_HEAD_PAD * hd:MLA_HEAD_PAD * (hd + 1)]
        if rope:
            blk = _rope_tile(blk, cq_ref[...], sq_ref[...])
        _store_queries(q_out, MLA_HEAD_PAD * hd, blk * MLA_Q_SCALE)


def _store_head_rows(out_ref, pair, blk):
    tokens = blk.shape[0]
    out_ref[pl.ds(2 * pair, tokens, stride=KV_HEADS), :] = blk
    out_ref[pl.ds(2 * pair + 1, tokens, stride=KV_HEADS), :] = pltpu.roll(blk, HEAD_DIM, axis=1)


def _proj_gqa_kernel(*refs, qk_norm, rope, ctx):
    refs = list(refs)
    x_ref, mod_ref, g_ref, win_ref = refs[:4]
    pos = 4
    if qk_norm:
        qn_ref, kn_ref = refs[pos:pos + 2]
        pos += 2
    if rope:
        cos_ref, sin_ref = refs[pos:pos + 2]
        pos += 2
    q_out, k_out, vt_out, gate_out = refs[pos:pos + 4]
    pos += 4
    if ctx:
        kc_out, vc_out = refs[pos:pos + 2]

    nq = N_HEADS * HEAD_DIM
    nk = KV_HEADS * HEAD_DIM
    h = _mod_norm(x_ref[...], g_ref[...], mod_ref[...]).astype(BF16)
    z = jnp.dot(h, win_ref[...], preferred_element_type=F32)
    tm = z.shape[0]
    lo = lax.broadcasted_iota(jnp.int32, (tm, LANES), 1) < HEAD_DIM
    n_q_tiles = nq // LANES
    for c in range((nq + nk) // LANES):
        blk = z[:, LANES * c:LANES * (c + 1)]
        is_q = c < n_q_tiles
        if qk_norm:
            sq = blk * blk
            s_lo = jnp.sum(jnp.where(lo, sq, 0.0), axis=-1, keepdims=True)
            s_hi = jnp.sum(jnp.where(lo, 0.0, sq), axis=-1, keepdims=True)
            inv = jnp.where(lo, lax.rsqrt(s_lo * (1.0 / HEAD_DIM) + EPS),
                            lax.rsqrt(s_hi * (1.0 / HEAD_DIM) + EPS))
            blk = blk * inv * (qn_ref[...] if is_q else kn_ref[...])
        if ctx and not is_q:
            _store_head_rows(kc_out, c - n_q_tiles, blk)
        if rope:
            blk = _rope_tile(blk, cos_ref[...], sin_ref[...])
        if is_q:
            _store_queries(q_out, LANES * c, blk * GQA_Q_SCALE)
        else:
            k_out[:, LANES * (c - n_q_tiles):LANES * (c - n_q_tiles + 1)] = blk.astype(BF16)
    v = z[:, nq + nk:nq + 2 * nk]
    if ctx:
        for c in range(nk // LANES):
            _store_head_rows(vc_out, c, v[:, LANES * c:LANES * (c + 1)])
    _store_values_t(vt_out, v.T, KV_HEADS)
    gate_out[...] = _silu(z[:, nq + 2 * nk:]).astype(BF16)


def _const_spec(shape):
    return pl.BlockSpec(shape, lambda i: (0,) * len(shape))


def _project(kind, x, mods_l, norm_g, weights, tables, *, batch, seq, ctx):
    n_tok = batch * seq
    tm = PROJ_TILE
    tiles_per_seq = max(seq // tm, 1)
    seqs_per_tile = max(tm // seq, 1)
    assert tm == seqs_per_tile * seq // tiles_per_seq and (ctx or seqs_per_tile == 1)
    v_rows = V_ROWS if ctx else HEAD_DIM
    rope = not ctx
    mod_row = (lambda i: (4, 0, 0)) if ctx else (lambda i: (i // tiles_per_seq, 0, 0))
    in_specs = [
        pl.BlockSpec((tm, D_MODEL), lambda i: (i, 0)),
        pl.BlockSpec((None, 1, 3 * D_MODEL), mod_row),
        _const_spec((1, D_MODEL)),
    ]
    args = [x, mods_l, norm_g.reshape(1, D_MODEL)]
    for w in weights:
        in_specs.append(_const_spec(w.shape))
        args.append(w)
    if rope:
        for t in tables:
            in_specs.append(pl.BlockSpec((tm, LANES), lambda i: (i % tiles_per_seq, 0)))
            args.append(t)
    if kind == 0:
        qw, kw, vw = N_HEADS * MLA_HEAD_PAD, N_HEADS * MLA_HEAD_PAD, N_HEADS * v_rows
        body = functools.partial(_proj_mla_kernel, rope=rope, ctx=ctx)
        cache_shapes = [(n_tok, MLA_KV_LORA), (n_tok, MLA_ROPE)]
    else:
        qw, kw, vw = N_HEADS * HEAD_DIM, KV_HEADS * HEAD_DIM, KV_HEADS * v_rows
        body = functools.partial(_proj_gqa_kernel, qk_norm=(kind == 1), rope=rope, ctx=ctx)
        cache_shapes = [(n_tok, kw), (n_tok, kw)]
    def feature_major_spec(w):
        if seqs_per_tile > 1:
            return pl.BlockSpec((seqs_per_tile, w, seq), lambda i: (i, 0, 0))
        return pl.BlockSpec((None, w, tm), lambda i: (i // tiles_per_seq, 0, i % tiles_per_seq))

    out_shape = [
        jax.ShapeDtypeStruct((batch, qw, seq) if ctx else (n_tok, qw), BF16),
        jax.ShapeDtypeStruct((n_tok, kw), BF16),
        jax.ShapeDtypeStruct((batch, vw, seq), BF16),
        jax.ShapeDtypeStruct((n_tok, D_MODEL), BF16),
    ]
    out_specs = [
        feature_major_spec(qw) if ctx else pl.BlockSpec((tm, qw), lambda i: (i, 0)),
        pl.BlockSpec((tm, kw), lambda i: (i, 0)),
        feature_major_spec(vw),
        pl.BlockSpec((tm, D_MODEL), lambda i: (i, 0)),
    ]
    if ctx and kind != 0:
        for _ in cache_shapes:
            out_shape.append(jax.ShapeDtypeStruct((n_tok * KV_HEADS, LANES), F32))
            out_specs.append(pl.BlockSpec((tm * KV_HEADS, LANES), lambda i: (i, 0)))
    elif ctx:
        for s in cache_shapes:
            out_shape.append(jax.ShapeDtypeStruct(s, F32))
            out_specs.append(pl.BlockSpec((tm, s[1]), lambda i: (i, 0)))
    return pl.pallas_call(
        body,
        grid=(n_tok // tm,),
        in_specs=in_specs,
        out_specs=out_specs,
        out_shape=out_shape,
        compiler_params=pltpu.CompilerParams(
            dimension_semantics=("arbitrary",), vmem_limit_bytes=VMEM_LIMIT),
        name=f"proj_k{kind}_{'ctx' if ctx else 'lat'}",
    )(*args)


def _mla_cache_kernel(ckv_ref, kpe_ref, wk_ref, wvt_ref, k_out, vt_out):
    _mla_expand(ckv_ref[...].astype(BF16), kpe_ref[...], wk_ref, wvt_ref, k_out, vt_out)


def _mla_cache(ckv, kpe_pad, wk, wvt):
    b, n, _ = ckv.shape
    return pl.pallas_call(
        _mla_cache_kernel,
        grid=(b,),
        in_specs=[
            pl.BlockSpec((None, n, MLA_KV_LORA), lambda i: (i, 0, 0)),
            pl.BlockSpec((None, n, LANES), lambda i: (i, 0, 0)),
            _const_spec(wk.shape), _const_spec(wvt.shape),
        ],
        out_specs=[
            pl.BlockSpec((None, n, N_HEADS * MLA_HEAD_PAD), lambda i: (i, 0, 0)),
            pl.BlockSpec((None, N_HEADS * HEAD_DIM, n), lambda i: (i, 0, 0)),
        ],
        out_shape=[
            jax.ShapeDtypeStruct((b, n, N_HEADS * MLA_HEAD_PAD), BF16),
            jax.ShapeDtypeStruct((b, N_HEADS * HEAD_DIM, n), BF16),
        ],
        compiler_params=pltpu.CompilerParams(
            dimension_semantics=("arbitrary",), vmem_limit_bytes=VMEM_LIMIT),
        name="mla_cache",
    )(ckv, kpe_pad, wk, wvt)


def _gqa_cache_kernel(k_ref, v_ref, k_out, vt_out):
    k_out[...] = k_ref[...].astype(BF16)
    _store_values_t(vt_out, v_ref[...].T, KV_HEADS)


def _gqa_cache(k, v):
    b, n, w = k.shape
    return pl.pallas_call(
        _gqa_cache_kernel,
        grid=(b,),
        in_specs=[pl.BlockSpec((None, n, w), lambda i: (i, 0, 0))] * 2,
        out_specs=[
            pl.BlockSpec((None, n, w), lambda i: (i, 0, 0)),
            pl.BlockSpec((None, KV_HEADS * HEAD_DIM, n), lambda i: (i, 0, 0)),
        ],
        out_shape=[
            jax.ShapeDtypeStruct((b, n, w), BF16),
            jax.ShapeDtypeStruct((b, KV_HEADS * HEAD_DIM, n), BF16),
        ],
        compiler_params=pltpu.CompilerParams(dimension_semantics=("arbitrary",)),
        name="gqa_cache",
    )(k, v)


def _attn_kernel(*refs, head_w, group, has_cache, has_sink, band, final, lag_b, lag_c, interleave,
                 subs, own_keys, seq):
    refs = list(refs)
    q_ref = refs[0]
    n_own = subs + 2 if band else 1
    kn_refs = refs[1:1 + n_own]
    vtn_refs = refs[1 + n_own:1 + 2 * n_own]
    kn_ref, vtn_ref = kn_refs[0], vtn_refs[0]
    pos = 1 + 2 * n_own
    if has_cache:
        kc_ref, vtc_ref = refs[pos:pos + 2]
        pos += 2
    gate_ref, x_ref, mod_ref, wo_ref = refs[pos:pos + 4]
    pos += 4
    if has_sink:
        sink_ref = refs[pos]
        pos += 1
    if final:
        fg_ref = refs[pos]
        pos += 1
    o_ref, ot_scr, s_scr, p_scr, g_scr, y_scr = refs[pos:pos + 6]
    if band:
        bias_scr = refs[pos + 6]

    tq = TOKEN_TILE
    ch = KEY_CHUNK
    slots = s_scr.shape[0]
    segs = []
    if has_cache:
        segs.append((kc_ref.shape[0],
                     lambda e, r, n, cols: kc_ref[r:r + n, cols],
                     lambda e, rows: vtc_ref[rows, :]))
    band_of = {}
    if own_keys:
        segs.append((seq,
                     lambda e, r, n, cols: kn_ref[e * seq + r:e * seq + r + n, cols],
                     lambda e, rows: vtn_ref[e, rows, :]))
    elif band:
        for a in range(3):
            band_of[len(segs)] = a
            segs.append((tq,
                         lambda e, r, n, cols, a=a: kn_refs[e + a][r:r + n, cols],
                         lambda e, rows, a=a: vtn_refs[e + a][rows, :]))
    else:
        segs.append((seq,
                     lambda e, r, n, cols: kn_ref[r:r + n, cols],
                     lambda e, rows: vtn_ref[rows, :]))
    seg_rows = []
    chunks = []
    off = 0
    for si, (n_seg, _, _) in enumerate(segs):
        seg_rows.append((off, n_seg))
        chunks += [(si, r, off + r) for r in range(0, n_seg, ch)]
        off += n_seg
    if interleave:
        score_chunks = [c + (ch,) for c in chunks]
    else:
        score_chunks = [(si, 0, r0, n) for si, (r0, n) in enumerate(seg_rows)]
    if band:
        row = lax.broadcasted_iota(jnp.int32, (tq, tq), 0)
        col = lax.broadcasted_iota(jnp.int32, (tq, tq), 1)
        for e in range(subs):
            tile = pl.program_id(1) * subs + e
            for a in range(3):
                key_block = tile - 1 + a
                ok = ((jnp.abs((a - 1) * tq + row - col) <= WINDOW)
                      & (key_block >= 0) & (key_block < seq // tq))
                bias_scr[e, a] = jnp.where(ok, 0.0, NEG_INF)

    assert lag_b >= 1 and lag_c >= 1 and slots > max(lag_b, lag_c)
    row_max = {}
    den_part = {}
    v_rows = vtn_ref.shape[-2] * group // N_HEADS
    den_on_mxu = v_rows > HEAD_DIM

    def scores_and_max(i):
        e, hd = divmod(i, N_HEADS)
        kh = hd // group
        q_cols = slice(head_w * hd, head_w * (hd + 1))
        qh = q_ref[e, q_cols, :] if own_keys else q_ref[e * tq:(e + 1) * tq, q_cols]
        mx = None
        for si, r, g, n in score_chunks:
            kk = segs[si][1](e, r, n, slice(head_w * kh, head_w * (kh + 1)))
            if own_keys:
                s = jnp.dot(kk, qh, preferred_element_type=F32)
            else:
                s = lax.dot_general(kk, qh, NT_DIMS, preferred_element_type=F32)
            if si in band_of:
                s = s + bias_scr[e, band_of[si], r:r + n, :]
            s_scr[i % slots, g:g + n, :] = s
            part = jnp.max(s.reshape(n // 8, 8, tq), axis=0)
            mx = part if mx is None else jnp.maximum(mx, part)
            yield
        m = jnp.max(mx, axis=0, keepdims=True)
        if has_sink:
            m = jnp.maximum(m, sink_ref[hd] * LOG2E)
        row_max[i] = m

    def exponentials(i):
        hd = i % N_HEADS
        m = row_max.pop(i)
        mb = jnp.broadcast_to(m, (8, tq))
        tot = None
        for _, _, g in chunks:
            p = jnp.exp2(s_scr[i % slots, g:g + ch, :].reshape(ch // 8, 8, tq) - mb)
            if not den_on_mxu:
                part = jnp.sum(p, axis=0)
                tot = part if tot is None else tot + part
            p_scr[i % slots, g:g + ch, :] = p.reshape(ch, tq).astype(BF16)
            yield
        den = None if den_on_mxu else jnp.sum(tot, axis=0, keepdims=True)
        if has_sink:
            sink = jnp.exp2(sink_ref[hd] * LOG2E - m)
            den = sink if den is None else den + sink
        den_part[i] = den

    def weighted_values(i):
        e, hd = divmod(i, N_HEADS)
        kh = hd // group
        acc = None
        for si, (_, _, values_t) in enumerate(segs):
            r0, n = seg_rows[si]
            vt = values_t(e, slice(v_rows * kh, v_rows * (kh + 1)))
            part = jnp.dot(vt, p_scr[i % slots, r0:r0 + n, :], preferred_element_type=F32)
            acc = part if acc is None else acc + part
            yield
        den = den_part.pop(i)
        if den_on_mxu:
            mxu_sum = acc[HEAD_DIM:HEAD_DIM + 1, :]
            den = mxu_sum if den is None else den + mxu_sum
        ot_scr[e, HEAD_DIM * hd:HEAD_DIM * (hd + 1), :] = acc[:HEAD_DIM, :] / den

    def finish(e):
        rows = slice(e * tq, (e + 1) * tq)
        for c0 in range(0, D_MODEL, OUT_BLOCK):
            cols = slice(c0, c0 + OUT_BLOCK)
            o = ot_scr[e, cols, :].T
            g_scr[:, cols] = (o * gate_ref[rows, cols].astype(F32)).astype(BF16)
            yield
        for c0 in range(0, D_MODEL, OUT_BLOCK):
            cols = slice(c0, c0 + OUT_BLOCK)
            y = jnp.dot(g_scr[...], wo_ref[:, cols], preferred_element_type=F32)
            xn = x_ref[rows, cols] + mod_ref[:, 2 * D_MODEL + c0:2 * D_MODEL + c0 + OUT_BLOCK] * y
            if final:
                y_scr[:, cols] = xn
            else:
                o_ref[rows, cols] = xn
            yield
        if final:
            o_ref[rows, :] = _rms(y_scr[...], fg_ref[...])

    n_items = subs * N_HEADS
    for step in range(n_items + lag_b + lag_c + 1):
        live = []
        if step < n_items:
            live.append(scores_and_max(step))
        if 0 <= step - lag_b < n_items:
            live.append(exponentials(step - lag_b))
        done = step - lag_b - lag_c
        if 0 <= done < n_items:
            live.append(weighted_values(done))
        if done >= N_HEADS and done % N_HEADS == 0:
            live.append(finish(done // N_HEADS - 1))
        if not interleave:
            for g in live:
                for _ in g:
                    pass
            live = []
        while live:
            live = [g for g in live if next(g, True) is None]


def _attend(kind, q, k_new, vt_new, cache, gate, x, mods_l, w_out, sink, final_g, *, batch, seq, ctx,
            lags):
    lag_b, lag_c, interleave, subs = lags
    tq = TOKEN_TILE
    n_tok = batch * seq
    band = kind == 2 and not ctx
    head_w = MLA_HEAD_PAD if kind == 0 else HEAD_DIM
    kw = k_new.shape[1]
    vw = vt_new.shape[1]
    own_keys = cache is None
    assert seq == tq if own_keys else seq % (subs * tq) == 0
    grid = (batch // subs, 1) if own_keys else (batch, seq // (subs * tq))
    tok_spec = lambda w: pl.BlockSpec((subs * tq, w), lambda b, t: (b * grid[1] + t, 0))
    qw = q.shape[1]
    if own_keys:
        in_specs = [
            pl.BlockSpec((subs, qw, seq), lambda b, t: (b, 0, 0)),
            pl.BlockSpec((subs * seq, kw), lambda b, t: (b, 0)),
            pl.BlockSpec((subs, vw, seq), lambda b, t: (b, 0, 0)),
        ]
    elif band:
        n_blk = seq // tq
        block = lambda j: (lambda b, t: jnp.clip(subs * t - 1 + j, 0, n_blk - 1))
        in_specs = [tok_spec(qw)]
        in_specs += [pl.BlockSpec((tq, kw), lambda b, t, f=block(j): (b * n_blk + f(b, t), 0))
                     for j in range(subs + 2)]
        in_specs += [pl.BlockSpec((None, vw, tq), lambda b, t, f=block(j): (b, 0, f(b, t)))
                     for j in range(subs + 2)]
    else:
        in_specs = [
            tok_spec(qw),
            pl.BlockSpec((seq, kw), lambda b, t: (b, 0)),
            pl.BlockSpec((None, vw, seq), lambda b, t: (b, 0, 0)),
        ]
    n_own = subs + 2 if band else 1
    args = [q] + [k_new] * n_own + [vt_new] * n_own
    n_keys = 3 * tq if band else seq
    if cache is not None:
        kc, vtc = cache
        n_c = kc.shape[1]
        n_keys += n_c
        in_specs += [
            pl.BlockSpec((None, n_c, kw), lambda b, t: (b, 0, 0)),
            pl.BlockSpec((None, vw, n_c), lambda b, t: (b, 0, 0)),
        ]
        args += [kc, vtc]
    mod_row = (lambda b, t: (4, 0, 0)) if ctx else (lambda b, t: (b, 0, 0))
    in_specs += [
        tok_spec(D_MODEL),
        tok_spec(D_MODEL),
        pl.BlockSpec((None, 1, 3 * D_MODEL), mod_row),
        pl.BlockSpec((D_MODEL, D_MODEL), lambda b, t: (0, 0)),
    ]
    args += [gate, x, mods_l, w_out]
    if sink is not None:
        in_specs.append(pl.BlockSpec(memory_space=pltpu.SMEM))
        args.append(sink)
    if final_g is not None:
        in_specs.append(pl.BlockSpec((1, D_MODEL), lambda b, t: (0, 0)))
        args.append(final_g.reshape(1, D_MODEL))
    body = functools.partial(
        _attn_kernel,
        head_w=head_w,
        group=1 if kind == 0 else GROUPS,
        has_cache=cache is not None,
        has_sink=sink is not None,
        band=band,
        final=final_g is not None,
        lag_b=lag_b,
        lag_c=lag_c,
        interleave=interleave,
        subs=subs,
        own_keys=own_keys,
        seq=seq,
    )
    slots = max(lag_b, lag_c) + 1
    scratch = [pltpu.VMEM((subs, N_HEADS * HEAD_DIM, tq), F32),
               pltpu.VMEM((slots, n_keys, tq), F32),
               pltpu.VMEM((slots, n_keys, tq), BF16),
               pltpu.VMEM((tq, D_MODEL), BF16),
               pltpu.VMEM((tq, D_MODEL), F32)]
    if band:
        scratch.append(pltpu.VMEM((subs, 3, tq, tq), F32))
    return pl.pallas_call(
        body,
        grid=grid,
        in_specs=in_specs,
        out_specs=tok_spec(D_MODEL),
        out_shape=jax.ShapeDtypeStruct((n_tok, D_MODEL), F32),
        scratch_shapes=scratch,
        compiler_params=pltpu.CompilerParams(
            dimension_semantics=("arbitrary", "arbitrary"), vmem_limit_bytes=VMEM_LIMIT),
        name=f"attn_k{kind}_{'ctx' if ctx else 'lat'}",
    )(*args)


def _rope_lane_tables(seq, rot_dim):
    rows = seq // GRID_W
    row = jnp.repeat(jnp.arange(rows), GRID_W).astype(F32)
    col = jnp.tile(jnp.arange(GRID_W), rows).astype(F32)
    nf = rot_dim // 4
    freqs = ROPE_THETA ** (-jnp.arange(nf, dtype=F32) / nf)
    ang = jnp.concatenate([row[:, None] * freqs, col[:, None] * freqs], axis=-1)
    cos, sin = jnp.cos(ang), jnp.sin(ang)
    cos_l = jnp.repeat(cos, 2, axis=-1)
    sin_l = jnp.stack([-sin, sin], axis=-1).reshape(seq, rot_dim)
    return cos_l, sin_l


def _embed_lanes(table, start, fill):
    seq, w = table.shape
    return jnp.concatenate(
        [jnp.full((seq, start), fill, F32), table, jnp.full((seq, LANES - start - w), fill, F32)], axis=1)


def _mla_weights(w_in, q_norm, w_uq, kv_norm, w_ukv):
    c0 = MLA_Q_LORA + MLA_KV_LORA
    win = jnp.concatenate(
        [w_in[:, :c0], w_in[:, c0 + MLA_ROPE:], w_in[:, c0:c0 + MLA_ROPE],
         jnp.zeros((D_MODEL, MLA_IN_PAD - w_in.shape[1]), F32)], axis=1).astype(BF16)
    hq = MLA_NOPE + MLA_ROPE
    wuq = jnp.pad(w_uq.reshape(MLA_Q_LORA, N_HEADS, hq),
                  ((0, 0), (0, 0), (0, MLA_HEAD_PAD - hq))).reshape(MLA_Q_LORA, -1).astype(BF16)
    wkv = w_ukv.reshape(MLA_KV_LORA, N_HEADS, MLA_NOPE + MLA_V)
    wk = jnp.pad(wkv[:, :, :MLA_NOPE],
                 ((0, 0), (0, 0), (0, MLA_HEAD_PAD - MLA_NOPE))).reshape(MLA_KV_LORA, -1).astype(BF16)
    wvt = wkv[:, :, MLA_NOPE:].reshape(MLA_KV_LORA, -1).T.astype(BF16)
    return [win, q_norm.reshape(1, -1), wuq, kv_norm.reshape(1, -1), wk, wvt]


def kernel(x_prompt, x_sample, cache_mla_ckv, cache_mla_kpe, cache_gqa_k, cache_gqa_v,
           cache_swa_k, cache_swa_v, c, c_ctx, norm_g, w_ada, b_ada, w_out,
           mla_w_in, mla_q_norm, mla_w_uq, mla_kv_norm, mla_w_ukv,
           gqa_w_in, gqa_q_norm, gqa_k_norm, swa_w_in, swa_sink, final_norm_g):
    bc, sc, _ = x_prompt.shape
    bl, sl, _ = x_sample.shape
    n_past = cache_mla_ckv.shape[2]
    kvw = KV_HEADS * HEAD_DIM

    cc = jnp.concatenate([c, c_ctx[None, :], jnp.zeros((8 - bl - 1, D_MODEL), F32)], axis=0)
    mods = _ada_mods(cc, w_ada, b_ada).reshape(DEPTH, 8, 1, 3 * D_MODEL)

    cos_h, sin_h = _rope_lane_tables(sl, HEAD_DIM)
    gqa_tables = [jnp.tile(cos_h, (1, 2)), jnp.tile(sin_h, (1, 2))]
    cos_r, sin_r = _rope_lane_tables(sl, MLA_ROPE)
    mla_tables = [_embed_lanes(cos_r, MLA_NOPE, 1.0), _embed_lanes(sin_r, MLA_NOPE, 0.0),
                  _embed_lanes(cos_r, 0, 1.0), _embed_lanes(sin_r, 0, 0.0)]

    w_out_bf = w_out.astype(BF16)

    xc = x_prompt.reshape(bc * sc, D_MODEL)
    xl = x_sample.reshape(bl * sl, D_MODEL)
    new_caches = {0: ([], []), 1: ([], []), 2: ([], [])}
    for i in range(DEPTH):
        kind, j = i % 3, i // 3
        mods_l = mods[i]
        wo = w_out_bf[i]
        final_g = final_norm_g if i == DEPTH - 1 else None
        sink = None
        if kind == 0:
            weights = _mla_weights(mla_w_in[j], mla_q_norm[j], mla_w_uq[j], mla_kv_norm[j], mla_w_ukv[j])
            tables = mla_tables
            kpe_pad = jnp.pad(cache_mla_kpe[:, j], ((0, 0), (0, 0), (0, LANES - MLA_ROPE)))
            cache = _mla_cache(cache_mla_ckv[:, j], kpe_pad, weights[4], weights[5])
        else:
            tables = gqa_tables
            if kind == 1:
                weights = [gqa_w_in[j].astype(BF16),
                           jnp.tile(gqa_q_norm[j], 2).reshape(1, LANES),
                           jnp.tile(gqa_k_norm[j], 2).reshape(1, LANES)]
                cache = _gqa_cache(cache_gqa_k[:, j].reshape(bl, n_past, kvw),
                                   cache_gqa_v[:, j].reshape(bl, n_past, kvw))
            else:
                weights = [swa_w_in[j].astype(BF16)]
                sink = swa_sink[j]
                cache = _gqa_cache(cache_swa_k[:, j].reshape(bl, n_past, kvw),
                                   cache_swa_v[:, j].reshape(bl, n_past, kvw))

        q, k, vt, gate, c_a, c_b = _project(kind, xc, mods_l, norm_g[i], weights, tables,
                                            batch=bc, seq=sc, ctx=True)
        new_caches[kind][0].append(c_a)
        new_caches[kind][1].append(c_b)
        xc = _attend(kind, q, k, vt, None, gate, xc, mods_l, wo, sink, final_g,
                     batch=bc, seq=sc, ctx=True, lags=CTX_STAGE_LAGS)

        q, k, vt, gate = _project(kind, xl, mods_l, norm_g[i], weights, tables,
                                  batch=bl, seq=sl, ctx=False)
        xl = _attend(kind, q, k, vt, cache, gate, xl, mods_l, wo, sink, final_g,
                     batch=bl, seq=sl, ctx=False, lags=LAT_STAGE_LAGS)

    def stack(parts, tail):
        if len(tail) == 2:
            parts = [p[:, :HEAD_DIM] for p in parts]
        return jnp.stack([p.reshape((bc, sc) + tail) for p in parts], axis=1)

    return (xc.reshape(bc, sc, D_MODEL), xl.reshape(bl, sl, D_MODEL),
            stack(new_caches[0][0], (MLA_KV_LORA,)), stack(new_caches[0][1], (MLA_ROPE,)),
            stack(new_caches[1][0], (KV_HEADS, HEAD_DIM)), stack(new_caches[1][1], (KV_HEADS, HEAD_DIM)),
            stack(new_caches[2][0], (KV_HEADS, HEAD_DIM)), stack(new_caches[2][1], (KV_HEADS, HEAD_DIM)))
```

```python
import functools

import jax
import jax.numpy as jnp
from jax import lax
from jax.experimental import pallas as pl
from jax.experimental.pallas import tpu as pltpu

F32 = jnp.float32
BF16 = jnp.bfloat16

D_MODEL = 1024
DEPTH = 4
GRID_W = 64
N_HEADS = 16
HEAD_DIM = 64
KV_HEADS = 4
GROUPS = N_HEADS // KV_HEADS
MLA_Q_LORA = 384
MLA_KV_LORA = 256
MLA_NOPE = 64
MLA_ROPE = 32
MLA_V = 64
WINDOW = 128
ROPE_THETA = 10000.0
EPS = 1e-6
NEG_INF = -1e30

LOG2E = 1.4426950408889634
GQA_Q_SCALE = HEAD_DIM ** -0.5 * LOG2E
MLA_Q_SCALE = (MLA_NOPE + MLA_ROPE) ** -0.5 * LOG2E

LANES = 128
TOKEN_TILE = 256
PROJ_TILE = 512
KEY_CHUNK = 256
OUT_BLOCK = 256
CTX_STAGE_LAGS = (3, 3, True, 4)
LAT_STAGE_LAGS = (3, 3, False, 2)
MLA_HEAD_PAD = 128
V_ROWS = HEAD_DIM + 16
MLA_IN_PAD = 1792
VMEM_LIMIT = 56 * 1024 * 1024
NT_DIMS = (((1,), (1,)), ((), ()))


def _silu(v):
    return v * jax.nn.sigmoid(v)


def _rms(v, g):
    return v * lax.rsqrt(jnp.mean(v * v, axis=-1, keepdims=True) + EPS) * g


def _mod_norm(x, g, mod):
    shift = mod[:, :D_MODEL]
    scale = mod[:, D_MODEL:2 * D_MODEL]
    return _rms(x, g) * (1.0 + scale) + shift


def _rope_tile(v, cos, sin_signed):
    lane = lax.broadcasted_iota(jnp.int32, v.shape, 1)
    nxt = pltpu.roll(v, LANES - 1, axis=1)
    prv = pltpu.roll(v, 1, axis=1)
    swapped = jnp.where((lane & 1) == 0, nxt, prv)
    return v * cos + swapped * sin_signed


def _ada_kernel(c_ref, w_ref, b_ref, o_ref):
    s = _silu(c_ref[...]).astype(BF16)
    o_ref[...] = jnp.dot(s, w_ref[...].astype(BF16), preferred_element_type=F32) + b_ref[...]


def _ada_mods(cc, w_ada, b_ada):
    tn = D_MODEL
    return pl.pallas_call(
        _ada_kernel,
        grid=(DEPTH, 3 * D_MODEL // tn),
        in_specs=[
            pl.BlockSpec((8, D_MODEL), lambda i, n: (0, 0)),
            pl.BlockSpec((None, D_MODEL, tn), lambda i, n: (i, 0, n)),
            pl.BlockSpec((None, 1, tn), lambda i, n: (i, 0, n)),
        ],
        out_specs=pl.BlockSpec((None, 8, tn), lambda i, n: (i, 0, n)),
        out_shape=jax.ShapeDtypeStruct((DEPTH, 8, 3 * D_MODEL), F32),
        compiler_params=pltpu.CompilerParams(vmem_limit_bytes=VMEM_LIMIT),
        name="ada_mods",
    )(cc, w_ada, b_ada.reshape(DEPTH, 1, 3 * D_MODEL))


def _store_queries(q_out, col0, blk):
    w = blk.shape[1]
    if len(q_out.shape) == 2:
        q_out[:, col0:col0 + w] = blk.astype(BF16)
        return
    blk_t = blk.T.astype(BF16)
    seq = q_out.shape[2]
    for e in range(q_out.shape[0]):
        q_out[e, col0:col0 + w, :] = blk_t[:, e * seq:(e + 1) * seq]


def _store_values_t(vt_out, vt, n_heads):
    if len(vt_out.shape) == 3:
        seq = vt_out.shape[2]
        for e in range(vt_out.shape[0]):
            _store_values_t(vt_out.at[e], vt[:, e * seq:(e + 1) * seq], n_heads)
        return
    v_rows = vt_out.shape[0] // n_heads
    if v_rows == HEAD_DIM:
        vt_out[...] = vt.astype(BF16)
        return
    ones = jnp.ones((v_rows - HEAD_DIM, vt.shape[1]), BF16)
    for h in range(n_heads):
        vt_out[v_rows * h:v_rows * h + HEAD_DIM, :] = vt[HEAD_DIM * h:HEAD_DIM * (h + 1), :].astype(BF16)
        vt_out[v_rows * h + HEAD_DIM:v_rows * (h + 1), :] = ones


def _mla_expand(ckv_bf, kpe, wk_ref, wvt_ref, k_out, vt_out):
    k = jnp.dot(ckv_bf, wk_ref[...], preferred_element_type=F32)
    lane = lax.broadcasted_iota(jnp.int32, kpe.shape, 1)
    rope_lanes = (lane >= MLA_NOPE) & (lane < MLA_NOPE + MLA_ROPE)
    kpe_at_rope = pltpu.roll(kpe, MLA_NOPE, axis=1)
    for hd in range(N_HEADS):
        cols = slice(MLA_HEAD_PAD * hd, MLA_HEAD_PAD * (hd + 1))
        k_out[:, cols] = jnp.where(rope_lanes, kpe_at_rope, k[:, cols]).astype(BF16)
    vt = lax.dot_general(wvt_ref[...], ckv_bf, NT_DIMS, preferred_element_type=F32)
    _store_values_t(vt_out, vt, N_HEADS)


def _proj_mla_kernel(*refs, rope, ctx):
    refs = list(refs)
    x_ref, mod_ref, g_ref, win_ref, qn_ref, wuq_ref, kvn_ref, wk_ref, wvt_ref = refs[:9]
    pos = 9
    if rope:
        cq_ref, sq_ref, ck_ref, sk_ref = refs[pos:pos + 4]
        pos += 4
    q_out, k_out, vt_out, gate_out = refs[pos:pos + 4]
    pos += 4
    if ctx:
        ckv_out, kpe_out = refs[pos:pos + 2]

    h = _mod_norm(x_ref[...], g_ref[...], mod_ref[...]).astype(BF16)
    z = jnp.dot(h, win_ref[...], preferred_element_type=F32)
    cq = z[:, :MLA_Q_LORA]
    ckv = z[:, MLA_Q_LORA:MLA_Q_LORA + MLA_KV_LORA]
    gate = z[:, 640:640 + D_MODEL]
    kpe = z[:, 640 + D_MODEL:]
    gate_out[...] = _silu(gate).astype(BF16)

    qf = jnp.dot(_rms(cq, qn_ref[...]).astype(BF16), wuq_ref[...], preferred_element_type=F32)
    ckvn = _rms(ckv, kvn_ref[...])
    if ctx:
        ckv_out[...] = ckvn
        kpe_out[...] = kpe[:, :MLA_ROPE]
    if rope:
        kpe = _rope_tile(kpe, ck_ref[...], sk_ref[...])
    _mla_expand(ckvn.astype(BF16), kpe, wk_ref, wvt_ref, k_out, vt_out)
    for hd in range(N_HEADS):
        blk = qf[:, MLA_HEAD_PAD * hd:MLA_HEAD_PAD * (hd + 1)]
        if rope:
            blk = _rope_tile(blk, cq_ref[...], sq_ref[...])
        _store_queries(q_out, MLA_HEAD_PAD * hd, blk * MLA_Q_SCALE)


def _store_head_rows(out_ref, pair, blk):
    tokens = blk.shape[0]
    out_ref[pl.ds(2 * pair, tokens, stride=KV_HEADS), :] = blk
    out_ref[pl.ds(2 * pair + 1, tokens, stride=KV_HEADS), :] = pltpu.roll(blk, HEAD_DIM, axis=1)


def _proj_gqa_kernel(*refs, qk_norm, rope, ctx):
    refs = list(refs)
    x_ref, mod_ref, g_ref, win_ref = refs[:4]
    pos = 4
    if qk_norm:
        qn_ref, kn_ref = refs[pos:pos + 2]
        pos += 2
    if rope:
        cos_ref, sin_ref = refs[pos:pos + 2]
        pos += 2
    q_out, k_out, vt_out, gate_out = refs[pos:pos + 4]
    pos += 4
    if ctx:
        kc_out, vc_out = refs[pos:pos + 2]

    nq = N_HEADS * HEAD_DIM
    nk = KV_HEADS * HEAD_DIM
    h = _mod_norm(x_ref[...], g_ref[...], mod_ref[...]).astype(BF16)
    z = jnp.dot(h, win_ref[...], preferred_element_type=F32)
    tm = z.shape[0]
    lo = lax.broadcasted_iota(jnp.int32, (tm, LANES), 1) < HEAD_DIM
    n_q_tiles = nq // LANES
    for c in range((nq + nk) // LANES):
        blk = z[:, LANES * c:LANES * (c + 1)]
        is_q = c < n_q_tiles
        if qk_norm:
            sq = blk * blk
            s_lo = jnp.sum(jnp.where(lo, sq, 0.0), axis=-1, keepdims=True)
            s_hi = jnp.sum(jnp.where(lo, 0.0, sq), axis=-1, keepdims=True)
            inv = jnp.where(lo, lax.rsqrt(s_lo * (1.0 / HEAD_DIM) + EPS),
                            lax.rsqrt(s_hi * (1.0 / HEAD_DIM) + EPS))
            blk = blk * inv * (qn_ref[...] if is_q else kn_ref[...])
        if ctx and not is_q:
            _store_head_rows(kc_out, c - n_q_tiles, blk)
        if rope:
            blk = _rope_tile(blk, cos_ref[...], sin_ref[...])
        if is_q:
            _store_queries(q_out, LANES * c, blk * GQA_Q_SCALE)
        else:
            k_out[:, LANES * (c - n_q_tiles):LANES * (c - n_q_tiles + 1)] = blk.astype(BF16)
    v = z[:, nq + nk:nq + 2 * nk]
    if ctx:
        for c in range(nk // LANES):
            _store_head_rows(vc_out, c, v[:, LANES * c:LANES * (c + 1)])
    _store_values_t(vt_out, v.T, KV_HEADS)
    gate_out[...] = _silu(z[:, nq + 2 * nk:]).astype(BF16)


def _const_spec(shape):
    return pl.BlockSpec(shape, lambda i: (0,) * len(shape))


def _project(kind, x, mods_l, norm_g, weights, tables, *, batch, seq, ctx):
    n_tok = batch * seq
    tm = PROJ_TILE
    tiles_per_seq = max(seq // tm, 1)
    seqs_per_tile = max(tm // seq, 1)
    assert tm == seqs_per_tile * seq // tiles_per_seq and (ctx or seqs_per_tile == 1)
    v_rows = V_ROWS if ctx else HEAD_DIM
    rope = not ctx
    mod_row = (lambda i: (4, 0, 0)) if ctx else (lambda i: (i // tiles_per_seq, 0, 0))
    in_specs = [
        pl.BlockSpec((tm, D_MODEL), lambda i: (i, 0)),
        pl.BlockSpec((None, 1, 3 * D_MODEL), mod_row),
        _const_spec((1, D_MODEL)),
    ]
    args = [x, mods_l, norm_g.reshape(1, D_MODEL)]
    for w in weights:
        in_specs.append(_const_spec(w.shape))
        args.append(w)
    if rope:
        for t in tables:
            in_specs.append(pl.BlockSpec((tm, LANES), lambda i: (i % tiles_per_seq, 0)))
            args.append(t)
    if kind == 0:
        qw, kw, vw = N_HEADS * MLA_HEAD_PAD, N_HEADS * MLA_HEAD_PAD, N_HEADS * v_rows
        body = functools.partial(_proj_mla_kernel, rope=rope, ctx=ctx)
        cache_shapes = [(n_tok, MLA_KV_LORA), (n_tok, MLA_ROPE)]
    else:
        qw, kw, vw = N_HEADS * HEAD_DIM, KV_HEADS * HEAD_DIM, KV_HEADS * v_rows
        body = functools.partial(_proj_gqa_kernel, qk_norm=(kind == 1), rope=rope, ctx=ctx)
        cache_shapes = [(n_tok, kw), (n_tok, kw)]
    def feature_major_spec(w):
        if seqs_per_tile > 1:
            return pl.BlockSpec((seqs_per_tile, w, seq), lambda i: (i, 0, 0))
        return pl.BlockSpec((None, w, tm), lambda i: (i // tiles_per_seq, 0, i % tiles_per_seq))

    out_shape = [
        jax.ShapeDtypeStruct((batch, qw, seq) if ctx else (n_tok, qw), BF16),
        jax.ShapeDtypeStruct((n_tok, kw), BF16),
        jax.ShapeDtypeStruct((batch, vw, seq), BF16),
        jax.ShapeDtypeStruct((n_tok, D_MODEL), BF16),
    ]
    out_specs = [
        feature_major_spec(qw) if ctx else pl.BlockSpec((tm, qw), lambda i: (i, 0)),
        pl.BlockSpec((tm, kw), lambda i: (i, 0)),
        feature_major_spec(vw),
        pl.BlockSpec((tm, D_MODEL), lambda i: (i, 0)),
    ]
    if ctx and kind != 0:
        for _ in cache_shapes:
            out_shape.append(jax.ShapeDtypeStruct((n_tok * KV_HEADS, LANES), F32))
            out_specs.append(pl.BlockSpec((tm * KV_HEADS, LANES), lambda i: (i, 0)))
    elif ctx:
        for s in cache_shapes:
            out_shape.append(jax.ShapeDtypeStruct(s, F32))
            out_specs.append(pl.BlockSpec((tm, s[1]), lambda i: (i, 0)))
    return pl.pallas_call(
        body,
        grid=(n_tok // tm,),
        in_specs=in_specs,
        out_specs=out_specs,
        out_shape=out_shape,
        compiler_params=pltpu.CompilerParams(
            dimension_semantics=("arbitrary",), vmem_limit_bytes=VMEM_LIMIT),
        name=f"proj_k{kind}_{'ctx' if ctx else 'lat'}",
    )(*args)


def _mla_cache_kernel(ckv_ref, kpe_ref, wk_ref, wvt_ref, k_out, vt_out):
    _mla_expand(ckv_ref[...].astype(BF16), kpe_ref[...], wk_ref, wvt_ref, k_out, vt_out)


def _mla_cache(ckv, kpe_pad, wk, wvt):
    b, n, _ = ckv.shape
    return pl.pallas_call(
        _mla_cache_kernel,
        grid=(b,),
        in_specs=[
            pl.BlockSpec((None, n, MLA_KV_LORA), lambda i: (i, 0, 0)),
            pl.BlockSpec((None, n, LANES), lambda i: (i, 0, 0)),
            _const_spec(wk.shape), _const_spec(wvt.shape),
        ],
        out_specs=[
            pl.BlockSpec((None, n, N_HEADS * MLA_HEAD_PAD), lambda i: (i, 0, 0)),
            pl.BlockSpec((None, N_HEADS * HEAD_DIM, n), lambda i: (i, 0, 0)),
        ],
        out_shape=[
            jax.ShapeDtypeStruct((b, n, N_HEADS * MLA_HEAD_PAD), BF16),
            jax.ShapeDtypeStruct((b, N_HEADS * HEAD_DIM, n), BF16),
        ],
        compiler_params=pltpu.CompilerParams(
            dimension_semantics=("arbitrary",), vmem_limit_bytes=VMEM_LIMIT),
        name="mla_cache",
    )(ckv, kpe_pad, wk, wvt)


def _gqa_cache_kernel(k_ref, v_ref, k_out, vt_out):
    k_out[...] = k_ref[...].astype(BF16)
    _store_values_t(vt_out, v_ref[...].T, KV_HEADS)


def _gqa_cache(k, v):
    b, n, w = k.shape
    return pl.pallas_call(
        _gqa_cache_kernel,
        grid=(b,),
        in_specs=[pl.BlockSpec((None, n, w), lambda i: (i, 0, 0))] * 2,
        out_specs=[
            pl.BlockSpec((None, n, w), lambda i: (i, 0, 0)),
            pl.BlockSpec((None, KV_HEADS * HEAD_DIM, n), lambda i: (i, 0, 0)),
        ],
        out_shape=[
            jax.ShapeDtypeStruct((b, n, w), BF16),
            jax.ShapeDtypeStruct((b, KV_HEADS * HEAD_DIM, n), BF16),
        ],
        compiler_params=pltpu.CompilerParams(dimension_semantics=("arbitrary",)),
        name="gqa_cache",
    )(k, v)


def _attn_kernel(*refs, head_w, group, has_cache, has_sink, band, final, lag_b, lag_c, interleave,
                 subs, own_keys, seq):
    refs = list(refs)
    q_ref = refs[0]
    n_own = subs + 2 if band else 1
    kn_refs = refs[1:1 + n_own]
    vtn_refs = refs[1 + n_own:1 + 2 * n_own]
    kn_ref, vtn_ref = kn_refs[0], vtn_refs[0]
    pos = 1 + 2 * n_own
    if has_cache:
        kc_ref, vtc_ref = refs[pos:pos + 2]
        pos += 2
    gate_ref, x_ref, mod_ref, wo_ref = refs[pos:pos + 4]
    pos += 4
    if has_sink:
        sink_ref = refs[pos]
        pos += 1
    if final:
        fg_ref = refs[pos]
        pos += 1
    o_ref, ot_scr, s_scr, p_scr, g_scr, y_scr = refs[pos:pos + 6]
    if band:
        bias_scr = refs[pos + 6]

    tq = TOKEN_TILE
    ch = KEY_CHUNK
    slots = s_scr.shape[0]
    segs = []
    if has_cache:
        segs.append((kc_ref.shape[0],
                     lambda e, r, n, cols: kc_ref[r:r + n, cols],
                     lambda e, rows: vtc_ref[rows, :]))
    band_of = {}
    if own_keys:
        segs.append((seq,
                     lambda e, r, n, cols: kn_ref[e * seq + r:e * seq + r + n, cols],
                     lambda e, rows: vtn_ref[e, rows, :]))
    elif band:
        for a in range(3):
            band_of[len(segs)] = a
            segs.append((tq,
                         lambda e, r, n, cols, a=a: kn_refs[e + a][r:r + n, cols],
                         lambda e, rows, a=a: vtn_refs[e + a][rows, :]))
    else:
        segs.append((seq,
                     lambda e, r, n, cols: kn_ref[r:r + n, cols],
                     lambda e, rows: vtn_ref[rows, :]))
    seg_rows = []
    chunks = []
    off = 0
    for si, (n_seg, _, _) in enumerate(segs):
        seg_rows.append((off, n_seg))
        chunks += [(si, r, off + r) for r in range(0, n_seg, ch)]
        off += n_seg
    if interleave:
        score_chunks = [c + (ch,) for c in chunks]
    else:
        score_chunks = [(si, 0, r0, n) for si, (r0, n) in enumerate(seg_rows)]
    if band:
        row = lax.broadcasted_iota(jnp.int32, (tq, tq), 0)
        col = lax.broadcasted_iota(jnp.int32, (tq, tq), 1)
        for e in range(subs):
            tile = pl.program_id(1) * subs + e
            for a in range(3):
                key_block = tile - 1 + a
                ok = ((jnp.abs((a - 1) * tq + row - col) <= WINDOW)
                      & (key_block >= 0) & (key_block < seq // tq))
                bias_scr[e, a] = jnp.where(ok, 0.0, NEG_INF)

    assert lag_b >= 1 and lag_c >= 1 and slots > max(lag_b, lag_c)
    row_max = {}
    den_part = {}
    v_rows = vtn_ref.shape[-2] * group // N_HEADS
    den_on_mxu = v_rows > HEAD_DIM

    def scores_and_max(i):
        e, hd = divmod(i, N_HEADS)
        kh = hd // group
        q_cols = slice(head_w * hd, head_w * (hd + 1))
        qh = q_ref[e, q_cols, :] if own_keys else q_ref[e * tq:(e + 1) * tq, q_cols]
        mx = None
        for si, r, g, n in score_chunks:
            kk = segs[si][1](e, r, n, slice(head_w * kh, head_w * (kh + 1)))
            if own_keys:
                s = jnp.dot(kk, qh, preferred_element_type=F32)
            else:
                s = lax.dot_general(kk, qh, NT_DIMS, preferred_element_type=F32)
            if si in band_of:
                s = s + bias_scr[e, band_of[si], r:r + n, :]
            s_scr[i % slots, g:g + n, :] = s
            part = jnp.max(s.reshape(n // 8, 8, tq), axis=0)
            mx = part if mx is None else jnp.maximum(mx, part)
            yield
        m = jnp.max(mx, axis=0, keepdims=True)
        if has_sink:
            m = jnp.maximum(m, sink_ref[hd] * LOG2E)
        row_max[i] = m

    def exponentials(i):
        hd = i % N_HEADS
        m = row_max.pop(i)
        mb = jnp.broadcast_to(m, (8, tq))
        tot = None
        for _, _, g in chunks:
            p = jnp.exp2(s_scr[i % slots, g:g + ch, :].reshape(ch // 8, 8, tq) - mb)
            if not den_on_mxu:
                part = jnp.sum(p, axis=0)
                tot = part if tot is None else tot + part
            p_scr[i % slots, g:g + ch, :] = p.reshape(ch, tq).astype(BF16)
            yield
        den = None if den_on_mxu else jnp.sum(tot, axis=0, keepdims=True)
        if has_sink:
            sink = jnp.exp2(sink_ref[hd] * LOG2E - m)
            den = sink if den is None else den + sink
        den_part[i] = den

    def weighted_values(i):
        e, hd = divmod(i, N_HEADS)
        kh = hd // group
        acc = None
        for si, (_, _, values_t) in enumerate(segs):
            r0, n = seg_rows[si]
            vt = values_t(e, slice(v_rows * kh, v_rows * (kh + 1)))
            part = jnp.dot(vt, p_scr[i % slots, r0:r0 + n, :], preferred_element_type=F32)
            acc = part if acc is None else acc + part
            yield
        den = den_part.pop(i)
        if den_on_mxu:
            mxu_sum = acc[HEAD_DIM:HEAD_DIM + 1, :]
            den = mxu_sum if den is None else den + mxu_sum
        ot_scr[e, HEAD_DIM * hd:HEAD_DIM * (hd + 1), :] = acc[:HEAD_DIM, :] / den

    def finish(e):
        rows = slice(e * tq, (e + 1) * tq)
        for c0 in range(0, D_MODEL, OUT_BLOCK):
            cols = slice(c0, c0 + OUT_BLOCK)
            o = ot_scr[e, cols, :].T
            g_scr[:, cols] = (o * gate_ref[rows, cols].astype(F32)).astype(BF16)
            yield
        for c0 in range(0, D_MODEL, OUT_BLOCK):
            cols = slice(c0, c0 + OUT_BLOCK)
            y = jnp.dot(g_scr[...], wo_ref[:, cols], preferred_element_type=F32)
            xn = x_ref[rows, cols] + mod_ref[:, 2 * D_MODEL + c0:2 * D_MODEL + c0 + OUT_BLOCK] * y
            if final:
                y_scr[:, cols] = xn
            else:
                o_ref[rows, cols] = xn
            yield
        if final:
            o_ref[rows, :] = _rms(y_scr[...], fg_ref[...])

    n_items = subs * N_HEADS
    for step in range(n_items + lag_b + lag_c + 1):
        live = []
        if step < n_items:
            live.append(scores_and_max(step))
        if 0 <= step - lag_b < n_items:
            live.append(exponentials(step - lag_b))
        done = step - lag_b - lag_c
        if 0 <= done < n_items:
            live.append(weighted_values(done))
        if done >= N_HEADS and done % N_HEADS == 0:
            live.append(finish(done // N_HEADS - 1))
        if not interleave:
            for g in live:
                for _ in g:
                    pass
            live = []
        while live:
            live = [g for g in live if next(g, True) is None]


def _attend(kind, q, k_new, vt_new, cache, gate, x, mods_l, w_out, sink, final_g, *, batch, seq, ctx,
            lags):
    lag_b, lag_c, interleave, subs = lags
    tq = TOKEN_TILE
    n_tok = batch * seq
    band = kind == 2 and not ctx
    head_w = MLA_HEAD_PAD if kind == 0 else HEAD_DIM
    kw = k_new.shape[1]
    vw = vt_new.shape[1]
    own_keys = cache is None
    assert seq == tq if own_keys else seq % (subs * tq) == 0
    grid = (batch // subs, 1) if own_keys else (batch, seq // (subs * tq))
    tok_spec = lambda w: pl.BlockSpec((subs * tq, w), lambda b, t: (b * grid[1] + t, 0))
    qw = q.shape[1]
    if own_keys:
        in_specs = [
            pl.BlockSpec((subs, qw, seq), lambda b, t: (b, 0, 0)),
            pl.BlockSpec((subs * seq, kw), lambda b, t: (b, 0)),
            pl.BlockSpec((subs, vw, seq), lambda b, t: (b, 0, 0)),
        ]
    elif band:
        n_blk = seq // tq
        block = lambda j: (lambda b, t: jnp.clip(subs * t - 1 + j, 0, n_blk - 1))
        in_specs = [tok_spec(qw)]
        in_specs += [pl.BlockSpec((tq, kw), lambda b, t, f=block(j): (b * n_blk + f(b, t), 0))
                     for j in range(subs + 2)]
        in_specs += [pl.BlockSpec((None, vw, tq), lambda b, t, f=block(j): (b, 0, f(b, t)))
                     for j in range(subs + 2)]
    else:
        in_specs = [
            tok_spec(qw),
            pl.BlockSpec((seq, kw), lambda b, t: (b, 0)),
            pl.BlockSpec((None, vw, seq), lambda b, t: (b, 0, 0)),
        ]
    n_own = subs + 2 if band else 1
    args = [q] + [k_new] * n_own + [vt_new] * n_own
    n_keys = 3 * tq if band else seq
    if cache is not None:
        kc, vtc = cache
        n_c = kc.shape[1]
        n_keys += n_c
        in_specs += [
            pl.BlockSpec((None, n_c, kw), lambda b, t: (b, 0, 0)),
            pl.BlockSpec((None, vw, n_c), lambda b, t: (b, 0, 0)),
        ]
        args += [kc, vtc]
    mod_row = (lambda b, t: (4, 0, 0)) if ctx else (lambda b, t: (b, 0, 0))
    in_specs += [
        tok_spec(D_MODEL),
        tok_spec(D_MODEL),
        pl.BlockSpec((None, 1, 3 * D_MODEL), mod_row),
        pl.BlockSpec((D_MODEL, D_MODEL), lambda b, t: (0, 0)),
    ]
    args += [gate, x, mods_l, w_out]
    if sink is not None:
        in_specs.append(pl.BlockSpec(memory_space=pltpu.SMEM))
        args.append(sink)
    if final_g is not None:
        in_specs.append(pl.BlockSpec((1, D_MODEL), lambda b, t: (0, 0)))
        args.append(final_g.reshape(1, D_MODEL))
    body = functools.partial(
        _attn_kernel,
        head_w=head_w,
        group=1 if kind == 0 else GROUPS,
        has_cache=cache is not None,
        has_sink=sink is not None,
        band=band,
        final=final_g is not None,
        lag_b=lag_b,
        lag_c=lag_c,
        interleave=interleave,
        subs=subs,
        own_keys=own_keys,
        seq=seq,
    )
    slots = max(lag_b, lag_c) + 1
    scratch = [pltpu.VMEM((subs, N_HEADS * HEAD_DIM, tq), F32),
               pltpu.VMEM((slots, n_keys, tq), F32),
               pltpu.VMEM((slots, n_keys, tq), BF16),
               pltpu.VMEM((tq, D_MODEL), BF16),
               pltpu.VMEM((tq, D_MODEL), F32)]
    if band:
        scratch.append(pltpu.VMEM((subs, 3, tq, tq), F32))
    return pl.pallas_call(
        body,
        grid=grid,
        in_specs=in_specs,
        out_specs=tok_spec(D_MODEL),
        out_shape=jax.ShapeDtypeStruct((n_tok, D_MODEL), F32),
        scratch_shapes=scratch,
        compiler_params=pltpu.CompilerParams(
            dimension_semantics=("arbitrary", "arbitrary"), vmem_limit_bytes=VMEM_LIMIT),
        name=f"attn_k{kind}_{'ctx' if ctx else 'lat'}",
    )(*args)


def _rope_lane_tables(seq, rot_dim):
    rows = seq // GRID_W
    row = jnp.repeat(jnp.arange(rows), GRID_W).astype(F32)
    col = jnp.tile(jnp.arange(GRID_W), rows).astype(F32)
    nf = rot_dim // 4
    freqs = ROPE_THETA ** (-jnp.arange(nf, dtype=F32) / nf)
    ang = jnp.concatenate([row[:, None] * freqs, col[:, None] * freqs], axis=-1)
    cos, sin = jnp.cos(ang), jnp.sin(ang)
    cos_l = jnp.repeat(cos, 2, axis=-1)
    sin_l = jnp.stack([-sin, sin], axis=-1).reshape(seq, rot_dim)
    return cos_l, sin_l


def _embed_lanes(table, start, fill):
    seq, w = table.shape
    return jnp.concatenate(
        [jnp.full((seq, start), fill, F32), table, jnp.full((seq, LANES - start - w), fill, F32)], axis=1)


def _mla_weights(w_in, q_norm, w_uq, kv_norm, w_ukv):
    c0 = MLA_Q_LORA + MLA_KV_LORA
    win = jnp.concatenate(
        [w_in[:, :c0], w_in[:, c0 + MLA_ROPE:], w_in[:, c0:c0 + MLA_ROPE],
         jnp.zeros((D_MODEL, MLA_IN_PAD - w_in.shape[1]), F32)], axis=1).astype(BF16)
    hq = MLA_NOPE + MLA_ROPE
    wuq = jnp.pad(w_uq.reshape(MLA_Q_LORA, N_HEADS, hq),
                  ((0, 0), (0, 0), (0, MLA_HEAD_PAD - hq))).reshape(MLA_Q_LORA, -1).astype(BF16)
    wkv = w_ukv.reshape(MLA_KV_LORA, N_HEADS, MLA_NOPE + MLA_V)
    wk = jnp.pad(wkv[:, :, :MLA_NOPE],
                 ((0, 0), (0, 0), (0, MLA_HEAD_PAD - MLA_NOPE))).reshape(MLA_KV_LORA, -1).astype(BF16)
    wvt = wkv[:, :, MLA_NOPE:].reshape(MLA_KV_LORA, -1).T.astype(BF16)
    return [win, q_norm.reshape(1, -1), wuq, kv_norm.reshape(1, -1), wk, wvt]


def kernel(x_prompt, x_sample, cache_mla_ckv, cache_mla_kpe, cache_gqa_k, cache_gqa_v,
           cache_swa_k, cache_swa_v, c, c_ctx, norm_g, w_ada, b_ada, w_out,
           mla_w_in, mla_q_norm, mla_w_uq, mla_kv_norm, mla_w_ukv,
           gqa_w_in, gqa_q_norm, gqa_k_norm, swa_w_in, swa_sink, final_norm_g):
    bc, sc, _ = x_prompt.shape
    bl, sl, _ = x_sample.shape
    n_past = cache_mla_ckv.shape[2]
    kvw = KV_HEADS * HEAD_DIM

    cc = jnp.concatenate([c, c_ctx[None, :], jnp.zeros((8 - bl - 1, D_MODEL), F32)], axis=0)
    mods = _ada_mods(cc, w_ada, b_ada).reshape(DEPTH, 8, 1, 3 * D_MODEL)

    cos_h, sin_h = _rope_lane_tables(sl, HEAD_DIM)
    gqa_tables = [jnp.tile(cos_h, (1, 2)), jnp.tile(sin_h, (1, 2))]
    cos_r, sin_r = _rope_lane_tables(sl, MLA_ROPE)
    mla_tables = [_embed_lanes(cos_r, MLA_NOPE, 1.0), _embed_lanes(sin_r, MLA_NOPE, 0.0),
                  _embed_lanes(cos_r, 0, 1.0), _embed_lanes(sin_r, 0, 0.0)]

    w_out_bf = w_out.astype(BF16)

    xc = x_prompt.reshape(bc * sc, D_MODEL)
    xl = x_sample.reshape(bl * sl, D_MODEL)
    new_caches = {0: ([], []), 1: ([], []), 2: ([], [])}
    for i in range(DEPTH):
        kind, j = i % 3, i // 3
        mods_l = mods[i]
        wo = w_out_bf[i]
        final_g = final_norm_g if i == DEPTH - 1 else None
        sink = None
        if kind == 0:
            weights = _mla_weights(mla_w_in[j], mla_q_norm[j], mla_w_uq[j], mla_kv_norm[j], mla_w_ukv[j])
            tables = mla_tables
            kpe_pad = jnp.pad(cache_mla_kpe[:, j], ((0, 0), (0, 0), (0, LANES - MLA_ROPE)))
            cache = _mla_cache(cache_mla_ckv[:, j], kpe_pad, weights[4], weights[5])
        else:
            tables = gqa_tables
            if kind == 1:
                weights = [gqa_w_in[j].astype(BF16),
                           jnp.tile(gqa_q_norm[j], 2).reshape(1, LANES),
                           jnp.tile(gqa_k_norm[j], 2).reshape(1, LANES)]
                cache = _gqa_cache(cache_gqa_k[:, j].reshape(bl, n_past, kvw),
                                   cache_gqa_v[:, j].reshape(bl, n_past, kvw))
            else:
                weights = [swa_w_in[j].astype(BF16)]
                sink = swa_sink[j]
                cache = _gqa_cache(cache_swa_k[:, j].reshape(bl, n_past, kvw),
                                   cache_swa_v[:, j].reshape(bl, n_past, kvw))

        q, k, vt, gate, c_a, c_b = _project(kind, xc, mods_l, norm_g[i], weights, tables,
                                            batch=bc, seq=sc, ctx=True)
        new_caches[kind][0].append(c_a)
        new_caches[kind][1].append(c_b)
        xc = _attend(kind, q, k, vt, None, gate, xc, mods_l, wo, sink, final_g,
                     batch=bc, seq=sc, ctx=True, lags=CTX_STAGE_LAGS)

        q, k, vt, gate = _project(kind, xl, mods_l, norm_g[i], weights, tables,
                                  batch=bl, seq=sl, ctx=False)
        xl = _attend(kind, q, k, vt, cache, gate, xl, mods_l, wo, sink, final_g,
                     batch=bl, seq=sl, ctx=False, lags=LAT_STAGE_LAGS)

    def stack(parts, tail):
        if len(tail) == 2:
            parts = [p[:, :HEAD_DIM] for p in parts]
        return jnp.stack([p.reshape((bc, sc) + tail) for p in parts], axis=1)

    return (xc.reshape(bc, sc, D_MODEL), xl.reshape(bl, sl, D_MODEL),
            stack(new_caches[0][0], (MLA_KV_LORA,)), stack(new_caches[0][1], (MLA_ROPE,)),
            stack(new_caches[1][0], (KV_HEADS, HEAD_DIM)), stack(new_caches[1][1], (KV_HEADS, HEAD_DIM)),
            stack(new_caches[2][0], (KV_HEADS, HEAD_DIM)), stack(new_caches[2][1], (KV_HEADS, HEAD_DIM)))
```

```python
import functools

import jax
import jax.numpy as jnp
from jax import lax
from jax.experimental import pallas as pl
from jax.experimental.pallas import tpu as pltpu

F32 = jnp.float32
BF16 = jnp.bfloat16

D_MODEL = 1024
DEPTH = 4
GRID_W = 64
N_HEADS = 16
HEAD_DIM = 64
KV_HEADS = 4
GROUPS = N_HEADS // KV_HEADS
MLA_Q_LORA = 384
MLA_KV_LORA = 256
MLA_NOPE = 64
MLA_ROPE = 32
MLA_V = 64
WINDOW = 128
ROPE_THETA = 10000.0
EPS = 1e-6
NEG_INF = -1e30

LOG2E = 1.4426950408889634
GQA_Q_SCALE = HEAD_DIM ** -0.5 * LOG2E
MLA_Q_SCALE = (MLA_NOPE + MLA_ROPE) ** -0.5 * LOG2E

LANES = 128
TOKEN_TILE = 256
PROJ_TILE = 512
KEY_CHUNK = 256
OUT_BLOCK = 256
CTX_STAGE_LAGS = (3, 3, True, 4)
LAT_STAGE_LAGS = (2, 2, True, 2)
MLA_HEAD_PAD = 128
V_ROWS = HEAD_DIM + 16
MLA_IN_PAD = 1792
VMEM_LIMIT = 56 * 1024 * 1024
NT_DIMS = (((1,), (1,)), ((), ()))


def _silu(v):
    return v * jax.nn.sigmoid(v)


def _rms(v, g):
    return v * lax.rsqrt(jnp.mean(v * v, axis=-1, keepdims=True) + EPS) * g


def _mod_norm(x, g, mod):
    shift = mod[:, :D_MODEL]
    scale = mod[:, D_MODEL:2 * D_MODEL]
    return _rms(x, g) * (1.0 + scale) + shift


def _rope_tile(v, cos, sin_signed):
    lane = lax.broadcasted_iota(jnp.int32, v.shape, 1)
    nxt = pltpu.roll(v, LANES - 1, axis=1)
    prv = pltpu.roll(v, 1, axis=1)
    swapped = jnp.where((lane & 1) == 0, nxt, prv)
    return v * cos + swapped * sin_signed


def _ada_kernel(c_ref, w_ref, b_ref, o_ref):
    s = _silu(c_ref[...]).astype(BF16)
    o_ref[...] = jnp.dot(s, w_ref[...].astype(BF16), preferred_element_type=F32) + b_ref[...]


def _ada_mods(cc, w_ada, b_ada):
    tn = D_MODEL
    return pl.pallas_call(
        _ada_kernel,
        grid=(DEPTH, 3 * D_MODEL // tn),
        in_specs=[
            pl.BlockSpec((8, D_MODEL), lambda i, n: (0, 0)),
            pl.BlockSpec((None, D_MODEL, tn), lambda i, n: (i, 0, n)),
            pl.BlockSpec((None, 1, tn), lambda i, n: (i, 0, n)),
        ],
        out_specs=pl.BlockSpec((None, 8, tn), lambda i, n: (i, 0, n)),
        out_shape=jax.ShapeDtypeStruct((DEPTH, 8, 3 * D_MODEL), F32),
        compiler_params=pltpu.CompilerParams(vmem_limit_bytes=VMEM_LIMIT),
        name="ada_mods",
    )(cc, w_ada, b_ada.reshape(DEPTH, 1, 3 * D_MODEL))


def _store_queries(q_out, col0, blk):
    w = blk.shape[1]
    if len(q_out.shape) == 2:
        q_out[:, col0:col0 + w] = blk.astype(BF16)
        return
    blk_t = blk.T.astype(BF16)
    seq = q_out.shape[2]
    for e in range(q_out.shape[0]):
        q_out[e, col0:col0 + w, :] = blk_t[:, e * seq:(e + 1) * seq]


def _store_values_t(vt_out, vt, n_heads):
    if len(vt_out.shape) == 3:
        seq = vt_out.shape[2]
        for e in range(vt_out.shape[0]):
            _store_values_t(vt_out.at[e], vt[:, e * seq:(e + 1) * seq], n_heads)
        return
    v_rows = vt_out.shape[0] // n_heads
    if v_rows == HEAD_DIM:
        vt_out[...] = vt.astype(BF16)
        return
    ones = jnp.ones((v_rows - HEAD_DIM, vt.shape[1]), BF16)
    for h in range(n_heads):
        vt_out[v_rows * h:v_rows * h + HEAD_DIM, :] = vt[HEAD_DIM * h:HEAD_DIM * (h + 1), :].astype(BF16)
        vt_out[v_rows * h + HEAD_DIM:v_rows * (h + 1), :] = ones


def _mla_expand(ckv_bf, kpe, wk_ref, wvt_ref, k_out, vt_out):
    k = jnp.dot(ckv_bf, wk_ref[...], preferred_element_type=F32)
    lane = lax.broadcasted_iota(jnp.int32, kpe.shape, 1)
    rope_lanes = (lane >= MLA_NOPE) & (lane < MLA_NOPE + MLA_ROPE)
    kpe_at_rope = pltpu.roll(kpe, MLA_NOPE, axis=1)
    for hd in range(N_HEADS):
        cols = slice(MLA_HEAD_PAD * hd, MLA_HEAD_PAD * (hd + 1))
        k_out[:, cols] = jnp.where(rope_lanes, kpe_at_rope, k[:, cols]).astype(BF16)
    vt = lax.dot_general(wvt_ref[...], ckv_bf, NT_DIMS, preferred_element_type=F32)
    _store_values_t(vt_out, vt, N_HEADS)


def _proj_mla_kernel(*refs, rope, ctx):
    refs = list(refs)
    x_ref, mod_ref, g_ref, win_ref, qn_ref, wuq_ref, kvn_ref, wk_ref, wvt_ref = refs[:9]
    pos = 9
    if rope:
        cq_ref, sq_ref, ck_ref, sk_ref = refs[pos:pos + 4]
        pos += 4
    q_out, k_out, vt_out, gate_out = refs[pos:pos + 4]
    pos += 4
    if ctx:
        ckv_out, kpe_out = refs[pos:pos + 2]

    h = _mod_norm(x_ref[...], g_ref[...], mod_ref[...]).astype(BF16)
    z = jnp.dot(h, win_ref[...], preferred_element_type=F32)
    cq = z[:, :MLA_Q_LORA]
    ckv = z[:, MLA_Q_LORA:MLA_Q_LORA + MLA_KV_LORA]
    gate = z[:, 640:640 + D_MODEL]
    kpe = z[:, 640 + D_MODEL:]
    gate_out[...] = _silu(gate).astype(BF16)

    qf = jnp.dot(_rms(cq, qn_ref[...]).astype(BF16), wuq_ref[...], preferred_element_type=F32)
    ckvn = _rms(ckv, kvn_ref[...])
    if ctx:
        ckv_out[...] = ckvn
        kpe_out[...] = kpe[:, :MLA_ROPE]
    if rope:
        kpe = _rope_tile(kpe, ck_ref[...], sk_ref[...])
    _mla_expand(ckvn.astype(BF16), kpe, wk_ref, wvt_ref, k_out, vt_out)
    for hd in range(N_HEADS):
        blk = qf[:, MLA_HEAD_PAD * hd:MLA_HEAD_PAD * (hd + 1)]
        if rope:
            blk = _rope_tile(blk, cq_ref[...], sq_ref[...])
        _store_queries(q_out, MLA_HEAD_PAD * hd, blk * MLA_Q_SCALE)


def _store_head_rows(out_ref, pair, blk):
    tokens = blk.shape[0]
    out_ref[pl.ds(2 * pair, tokens, stride=KV_HEADS), :] = blk
    out_ref[pl.ds(2 * pair + 1, tokens, stride=KV_HEADS), :] = pltpu.roll(blk, HEAD_DIM, axis=1)


def _proj_gqa_kernel(*refs, qk_norm, rope, ctx):
    refs = list(refs)
    x_ref, mod_ref, g_ref, win_ref = refs[:4]
    pos = 4
    if qk_norm:
        qn_ref, kn_ref = refs[pos:pos + 2]
        pos += 2
    if rope:
        cos_ref, sin_ref = refs[pos:pos + 2]
        pos += 2
    q_out, k_out, vt_out, gate_out = refs[pos:pos + 4]
    pos += 4
    if ctx:
        kc_out, vc_out = refs[pos:pos + 2]

    nq = N_HEADS * HEAD_DIM
    nk = KV_HEADS * HEAD_DIM
    h = _mod_norm(x_ref[...], g_ref[...], mod_ref[...]).astype(BF16)
    z = jnp.dot(h, win_ref[...], preferred_element_type=F32)
    tm = z.shape[0]
    lo = lax.broadcasted_iota(jnp.int32, (tm, LANES), 1) < HEAD_DIM
    n_q_tiles = nq // LANES
    for c in range((nq + nk) // LANES):
        blk = z[:, LANES * c:LANES * (c + 1)]
        is_q = c < n_q_tiles
        if qk_norm:
            sq = blk * blk
            s_lo = jnp.sum(jnp.where(lo, sq, 0.0), axis=-1, keepdims=True)
            s_hi = jnp.sum(jnp.where(lo, 0.0, sq), axis=-1, keepdims=True)
            inv = jnp.where(lo, lax.rsqrt(s_lo * (1.0 / HEAD_DIM) + EPS),
                            lax.rsqrt(s_hi * (1.0 / HEAD_DIM) + EPS))
            blk = blk * inv * (qn_ref[...] if is_q else kn_ref[...])
        if ctx and not is_q:
            _store_head_rows(kc_out, c - n_q_tiles, blk)
        if rope:
            blk = _rope_tile(blk, cos_ref[...], sin_ref[...])
        if is_q:
            _store_queries(q_out, LANES * c, blk * GQA_Q_SCALE)
        else:
            k_out[:, LANES * (c - n_q_tiles):LANES * (c - n_q_tiles + 1)] = blk.astype(BF16)
    v = z[:, nq + nk:nq + 2 * nk]
    if ctx:
        for c in range(nk // LANES):
            _store_head_rows(vc_out, c, v[:, LANES * c:LANES * (c + 1)])
    _store_values_t(vt_out, v.T, KV_HEADS)
    gate_out[...] = _silu(z[:, nq + 2 * nk:]).astype(BF16)


def _const_spec(shape):
    return pl.BlockSpec(shape, lambda i: (0,) * len(shape))


def _project(kind, x, mods_l, norm_g, weights, tables, *, batch, seq, ctx):
    n_tok = batch * seq
    tm = PROJ_TILE
    tiles_per_seq = max(seq // tm, 1)
    seqs_per_tile = max(tm // seq, 1)
    assert tm == seqs_per_tile * seq // tiles_per_seq and (ctx or seqs_per_tile == 1)
    v_rows = V_ROWS if ctx else HEAD_DIM
    rope = not ctx
    mod_row = (lambda i: (4, 0, 0)) if ctx else (lambda i: (i // tiles_per_seq, 0, 0))
    in_specs = [
        pl.BlockSpec((tm, D_MODEL), lambda i: (i, 0)),
        pl.BlockSpec((None, 1, 3 * D_MODEL), mod_row),
        _const_spec((1, D_MODEL)),
    ]
    args = [x, mods_l, norm_g.reshape(1, D_MODEL)]
    for w in weights:
        in_specs.append(_const_spec(w.shape))
        args.append(w)
    if rope:
        for t in tables:
            in_specs.append(pl.BlockSpec((tm, LANES), lambda i: (i % tiles_per_seq, 0)))
            args.append(t)
    if kind == 0:
        qw, kw, vw = N_HEADS * MLA_HEAD_PAD, N_HEADS * MLA_HEAD_PAD, N_HEADS * v_rows
        body = functools.partial(_proj_mla_kernel, rope=rope, ctx=ctx)
        cache_shapes = [(n_tok, MLA_KV_LORA), (n_tok, MLA_ROPE)]
    else:
        qw, kw, vw = N_HEADS * HEAD_DIM, KV_HEADS * HEAD_DIM, KV_HEADS * v_rows
        body = functools.partial(_proj_gqa_kernel, qk_norm=(kind == 1), rope=rope, ctx=ctx)
        cache_shapes = [(n_tok, kw), (n_tok, kw)]
    def feature_major_spec(w):
        if seqs_per_tile > 1:
            return pl.BlockSpec((seqs_per_tile, w, seq), lambda i: (i, 0, 0))
        return pl.BlockSpec((None, w, tm), lambda i: (i // tiles_per_seq, 0, i % tiles_per_seq))

    out_shape = [
        jax.ShapeDtypeStruct((batch, qw, seq) if ctx else (n_tok, qw), BF16),
        jax.ShapeDtypeStruct((n_tok, kw), BF16),
        jax.ShapeDtypeStruct((batch, vw, seq), BF16),
        jax.ShapeDtypeStruct((n_tok, D_MODEL), BF16),
    ]
    out_specs = [
        feature_major_spec(qw) if ctx else pl.BlockSpec((tm, qw), lambda i: (i, 0)),
        pl.BlockSpec((tm, kw), lambda i: (i, 0)),
        feature_major_spec(vw),
        pl.BlockSpec((tm, D_MODEL), lambda i: (i, 0)),
    ]
    if ctx and kind != 0:
        for _ in cache_shapes:
            out_shape.append(jax.ShapeDtypeStruct((n_tok * KV_HEADS, LANES), F32))
            out_specs.append(pl.BlockSpec((tm * KV_HEADS, LANES), lambda i: (i, 0)))
    elif ctx:
        for s in cache_shapes:
            out_shape.append(jax.ShapeDtypeStruct(s, F32))
            out_specs.append(pl.BlockSpec((tm, s[1]), lambda i: (i, 0)))
    return pl.pallas_call(
        body,
        grid=(n_tok // tm,),
        in_specs=in_specs,
        out_specs=out_specs,
        out_shape=out_shape,
        compiler_params=pltpu.CompilerParams(
            dimension_semantics=("arbitrary",), vmem_limit_bytes=VMEM_LIMIT),
        name=f"proj_k{kind}_{'ctx' if ctx else 'lat'}",
    )(*args)


def _mla_cache_kernel(ckv_ref, kpe_ref, wk_ref, wvt_ref, k_out, vt_out):
    _mla_expand(ckv_ref[...].astype(BF16), kpe_ref[...], wk_ref, wvt_ref, k_out, vt_out)


def _mla_cache(ckv, kpe_pad, wk, wvt):
    b, n, _ = ckv.shape
    return pl.pallas_call(
        _mla_cache_kernel,
        grid=(b,),
        in_specs=[
            pl.BlockSpec((None, n, MLA_KV_LORA), lambda i: (i, 0, 0)),
            pl.BlockSpec((None, n, LANES), lambda i: (i, 0, 0)),
            _const_spec(wk.shape), _const_spec(wvt.shape),
        ],
        out_specs=[
            pl.BlockSpec((None, n, N_HEADS * MLA_HEAD_PAD), lambda i: (i, 0, 0)),
            pl.BlockSpec((None, N_HEADS * HEAD_DIM, n), lambda i: (i, 0, 0)),
        ],
        out_shape=[
            jax.ShapeDtypeStruct((b, n, N_HEADS * MLA_HEAD_PAD), BF16),
            jax.ShapeDtypeStruct((b, N_HEADS * HEAD_DIM, n), BF16),
        ],
        compiler_params=pltpu.CompilerParams(
            dimension_semantics=("arbitrary",), vmem_limit_bytes=VMEM_LIMIT),
        name="mla_cache",
    )(ckv, kpe_pad, wk, wvt)


def _gqa_cache_kernel(k_ref, v_ref, k_out, vt_out):
    k_out[...] = k_ref[...].astype(BF16)
    _store_values_t(vt_out, v_ref[...].T, KV_HEADS)


def _gqa_cache(k, v):
    b, n, w = k.shape
    return pl.pallas_call(
        _gqa_cache_kernel,
        grid=(b,),
        in_specs=[pl.BlockSpec((None, n, w), lambda i: (i, 0, 0))] * 2,
        out_specs=[
            pl.BlockSpec((None, n, w), lambda i: (i, 0, 0)),
            pl.BlockSpec((None, KV_HEADS * HEAD_DIM, n), lambda i: (i, 0, 0)),
        ],
        out_shape=[
            jax.ShapeDtypeStruct((b, n, w), BF16),
            jax.ShapeDtypeStruct((b, KV_HEADS * HEAD_DIM, n), BF16),
        ],
        compiler_params=pltpu.CompilerParams(dimension_semantics=("arbitrary",)),
        name="gqa_cache",
    )(k, v)


def _attn_kernel(*refs, head_w, group, has_cache, has_sink, band, final, lag_b, lag_c, interleave,
                 subs, own_keys, seq):
    refs = list(refs)
    q_ref = refs[0]
    n_own = subs + 2 if band else 1
    kn_refs = refs[1:1 + n_own]
    vtn_refs = refs[1 + n_own:1 + 2 * n_own]
    kn_ref, vtn_ref = kn_refs[0], vtn_refs[0]
    pos = 1 + 2 * n_own
    if has_cache:
        kc_ref, vtc_ref = refs[pos:pos + 2]
        pos += 2
    gate_ref, x_ref, mod_ref, wo_ref = refs[pos:pos + 4]
    pos += 4
    if has_sink:
        sink_ref = refs[pos]
        pos += 1
    if final:
        fg_ref = refs[pos]
        pos += 1
    o_ref, ot_scr, s_scr, p_scr, g_scr, y_scr = refs[pos:pos + 6]
    if band:
        bias_scr = refs[pos + 6]

    tq = TOKEN_TILE
    ch = KEY_CHUNK
    slots = s_scr.shape[0]
    segs = []
    if has_cache:
        segs.append((kc_ref.shape[0],
                     lambda e, r, n, cols: kc_ref[r:r + n, cols],
                     lambda e, rows: vtc_ref[rows, :]))
    band_of = {}
    if own_keys:
        segs.append((seq,
                     lambda e, r, n, cols: kn_ref[e * seq + r:e * seq + r + n, cols],
                     lambda e, rows: vtn_ref[e, rows, :]))
    elif band:
        for a in range(3):
            band_of[len(segs)] = a
            segs.append((tq,
                         lambda e, r, n, cols, a=a: kn_refs[e + a][r:r + n, cols],
                         lambda e, rows, a=a: vtn_refs[e + a][rows, :]))
    else:
        segs.append((seq,
                     lambda e, r, n, cols: kn_ref[r:r + n, cols],
                     lambda e, rows: vtn_ref[rows, :]))
    seg_rows = []
    chunks = []
    off = 0
    for si, (n_seg, _, _) in enumerate(segs):
        seg_rows.append((off, n_seg))
        chunks += [(si, r, off + r) for r in range(0, n_seg, ch)]
        off += n_seg
    if interleave:
        score_chunks = [c + (ch,) for c in chunks]
    else:
        score_chunks = [(si, 0, r0, n) for si, (r0, n) in enumerate(seg_rows)]
    if band:
        row = lax.broadcasted_iota(jnp.int32, (tq, tq), 0)
        col = lax.broadcasted_iota(jnp.int32, (tq, tq), 1)
        for e in range(subs):
            tile = pl.program_id(1) * subs + e
            for a in range(3):
                key_block = tile - 1 + a
                ok = ((jnp.abs((a - 1) * tq + row - col) <= WINDOW)
                      & (key_block >= 0) & (key_block < seq // tq))
                bias_scr[e, a] = jnp.where(ok, 0.0, NEG_INF)

    assert lag_b >= 1 and lag_c >= 1 and slots > max(lag_b, lag_c)
    row_max = {}
    den_part = {}
    v_rows = vtn_ref.shape[-2] * group // N_HEADS
    den_on_mxu = v_rows > HEAD_DIM

    def scores_and_max(i):
        e, hd = divmod(i, N_HEADS)
        kh = hd // group
        q_cols = slice(head_w * hd, head_w * (hd + 1))
        qh = q_ref[e, q_cols, :] if own_keys else q_ref[e * tq:(e + 1) * tq, q_cols]
        mx = None
        for si, r, g, n in score_chunks:
            kk = segs[si][1](e, r, n, slice(head_w * kh, head_w * (kh + 1)))
            if own_keys:
                s = jnp.dot(kk, qh, preferred_element_type=F32)
            else:
                s = lax.dot_general(kk, qh, NT_DIMS, preferred_element_type=F32)
            if si in band_of:
                s = s + bias_scr[e, band_of[si], r:r + n, :]
            s_scr[i % slots, g:g + n, :] = s
            part = jnp.max(s.reshape(n // 8, 8, tq), axis=0)
            mx = part if mx is None else jnp.maximum(mx, part)
            yield
        m = jnp.max(mx, axis=0, keepdims=True)
        if has_sink:
            m = jnp.maximum(m, sink_ref[hd] * LOG2E)
        row_max[i] = m

    def exponentials(i):
        hd = i % N_HEADS
        m = row_max.pop(i)
        mb = jnp.broadcast_to(m, (8, tq))
        tot = None
        for _, _, g in chunks:
            p = jnp.exp2(s_scr[i % slots, g:g + ch, :].reshape(ch // 8, 8, tq) - mb)
            if not den_on_mxu:
                part = jnp.sum(p, axis=0)
                tot = part if tot is None else tot + part
            p_scr[i % slots, g:g + ch, :] = p.reshape(ch, tq).astype(BF16)
            yield
        den = None if den_on_mxu else jnp.sum(tot, axis=0, keepdims=True)
        if has_sink:
            sink = jnp.exp2(sink_ref[hd] * LOG2E - m)
            den = sink if den is None else den + sink
        den_part[i] = den

    def weighted_values(i):
        e, hd = divmod(i, N_HEADS)
        kh = hd // group
        acc = None
        for si, (_, _, values_t) in enumerate(segs):
            r0, n = seg_rows[si]
            vt = values_t(e, slice(v_rows * kh, v_rows * (kh + 1)))
            part = jnp.dot(vt, p_scr[i % slots, r0:r0 + n, :], preferred_element_type=F32)
            acc = part if acc is None else acc + part
            yield
        den = den_part.pop(i)
        if den_on_mxu:
            mxu_sum = acc[HEAD_DIM:HEAD_DIM + 1, :]
            den = mxu_sum if den is None else den + mxu_sum
        ot_scr[e, HEAD_DIM * hd:HEAD_DIM * (hd + 1), :] = acc[:HEAD_DIM, :] / den

    def finish(e):
        rows = slice(e * tq, (e + 1) * tq)
        for c0 in range(0, D_MODEL, OUT_BLOCK):
            cols = slice(c0, c0 + OUT_BLOCK)
            o = ot_scr[e, cols, :].T
            g_scr[:, cols] = (o * gate_ref[rows, cols].astype(F32)).astype(BF16)
            yield
        for c0 in range(0, D_MODEL, OUT_BLOCK):
            cols = slice(c0, c0 + OUT_BLOCK)
            y = jnp.dot(g_scr[...], wo_ref[:, cols], preferred_element_type=F32)
            xn = x_ref[rows, cols] + mod_ref[:, 2 * D_MODEL + c0:2 * D_MODEL + c0 + OUT_BLOCK] * y
            if final:
                y_scr[:, cols] = xn
            else:
                o_ref[rows, cols] = xn
            yield
        if final:
            o_ref[rows, :] = _rms(y_scr[...], fg_ref[...])

    n_items = subs * N_HEADS
    for step in range(n_items + lag_b + lag_c + 1):
        live = []
        if step < n_items:
            live.append(scores_and_max(step))
        if 0 <= step - lag_b < n_items:
            live.append(exponentials(step - lag_b))
        done = step - lag_b - lag_c
        if 0 <= done < n_items:
            live.append(weighted_values(done))
        if done >= N_HEADS and done % N_HEADS == 0:
            live.append(finish(done // N_HEADS - 1))
        if not interleave:
            for g in live:
                for _ in g:
                    pass
            live = []
        while live:
            live = [g for g in live if next(g, True) is None]


def _attend(kind, q, k_new, vt_new, cache, gate, x, mods_l, w_out, sink, final_g, *, batch, seq, ctx,
            lags):
    lag_b, lag_c, interleave, subs = lags
    tq = TOKEN_TILE
    n_tok = batch * seq
    band = kind == 2 and not ctx
    head_w = MLA_HEAD_PAD if kind == 0 else HEAD_DIM
    kw = k_new.shape[1]
    vw = vt_new.shape[1]
    own_keys = cache is None
    assert seq == tq if own_keys else seq % (subs * tq) == 0
    grid = (batch // subs, 1) if own_keys else (batch, seq // (subs * tq))
    tok_spec = lambda w: pl.BlockSpec((subs * tq, w), lambda b, t: (b * grid[1] + t, 0))
    qw = q.shape[1]
    if own_keys:
        in_specs = [
            pl.BlockSpec((subs, qw, seq), lambda b, t: (b, 0, 0)),
            pl.BlockSpec((subs * seq, kw), lambda b, t: (b, 0)),
            pl.BlockSpec((subs, vw, seq), lambda b, t: (b, 0, 0)),
        ]
    elif band:
        n_blk = seq // tq
        block = lambda j: (lambda b, t: jnp.clip(subs * t - 1 + j, 0, n_blk - 1))
        in_specs = [tok_spec(qw)]
        in_specs += [pl.BlockSpec((tq, kw), lambda b, t, f=block(j): (b * n_blk + f(b, t), 0))
                     for j in range(subs + 2)]
        in_specs += [pl.BlockSpec((None, vw, tq), lambda b, t, f=block(j): (b, 0, f(b, t)))
                     for j in range(subs + 2)]
    else:
        in_specs = [
            tok_spec(qw),
            pl.BlockSpec((seq, kw), lambda b, t: (b, 0)),
            pl.BlockSpec((None, vw, seq), lambda b, t: (b, 0, 0)),
        ]
    n_own = subs + 2 if band else 1
    args = [q] + [k_new] * n_own + [vt_new] * n_own
    n_keys = 3 * tq if band else seq
    if cache is not None:
        kc, vtc = cache
        n_c = kc.shape[1]
        n_keys += n_c
        in_specs += [
            pl.BlockSpec((None, n_c, kw), lambda b, t: (b, 0, 0)),
            pl.BlockSpec((None, vw, n_c), lambda b, t: (b, 0, 0)),
        ]
        args += [kc, vtc]
    mod_row = (lambda b, t: (4, 0, 0)) if ctx else (lambda b, t: (b, 0, 0))
    in_specs += [
        tok_spec(D_MODEL),
        tok_spec(D_MODEL),
        pl.BlockSpec((None, 1, 3 * D_MODEL), mod_row),
        pl.BlockSpec((D_MODEL, D_MODEL), lambda b, t: (0, 0)),
    ]
    args += [gate, x, mods_l, w_out]
    if sink is not None:
        in_specs.append(pl.BlockSpec(memory_space=pltpu.SMEM))
        args.append(sink)
    if final_g is not None:
        in_specs.append(pl.BlockSpec((1, D_MODEL), lambda b, t: (0, 0)))
        args.append(final_g.reshape(1, D_MODEL))
    body = functools.partial(
        _attn_kernel,
        head_w=head_w,
        group=1 if kind == 0 else GROUPS,
        has_cache=cache is not None,
        has_sink=sink is not None,
        band=band,
        final=final_g is not None,
        lag_b=lag_b,
        lag_c=lag_c,
        interleave=interleave,
        subs=subs,
        own_keys=own_keys,
        seq=seq,
    )
    slots = max(lag_b, lag_c) + 1
    scratch = [pltpu.VMEM((subs, N_HEADS * HEAD_DIM, tq), F32),
               pltpu.VMEM((slots, n_keys, tq), F32),
               pltpu.VMEM((slots, n_keys, tq), BF16),
               pltpu.VMEM((tq, D_MODEL), BF16),
               pltpu.VMEM((tq, D_MODEL), F32)]
    if band:
        scratch.append(pltpu.VMEM((subs, 3, tq, tq), F32))
    return pl.pallas_call(
        body,
        grid=grid,
        in_specs=in_specs,
        out_specs=tok_spec(D_MODEL),
        out_shape=jax.ShapeDtypeStruct((n_tok, D_MODEL), F32),
        scratch_shapes=scratch,
        compiler_params=pltpu.CompilerParams(
            dimension_semantics=("arbitrary", "arbitrary"), vmem_limit_bytes=VMEM_LIMIT),
        name=f"attn_k{kind}_{'ctx' if ctx else 'lat'}",
    )(*args)


def _rope_lane_tables(seq, rot_dim):
    rows = seq // GRID_W
    row = jnp.repeat(jnp.arange(rows), GRID_W).astype(F32)
    col = jnp.tile(jnp.arange(GRID_W), rows).astype(F32)
    nf = rot_dim // 4
    freqs = ROPE_THETA ** (-jnp.arange(nf, dtype=F32) / nf)
    ang = jnp.concatenate([row[:, None] * freqs, col[:, None] * freqs], axis=-1)
    cos, sin = jnp.cos(ang), jnp.sin(ang)
    cos_l = jnp.repeat(cos, 2, axis=-1)
    sin_l = jnp.stack([-sin, sin], axis=-1).reshape(seq, rot_dim)
    return cos_l, sin_l


def _embed_lanes(table, start, fill):
    seq, w = table.shape
    return jnp.concatenate(
        [jnp.full((seq, start), fill, F32), table, jnp.full((seq, LANES - start - w), fill, F32)], axis=1)


def _mla_weights(w_in, q_norm, w_uq, kv_norm, w_ukv):
    c0 = MLA_Q_LORA + MLA_KV_LORA
    win = jnp.concatenate(
        [w_in[:, :c0], w_in[:, c0 + MLA_ROPE:], w_in[:, c0:c0 + MLA_ROPE],
         jnp.zeros((D_MODEL, MLA_IN_PAD - w_in.shape[1]), F32)], axis=1).astype(BF16)
    hq = MLA_NOPE + MLA_ROPE
    wuq = jnp.pad(w_uq.reshape(MLA_Q_LORA, N_HEADS, hq),
                  ((0, 0), (0, 0), (0, MLA_HEAD_PAD - hq))).reshape(MLA_Q_LORA, -1).astype(BF16)
    wkv = w_ukv.reshape(MLA_KV_LORA, N_HEADS, MLA_NOPE + MLA_V)
    wk = jnp.pad(wkv[:, :, :MLA_NOPE],
                 ((0, 0), (0, 0), (0, MLA_HEAD_PAD - MLA_NOPE))).reshape(MLA_KV_LORA, -1).astype(BF16)
    wvt = wkv[:, :, MLA_NOPE:].reshape(MLA_KV_LORA, -1).T.astype(BF16)
    return [win, q_norm.reshape(1, -1), wuq, kv_norm.reshape(1, -1), wk, wvt]


def kernel(x_prompt, x_sample, cache_mla_ckv, cache_mla_kpe, cache_gqa_k, cache_gqa_v,
           cache_swa_k, cache_swa_v, c, c_ctx, norm_g, w_ada, b_ada, w_out,
           mla_w_in, mla_q_norm, mla_w_uq, mla_kv_norm, mla_w_ukv,
           gqa_w_in, gqa_q_norm, gqa_k_norm, swa_w_in, swa_sink, final_norm_g):
    bc, sc, _ = x_prompt.shape
    bl, sl, _ = x_sample.shape
    n_past = cache_mla_ckv.shape[2]
    kvw = KV_HEADS * HEAD_DIM

    cc = jnp.concatenate([c, c_ctx[None, :], jnp.zeros((8 - bl - 1, D_MODEL), F32)], axis=0)
    mods = _ada_mods(cc, w_ada, b_ada).reshape(DEPTH, 8, 1, 3 * D_MODEL)

    cos_h, sin_h = _rope_lane_tables(sl, HEAD_DIM)
    gqa_tables = [jnp.tile(cos_h, (1, 2)), jnp.tile(sin_h, (1, 2))]
    cos_r, sin_r = _rope_lane_tables(sl, MLA_ROPE)
    mla_tables = [_embed_lanes(cos_r, MLA_NOPE, 1.0), _embed_lanes(sin_r, MLA_NOPE, 0.0),
                  _embed_lanes(cos_r, 0, 1.0), _embed_lanes(sin_r, 0, 0.0)]

    w_out_bf = w_out.astype(BF16)

    xc = x_prompt.reshape(bc * sc, D_MODEL)
    xl = x_sample.reshape(bl * sl, D_MODEL)
    new_caches = {0: ([], []), 1: ([], []), 2: ([], [])}
    for i in range(DEPTH):
        kind, j = i % 3, i // 3
        mods_l = mods[i]
        wo = w_out_bf[i]
        final_g = final_norm_g if i == DEPTH - 1 else None
        sink = None
        if kind == 0:
            weights = _mla_weights(mla_w_in[j], mla_q_norm[j], mla_w_uq[j], mla_kv_norm[j], mla_w_ukv[j])
            tables = mla_tables
            kpe_pad = jnp.pad(cache_mla_kpe[:, j], ((0, 0), (0, 0), (0, LANES - MLA_ROPE)))
            cache = _mla_cache(cache_mla_ckv[:, j], kpe_pad, weights[4], weights[5])
        else:
            tables = gqa_tables
            if kind == 1:
                weights = [gqa_w_in[j].astype(BF16),
                           jnp.tile(gqa_q_norm[j], 2).reshape(1, LANES),
                           jnp.tile(gqa_k_norm[j], 2).reshape(1, LANES)]
                cache = _gqa_cache(cache_gqa_k[:, j].reshape(bl, n_past, kvw),
                                   cache_gqa_v[:, j].reshape(bl, n_past, kvw))
            else:
                weights = [swa_w_in[j].astype(BF16)]
                sink = swa_sink[j]
                cache = _gqa_cache(cache_swa_k[:, j].reshape(bl, n_past, kvw),
                                   cache_swa_v[:, j].reshape(bl, n_past, kvw))

        q, k, vt, gate, c_a, c_b = _project(kind, xc, mods_l, norm_g[i], weights, tables,
                                            batch=bc, seq=sc, ctx=True)
        new_caches[kind][0].append(c_a)
        new_caches[kind][1].append(c_b)
        xc = _attend(kind, q, k, vt, None, gate, xc, mods_l, wo, sink, final_g,
                     batch=bc, seq=sc, ctx=True, lags=CTX_STAGE_LAGS)

        q, k, vt, gate = _project(kind, xl, mods_l, norm_g[i], weights, tables,
                                  batch=bl, seq=sl, ctx=False)
        xl = _attend(kind, q, k, vt, cache, gate, xl, mods_l, wo, sink, final_g,
                     batch=bl, seq=sl, ctx=False, lags=LAT_STAGE_LAGS)

    def stack(parts, tail):
        if len(tail) == 2:
            parts = [p[:, :HEAD_DIM] for p in parts]
        return jnp.stack([p.reshape((bc, sc) + tail) for p in parts], axis=1)

    return (xc.reshape(bc, sc, D_MODEL), xl.reshape(bl, sl, D_MODEL),
            stack(new_caches[0][0], (MLA_KV_LORA,)), stack(new_caches[0][1], (MLA_ROPE,)),
            stack(new_caches[1][0], (KV_HEADS, HEAD_DIM)), stack(new_caches[1][1], (KV_HEADS, HEAD_DIM)),
            stack(new_caches[2][0], (KV_HEADS, HEAD_DIM)), stack(new_caches[2][1], (KV_HEADS, HEAD_DIM)))
```

```python
import functools

import jax
import jax.numpy as jnp
from jax import lax
from jax.experimental import pallas as pl
from jax.experimental.pallas import tpu as pltpu

F32 = jnp.float32
BF16 = jnp.bfloat16

D_MODEL = 1024
DEPTH = 4
GRID_W = 64
N_HEADS = 16
HEAD_DIM = 64
KV_HEADS = 4
GROUPS = N_HEADS // KV_HEADS
MLA_Q_LORA = 384
MLA_KV_LORA = 256
MLA_NOPE = 64
MLA_ROPE = 32
MLA_V = 64
WINDOW = 128
ROPE_THETA = 10000.0
EPS = 1e-6
NEG_INF = -1e30

LOG2E = 1.4426950408889634
GQA_Q_SCALE = HEAD_DIM ** -0.5 * LOG2E
MLA_Q_SCALE = (MLA_NOPE + MLA_ROPE) ** -0.5 * LOG2E

LANES = 128
TOKEN_TILE = 256
PROJ_TILE = 512
KEY_CHUNK = 256
OUT_BLOCK = 256
CTX_STAGE_LAGS = (3, 3, True, 4)
LAT_STAGE_LAGS = {0: (2, 2, True, 2), 1: (2, 2, False, 2), 2: (2, 2, False, 2)}
MLA_HEAD_PAD = 128
V_ROWS = HEAD_DIM + 16
MLA_IN_PAD = 1792
VMEM_LIMIT = 56 * 1024 * 1024
NT_DIMS = (((1,), (1,)), ((), ()))


def _silu(v):
    return v * jax.nn.sigmoid(v)


def _rms(v, g):
    return v * lax.rsqrt(jnp.mean(v * v, axis=-1, keepdims=True) + EPS) * g


def _mod_norm(x, g, mod):
    shift = mod[:, :D_MODEL]
    scale = mod[:, D_MODEL:2 * D_MODEL]
    return _rms(x, g) * (1.0 + scale) + shift


def _rope_tile(v, cos, sin_signed):
    lane = lax.broadcasted_iota(jnp.int32, v.shape, 1)
    nxt = pltpu.roll(v, LANES - 1, axis=1)
    prv = pltpu.roll(v, 1, axis=1)
    swapped = jnp.where((lane & 1) == 0, nxt, prv)
    return v * cos + swapped * sin_signed


def _ada_kernel(c_ref, w_ref, b_ref, o_ref):
    s = _silu(c_ref[...]).astype(BF16)
    o_ref[...] = jnp.dot(s, w_ref[...].astype(BF16), preferred_element_type=F32) + b_ref[...]


def _ada_mods(cc, w_ada, b_ada):
    tn = D_MODEL
    return pl.pallas_call(
        _ada_kernel,
        grid=(DEPTH, 3 * D_MODEL // tn),
        in_specs=[
            pl.BlockSpec((8, D_MODEL), lambda i, n: (0, 0)),
            pl.BlockSpec((None, D_MODEL, tn), lambda i, n: (i, 0, n)),
            pl.BlockSpec((None, 1, tn), lambda i, n: (i, 0, n)),
        ],
        out_specs=pl.BlockSpec((None, 8, tn), lambda i, n: (i, 0, n)),
        out_shape=jax.ShapeDtypeStruct((DEPTH, 8, 3 * D_MODEL), F32),
        compiler_params=pltpu.CompilerParams(vmem_limit_bytes=VMEM_LIMIT),
        name="ada_mods",
    )(cc, w_ada, b_ada.reshape(DEPTH, 1, 3 * D_MODEL))


def _store_queries(q_out, col0, blk):
    w = blk.shape[1]
    if len(q_out.shape) == 2:
        q_out[:, col0:col0 + w] = blk.astype(BF16)
        return
    blk_t = blk.T.astype(BF16)
    seq = q_out.shape[2]
    for e in range(q_out.shape[0]):
        q_out[e, col0:col0 + w, :] = blk_t[:, e * seq:(e + 1) * seq]


def _store_values_t(vt_out, vt, n_heads):
    if len(vt_out.shape) == 3:
        seq = vt_out.shape[2]
        for e in range(vt_out.shape[0]):
            _store_values_t(vt_out.at[e], vt[:, e * seq:(e + 1) * seq], n_heads)
        return
    v_rows = vt_out.shape[0] // n_heads
    if v_rows == HEAD_DIM:
        vt_out[...] = vt.astype(BF16)
        return
    ones = jnp.ones((v_rows - HEAD_DIM, vt.shape[1]), BF16)
    for h in range(n_heads):
        vt_out[v_rows * h:v_rows * h + HEAD_DIM, :] = vt[HEAD_DIM * h:HEAD_DIM * (h + 1), :].astype(BF16)
        vt_out[v_rows * h + HEAD_DIM:v_rows * (h + 1), :] = ones


def _mla_expand(ckv_bf, kpe, wk_ref, wvt_ref, k_out, vt_out):
    k = jnp.dot(ckv_bf, wk_ref[...], preferred_element_type=F32)
    lane = lax.broadcasted_iota(jnp.int32, kpe.shape, 1)
    rope_lanes = (lane >= MLA_NOPE) & (lane < MLA_NOPE + MLA_ROPE)
    tail = jnp.where(rope_lanes, pltpu.roll(kpe, MLA_NOPE, axis=1), 0.0)
    for hd in range(N_HEADS):
        pair = k[:, LANES * (hd // 2):LANES * (hd // 2 + 1)]
        if hd % 2:
            pair = pltpu.roll(pair, MLA_NOPE, axis=1)
        k_out[:, MLA_HEAD_PAD * hd:MLA_HEAD_PAD * (hd + 1)] = jnp.where(lane < MLA_NOPE, pair, tail).astype(BF16)
    vt = lax.dot_general(wvt_ref[...], ckv_bf, NT_DIMS, preferred_element_type=F32)
    _store_values_t(vt_out, vt, N_HEADS)


def _proj_mla_kernel(*refs, rope, ctx):
    refs = list(refs)
    x_ref, mod_ref, g_ref, win_ref, qn_ref, wuq_ref, kvn_ref, wk_ref, wvt_ref = refs[:9]
    pos = 9
    if rope:
        cq_ref, sq_ref, ck_ref, sk_ref = refs[pos:pos + 4]
        pos += 4
    q_out, k_out, vt_out, gate_out = refs[pos:pos + 4]
    pos += 4
    if ctx:
        ckv_out, kpe_out = refs[pos:pos + 2]

    h = _mod_norm(x_ref[...], g_ref[...], mod_ref[...]).astype(BF16)
    z = jnp.dot(h, win_ref[...], preferred_element_type=F32)
    cq = z[:, :MLA_Q_LORA]
    ckv = z[:, MLA_Q_LORA:MLA_Q_LORA + MLA_KV_LORA]
    gate = z[:, 640:640 + D_MODEL]
    kpe = z[:, 640 + D_MODEL:]
    gate_out[...] = _silu(gate).astype(BF16)

    qf = jnp.dot(_rms(cq, qn_ref[...]).astype(BF16), wuq_ref[...], preferred_element_type=F32)
    ckvn = _rms(ckv, kvn_ref[...])
    if ctx:
        ckv_out[...] = ckvn
        kpe_out[...] = kpe[:, :MLA_ROPE]
    if rope:
        kpe = _rope_tile(kpe, ck_ref[...], sk_ref[...])
    _mla_expand(ckvn.astype(BF16), kpe, wk_ref, wvt_ref, k_out, vt_out)
    for hd in range(N_HEADS):
        blk = qf[:, MLA_HEAD_PAD * hd:MLA_HEAD_PAD * (hd + 1)]
        if rope:
            blk = _rope_tile(blk, cq_ref[...], sq_ref[...])
        _store_queries(q_out, MLA_HEAD_PAD * hd, blk * MLA_Q_SCALE)


def _store_head_rows(out_ref, pair, blk):
    tokens = blk.shape[0]
    out_ref[pl.ds(2 * pair, tokens, stride=KV_HEADS), :] = blk
    out_ref[pl.ds(2 * pair + 1, tokens, stride=KV_HEADS), :] = pltpu.roll(blk, HEAD_DIM, axis=1)


def _proj_gqa_kernel(*refs, qk_norm, rope, ctx):
    refs = list(refs)
    x_ref, mod_ref, g_ref, win_ref = refs[:4]
    pos = 4
    if qk_norm:
        qn_ref, kn_ref = refs[pos:pos + 2]
        pos += 2
    if rope:
        cos_ref, sin_ref = refs[pos:pos + 2]
        pos += 2
    q_out, k_out, vt_out, gate_out = refs[pos:pos + 4]
    pos += 4
    if ctx:
        kc_out, vc_out = refs[pos:pos + 2]

    nq = N_HEADS * HEAD_DIM
    nk = KV_HEADS * HEAD_DIM
    h = _mod_norm(x_ref[...], g_ref[...], mod_ref[...]).astype(BF16)
    z = jnp.dot(h, win_ref[...], preferred_element_type=F32)
    tm = z.shape[0]
    lo = lax.broadcasted_iota(jnp.int32, (tm, LANES), 1) < HEAD_DIM
    n_q_tiles = nq // LANES
    for c in range((nq + nk) // LANES):
        blk = z[:, LANES * c:LANES * (c + 1)]
        is_q = c < n_q_tiles
        if qk_norm:
            sq = blk * blk
            s_lo = jnp.sum(jnp.where(lo, sq, 0.0), axis=-1, keepdims=True)
            s_hi = jnp.sum(jnp.where(lo, 0.0, sq), axis=-1, keepdims=True)
            inv = jnp.where(lo, lax.rsqrt(s_lo * (1.0 / HEAD_DIM) + EPS),
                            lax.rsqrt(s_hi * (1.0 / HEAD_DIM) + EPS))
            blk = blk * inv * (qn_ref[...] if is_q else kn_ref[...])
        if ctx and not is_q:
            _store_head_rows(kc_out, c - n_q_tiles, blk)
        if rope:
            blk = _rope_tile(blk, cos_ref[...], sin_ref[...])
        if is_q:
            _store_queries(q_out, LANES * c, blk * GQA_Q_SCALE)
        else:
            k_out[:, LANES * (c - n_q_tiles):LANES * (c - n_q_tiles + 1)] = blk.astype(BF16)
    v = z[:, nq + nk:nq + 2 * nk]
    if ctx:
        for c in range(nk // LANES):
            _store_head_rows(vc_out, c, v[:, LANES * c:LANES * (c + 1)])
    _store_values_t(vt_out, v.T, KV_HEADS)
    gate_out[...] = _silu(z[:, nq + 2 * nk:]).astype(BF16)


def _const_spec(shape):
    return pl.BlockSpec(shape, lambda i: (0,) * len(shape))


def _project(kind, x, mods_l, norm_g, weights, tables, *, batch, seq, ctx):
    n_tok = batch * seq
    tm = PROJ_TILE
    tiles_per_seq = max(seq // tm, 1)
    seqs_per_tile = max(tm // seq, 1)
    assert tm == seqs_per_tile * seq // tiles_per_seq and (ctx or seqs_per_tile == 1)
    v_rows = V_ROWS if ctx else HEAD_DIM
    rope = not ctx
    mod_row = (lambda i: (4, 0, 0)) if ctx else (lambda i: (i // tiles_per_seq, 0, 0))
    in_specs = [
        pl.BlockSpec((tm, D_MODEL), lambda i: (i, 0)),
        pl.BlockSpec((None, 1, 3 * D_MODEL), mod_row),
        _const_spec((1, D_MODEL)),
    ]
    args = [x, mods_l, norm_g.reshape(1, D_MODEL)]
    for w in weights:
        in_specs.append(_const_spec(w.shape))
        args.append(w)
    if rope:
        for t in tables:
            in_specs.append(pl.BlockSpec((tm, LANES), lambda i: (i % tiles_per_seq, 0)))
            args.append(t)
    if kind == 0:
        qw, kw, vw = N_HEADS * MLA_HEAD_PAD, N_HEADS * MLA_HEAD_PAD, N_HEADS * v_rows
        body = functools.partial(_proj_mla_kernel, rope=rope, ctx=ctx)
        cache_shapes = [(n_tok, MLA_KV_LORA), (n_tok, MLA_ROPE)]
    else:
        qw, kw, vw = N_HEADS * HEAD_DIM, KV_HEADS * HEAD_DIM, KV_HEADS * v_rows
        body = functools.partial(_proj_gqa_kernel, qk_norm=(kind == 1), rope=rope, ctx=ctx)
        cache_shapes = [(n_tok, kw), (n_tok, kw)]
    def feature_major_spec(w):
        if seqs_per_tile > 1:
            return pl.BlockSpec((seqs_per_tile, w, seq), lambda i: (i, 0, 0))
        return pl.BlockSpec((None, w, tm), lambda i: (i // tiles_per_seq, 0, i % tiles_per_seq))

    out_shape = [
        jax.ShapeDtypeStruct((batch, qw, seq) if ctx else (n_tok, qw), BF16),
        jax.ShapeDtypeStruct((n_tok, kw), BF16),
        jax.ShapeDtypeStruct((batch, vw, seq), BF16),
        jax.ShapeDtypeStruct((n_tok, D_MODEL), BF16),
    ]
    out_specs = [
        feature_major_spec(qw) if ctx else pl.BlockSpec((tm, qw), lambda i: (i, 0)),
        pl.BlockSpec((tm, kw), lambda i: (i, 0)),
        feature_major_spec(vw),
        pl.BlockSpec((tm, D_MODEL), lambda i: (i, 0)),
    ]
    if ctx and kind != 0:
        for _ in cache_shapes:
            out_shape.append(jax.ShapeDtypeStruct((n_tok * KV_HEADS, LANES), F32))
            out_specs.append(pl.BlockSpec((tm * KV_HEADS, LANES), lambda i: (i, 0)))
    elif ctx:
        for s in cache_shapes:
            out_shape.append(jax.ShapeDtypeStruct(s, F32))
            out_specs.append(pl.BlockSpec((tm, s[1]), lambda i: (i, 0)))
    return pl.pallas_call(
        body,
        grid=(n_tok // tm,),
        in_specs=in_specs,
        out_specs=out_specs,
        out_shape=out_shape,
        compiler_params=pltpu.CompilerParams(
            dimension_semantics=("arbitrary",), vmem_limit_bytes=VMEM_LIMIT),
        name=f"proj_k{kind}_{'ctx' if ctx else 'lat'}",
    )(*args)


def _mla_cache_kernel(ckv_ref, kpe_ref, wk_ref, wvt_ref, k_out, vt_out):
    _mla_expand(ckv_ref[...].astype(BF16), kpe_ref[...], wk_ref, wvt_ref, k_out, vt_out)


def _mla_cache(ckv, kpe_pad, wk, wvt):
    b, n, _ = ckv.shape
    return pl.pallas_call(
        _mla_cache_kernel,
        grid=(b,),
        in_specs=[
            pl.BlockSpec((None, n, MLA_KV_LORA), lambda i: (i, 0, 0)),
            pl.BlockSpec((None, n, LANES), lambda i: (i, 0, 0)),
            _const_spec(wk.shape), _const_spec(wvt.shape),
        ],
        out_specs=[
            pl.BlockSpec((None, n, N_HEADS * MLA_HEAD_PAD), lambda i: (i, 0, 0)),
            pl.BlockSpec((None, N_HEADS * HEAD_DIM, n), lambda i: (i, 0, 0)),
        ],
        out_shape=[
            jax.ShapeDtypeStruct((b, n, N_HEADS * MLA_HEAD_PAD), BF16),
            jax.ShapeDtypeStruct((b, N_HEADS * HEAD_DIM, n), BF16),
        ],
        compiler_params=pltpu.CompilerParams(
            dimension_semantics=("arbitrary",), vmem_limit_bytes=VMEM_LIMIT),
        name="mla_cache",
    )(ckv, kpe_pad, wk, wvt)


def _gqa_cache_kernel(k_ref, v_ref, k_out, vt_out):
    k_out[...] = k_ref[...].astype(BF16)
    _store_values_t(vt_out, v_ref[...].T, KV_HEADS)


def _gqa_cache(k, v):
    b, n, w = k.shape
    return pl.pallas_call(
        _gqa_cache_kernel,
        grid=(b,),
        in_specs=[pl.BlockSpec((None, n, w), lambda i: (i, 0, 0))] * 2,
        out_specs=[
            pl.BlockSpec((None, n, w), lambda i: (i, 0, 0)),
            pl.BlockSpec((None, KV_HEADS * HEAD_DIM, n), lambda i: (i, 0, 0)),
        ],
        out_shape=[
            jax.ShapeDtypeStruct((b, n, w), BF16),
            jax.ShapeDtypeStruct((b, KV_HEADS * HEAD_DIM, n), BF16),
        ],
        compiler_params=pltpu.CompilerParams(dimension_semantics=("arbitrary",)),
        name="gqa_cache",
    )(k, v)


def _attn_kernel(*refs, head_w, group, has_cache, has_sink, band, final, lag_b, lag_c, interleave,
                 subs, own_keys, seq):
    refs = list(refs)
    q_ref = refs[0]
    n_own = subs + 2 if band else 1
    kn_refs = refs[1:1 + n_own]
    vtn_refs = refs[1 + n_own:1 + 2 * n_own]
    kn_ref, vtn_ref = kn_refs[0], vtn_refs[0]
    pos = 1 + 2 * n_own
    if has_cache:
        kc_ref, vtc_ref = refs[pos:pos + 2]
        pos += 2
    gate_ref, x_ref, mod_ref, wo_ref = refs[pos:pos + 4]
    pos += 4
    if has_sink:
        sink_ref = refs[pos]
        pos += 1
    if final:
        fg_ref = refs[pos]
        pos += 1
    o_ref, ot_scr, s_scr, p_scr, g_scr, y_scr = refs[pos:pos + 6]
    if band:
        bias_scr = refs[pos + 6]

    tq = TOKEN_TILE
    ch = KEY_CHUNK
    slots = s_scr.shape[0]
    segs = []
    if has_cache:
        segs.append((kc_ref.shape[0],
                     lambda e, r, n, cols: kc_ref[r:r + n, cols],
                     lambda e, rows: vtc_ref[rows, :]))
    band_of = {}
    if own_keys:
        segs.append((seq,
                     lambda e, r, n, cols: kn_ref[e * seq + r:e * seq + r + n, cols],
                     lambda e, rows: vtn_ref[e, rows, :]))
    elif band:
        for a in range(3):
            band_of[len(segs)] = a
            segs.append((tq,
                         lambda e, r, n, cols, a=a: kn_refs[e + a][r:r + n, cols],
                         lambda e, rows, a=a: vtn_refs[e + a][rows, :]))
    else:
        segs.append((seq,
                     lambda e, r, n, cols: kn_ref[r:r + n, cols],
                     lambda e, rows: vtn_ref[rows, :]))
    seg_rows = []
    chunks = []
    off = 0
    for si, (n_seg, _, _) in enumerate(segs):
        seg_rows.append((off, n_seg))
        chunks += [(si, r, off + r) for r in range(0, n_seg, ch)]
        off += n_seg
    if interleave:
        score_chunks = [c + (ch,) for c in chunks]
    else:
        score_chunks = [(si, 0, r0, n) for si, (r0, n) in enumerate(seg_rows)]
    if band:
        row = lax.broadcasted_iota(jnp.int32, (tq, tq), 0)
        col = lax.broadcasted_iota(jnp.int32, (tq, tq), 1)
        for e in range(subs):
            tile = pl.program_id(1) * subs + e
            for a in range(3):
                key_block = tile - 1 + a
                ok = ((jnp.abs((a - 1) * tq + row - col) <= WINDOW)
                      & (key_block >= 0) & (key_block < seq // tq))
                bias_scr[e, a] = jnp.where(ok, 0.0, NEG_INF)

    assert lag_b >= 1 and lag_c >= 1 and slots > max(lag_b, lag_c)
    row_max = {}
    den_part = {}
    v_rows = vtn_ref.shape[-2] * group // N_HEADS
    den_on_mxu = v_rows > HEAD_DIM

    def scores_and_max(i):
        e, hd = divmod(i, N_HEADS)
        kh = hd // group
        q_cols = slice(head_w * hd, head_w * (hd + 1))
        qh = q_ref[e, q_cols, :] if own_keys else q_ref[e * tq:(e + 1) * tq, q_cols]
        mx = None
        for si, r, g, n in score_chunks:
            kk = segs[si][1](e, r, n, slice(head_w * kh, head_w * (kh + 1)))
            if own_keys:
                s = jnp.dot(kk, qh, preferred_element_type=F32)
            else:
                s = lax.dot_general(kk, qh, NT_DIMS, preferred_element_type=F32)
            if si in band_of:
                s = s + bias_scr[e, band_of[si], r:r + n, :]
            s_scr[i % slots, g:g + n, :] = s
            part = jnp.max(s.reshape(n // 8, 8, tq), axis=0)
            mx = part if mx is None else jnp.maximum(mx, part)
            yield
        m = jnp.max(mx, axis=0, keepdims=True)
        if has_sink:
            m = jnp.maximum(m, sink_ref[hd] * LOG2E)
        row_max[i] = m

    def exponentials(i):
        hd = i % N_HEADS
        m = row_max.pop(i)
        mb = jnp.broadcast_to(m, (8, tq))
        tot = None
        for _, _, g in chunks:
            p = jnp.exp2(s_scr[i % slots, g:g + ch, :].reshape(ch // 8, 8, tq) - mb)
            if not den_on_mxu:
                part = jnp.sum(p, axis=0)
                tot = part if tot is None else tot + part
            p_scr[i % slots, g:g + ch, :] = p.reshape(ch, tq).astype(BF16)
            yield
        den = None if den_on_mxu else jnp.sum(tot, axis=0, keepdims=True)
        if has_sink:
            sink = jnp.exp2(sink_ref[hd] * LOG2E - m)
            den = sink if den is None else den + sink
        den_part[i] = den

    def weighted_values(i):
        e, hd = divmod(i, N_HEADS)
        kh = hd // group
        acc = None
        for si, (_, _, values_t) in enumerate(segs):
            r0, n = seg_rows[si]
            vt = values_t(e, slice(v_rows * kh, v_rows * (kh + 1)))
            part = jnp.dot(vt, p_scr[i % slots, r0:r0 + n, :], preferred_element_type=F32)
            acc = part if acc is None else acc + part
            yield
        den = den_part.pop(i)
        if den_on_mxu:
            mxu_sum = acc[HEAD_DIM:HEAD_DIM + 1, :]
            den = mxu_sum if den is None else den + mxu_sum
        ot_scr[e, HEAD_DIM * hd:HEAD_DIM * (hd + 1), :] = acc[:HEAD_DIM, :] / den

    def finish(e):
        rows = slice(e * tq, (e + 1) * tq)
        for c0 in range(0, D_MODEL, OUT_BLOCK):
            cols = slice(c0, c0 + OUT_BLOCK)
            o = ot_scr[e, cols, :].T
            g_scr[:, cols] = (o * gate_ref[rows, cols].astype(F32)).astype(BF16)
            yield
        for c0 in range(0, D_MODEL, OUT_BLOCK):
            cols = slice(c0, c0 + OUT_BLOCK)
            y = jnp.dot(g_scr[...], wo_ref[:, cols], preferred_element_type=F32)
            xn = x_ref[rows, cols] + mod_ref[:, 2 * D_MODEL + c0:2 * D_MODEL + c0 + OUT_BLOCK] * y
            if final:
                y_scr[:, cols] = xn
            else:
                o_ref[rows, cols] = xn
            yield
        if final:
            o_ref[rows, :] = _rms(y_scr[...], fg_ref[...])

    n_items = subs * N_HEADS
    for step in range(n_items + lag_b + lag_c + 1):
        live = []
        if step < n_items:
            live.append(scores_and_max(step))
        if 0 <= step - lag_b < n_items:
            live.append(exponentials(step - lag_b))
        done = step - lag_b - lag_c
        if 0 <= done < n_items:
            live.append(weighted_values(done))
        if done >= N_HEADS and done % N_HEADS == 0:
            live.append(finish(done // N_HEADS - 1))
        if not interleave:
            for g in live:
                for _ in g:
                    pass
            live = []
        while live:
            live = [g for g in live if next(g, True) is None]


def _attend(kind, q, k_new, vt_new, cache, gate, x, mods_l, w_out, sink, final_g, *, batch, seq, ctx,
            lags):
    lag_b, lag_c, interleave, subs = lags
    tq = TOKEN_TILE
    n_tok = batch * seq
    band = kind == 2 and not ctx
    head_w = MLA_HEAD_PAD if kind == 0 else HEAD_DIM
    kw = k_new.shape[1]
    vw = vt_new.shape[1]
    own_keys = cache is None
    assert seq == tq if own_keys else seq % (subs * tq) == 0
    grid = (batch // subs, 1) if own_keys else (batch, seq // (subs * tq))
    tok_spec = lambda w: pl.BlockSpec((subs * tq, w), lambda b, t: (b * grid[1] + t, 0))
    qw = q.shape[1]
    if own_keys:
        in_specs = [
            pl.BlockSpec((subs, qw, seq), lambda b, t: (b, 0, 0)),
            pl.BlockSpec((subs * seq, kw), lambda b, t: (b, 0)),
            pl.BlockSpec((subs, vw, seq), lambda b, t: (b, 0, 0)),
        ]
    elif band:
        n_blk = seq // tq
        block = lambda j: (lambda b, t: jnp.clip(subs * t - 1 + j, 0, n_blk - 1))
        in_specs = [tok_spec(qw)]
        in_specs += [pl.BlockSpec((tq, kw), lambda b, t, f=block(j): (b * n_blk + f(b, t), 0))
                     for j in range(subs + 2)]
        in_specs += [pl.BlockSpec((None, vw, tq), lambda b, t, f=block(j): (b, 0, f(b, t)))
                     for j in range(subs + 2)]
    else:
        in_specs = [
            tok_spec(qw),
            pl.BlockSpec((seq, kw), lambda b, t: (b, 0)),
            pl.BlockSpec((None, vw, seq), lambda b, t: (b, 0, 0)),
        ]
    n_own = subs + 2 if band else 1
    args = [q] + [k_new] * n_own + [vt_new] * n_own
    n_keys = 3 * tq if band else seq
    if cache is not None:
        kc, vtc = cache
        n_c = kc.shape[1]
        n_keys += n_c
        in_specs += [
            pl.BlockSpec((None, n_c, kw), lambda b, t: (b, 0, 0)),
            pl.BlockSpec((None, vw, n_c), lambda b, t: (b, 0, 0)),
        ]
        args += [kc, vtc]
    mod_row = (lambda b, t: (4, 0, 0)) if ctx else (lambda b, t: (b, 0, 0))
    in_specs += [
        tok_spec(D_MODEL),
        tok_spec(D_MODEL),
        pl.BlockSpec((None, 1, 3 * D_MODEL), mod_row),
        pl.BlockSpec((D_MODEL, D_MODEL), lambda b, t: (0, 0)),
    ]
    args += [gate, x, mods_l, w_out]
    if sink is not None:
        in_specs.append(pl.BlockSpec(memory_space=pltpu.SMEM))
        args.append(sink)
    if final_g is not None:
        in_specs.append(pl.BlockSpec((1, D_MODEL), lambda b, t: (0, 0)))
        args.append(final_g.reshape(1, D_MODEL))
    body = functools.partial(
        _attn_kernel,
        head_w=head_w,
        group=1 if kind == 0 else GROUPS,
        has_cache=cache is not None,
        has_sink=sink is not None,
        band=band,
        final=final_g is not None,
        lag_b=lag_b,
        lag_c=lag_c,
        interleave=interleave,
        subs=subs,
        own_keys=own_keys,
        seq=seq,
    )
    slots = max(lag_b, lag_c) + 1
    scratch = [pltpu.VMEM((subs, N_HEADS * HEAD_DIM, tq), F32),
               pltpu.VMEM((slots, n_keys, tq), F32),
               pltpu.VMEM((slots, n_keys, tq), BF16),
               pltpu.VMEM((tq, D_MODEL), BF16),
               pltpu.VMEM((tq, D_MODEL), F32)]
    if band:
        scratch.append(pltpu.VMEM((subs, 3, tq, tq), F32))
    return pl.pallas_call(
        body,
        grid=grid,
        in_specs=in_specs,
        out_specs=tok_spec(D_MODEL),
        out_shape=jax.ShapeDtypeStruct((n_tok, D_MODEL), F32),
        scratch_shapes=scratch,
        compiler_params=pltpu.CompilerParams(
            dimension_semantics=("arbitrary", "arbitrary"), vmem_limit_bytes=VMEM_LIMIT),
        name=f"attn_k{kind}_{'ctx' if ctx else 'lat'}",
    )(*args)


def _rope_lane_tables(seq, rot_dim):
    rows = seq // GRID_W
    row = jnp.repeat(jnp.arange(rows), GRID_W).astype(F32)
    col = jnp.tile(jnp.arange(GRID_W), rows).astype(F32)
    nf = rot_dim // 4
    freqs = ROPE_THETA ** (-jnp.arange(nf, dtype=F32) / nf)
    ang = jnp.concatenate([row[:, None] * freqs, col[:, None] * freqs], axis=-1)
    cos, sin = jnp.cos(ang), jnp.sin(ang)
    cos_l = jnp.repeat(cos, 2, axis=-1)
    sin_l = jnp.stack([-sin, sin], axis=-1).reshape(seq, rot_dim)
    return cos_l, sin_l


def _embed_lanes(table, start, fill):
    seq, w = table.shape
    return jnp.concatenate(
        [jnp.full((seq, start), fill, F32), table, jnp.full((seq, LANES - start - w), fill, F32)], axis=1)


def _mla_weights(w_in, q_norm, w_uq, kv_norm, w_ukv):
    c0 = MLA_Q_LORA + MLA_KV_LORA
    win = jnp.concatenate(
        [w_in[:, :c0], w_in[:, c0 + MLA_ROPE:], w_in[:, c0:c0 + MLA_ROPE],
         jnp.zeros((D_MODEL, MLA_IN_PAD - w_in.shape[1]), F32)], axis=1).astype(BF16)
    hq = MLA_NOPE + MLA_ROPE
    wuq = jnp.pad(w_uq.reshape(MLA_Q_LORA, N_HEADS, hq),
                  ((0, 0), (0, 0), (0, MLA_HEAD_PAD - hq))).reshape(MLA_Q_LORA, -1).astype(BF16)
    wkv = w_ukv.reshape(MLA_KV_LORA, N_HEADS, MLA_NOPE + MLA_V)
    wk = wkv[:, :, :MLA_NOPE].reshape(MLA_KV_LORA, -1).astype(BF16)
    wvt = wkv[:, :, MLA_NOPE:].reshape(MLA_KV_LORA, -1).T.astype(BF16)
    return [win, q_norm.reshape(1, -1), wuq, kv_norm.reshape(1, -1), wk, wvt]


def kernel(x_prompt, x_sample, cache_mla_ckv, cache_mla_kpe, cache_gqa_k, cache_gqa_v,
           cache_swa_k, cache_swa_v, c, c_ctx, norm_g, w_ada, b_ada, w_out,
           mla_w_in, mla_q_norm, mla_w_uq, mla_kv_norm, mla_w_ukv,
           gqa_w_in, gqa_q_norm, gqa_k_norm, swa_w_in, swa_sink, final_norm_g):
    bc, sc, _ = x_prompt.shape
    bl, sl, _ = x_sample.shape
    n_past = cache_mla_ckv.shape[2]
    kvw = KV_HEADS * HEAD_DIM

    cc = jnp.concatenate([c, c_ctx[None, :], jnp.zeros((8 - bl - 1, D_MODEL), F32)], axis=0)
    mods = _ada_mods(cc, w_ada, b_ada).reshape(DEPTH, 8, 1, 3 * D_MODEL)

    cos_h, sin_h = _rope_lane_tables(sl, HEAD_DIM)
    gqa_tables = [jnp.tile(cos_h, (1, 2)), jnp.tile(sin_h, (1, 2))]
    cos_r, sin_r = _rope_lane_tables(sl, MLA_ROPE)
    mla_tables = [_embed_lanes(cos_r, MLA_NOPE, 1.0), _embed_lanes(sin_r, MLA_NOPE, 0.0),
                  _embed_lanes(cos_r, 0, 1.0), _embed_lanes(sin_r, 0, 0.0)]

    w_out_bf = w_out.astype(BF16)

    xc = x_prompt.reshape(bc * sc, D_MODEL)
    xl = x_sample.reshape(bl * sl, D_MODEL)
    new_caches = {0: ([], []), 1: ([], []), 2: ([], [])}
    for i in range(DEPTH):
        kind, j = i % 3, i // 3
        mods_l = mods[i]
        wo = w_out_bf[i]
        final_g = final_norm_g if i == DEPTH - 1 else None
        sink = None
        if kind == 0:
            weights = _mla_weights(mla_w_in[j], mla_q_norm[j], mla_w_uq[j], mla_kv_norm[j], mla_w_ukv[j])
            tables = mla_tables
            kpe_pad = jnp.pad(cache_mla_kpe[:, j], ((0, 0), (0, 0), (0, LANES - MLA_ROPE)))
            cache = _mla_cache(cache_mla_ckv[:, j], kpe_pad, weights[4], weights[5])
        else:
            tables = gqa_tables
            if kind == 1:
                weights = [gqa_w_in[j].astype(BF16),
                           jnp.tile(gqa_q_norm[j], 2).reshape(1, LANES),
                           jnp.tile(gqa_k_norm[j], 2).reshape(1, LANES)]
                cache = _gqa_cache(cache_gqa_k[:, j].reshape(bl, n_past, kvw),
                                   cache_gqa_v[:, j].reshape(bl, n_past, kvw))
            else:
                weights = [swa_w_in[j].astype(BF16)]
                sink = swa_sink[j]
                cache = _gqa_cache(cache_swa_k[:, j].reshape(bl, n_past, kvw),
                                   cache_swa_v[:, j].reshape(bl, n_past, kvw))

        q, k, vt, gate, c_a, c_b = _project(kind, xc, mods_l, norm_g[i], weights, tables,
                                            batch=bc, seq=sc, ctx=True)
        new_caches[kind][0].append(c_a)
        new_caches[kind][1].append(c_b)
        xc = _attend(kind, q, k, vt, None, gate, xc, mods_l, wo, sink, final_g,
                     batch=bc, seq=sc, ctx=True, lags=CTX_STAGE_LAGS)

        q, k, vt, gate = _project(kind, xl, mods_l, norm_g[i], weights, tables,
                                  batch=bl, seq=sl, ctx=False)
        xl = _attend(kind, q, k, vt, cache, gate, xl, mods_l, wo, sink, final_g,
                     batch=bl, seq=sl, ctx=False, lags=LAT_STAGE_LAGS[kind])

    def stack(parts, tail):
        if len(tail) == 2:
            parts = [p[:, :HEAD_DIM] for p in parts]
        return jnp.stack([p.reshape((bc, sc) + tail) for p in parts], axis=1)

    return (xc.reshape(bc, sc, D_MODEL), xl.reshape(bl, sl, D_MODEL),
            stack(new_caches[0][0], (MLA_KV_LORA,)), stack(new_caches[0][1], (MLA_ROPE,)),
            stack(new_caches[1][0], (KV_HEADS, HEAD_DIM)), stack(new_caches[1][1], (KV_HEADS, HEAD_DIM)),
            stack(new_caches[2][0], (KV_HEADS, HEAD_DIM)), stack(new_caches[2][1], (KV_HEADS, HEAD_DIM)))
```

```python
import functools

import jax
import jax.numpy as jnp
from jax import lax
from jax.experimental import pallas as pl
from jax.experimental.pallas import tpu as pltpu

F32 = jnp.float32
BF16 = jnp.bfloat16

D_MODEL = 1024
DEPTH = 4
GRID_W = 64
N_HEADS = 16
HEAD_DIM = 64
KV_HEADS = 4
GROUPS = N_HEADS // KV_HEADS
MLA_Q_LORA = 384
MLA_KV_LORA = 256
MLA_NOPE = 64
MLA_ROPE = 32
MLA_V = 64
WINDOW = 128
ROPE_THETA = 10000.0
EPS = 1e-6
NEG_INF = -1e30

LOG2E = 1.4426950408889634
GQA_Q_SCALE = HEAD_DIM ** -0.5 * LOG2E
MLA_Q_SCALE = (MLA_NOPE + MLA_ROPE) ** -0.5 * LOG2E

LANES = 128
TOKEN_TILE = 256
PROJ_TILE = 512
KEY_CHUNK = 256
OUT_BLOCK = 256
CTX_STAGE_LAGS = (3, 3, True, 4)
CTX_LAYER_LAGS = (3, 3, 2)
LAT_STAGE_LAGS = {0: (2, 2, True, 2), 1: (2, 2, False, 2), 2: (2, 2, False, 2)}
MLA_HEAD_PAD = 128
V_ROWS = HEAD_DIM + 16
MLA_IN_PAD = 1792
VMEM_LIMIT = 56 * 1024 * 1024
NT_DIMS = (((1,), (1,)), ((), ()))


def _silu(v):
    return v * jax.nn.sigmoid(v)


def _rms(v, g):
    return v * lax.rsqrt(jnp.mean(v * v, axis=-1, keepdims=True) + EPS) * g


def _mod_norm(x, g, mod):
    shift = mod[:, :D_MODEL]
    scale = mod[:, D_MODEL:2 * D_MODEL]
    return _rms(x, g) * (1.0 + scale) + shift


def _rope_tile(v, cos, sin_signed):
    lane = lax.broadcasted_iota(jnp.int32, v.shape, 1)
    nxt = pltpu.roll(v, LANES - 1, axis=1)
    prv = pltpu.roll(v, 1, axis=1)
    swapped = jnp.where((lane & 1) == 0, nxt, prv)
    return v * cos + swapped * sin_signed


def _ada_kernel(c_ref, w_ref, b_ref, o_ref):
    s = _silu(c_ref[...]).astype(BF16)
    o_ref[...] = jnp.dot(s, w_ref[...].astype(BF16), preferred_element_type=F32) + b_ref[...]


def _ada_mods(cc, w_ada, b_ada):
    tn = D_MODEL
    return pl.pallas_call(
        _ada_kernel,
        grid=(DEPTH, 3 * D_MODEL // tn),
        in_specs=[
            pl.BlockSpec((8, D_MODEL), lambda i, n: (0, 0)),
            pl.BlockSpec((None, D_MODEL, tn), lambda i, n: (i, 0, n)),
            pl.BlockSpec((None, 1, tn), lambda i, n: (i, 0, n)),
        ],
        out_specs=pl.BlockSpec((None, 8, tn), lambda i, n: (i, 0, n)),
        out_shape=jax.ShapeDtypeStruct((DEPTH, 8, 3 * D_MODEL), F32),
        compiler_params=pltpu.CompilerParams(vmem_limit_bytes=VMEM_LIMIT),
        name="ada_mods",
    )(cc, w_ada, b_ada.reshape(DEPTH, 1, 3 * D_MODEL))


def _store_queries(q_out, col0, blk):
    w = blk.shape[1]
    if len(q_out.shape) == 2:
        q_out[:, col0:col0 + w] = blk.astype(BF16)
        return
    blk_t = blk.T.astype(BF16)
    seq = q_out.shape[2]
    for e in range(q_out.shape[0]):
        q_out[e, col0:col0 + w, :] = blk_t[:, e * seq:(e + 1) * seq]


def _store_values_t(vt_out, vt, n_heads):
    if len(vt_out.shape) == 3:
        seq = vt_out.shape[2]
        for e in range(vt_out.shape[0]):
            _store_values_t(vt_out.at[e], vt[:, e * seq:(e + 1) * seq], n_heads)
        return
    v_rows = vt_out.shape[0] // n_heads
    if v_rows == HEAD_DIM:
        vt_out[...] = vt.astype(BF16)
        return
    ones = jnp.ones((v_rows - HEAD_DIM, vt.shape[1]), BF16)
    for h in range(n_heads):
        vt_out[v_rows * h:v_rows * h + HEAD_DIM, :] = vt[HEAD_DIM * h:HEAD_DIM * (h + 1), :].astype(BF16)
        vt_out[v_rows * h + HEAD_DIM:v_rows * (h + 1), :] = ones


def _mla_expand(ckv_bf, kpe, wk_ref, wvt_ref, k_out, vt_out):
    k = jnp.dot(ckv_bf, wk_ref[...], preferred_element_type=F32)
    lane = lax.broadcasted_iota(jnp.int32, kpe.shape, 1)
    rope_lanes = (lane >= MLA_NOPE) & (lane < MLA_NOPE + MLA_ROPE)
    tail = jnp.where(rope_lanes, pltpu.roll(kpe, MLA_NOPE, axis=1), 0.0)
    for hd in range(N_HEADS):
        pair = k[:, LANES * (hd // 2):LANES * (hd // 2 + 1)]
        if hd % 2:
            pair = pltpu.roll(pair, MLA_NOPE, axis=1)
        k_out[:, MLA_HEAD_PAD * hd:MLA_HEAD_PAD * (hd + 1)] = jnp.where(lane < MLA_NOPE, pair, tail).astype(BF16)
    vt = lax.dot_general(wvt_ref[...], ckv_bf, NT_DIMS, preferred_element_type=F32)
    _store_values_t(vt_out, vt, N_HEADS)


def _proj_mla_kernel(*refs, rope, ctx):
    refs = list(refs)
    x_ref, mod_ref, g_ref, win_ref, qn_ref, wuq_ref, kvn_ref, wk_ref, wvt_ref = refs[:9]
    pos = 9
    if rope:
        cq_ref, sq_ref, ck_ref, sk_ref = refs[pos:pos + 4]
        pos += 4
    q_out, k_out, vt_out, gate_out = refs[pos:pos + 4]
    pos += 4
    if ctx:
        ckv_out, kpe_out = refs[pos:pos + 2]

    h = _mod_norm(x_ref[...], g_ref[...], mod_ref[...]).astype(BF16)
    z = jnp.dot(h, win_ref[...], preferred_element_type=F32)
    cq = z[:, :MLA_Q_LORA]
    ckv = z[:, MLA_Q_LORA:MLA_Q_LORA + MLA_KV_LORA]
    gate = z[:, 640:640 + D_MODEL]
    kpe = z[:, 640 + D_MODEL:]
    gate_out[...] = _silu(gate).astype(BF16)

    qf = jnp.dot(_rms(cq, qn_ref[...]).astype(BF16), wuq_ref[...], preferred_element_type=F32)
    ckvn = _rms(ckv, kvn_ref[...])
    if ctx:
        ckv_out[...] = ckvn
        kpe_out[...] = kpe[:, :MLA_ROPE]
    if rope:
        kpe = _rope_tile(kpe, ck_ref[...], sk_ref[...])
    _mla_expand(ckvn.astype(BF16), kpe, wk_ref, wvt_ref, k_out, vt_out)
    for hd in range(N_HEADS):
        blk = qf[:, MLA_HEAD_PAD * hd:MLA_HEAD_PAD * (hd + 1)]
        if rope:
            blk = _rope_tile(blk, cq_ref[...], sq_ref[...])
        _store_queries(q_out, MLA_HEAD_PAD * hd, blk * MLA_Q_SCALE)


def _store_head_rows(out_ref, pair, blk):
    tokens = blk.shape[0]
    out_ref[pl.ds(2 * pair, tokens, stride=KV_HEADS), :] = blk
    out_ref[pl.ds(2 * pair + 1, tokens, stride=KV_HEADS), :] = pltpu.roll(blk, HEAD_DIM, axis=1)


def _proj_gqa_kernel(*refs, qk_norm, rope, ctx):
    refs = list(refs)
    x_ref, mod_ref, g_ref, win_ref = refs[:4]
    pos = 4
    if qk_norm:
        qn_ref, kn_ref = refs[pos:pos + 2]
        pos += 2
    if rope:
        cos_ref, sin_ref = refs[pos:pos + 2]
        pos += 2
    q_out, k_out, vt_out, gate_out = refs[pos:pos + 4]
    pos += 4
    if ctx:
        kc_out, vc_out = refs[pos:pos + 2]

    nq = N_HEADS * HEAD_DIM
    nk = KV_HEADS * HEAD_DIM
    h = _mod_norm(x_ref[...], g_ref[...], mod_ref[...]).astype(BF16)
    z = jnp.dot(h, win_ref[...], preferred_element_type=F32)
    tm = z.shape[0]
    lo = lax.broadcasted_iota(jnp.int32, (tm, LANES), 1) < HEAD_DIM
    n_q_tiles = nq // LANES
    for c in range((nq + nk) // LANES):
        blk = z[:, LANES * c:LANES * (c + 1)]
        is_q = c < n_q_tiles
        if qk_norm:
            sq = blk * blk
            s_lo = jnp.sum(jnp.where(lo, sq, 0.0), axis=-1, keepdims=True)
            s_hi = jnp.sum(jnp.where(lo, 0.0, sq), axis=-1, keepdims=True)
            inv = jnp.where(lo, lax.rsqrt(s_lo * (1.0 / HEAD_DIM) + EPS),
                            lax.rsqrt(s_hi * (1.0 / HEAD_DIM) + EPS))
            blk = blk * inv * (qn_ref[...] if is_q else kn_ref[...])
        if ctx and not is_q:
            _store_head_rows(kc_out, c - n_q_tiles, blk)
        if rope:
            blk = _rope_tile(blk, cos_ref[...], sin_ref[...])
        if is_q:
            _store_queries(q_out, LANES * c, blk * GQA_Q_SCALE)
        else:
            k_out[:, LANES * (c - n_q_tiles):LANES * (c - n_q_tiles + 1)] = blk.astype(BF16)
    v = z[:, nq + nk:nq + 2 * nk]
    if ctx:
        for c in range(nk // LANES):
            _store_head_rows(vc_out, c, v[:, LANES * c:LANES * (c + 1)])
    _store_values_t(vt_out, v.T, KV_HEADS)
    gate_out[...] = _silu(z[:, nq + 2 * nk:]).astype(BF16)


def _const_spec(shape):
    return pl.BlockSpec(shape, lambda i: (0,) * len(shape))


def _project(kind, x, mods_l, norm_g, weights, tables, *, batch, seq, ctx):
    n_tok = batch * seq
    tm = PROJ_TILE
    tiles_per_seq = max(seq // tm, 1)
    seqs_per_tile = max(tm // seq, 1)
    assert tm == seqs_per_tile * seq // tiles_per_seq and (ctx or seqs_per_tile == 1)
    v_rows = V_ROWS if ctx else HEAD_DIM
    rope = not ctx
    mod_row = (lambda i: (4, 0, 0)) if ctx else (lambda i: (i // tiles_per_seq, 0, 0))
    in_specs = [
        pl.BlockSpec((tm, D_MODEL), lambda i: (i, 0)),
        pl.BlockSpec((None, 1, 3 * D_MODEL), mod_row),
        _const_spec((1, D_MODEL)),
    ]
    args = [x, mods_l, norm_g.reshape(1, D_MODEL)]
    for w in weights:
        in_specs.append(_const_spec(w.shape))
        args.append(w)
    if rope:
        for t in tables:
            in_specs.append(pl.BlockSpec((tm, LANES), lambda i: (i % tiles_per_seq, 0)))
            args.append(t)
    if kind == 0:
        qw, kw, vw = N_HEADS * MLA_HEAD_PAD, N_HEADS * MLA_HEAD_PAD, N_HEADS * v_rows
        body = functools.partial(_proj_mla_kernel, rope=rope, ctx=ctx)
        cache_shapes = [(n_tok, MLA_KV_LORA), (n_tok, MLA_ROPE)]
    else:
        qw, kw, vw = N_HEADS * HEAD_DIM, KV_HEADS * HEAD_DIM, KV_HEADS * v_rows
        body = functools.partial(_proj_gqa_kernel, qk_norm=(kind == 1), rope=rope, ctx=ctx)
        cache_shapes = [(n_tok, kw), (n_tok, kw)]
    def feature_major_spec(w):
        if seqs_per_tile > 1:
            return pl.BlockSpec((seqs_per_tile, w, seq), lambda i: (i, 0, 0))
        return pl.BlockSpec((None, w, tm), lambda i: (i // tiles_per_seq, 0, i % tiles_per_seq))

    out_shape = [
        jax.ShapeDtypeStruct((batch, qw, seq) if ctx else (n_tok, qw), BF16),
        jax.ShapeDtypeStruct((n_tok, kw), BF16),
        jax.ShapeDtypeStruct((batch, vw, seq), BF16),
        jax.ShapeDtypeStruct((n_tok, D_MODEL), BF16),
    ]
    out_specs = [
        feature_major_spec(qw) if ctx else pl.BlockSpec((tm, qw), lambda i: (i, 0)),
        pl.BlockSpec((tm, kw), lambda i: (i, 0)),
        feature_major_spec(vw),
        pl.BlockSpec((tm, D_MODEL), lambda i: (i, 0)),
    ]
    if ctx and kind != 0:
        for _ in cache_shapes:
            out_shape.append(jax.ShapeDtypeStruct((n_tok * KV_HEADS, LANES), F32))
            out_specs.append(pl.BlockSpec((tm * KV_HEADS, LANES), lambda i: (i, 0)))
    elif ctx:
        for s in cache_shapes:
            out_shape.append(jax.ShapeDtypeStruct(s, F32))
            out_specs.append(pl.BlockSpec((tm, s[1]), lambda i: (i, 0)))
    return pl.pallas_call(
        body,
        grid=(n_tok // tm,),
        in_specs=in_specs,
        out_specs=out_specs,
        out_shape=out_shape,
        compiler_params=pltpu.CompilerParams(
            dimension_semantics=("arbitrary",), vmem_limit_bytes=VMEM_LIMIT),
        name=f"proj_k{kind}_{'ctx' if ctx else 'lat'}",
    )(*args)


def _mla_cache_kernel(ckv_ref, kpe_ref, wk_ref, wvt_ref, k_out, vt_out):
    _mla_expand(ckv_ref[...].astype(BF16), kpe_ref[...], wk_ref, wvt_ref, k_out, vt_out)


def _mla_cache(ckv, kpe_pad, wk, wvt):
    b, n, _ = ckv.shape
    return pl.pallas_call(
        _mla_cache_kernel,
        grid=(b,),
        in_specs=[
            pl.BlockSpec((None, n, MLA_KV_LORA), lambda i: (i, 0, 0)),
            pl.BlockSpec((None, n, LANES), lambda i: (i, 0, 0)),
            _const_spec(wk.shape), _const_spec(wvt.shape),
        ],
        out_specs=[
            pl.BlockSpec((None, n, N_HEADS * MLA_HEAD_PAD), lambda i: (i, 0, 0)),
            pl.BlockSpec((None, N_HEADS * HEAD_DIM, n), lambda i: (i, 0, 0)),
        ],
        out_shape=[
            jax.ShapeDtypeStruct((b, n, N_HEADS * MLA_HEAD_PAD), BF16),
            jax.ShapeDtypeStruct((b, N_HEADS * HEAD_DIM, n), BF16),
        ],
        compiler_params=pltpu.CompilerParams(
            dimension_semantics=("arbitrary",), vmem_limit_bytes=VMEM_LIMIT),
        name="mla_cache",
    )(ckv, kpe_pad, wk, wvt)


def _gqa_cache_kernel(k_ref, v_ref, k_out, vt_out):
    k_out[...] = k_ref[...].astype(BF16)
    _store_values_t(vt_out, v_ref[...].T, KV_HEADS)


def _gqa_cache(k, v):
    b, n, w = k.shape
    return pl.pallas_call(
        _gqa_cache_kernel,
        grid=(b,),
        in_specs=[pl.BlockSpec((None, n, w), lambda i: (i, 0, 0))] * 2,
        out_specs=[
            pl.BlockSpec((None, n, w), lambda i: (i, 0, 0)),
            pl.BlockSpec((None, KV_HEADS * HEAD_DIM, n), lambda i: (i, 0, 0)),
        ],
        out_shape=[
            jax.ShapeDtypeStruct((b, n, w), BF16),
            jax.ShapeDtypeStruct((b, KV_HEADS * HEAD_DIM, n), BF16),
        ],
        compiler_params=pltpu.CompilerParams(dimension_semantics=("arbitrary",)),
        name="gqa_cache",
    )(k, v)


def _attn_kernel(*refs, head_w, group, has_cache, has_sink, band, final, lag_b, lag_c, interleave,
                 subs, own_keys, seq):
    refs = list(refs)
    q_ref = refs[0]
    n_own = subs + 2 if band else 1
    kn_refs = refs[1:1 + n_own]
    vtn_refs = refs[1 + n_own:1 + 2 * n_own]
    kn_ref, vtn_ref = kn_refs[0], vtn_refs[0]
    pos = 1 + 2 * n_own
    if has_cache:
        kc_ref, vtc_ref = refs[pos:pos + 2]
        pos += 2
    gate_ref, x_ref, mod_ref, wo_ref = refs[pos:pos + 4]
    pos += 4
    if has_sink:
        sink_ref = refs[pos]
        pos += 1
    if final:
        fg_ref = refs[pos]
        pos += 1
    o_ref, ot_scr, s_scr, p_scr, g_scr, y_scr = refs[pos:pos + 6]
    if band:
        bias_scr = refs[pos + 6]

    tq = TOKEN_TILE
    ch = KEY_CHUNK
    slots = s_scr.shape[0]
    segs = []
    if has_cache:
        segs.append((kc_ref.shape[0],
                     lambda e, r, n, cols: kc_ref[r:r + n, cols],
                     lambda e, rows: vtc_ref[rows, :]))
    band_of = {}
    if own_keys:
        segs.append((seq,
                     lambda e, r, n, cols: kn_ref[e * seq + r:e * seq + r + n, cols],
                     lambda e, rows: vtn_ref[e, rows, :]))
    elif band:
        for a in range(3):
            band_of[len(segs)] = a
            segs.append((tq,
                         lambda e, r, n, cols, a=a: kn_refs[e + a][r:r + n, cols],
                         lambda e, rows, a=a: vtn_refs[e + a][rows, :]))
    else:
        segs.append((seq,
                     lambda e, r, n, cols: kn_ref[r:r + n, cols],
                     lambda e, rows: vtn_ref[rows, :]))
    seg_rows = []
    chunks = []
    off = 0
    for si, (n_seg, _, _) in enumerate(segs):
        seg_rows.append((off, n_seg))
        chunks += [(si, r, off + r) for r in range(0, n_seg, ch)]
        off += n_seg
    if interleave:
        score_chunks = [c + (ch,) for c in chunks]
    else:
        score_chunks = [(si, 0, r0, n) for si, (r0, n) in enumerate(seg_rows)]
    if band:
        row = lax.broadcasted_iota(jnp.int32, (tq, tq), 0)
        col = lax.broadcasted_iota(jnp.int32, (tq, tq), 1)
        for e in range(subs):
            tile = pl.program_id(1) * subs + e
            for a in range(3):
                key_block = tile - 1 + a
                ok = ((jnp.abs((a - 1) * tq + row - col) <= WINDOW)
                      & (key_block >= 0) & (key_block < seq // tq))
                bias_scr[e, a] = jnp.where(ok, 0.0, NEG_INF)

    assert lag_b >= 1 and lag_c >= 1 and slots > max(lag_b, lag_c)
    row_max = {}
    den_part = {}
    v_rows = vtn_ref.shape[-2] * group // N_HEADS
    den_on_mxu = v_rows > HEAD_DIM

    def scores_and_max(i):
        e, hd = divmod(i, N_HEADS)
        kh = hd // group
        q_cols = slice(head_w * hd, head_w * (hd + 1))
        qh = q_ref[e, q_cols, :] if own_keys else q_ref[e * tq:(e + 1) * tq, q_cols]
        mx = None
        for si, r, g, n in score_chunks:
            kk = segs[si][1](e, r, n, slice(head_w * kh, head_w * (kh + 1)))
            if own_keys:
                s = jnp.dot(kk, qh, preferred_element_type=F32)
            else:
                s = lax.dot_general(kk, qh, NT_DIMS, preferred_element_type=F32)
            if si in band_of:
                s = s + bias_scr[e, band_of[si], r:r + n, :]
            s_scr[i % slots, g:g + n, :] = s
            part = jnp.max(s.reshape(n // 8, 8, tq), axis=0)
            mx = part if mx is None else jnp.maximum(mx, part)
            yield
        m = jnp.max(mx, axis=0, keepdims=True)
        if has_sink:
            m = jnp.maximum(m, sink_ref[hd] * LOG2E)
        row_max[i] = m

    def exponentials(i):
        hd = i % N_HEADS
        m = row_max.pop(i)
        mb = jnp.broadcast_to(m, (8, tq))
        tot = None
        for _, _, g in chunks:
            p = jnp.exp2(s_scr[i % slots, g:g + ch, :].reshape(ch // 8, 8, tq) - mb)
            if not den_on_mxu:
                part = jnp.sum(p, axis=0)
                tot = part if tot is None else tot + part
            p_scr[i % slots, g:g + ch, :] = p.reshape(ch, tq).astype(BF16)
            yield
        den = None if den_on_mxu else jnp.sum(tot, axis=0, keepdims=True)
        if has_sink:
            sink = jnp.exp2(sink_ref[hd] * LOG2E - m)
            den = sink if den is None else den + sink
        den_part[i] = den

    def weighted_values(i):
        e, hd = divmod(i, N_HEADS)
        kh = hd // group
        acc = None
        for si, (_, _, values_t) in enumerate(segs):
            r0, n = seg_rows[si]
            vt = values_t(e, slice(v_rows * kh, v_rows * (kh + 1)))
            part = jnp.dot(vt, p_scr[i % slots, r0:r0 + n, :], preferred_element_type=F32)
            acc = part if acc is None else acc + part
            yield
        den = den_part.pop(i)
        if den_on_mxu:
            mxu_sum = acc[HEAD_DIM:HEAD_DIM + 1, :]
            den = mxu_sum if den is None else den + mxu_sum
        ot_scr[e, HEAD_DIM * hd:HEAD_DIM * (hd + 1), :] = acc[:HEAD_DIM, :] / den

    def finish(e):
        rows = slice(e * tq, (e + 1) * tq)
        for c0 in range(0, D_MODEL, OUT_BLOCK):
            cols = slice(c0, c0 + OUT_BLOCK)
            o = ot_scr[e, cols, :].T
            g_scr[:, cols] = (o * gate_ref[rows, cols].astype(F32)).astype(BF16)
            yield
        for c0 in range(0, D_MODEL, OUT_BLOCK):
            cols = slice(c0, c0 + OUT_BLOCK)
            y = jnp.dot(g_scr[...], wo_ref[:, cols], preferred_element_type=F32)
            xn = x_ref[rows, cols] + mod_ref[:, 2 * D_MODEL + c0:2 * D_MODEL + c0 + OUT_BLOCK] * y
            if final:
                y_scr[:, cols] = xn
            else:
                o_ref[rows, cols] = xn
            yield
        if final:
            o_ref[rows, :] = _rms(y_scr[...], fg_ref[...])

    n_items = subs * N_HEADS
    for step in range(n_items + lag_b + lag_c + 1):
        live = []
        if step < n_items:
            live.append(scores_and_max(step))
        if 0 <= step - lag_b < n_items:
            live.append(exponentials(step - lag_b))
        done = step - lag_b - lag_c
        if 0 <= done < n_items:
            live.append(weighted_values(done))
        if done >= N_HEADS and done % N_HEADS == 0:
            live.append(finish(done // N_HEADS - 1))
        if not interleave:
            for g in live:
                for _ in g:
                    pass
            live = []
        while live:
            live = [g for g in live if next(g, True) is None]


def _attend(kind, q, k_new, vt_new, cache, gate, x, mods_l, w_out, sink, final_g, *, batch, seq, ctx,
            lags):
    lag_b, lag_c, interleave, subs = lags
    tq = TOKEN_TILE
    n_tok = batch * seq
    band = kind == 2 and not ctx
    head_w = MLA_HEAD_PAD if kind == 0 else HEAD_DIM
    kw = k_new.shape[1]
    vw = vt_new.shape[1]
    own_keys = cache is None
    assert seq == tq if own_keys else seq % (subs * tq) == 0
    grid = (batch // subs, 1) if own_keys else (batch, seq // (subs * tq))
    tok_spec = lambda w: pl.BlockSpec((subs * tq, w), lambda b, t: (b * grid[1] + t, 0))
    qw = q.shape[1]
    if own_keys:
        in_specs = [
            pl.BlockSpec((subs, qw, seq), lambda b, t: (b, 0, 0)),
            pl.BlockSpec((subs * seq, kw), lambda b, t: (b, 0)),
            pl.BlockSpec((subs, vw, seq), lambda b, t: (b, 0, 0)),
        ]
    elif band:
        n_blk = seq // tq
        block = lambda j: (lambda b, t: jnp.clip(subs * t - 1 + j, 0, n_blk - 1))
        in_specs = [tok_spec(qw)]
        in_specs += [pl.BlockSpec((tq, kw), lambda b, t, f=block(j): (b * n_blk + f(b, t), 0))
                     for j in range(subs + 2)]
        in_specs += [pl.BlockSpec((None, vw, tq), lambda b, t, f=block(j): (b, 0, f(b, t)))
                     for j in range(subs + 2)]
    else:
        in_specs = [
            tok_spec(qw),
            pl.BlockSpec((seq, kw), lambda b, t: (b, 0)),
            pl.BlockSpec((None, vw, seq), lambda b, t: (b, 0, 0)),
        ]
    n_own = subs + 2 if band else 1
    args = [q] + [k_new] * n_own + [vt_new] * n_own
    n_keys = 3 * tq if band else seq
    if cache is not None:
        kc, vtc = cache
        n_c = kc.shape[1]
        n_keys += n_c
        in_specs += [
            pl.BlockSpec((None, n_c, kw), lambda b, t: (b, 0, 0)),
            pl.BlockSpec((None, vw, n_c), lambda b, t: (b, 0, 0)),
        ]
        args += [kc, vtc]
    mod_row = (lambda b, t: (4, 0, 0)) if ctx else (lambda b, t: (b, 0, 0))
    in_specs += [
        tok_spec(D_MODEL),
        tok_spec(D_MODEL),
        pl.BlockSpec((None, 1, 3 * D_MODEL), mod_row),
        pl.BlockSpec((D_MODEL, D_MODEL), lambda b, t: (0, 0)),
    ]
    args += [gate, x, mods_l, w_out]
    if sink is not None:
        in_specs.append(pl.BlockSpec(memory_space=pltpu.SMEM))
        args.append(sink)
    if final_g is not None:
        in_specs.append(pl.BlockSpec((1, D_MODEL), lambda b, t: (0, 0)))
        args.append(final_g.reshape(1, D_MODEL))
    body = functools.partial(
        _attn_kernel,
        head_w=head_w,
        group=1 if kind == 0 else GROUPS,
        has_cache=cache is not None,
        has_sink=sink is not None,
        band=band,
        final=final_g is not None,
        lag_b=lag_b,
        lag_c=lag_c,
        interleave=interleave,
        subs=subs,
        own_keys=own_keys,
        seq=seq,
    )
    slots = max(lag_b, lag_c) + 1
    scratch = [pltpu.VMEM((subs, N_HEADS * HEAD_DIM, tq), F32),
               pltpu.VMEM((slots, n_keys, tq), F32),
               pltpu.VMEM((slots, n_keys, tq), BF16),
               pltpu.VMEM((tq, D_MODEL), BF16),
               pltpu.VMEM((tq, D_MODEL), F32)]
    if band:
        scratch.append(pltpu.VMEM((subs, 3, tq, tq), F32))
    return pl.pallas_call(
        body,
        grid=grid,
        in_specs=in_specs,
        out_specs=tok_spec(D_MODEL),
        out_shape=jax.ShapeDtypeStruct((n_tok, D_MODEL), F32),
        scratch_shapes=scratch,
        compiler_params=pltpu.CompilerParams(
            dimension_semantics=("arbitrary", "arbitrary"), vmem_limit_bytes=VMEM_LIMIT),
        name=f"attn_k{kind}_{'ctx' if ctx else 'lat'}",
    )(*args)


def _ctx_layer_kernel(*refs, kind, n_weights, has_sink, final, lag_b, lag_c, subs, seq):
    refs = list(refs)
    x_ref, mod_ref, g_ref = refs[:3]
    w_refs = refs[3:3 + n_weights]
    pos = 3 + n_weights
    tail_in = refs[pos:pos + 1 + has_sink + final]
    pos += len(tail_in)
    o_ref, ca_ref, cb_ref = refs[pos:pos + 3]
    q_s, k_s, vt_s, gate_s = refs[pos + 3:pos + 7]
    attn_scratch = refs[pos + 7:]
    if kind == 0:
        _proj_mla_kernel(x_ref, mod_ref, g_ref, *w_refs, q_s, k_s, vt_s, gate_s, ca_ref, cb_ref,
                         rope=False, ctx=True)
    else:
        _proj_gqa_kernel(x_ref, mod_ref, g_ref, *w_refs, q_s, k_s, vt_s, gate_s, ca_ref, cb_ref,
                         qk_norm=(kind == 1), rope=False, ctx=True)
    _attn_kernel(q_s, k_s, vt_s, gate_s, x_ref, mod_ref, *tail_in, o_ref, *attn_scratch,
                 head_w=MLA_HEAD_PAD if kind == 0 else HEAD_DIM, group=1 if kind == 0 else GROUPS,
                 has_cache=False, has_sink=has_sink, band=False, final=final, lag_b=lag_b, lag_c=lag_c,
                 interleave=True, subs=subs, own_keys=True, seq=seq)


def _ctx_layer(kind, x, mods_l, norm_g, weights, w_out, sink, final_g, *, batch, seq, lags):
    lag_b, lag_c, subs = lags
    tq = TOKEN_TILE
    assert seq == tq
    n_tok = batch * seq
    tm = subs * seq
    tok_spec = lambda w: pl.BlockSpec((tm, w), lambda i: (i, 0))
    in_specs = [tok_spec(D_MODEL), pl.BlockSpec((None, 1, 3 * D_MODEL), lambda i: (4, 0, 0)),
                _const_spec((1, D_MODEL))]
    args = [x, mods_l, norm_g.reshape(1, D_MODEL)]
    for w in weights:
        in_specs.append(_const_spec(w.shape))
        args.append(w)
    in_specs.append(_const_spec((D_MODEL, D_MODEL)))
    args.append(w_out)
    if sink is not None:
        in_specs.append(pl.BlockSpec(memory_space=pltpu.SMEM))
        args.append(sink)
    if final_g is not None:
        in_specs.append(_const_spec((1, D_MODEL)))
        args.append(final_g.reshape(1, D_MODEL))
    if kind == 0:
        qw, kw, vw = N_HEADS * MLA_HEAD_PAD, N_HEADS * MLA_HEAD_PAD, N_HEADS * V_ROWS
        cache_shapes = [(n_tok, MLA_KV_LORA), (n_tok, MLA_ROPE)]
        cache_blocks = [(tm, MLA_KV_LORA), (tm, MLA_ROPE)]
    else:
        qw, kw, vw = N_HEADS * HEAD_DIM, KV_HEADS * HEAD_DIM, KV_HEADS * V_ROWS
        cache_shapes = [(n_tok * KV_HEADS, LANES)] * 2
        cache_blocks = [(tm * KV_HEADS, LANES)] * 2
    out_shape = [jax.ShapeDtypeStruct((n_tok, D_MODEL), F32)]
    out_specs = [tok_spec(D_MODEL)]
    for s, b in zip(cache_shapes, cache_blocks):
        out_shape.append(jax.ShapeDtypeStruct(s, F32))
        out_specs.append(pl.BlockSpec(b, lambda i: (i, 0)))
    slots = max(lag_b, lag_c) + 1
    scratch = [pltpu.VMEM((subs, qw, seq), BF16),
               pltpu.VMEM((tm, kw), BF16),
               pltpu.VMEM((subs, vw, seq), BF16),
               pltpu.VMEM((tm, D_MODEL), BF16),
               pltpu.VMEM((subs, N_HEADS * HEAD_DIM, tq), F32),
               pltpu.VMEM((slots, seq, tq), F32),
               pltpu.VMEM((slots, seq, tq), BF16),
               pltpu.VMEM((tq, D_MODEL), BF16),
               pltpu.VMEM((tq, D_MODEL), F32)]
    body = functools.partial(
        _ctx_layer_kernel, kind=kind, n_weights=len(weights), has_sink=sink is not None,
        final=final_g is not None, lag_b=lag_b, lag_c=lag_c, subs=subs, seq=seq)
    return pl.pallas_call(
        body,
        grid=(n_tok // tm,),
        in_specs=in_specs,
        out_specs=out_specs,
        out_shape=out_shape,
        scratch_shapes=scratch,
        compiler_params=pltpu.CompilerParams(
            dimension_semantics=("arbitrary",), vmem_limit_bytes=VMEM_LIMIT),
        name=f"layer_k{kind}_ctx",
    )(*args)


def _rope_lane_tables(seq, rot_dim):
    rows = seq // GRID_W
    row = jnp.repeat(jnp.arange(rows), GRID_W).astype(F32)
    col = jnp.tile(jnp.arange(GRID_W), rows).astype(F32)
    nf = rot_dim // 4
    freqs = ROPE_THETA ** (-jnp.arange(nf, dtype=F32) / nf)
    ang = jnp.concatenate([row[:, None] * freqs, col[:, None] * freqs], axis=-1)
    cos, sin = jnp.cos(ang), jnp.sin(ang)
    cos_l = jnp.repeat(cos, 2, axis=-1)
    sin_l = jnp.stack([-sin, sin], axis=-1).reshape(seq, rot_dim)
    return cos_l, sin_l


def _embed_lanes(table, start, fill):
    seq, w = table.shape
    return jnp.concatenate(
        [jnp.full((seq, start), fill, F32), table, jnp.full((seq, LANES - start - w), fill, F32)], axis=1)


def _mla_weights(w_in, q_norm, w_uq, kv_norm, w_ukv):
    c0 = MLA_Q_LORA + MLA_KV_LORA
    win = jnp.concatenate(
        [w_in[:, :c0], w_in[:, c0 + MLA_ROPE:], w_in[:, c0:c0 + MLA_ROPE],
         jnp.zeros((D_MODEL, MLA_IN_PAD - w_in.shape[1]), F32)], axis=1).astype(BF16)
    hq = MLA_NOPE + MLA_ROPE
    wuq = jnp.pad(w_uq.reshape(MLA_Q_LORA, N_HEADS, hq),
                  ((0, 0), (0, 0), (0, MLA_HEAD_PAD - hq))).reshape(MLA_Q_LORA, -1).astype(BF16)
    wkv = w_ukv.reshape(MLA_KV_LORA, N_HEADS, MLA_NOPE + MLA_V)
    wk = wkv[:, :, :MLA_NOPE].reshape(MLA_KV_LORA, -1).astype(BF16)
    wvt = wkv[:, :, MLA_NOPE:].reshape(MLA_KV_LORA, -1).T.astype(BF16)
    return [win, q_norm.reshape(1, -1), wuq, kv_norm.reshape(1, -1), wk, wvt]


def kernel(x_prompt, x_sample, cache_mla_ckv, cache_mla_kpe, cache_gqa_k, cache_gqa_v,
           cache_swa_k, cache_swa_v, c, c_ctx, norm_g, w_ada, b_ada, w_out,
           mla_w_in, mla_q_norm, mla_w_uq, mla_kv_norm, mla_w_ukv,
           gqa_w_in, gqa_q_norm, gqa_k_norm, swa_w_in, swa_sink, final_norm_g):
    bc, sc, _ = x_prompt.shape
    bl, sl, _ = x_sample.shape
    n_past = cache_mla_ckv.shape[2]
    kvw = KV_HEADS * HEAD_DIM

    cc = jnp.concatenate([c, c_ctx[None, :], jnp.zeros((8 - bl - 1, D_MODEL), F32)], axis=0)
    mods = _ada_mods(cc, w_ada, b_ada).reshape(DEPTH, 8, 1, 3 * D_MODEL)

    cos_h, sin_h = _rope_lane_tables(sl, HEAD_DIM)
    gqa_tables = [jnp.tile(cos_h, (1, 2)), jnp.tile(sin_h, (1, 2))]
    cos_r, sin_r = _rope_lane_tables(sl, MLA_ROPE)
    mla_tables = [_embed_lanes(cos_r, MLA_NOPE, 1.0), _embed_lanes(sin_r, MLA_NOPE, 0.0),
                  _embed_lanes(cos_r, 0, 1.0), _embed_lanes(sin_r, 0, 0.0)]

    w_out_bf = w_out.astype(BF16)

    xc = x_prompt.reshape(bc * sc, D_MODEL)
    xl = x_sample.reshape(bl * sl, D_MODEL)
    new_caches = {0: ([], []), 1: ([], []), 2: ([], [])}
    for i in range(DEPTH):
        kind, j = i % 3, i // 3
        mods_l = mods[i]
        wo = w_out_bf[i]
        final_g = final_norm_g if i == DEPTH - 1 else None
        sink = None
        if kind == 0:
            weights = _mla_weights(mla_w_in[j], mla_q_norm[j], mla_w_uq[j], mla_kv_norm[j], mla_w_ukv[j])
            tables = mla_tables
            kpe_pad = jnp.pad(cache_mla_kpe[:, j], ((0, 0), (0, 0), (0, LANES - MLA_ROPE)))
            cache = _mla_cache(cache_mla_ckv[:, j], kpe_pad, weights[4], weights[5])
        else:
            tables = gqa_tables
            if kind == 1:
                weights = [gqa_w_in[j].astype(BF16),
                           jnp.tile(gqa_q_norm[j], 2).reshape(1, LANES),
                           jnp.tile(gqa_k_norm[j], 2).reshape(1, LANES)]
                cache = _gqa_cache(cache_gqa_k[:, j].reshape(bl, n_past, kvw),
                                   cache_gqa_v[:, j].reshape(bl, n_past, kvw))
            else:
                weights = [swa_w_in[j].astype(BF16)]
                sink = swa_sink[j]
                cache = _gqa_cache(cache_swa_k[:, j].reshape(bl, n_past, kvw),
                                   cache_swa_v[:, j].reshape(bl, n_past, kvw))

        xc, c_a, c_b = _ctx_layer(kind, xc, mods_l, norm_g[i], weights, wo, sink, final_g,
                                  batch=bc, seq=sc, lags=CTX_LAYER_LAGS)
        new_caches[kind][0].append(c_a)
        new_caches[kind][1].append(c_b)

        q, k, vt, gate = _project(kind, xl, mods_l, norm_g[i], weights, tables,
                                  batch=bl, seq=sl, ctx=False)
        xl = _attend(kind, q, k, vt, cache, gate, xl, mods_l, wo, sink, final_g,
                     batch=bl, seq=sl, ctx=False, lags=LAT_STAGE_LAGS[kind])

    def stack(parts, tail):
        if len(tail) == 2:
            parts = [p[:, :HEAD_DIM] for p in parts]
        return jnp.stack([p.reshape((bc, sc) + tail) for p in parts], axis=1)

    return (xc.reshape(bc, sc, D_MODEL), xl.reshape(bl, sl, D_MODEL),
            stack(new_caches[0][0], (MLA_KV_LORA,)), stack(new_caches[0][1], (MLA_ROPE,)),
            stack(new_caches[1][0], (KV_HEADS, HEAD_DIM)), stack(new_caches[1][1], (KV_HEADS, HEAD_DIM)),
            stack(new_caches[2][0], (KV_HEADS, HEAD_DIM)), stack(new_caches[2][1], (KV_HEADS, HEAD_DIM)))
```

```python
import functools

import jax
import jax.numpy as jnp
from jax import lax
from jax.experimental import pallas as pl
from jax.experimental.pallas import tpu as pltpu

F32 = jnp.float32
BF16 = jnp.bfloat16

D_MODEL = 1024
DEPTH = 4
GRID_W = 64
N_HEADS = 16
HEAD_DIM = 64
KV_HEADS = 4
GROUPS = N_HEADS // KV_HEADS
MLA_Q_LORA = 384
MLA_KV_LORA = 256
MLA_NOPE = 64
MLA_ROPE = 32
MLA_V = 64
WINDOW = 128
ROPE_THETA = 10000.0
EPS = 1e-6
NEG_INF = -1e30

LOG2E = 1.4426950408889634
GQA_Q_SCALE = HEAD_DIM ** -0.5 * LOG2E
MLA_Q_SCALE = (MLA_NOPE + MLA_ROPE) ** -0.5 * LOG2E

LANES = 128
TOKEN_TILE = 256
PROJ_TILE = 512
KEY_CHUNK = 256
OUT_BLOCK = 256
CTX_LAYER_LAGS = (3, 3, 2)
LAT_STAGE_LAGS = {0: (2, 2, True, 2), 1: (2, 2, False, 2), 2: (2, 2, False, 2)}
MLA_HEAD_PAD = 128
V_ROWS = HEAD_DIM + 16
MLA_IN_PAD = 1792
VMEM_LIMIT = 56 * 1024 * 1024
NT_DIMS = (((1,), (1,)), ((), ()))


def _silu(v):
    return v * jax.nn.sigmoid(v)


def _rms(v, g):
    return v * lax.rsqrt(jnp.mean(v * v, axis=-1, keepdims=True) + EPS) * g


def _mod_norm(x, g, mod):
    shift = mod[:, :D_MODEL]
    scale = mod[:, D_MODEL:2 * D_MODEL]
    return _rms(x, g) * (1.0 + scale) + shift


def _rope_tile(v, cos, sin_signed):
    lane = lax.broadcasted_iota(jnp.int32, v.shape, 1)
    nxt = pltpu.roll(v, LANES - 1, axis=1)
    prv = pltpu.roll(v, 1, axis=1)
    swapped = jnp.where((lane & 1) == 0, nxt, prv)
    return v * cos + swapped * sin_signed


def _ada_kernel(c_ref, w_ref, b_ref, o_ref):
    s = _silu(c_ref[...]).astype(BF16)
    o_ref[...] = jnp.dot(s, w_ref[...].astype(BF16), preferred_element_type=F32) + b_ref[...]


def _ada_mods(cc, w_ada, b_ada):
    tn = D_MODEL
    return pl.pallas_call(
        _ada_kernel,
        grid=(DEPTH, 3 * D_MODEL // tn),
        in_specs=[
            pl.BlockSpec((8, D_MODEL), lambda i, n: (0, 0)),
            pl.BlockSpec((None, D_MODEL, tn), lambda i, n: (i, 0, n)),
            pl.BlockSpec((None, 1, tn), lambda i, n: (i, 0, n)),
        ],
        out_specs=pl.BlockSpec((None, 8, tn), lambda i, n: (i, 0, n)),
        out_shape=jax.ShapeDtypeStruct((DEPTH, 8, 3 * D_MODEL), F32),
        compiler_params=pltpu.CompilerParams(vmem_limit_bytes=VMEM_LIMIT),
        name="ada_mods",
    )(cc, w_ada, b_ada.reshape(DEPTH, 1, 3 * D_MODEL))


def _store_queries(q_out, col0, blk):
    w = blk.shape[1]
    if len(q_out.shape) == 2:
        q_out[:, col0:col0 + w] = blk.astype(BF16)
        return
    blk_t = blk.T.astype(BF16)
    seq = q_out.shape[2]
    for e in range(q_out.shape[0]):
        q_out[e, col0:col0 + w, :] = blk_t[:, e * seq:(e + 1) * seq]


def _store_values_t(vt_out, vt, n_heads):
    if len(vt_out.shape) == 3:
        seq = vt_out.shape[2]
        for e in range(vt_out.shape[0]):
            _store_values_t(vt_out.at[e], vt[:, e * seq:(e + 1) * seq], n_heads)
        return
    v_rows = vt_out.shape[0] // n_heads
    if v_rows == HEAD_DIM:
        vt_out[...] = vt.astype(BF16)
        return
    ones = jnp.ones((v_rows - HEAD_DIM, vt.shape[1]), BF16)
    for h in range(n_heads):
        vt_out[v_rows * h:v_rows * h + HEAD_DIM, :] = vt[HEAD_DIM * h:HEAD_DIM * (h + 1), :].astype(BF16)
        vt_out[v_rows * h + HEAD_DIM:v_rows * (h + 1), :] = ones


def _mla_expand(ckv_bf, kpe, wk_ref, wvt_ref, k_out, vt_out):
    k = jnp.dot(ckv_bf, wk_ref[...], preferred_element_type=F32)
    lane = lax.broadcasted_iota(jnp.int32, kpe.shape, 1)
    rope_lanes = (lane >= MLA_NOPE) & (lane < MLA_NOPE + MLA_ROPE)
    tail = jnp.where(rope_lanes, pltpu.roll(kpe, MLA_NOPE, axis=1), 0.0)
    for hd in range(N_HEADS):
        pair = k[:, LANES * (hd // 2):LANES * (hd // 2 + 1)]
        if hd % 2:
            pair = pltpu.roll(pair, MLA_NOPE, axis=1)
        k_out[:, MLA_HEAD_PAD * hd:MLA_HEAD_PAD * (hd + 1)] = jnp.where(lane < MLA_NOPE, pair, tail).astype(BF16)
    vt = lax.dot_general(wvt_ref[...], ckv_bf, NT_DIMS, preferred_element_type=F32)
    _store_values_t(vt_out, vt, N_HEADS)


def _proj_mla_kernel(*refs, rope, ctx):
    refs = list(refs)
    x_ref, mod_ref, g_ref, win_ref, qn_ref, wuq_ref, kvn_ref, wk_ref, wvt_ref = refs[:9]
    pos = 9
    if rope:
        cq_ref, sq_ref, ck_ref, sk_ref = refs[pos:pos + 4]
        pos += 4
    q_out, k_out, vt_out, gate_out = refs[pos:pos + 4]
    pos += 4
    if ctx:
        ckv_out, kpe_out = refs[pos:pos + 2]

    h = _mod_norm(x_ref[...], g_ref[...], mod_ref[...]).astype(BF16)
    z = jnp.dot(h, win_ref[...], preferred_element_type=F32)
    cq = z[:, :MLA_Q_LORA]
    ckv = z[:, MLA_Q_LORA:MLA_Q_LORA + MLA_KV_LORA]
    gate = z[:, 640:640 + D_MODEL]
    kpe = z[:, 640 + D_MODEL:]
    gate_out[...] = _silu(gate).astype(BF16)

    qf = jnp.dot(_rms(cq, qn_ref[...]).astype(BF16), wuq_ref[...], preferred_element_type=F32)
    ckvn = _rms(ckv, kvn_ref[...])
    if ctx:
        ckv_out[...] = ckvn
        kpe_out[...] = kpe[:, :MLA_ROPE]
    if rope:
        kpe = _rope_tile(kpe, ck_ref[...], sk_ref[...])
    _mla_expand(ckvn.astype(BF16), kpe, wk_ref, wvt_ref, k_out, vt_out)
    for hd in range(N_HEADS):
        blk = qf[:, MLA_HEAD_PAD * hd:MLA_HEAD_PAD * (hd + 1)]
        if rope:
            blk = _rope_tile(blk, cq_ref[...], sq_ref[...])
        _store_queries(q_out, MLA_HEAD_PAD * hd, blk * MLA_Q_SCALE)


def _store_head_rows(out_ref, pair, blk):
    tokens = blk.shape[0]
    out_ref[pl.ds(2 * pair, tokens, stride=KV_HEADS), :] = blk
    out_ref[pl.ds(2 * pair + 1, tokens, stride=KV_HEADS), :] = pltpu.roll(blk, HEAD_DIM, axis=1)


def _proj_gqa_kernel(*refs, qk_norm, rope, ctx):
    refs = list(refs)
    x_ref, mod_ref, g_ref, win_ref = refs[:4]
    pos = 4
    if qk_norm:
        qn_ref, kn_ref = refs[pos:pos + 2]
        pos += 2
    if rope:
        cos_ref, sin_ref = refs[pos:pos + 2]
        pos += 2
    q_out, k_out, vt_out, gate_out = refs[pos:pos + 4]
    pos += 4
    if ctx:
        kc_out, vc_out = refs[pos:pos + 2]

    nq = N_HEADS * HEAD_DIM
    nk = KV_HEADS * HEAD_DIM
    h = _mod_norm(x_ref[...], g_ref[...], mod_ref[...]).astype(BF16)
    z = jnp.dot(h, win_ref[...], preferred_element_type=F32)
    tm = z.shape[0]
    lo = lax.broadcasted_iota(jnp.int32, (tm, LANES), 1) < HEAD_DIM
    n_q_tiles = nq // LANES
    for c in range((nq + nk) // LANES):
        blk = z[:, LANES * c:LANES * (c + 1)]
        is_q = c < n_q_tiles
        if qk_norm:
            sq = blk * blk
            s_lo = jnp.sum(jnp.where(lo, sq, 0.0), axis=-1, keepdims=True)
            s_hi = jnp.sum(jnp.where(lo, 0.0, sq), axis=-1, keepdims=True)
            inv = jnp.where(lo, lax.rsqrt(s_lo * (1.0 / HEAD_DIM) + EPS),
                            lax.rsqrt(s_hi * (1.0 / HEAD_DIM) + EPS))
            blk = blk * inv * (qn_ref[...] if is_q else kn_ref[...])
        if ctx and not is_q:
            _store_head_rows(kc_out, c - n_q_tiles, blk)
        if rope:
            blk = _rope_tile(blk, cos_ref[...], sin_ref[...])
        if is_q:
            _store_queries(q_out, LANES * c, blk * GQA_Q_SCALE)
        else:
            k_out[:, LANES * (c - n_q_tiles):LANES * (c - n_q_tiles + 1)] = blk.astype(BF16)
    v = z[:, nq + nk:nq + 2 * nk]
    if ctx:
        for c in range(nk // LANES):
            _store_head_rows(vc_out, c, v[:, LANES * c:LANES * (c + 1)])
    _store_values_t(vt_out, v.T, KV_HEADS)
    gate_out[...] = _silu(z[:, nq + 2 * nk:]).astype(BF16)


def _const_spec(shape):
    return pl.BlockSpec(shape, lambda i: (0,) * len(shape))


def _proj_latent_kernel(*refs, kind, n_in, tiles_per_seq):
    refs = list(refs)
    proj_in, cache_in = refs[:n_in], refs[n_in:n_in + 2]
    proj_out, cache_out = refs[n_in + 2:n_in + 6], refs[n_in + 6:]
    if kind == 0:
        _proj_mla_kernel(*proj_in, *proj_out, rope=True, ctx=False)
    else:
        _proj_gqa_kernel(*proj_in, *proj_out, qk_norm=(kind == 1), rope=True, ctx=False)

    @pl.when(pl.program_id(0) % tiles_per_seq == 0)
    def _():
        if kind == 0:
            _mla_expand(cache_in[0][...].astype(BF16), cache_in[1][...], proj_in[3 + 4], proj_in[3 + 5],
                        cache_out[0], cache_out[1])
        else:
            cache_out[0][...] = cache_in[0][...].astype(BF16)
            _store_values_t(cache_out[1], cache_in[1][...].T, KV_HEADS)


def _project_latent(kind, x, mods_l, norm_g, weights, tables, cache_raw, *, batch, seq):
    n_tok = batch * seq
    tm = PROJ_TILE
    tiles_per_seq = seq // tm
    n_past = cache_raw[0].shape[1]
    in_specs = [
        pl.BlockSpec((tm, D_MODEL), lambda i: (i, 0)),
        pl.BlockSpec((None, 1, 3 * D_MODEL), lambda i: (i // tiles_per_seq, 0, 0)),
        _const_spec((1, D_MODEL)),
    ]
    args = [x, mods_l, norm_g.reshape(1, D_MODEL)]
    for w in weights:
        in_specs.append(_const_spec(w.shape))
        args.append(w)
    for t in tables:
        in_specs.append(pl.BlockSpec((tm, LANES), lambda i: (i % tiles_per_seq, 0)))
        args.append(t)
    n_in = len(args)
    for c in cache_raw:
        in_specs.append(pl.BlockSpec((None,) + c.shape[1:], lambda i: (i // tiles_per_seq, 0, 0)))
        args.append(c)
    if kind == 0:
        qw, kw, vw = N_HEADS * MLA_HEAD_PAD, N_HEADS * MLA_HEAD_PAD, N_HEADS * HEAD_DIM
    else:
        qw, kw, vw = N_HEADS * HEAD_DIM, KV_HEADS * HEAD_DIM, KV_HEADS * HEAD_DIM
    body = functools.partial(_proj_latent_kernel, kind=kind, n_in=n_in, tiles_per_seq=tiles_per_seq)
    out_shape = [
        jax.ShapeDtypeStruct((n_tok, qw), BF16),
        jax.ShapeDtypeStruct((n_tok, kw), BF16),
        jax.ShapeDtypeStruct((batch, vw, seq), BF16),
        jax.ShapeDtypeStruct((n_tok, D_MODEL), BF16),
        jax.ShapeDtypeStruct((batch, n_past, kw), BF16),
        jax.ShapeDtypeStruct((batch, vw, n_past), BF16),
    ]
    out_specs = [
        pl.BlockSpec((tm, qw), lambda i: (i, 0)),
        pl.BlockSpec((tm, kw), lambda i: (i, 0)),
        pl.BlockSpec((None, vw, tm), lambda i: (i // tiles_per_seq, 0, i % tiles_per_seq)),
        pl.BlockSpec((tm, D_MODEL), lambda i: (i, 0)),
        pl.BlockSpec((None, n_past, kw), lambda i: (i // tiles_per_seq, 0, 0)),
        pl.BlockSpec((None, vw, n_past), lambda i: (i // tiles_per_seq, 0, 0)),
    ]
    return pl.pallas_call(
        body,
        grid=(n_tok // tm,),
        in_specs=in_specs,
        out_specs=out_specs,
        out_shape=out_shape,
        compiler_params=pltpu.CompilerParams(
            dimension_semantics=("arbitrary",), vmem_limit_bytes=VMEM_LIMIT),
        name=f"proj_k{kind}_lat",
    )(*args)


def _attn_kernel(*refs, head_w, group, has_cache, has_sink, band, final, lag_b, lag_c, interleave,
                 subs, own_keys, seq):
    refs = list(refs)
    q_ref = refs[0]
    n_own = subs + 2 if band else 1
    kn_refs = refs[1:1 + n_own]
    vtn_refs = refs[1 + n_own:1 + 2 * n_own]
    kn_ref, vtn_ref = kn_refs[0], vtn_refs[0]
    pos = 1 + 2 * n_own
    if has_cache:
        kc_ref, vtc_ref = refs[pos:pos + 2]
        pos += 2
    gate_ref, x_ref, mod_ref, wo_ref = refs[pos:pos + 4]
    pos += 4
    if has_sink:
        sink_ref = refs[pos]
        pos += 1
    if final:
        fg_ref = refs[pos]
        pos += 1
    o_ref, ot_scr, s_scr, p_scr, g_scr, y_scr = refs[pos:pos + 6]
    if band:
        bias_scr = refs[pos + 6]

    tq = TOKEN_TILE
    ch = KEY_CHUNK
    slots = s_scr.shape[0]
    segs = []
    if has_cache:
        segs.append((kc_ref.shape[0],
                     lambda e, r, n, cols: kc_ref[r:r + n, cols],
                     lambda e, rows: vtc_ref[rows, :]))
    band_of = {}
    if own_keys:
        segs.append((seq,
                     lambda e, r, n, cols: kn_ref[e * seq + r:e * seq + r + n, cols],
                     lambda e, rows: vtn_ref[e, rows, :]))
    elif band:
        for a in range(3):
            band_of[len(segs)] = a
            segs.append((tq,
                         lambda e, r, n, cols, a=a: kn_refs[e + a][r:r + n, cols],
                         lambda e, rows, a=a: vtn_refs[e + a][rows, :]))
    else:
        segs.append((seq,
                     lambda e, r, n, cols: kn_ref[r:r + n, cols],
                     lambda e, rows: vtn_ref[rows, :]))
    seg_rows = []
    chunks = []
    off = 0
    for si, (n_seg, _, _) in enumerate(segs):
        seg_rows.append((off, n_seg))
        chunks += [(si, r, off + r) for r in range(0, n_seg, ch)]
        off += n_seg
    if interleave:
        score_chunks = [c + (ch,) for c in chunks]
    else:
        score_chunks = [(si, 0, r0, n) for si, (r0, n) in enumerate(seg_rows)]
    if band:
        row = lax.broadcasted_iota(jnp.int32, (tq, tq), 0)
        col = lax.broadcasted_iota(jnp.int32, (tq, tq), 1)
        for e in range(subs):
            tile = pl.program_id(1) * subs + e
            for a in range(3):
                key_block = tile - 1 + a
                ok = ((jnp.abs((a - 1) * tq + row - col) <= WINDOW)
                      & (key_block >= 0) & (key_block < seq // tq))
                bias_scr[e, a] = jnp.where(ok, 0.0, NEG_INF)

    assert lag_b >= 1 and lag_c >= 1 and slots > max(lag_b, lag_c)
    row_max = {}
    den_part = {}
    v_rows = vtn_ref.shape[-2] * group // N_HEADS
    den_on_mxu = v_rows > HEAD_DIM

    def scores_and_max(i):
        e, hd = divmod(i, N_HEADS)
        kh = hd // group
        q_cols = slice(head_w * hd, head_w * (hd + 1))
        qh = q_ref[e, q_cols, :] if own_keys else q_ref[e * tq:(e + 1) * tq, q_cols]
        mx = None
        for si, r, g, n in score_chunks:
            kk = segs[si][1](e, r, n, slice(head_w * kh, head_w * (kh + 1)))
            if own_keys:
                s = jnp.dot(kk, qh, preferred_element_type=F32)
            else:
                s = lax.dot_general(kk, qh, NT_DIMS, preferred_element_type=F32)
            if si in band_of:
                s = s + bias_scr[e, band_of[si], r:r + n, :]
            s_scr[i % slots, g:g + n, :] = s
            part = jnp.max(s.reshape(n // 8, 8, tq), axis=0)
            mx = part if mx is None else jnp.maximum(mx, part)
            yield
        m = jnp.max(mx, axis=0, keepdims=True)
        if has_sink:
            m = jnp.maximum(m, sink_ref[hd] * LOG2E)
        row_max[i] = m

    def exponentials(i):
        hd = i % N_HEADS
        m = row_max.pop(i)
        mb = jnp.broadcast_to(m, (8, tq))
        tot = None
        for _, _, g in chunks:
            p = jnp.exp2(s_scr[i % slots, g:g + ch, :].reshape(ch // 8, 8, tq) - mb)
            if not den_on_mxu:
                part = jnp.sum(p, axis=0)
                tot = part if tot is None else tot + part
            p_scr[i % slots, g:g + ch, :] = p.reshape(ch, tq).astype(BF16)
            yield
        den = None if den_on_mxu else jnp.sum(tot, axis=0, keepdims=True)
        if has_sink:
            sink = jnp.exp2(sink_ref[hd] * LOG2E - m)
            den = sink if den is None else den + sink
        den_part[i] = den

    def weighted_values(i):
        e, hd = divmod(i, N_HEADS)
        kh = hd // group
        acc = None
        for si, (_, _, values_t) in enumerate(segs):
            r0, n = seg_rows[si]
            vt = values_t(e, slice(v_rows * kh, v_rows * (kh + 1)))
            part = jnp.dot(vt, p_scr[i % slots, r0:r0 + n, :], preferred_element_type=F32)
            acc = part if acc is None else acc + part
            yield
        den = den_part.pop(i)
        if den_on_mxu:
            mxu_sum = acc[HEAD_DIM:HEAD_DIM + 1, :]
            den = mxu_sum if den is None else den + mxu_sum
        ot_scr[e, HEAD_DIM * hd:HEAD_DIM * (hd + 1), :] = acc[:HEAD_DIM, :] / den

    def finish(e):
        rows = slice(e * tq, (e + 1) * tq)
        for c0 in range(0, D_MODEL, OUT_BLOCK):
            cols = slice(c0, c0 + OUT_BLOCK)
            o = ot_scr[e, cols, :].T
            g_scr[:, cols] = (o * gate_ref[rows, cols].astype(F32)).astype(BF16)
            yield
        for c0 in range(0, D_MODEL, OUT_BLOCK):
            cols = slice(c0, c0 + OUT_BLOCK)
            y = jnp.dot(g_scr[...], wo_ref[:, cols], preferred_element_type=F32)
            xn = x_ref[rows, cols] + mod_ref[:, 2 * D_MODEL + c0:2 * D_MODEL + c0 + OUT_BLOCK] * y
            if final:
                y_scr[:, cols] = xn
            else:
                o_ref[rows, cols] = xn
            yield
        if final:
            o_ref[rows, :] = _rms(y_scr[...], fg_ref[...])

    n_items = subs * N_HEADS
    for step in range(n_items + lag_b + lag_c + 1):
        live = []
        if step < n_items:
            live.append(scores_and_max(step))
        if 0 <= step - lag_b < n_items:
            live.append(exponentials(step - lag_b))
        done = step - lag_b - lag_c
        if 0 <= done < n_items:
            live.append(weighted_values(done))
        if done >= N_HEADS and done % N_HEADS == 0:
            live.append(finish(done // N_HEADS - 1))
        if not interleave:
            for g in live:
                for _ in g:
                    pass
            live = []
        while live:
            live = [g for g in live if next(g, True) is None]


def _attend_latent(kind, q, k_new, vt_new, cache, gate, x, mods_l, w_out, sink, final_g, *, batch, seq,
                   lags):
    lag_b, lag_c, interleave, subs = lags
    tq = TOKEN_TILE
    n_tok = batch * seq
    band = kind == 2
    head_w = MLA_HEAD_PAD if kind == 0 else HEAD_DIM
    kw = k_new.shape[1]
    vw = vt_new.shape[1]
    assert seq % (subs * tq) == 0
    grid = (batch, seq // (subs * tq))
    tok_spec = lambda w: pl.BlockSpec((subs * tq, w), lambda b, t: (b * grid[1] + t, 0))
    qw = q.shape[1]
    if band:
        n_blk = seq // tq
        block = lambda j: (lambda b, t: jnp.clip(subs * t - 1 + j, 0, n_blk - 1))
        in_specs = [tok_spec(qw)]
        in_specs += [pl.BlockSpec((tq, kw), lambda b, t, f=block(j): (b * n_blk + f(b, t), 0))
                     for j in range(subs + 2)]
        in_specs += [pl.BlockSpec((None, vw, tq), lambda b, t, f=block(j): (b, 0, f(b, t)))
                     for j in range(subs + 2)]
    else:
        in_specs = [
            tok_spec(qw),
            pl.BlockSpec((seq, kw), lambda b, t: (b, 0)),
            pl.BlockSpec((None, vw, seq), lambda b, t: (b, 0, 0)),
        ]
    n_own = subs + 2 if band else 1
    args = [q] + [k_new] * n_own + [vt_new] * n_own
    kc, vtc = cache
    n_c = kc.shape[1]
    n_keys = n_c + (3 * tq if band else seq)
    in_specs += [
        pl.BlockSpec((None, n_c, kw), lambda b, t: (b, 0, 0)),
        pl.BlockSpec((None, vw, n_c), lambda b, t: (b, 0, 0)),
    ]
    args += [kc, vtc]
    in_specs += [
        tok_spec(D_MODEL),
        tok_spec(D_MODEL),
        pl.BlockSpec((None, 1, 3 * D_MODEL), lambda b, t: (b, 0, 0)),
        pl.BlockSpec((D_MODEL, D_MODEL), lambda b, t: (0, 0)),
    ]
    args += [gate, x, mods_l, w_out]
    if sink is not None:
        in_specs.append(pl.BlockSpec(memory_space=pltpu.SMEM))
        args.append(sink)
    if final_g is not None:
        in_specs.append(pl.BlockSpec((1, D_MODEL), lambda b, t: (0, 0)))
        args.append(final_g.reshape(1, D_MODEL))
    body = functools.partial(
        _attn_kernel,
        head_w=head_w,
        group=1 if kind == 0 else GROUPS,
        has_cache=True,
        has_sink=sink is not None,
        band=band,
        final=final_g is not None,
        lag_b=lag_b,
        lag_c=lag_c,
        interleave=interleave,
        subs=subs,
        own_keys=False,
        seq=seq,
    )
    slots = max(lag_b, lag_c) + 1
    scratch = [pltpu.VMEM((subs, N_HEADS * HEAD_DIM, tq), F32),
               pltpu.VMEM((slots, n_keys, tq), F32),
               pltpu.VMEM((slots, n_keys, tq), BF16),
               pltpu.VMEM((tq, D_MODEL), BF16),
               pltpu.VMEM((tq, D_MODEL), F32)]
    if band:
        scratch.append(pltpu.VMEM((subs, 3, tq, tq), F32))
    return pl.pallas_call(
        body,
        grid=grid,
        in_specs=in_specs,
        out_specs=tok_spec(D_MODEL),
        out_shape=jax.ShapeDtypeStruct((n_tok, D_MODEL), F32),
        scratch_shapes=scratch,
        compiler_params=pltpu.CompilerParams(
            dimension_semantics=("arbitrary", "arbitrary"), vmem_limit_bytes=VMEM_LIMIT),
        name=f"attn_k{kind}_lat",
    )(*args)


def _ctx_layer_kernel(*refs, kind, n_weights, has_sink, final, lag_b, lag_c, subs, seq):
    refs = list(refs)
    x_ref, mod_ref, g_ref = refs[:3]
    w_refs = refs[3:3 + n_weights]
    pos = 3 + n_weights
    tail_in = refs[pos:pos + 1 + has_sink + final]
    pos += len(tail_in)
    o_ref, ca_ref, cb_ref = refs[pos:pos + 3]
    q_s, k_s, vt_s, gate_s = refs[pos + 3:pos + 7]
    attn_scratch = refs[pos + 7:]
    if kind == 0:
        _proj_mla_kernel(x_ref, mod_ref, g_ref, *w_refs, q_s, k_s, vt_s, gate_s, ca_ref, cb_ref,
                         rope=False, ctx=True)
    else:
        _proj_gqa_kernel(x_ref, mod_ref, g_ref, *w_refs, q_s, k_s, vt_s, gate_s, ca_ref, cb_ref,
                         qk_norm=(kind == 1), rope=False, ctx=True)
    _attn_kernel(q_s, k_s, vt_s, gate_s, x_ref, mod_ref, *tail_in, o_ref, *attn_scratch,
                 head_w=MLA_HEAD_PAD if kind == 0 else HEAD_DIM, group=1 if kind == 0 else GROUPS,
                 has_cache=False, has_sink=has_sink, band=False, final=final, lag_b=lag_b, lag_c=lag_c,
                 interleave=True, subs=subs, own_keys=True, seq=seq)


def _ctx_layer(kind, x, mods_l, norm_g, weights, w_out, sink, final_g, *, batch, seq, lags):
    lag_b, lag_c, subs = lags
    tq = TOKEN_TILE
    assert seq == tq
    n_tok = batch * seq
    tm = subs * seq
    tok_spec = lambda w: pl.BlockSpec((tm, w), lambda i: (i, 0))
    in_specs = [tok_spec(D_MODEL), pl.BlockSpec((None, 1, 3 * D_MODEL), lambda i: (4, 0, 0)),
                _const_spec((1, D_MODEL))]
    args = [x, mods_l, norm_g.reshape(1, D_MODEL)]
    for w in weights:
        in_specs.append(_const_spec(w.shape))
        args.append(w)
    in_specs.append(_const_spec((D_MODEL, D_MODEL)))
    args.append(w_out)
    if sink is not None:
        in_specs.append(pl.BlockSpec(memory_space=pltpu.SMEM))
        args.append(sink)
    if final_g is not None:
        in_specs.append(_const_spec((1, D_MODEL)))
        args.append(final_g.reshape(1, D_MODEL))
    if kind == 0:
        qw, kw, vw = N_HEADS * MLA_HEAD_PAD, N_HEADS * MLA_HEAD_PAD, N_HEADS * V_ROWS
        cache_shapes = [(n_tok, MLA_KV_LORA), (n_tok, MLA_ROPE)]
        cache_blocks = [(tm, MLA_KV_LORA), (tm, MLA_ROPE)]
    else:
        qw, kw, vw = N_HEADS * HEAD_DIM, KV_HEADS * HEAD_DIM, KV_HEADS * V_ROWS
        cache_shapes = [(n_tok * KV_HEADS, LANES)] * 2
        cache_blocks = [(tm * KV_HEADS, LANES)] * 2
    out_shape = [jax.ShapeDtypeStruct((n_tok, D_MODEL), F32)]
    out_specs = [tok_spec(D_MODEL)]
    for s, b in zip(cache_shapes, cache_blocks):
        out_shape.append(jax.ShapeDtypeStruct(s, F32))
        out_specs.append(pl.BlockSpec(b, lambda i: (i, 0)))
    slots = max(lag_b, lag_c) + 1
    scratch = [pltpu.VMEM((subs, qw, seq), BF16),
               pltpu.VMEM((tm, kw), BF16),
               pltpu.VMEM((subs, vw, seq), BF16),
               pltpu.VMEM((tm, D_MODEL), BF16),
               pltpu.VMEM((subs, N_HEADS * HEAD_DIM, tq), F32),
               pltpu.VMEM((slots, seq, tq), F32),
               pltpu.VMEM((slots, seq, tq), BF16),
               pltpu.VMEM((tq, D_MODEL), BF16),
               pltpu.VMEM((tq, D_MODEL), F32)]
    body = functools.partial(
        _ctx_layer_kernel, kind=kind, n_weights=len(weights), has_sink=sink is not None,
        final=final_g is not None, lag_b=lag_b, lag_c=lag_c, subs=subs, seq=seq)
    return pl.pallas_call(
        body,
        grid=(n_tok // tm,),
        in_specs=in_specs,
        out_specs=out_specs,
        out_shape=out_shape,
        scratch_shapes=scratch,
        compiler_params=pltpu.CompilerParams(
            dimension_semantics=("arbitrary",), vmem_limit_bytes=VMEM_LIMIT),
        name=f"layer_k{kind}_ctx",
    )(*args)


def _rope_lane_tables(seq, rot_dim):
    rows = seq // GRID_W
    row = jnp.repeat(jnp.arange(rows), GRID_W).astype(F32)
    col = jnp.tile(jnp.arange(GRID_W), rows).astype(F32)
    nf = rot_dim // 4
    freqs = ROPE_THETA ** (-jnp.arange(nf, dtype=F32) / nf)
    ang = jnp.concatenate([row[:, None] * freqs, col[:, None] * freqs], axis=-1)
    cos, sin = jnp.cos(ang), jnp.sin(ang)
    cos_l = jnp.repeat(cos, 2, axis=-1)
    sin_l = jnp.stack([-sin, sin], axis=-1).reshape(seq, rot_dim)
    return cos_l, sin_l


def _embed_lanes(table, start, fill):
    seq, w = table.shape
    return jnp.concatenate(
        [jnp.full((seq, start), fill, F32), table, jnp.full((seq, LANES - start - w), fill, F32)], axis=1)


def _mla_weights(w_in, q_norm, w_uq, kv_norm, w_ukv):
    c0 = MLA_Q_LORA + MLA_KV_LORA
    win = jnp.concatenate(
        [w_in[:, :c0], w_in[:, c0 + MLA_ROPE:], w_in[:, c0:c0 + MLA_ROPE],
         jnp.zeros((D_MODEL, MLA_IN_PAD - w_in.shape[1]), F32)], axis=1).astype(BF16)
    hq = MLA_NOPE + MLA_ROPE
    wuq = jnp.pad(w_uq.reshape(MLA_Q_LORA, N_HEADS, hq),
                  ((0, 0), (0, 0), (0, MLA_HEAD_PAD - hq))).reshape(MLA_Q_LORA, -1).astype(BF16)
    wkv = w_ukv.reshape(MLA_KV_LORA, N_HEADS, MLA_NOPE + MLA_V)
    wk = wkv[:, :, :MLA_NOPE].reshape(MLA_KV_LORA, -1).astype(BF16)
    wvt = wkv[:, :, MLA_NOPE:].reshape(MLA_KV_LORA, -1).T.astype(BF16)
    return [win, q_norm.reshape(1, -1), wuq, kv_norm.reshape(1, -1), wk, wvt]


def kernel(x_prompt, x_sample, cache_mla_ckv, cache_mla_kpe, cache_gqa_k, cache_gqa_v,
           cache_swa_k, cache_swa_v, c, c_ctx, norm_g, w_ada, b_ada, w_out,
           mla_w_in, mla_q_norm, mla_w_uq, mla_kv_norm, mla_w_ukv,
           gqa_w_in, gqa_q_norm, gqa_k_norm, swa_w_in, swa_sink, final_norm_g):
    bc, sc, _ = x_prompt.shape
    bl, sl, _ = x_sample.shape
    n_past = cache_mla_ckv.shape[2]
    kvw = KV_HEADS * HEAD_DIM

    cc = jnp.concatenate([c, c_ctx[None, :], jnp.zeros((8 - bl - 1, D_MODEL), F32)], axis=0)
    mods = _ada_mods(cc, w_ada, b_ada).reshape(DEPTH, 8, 1, 3 * D_MODEL)

    cos_h, sin_h = _rope_lane_tables(sl, HEAD_DIM)
    gqa_tables = [jnp.tile(cos_h, (1, 2)), jnp.tile(sin_h, (1, 2))]
    cos_r, sin_r = _rope_lane_tables(sl, MLA_ROPE)
    mla_tables = [_embed_lanes(cos_r, MLA_NOPE, 1.0), _embed_lanes(sin_r, MLA_NOPE, 0.0),
                  _embed_lanes(cos_r, 0, 1.0), _embed_lanes(sin_r, 0, 0.0)]

    w_out_bf = w_out.astype(BF16)

    xc = x_prompt.reshape(bc * sc, D_MODEL)
    xl = x_sample.reshape(bl * sl, D_MODEL)
    new_caches = {0: ([], []), 1: ([], []), 2: ([], [])}
    for i in range(DEPTH):
        kind, j = i % 3, i // 3
        mods_l = mods[i]
        wo = w_out_bf[i]
        final_g = final_norm_g if i == DEPTH - 1 else None
        sink = None
        if kind == 0:
            weights = _mla_weights(mla_w_in[j], mla_q_norm[j], mla_w_uq[j], mla_kv_norm[j], mla_w_ukv[j])
            tables = mla_tables
            cache_raw = (cache_mla_ckv[:, j],
                         jnp.pad(cache_mla_kpe[:, j], ((0, 0), (0, 0), (0, LANES - MLA_ROPE))))
        else:
            tables = gqa_tables
            if kind == 1:
                weights = [gqa_w_in[j].astype(BF16),
                           jnp.tile(gqa_q_norm[j], 2).reshape(1, LANES),
                           jnp.tile(gqa_k_norm[j], 2).reshape(1, LANES)]
                cache_raw = (cache_gqa_k[:, j].reshape(bl, n_past, kvw),
                             cache_gqa_v[:, j].reshape(bl, n_past, kvw))
            else:
                weights = [swa_w_in[j].astype(BF16)]
                sink = swa_sink[j]
                cache_raw = (cache_swa_k[:, j].reshape(bl, n_past, kvw),
                             cache_swa_v[:, j].reshape(bl, n_past, kvw))

        xc, c_a, c_b = _ctx_layer(kind, xc, mods_l, norm_g[i], weights, wo, sink, final_g,
                                  batch=bc, seq=sc, lags=CTX_LAYER_LAGS)
        new_caches[kind][0].append(c_a)
        new_caches[kind][1].append(c_b)

        q, k, vt, gate, kc, vtc = _project_latent(kind, xl, mods_l, norm_g[i], weights, tables, cache_raw,
                                                  batch=bl, seq=sl)
        xl = _attend_latent(kind, q, k, vt, (kc, vtc), gate, xl, mods_l, wo, sink, final_g,
                            batch=bl, seq=sl, lags=LAT_STAGE_LAGS[kind])

    def stack(parts, tail):
        if len(tail) == 2:
            parts = [p[:, :HEAD_DIM] for p in parts]
        return jnp.stack([p.reshape((bc, sc) + tail) for p in parts], axis=1)

    return (xc.reshape(bc, sc, D_MODEL), xl.reshape(bl, sl, D_MODEL),
            stack(new_caches[0][0], (MLA_KV_LORA,)), stack(new_caches[0][1], (MLA_ROPE,)),
            stack(new_caches[1][0], (KV_HEADS, HEAD_DIM)), stack(new_caches[1][1], (KV_HEADS, HEAD_DIM)),
            stack(new_caches[2][0], (KV_HEADS, HEAD_DIM)), stack(new_caches[2][1], (KV_HEADS, HEAD_DIM)))
```

```python
import functools

import jax
import jax.numpy as jnp
from jax import lax
from jax.experimental import pallas as pl
from jax.experimental.pallas import tpu as pltpu

F32 = jnp.float32
BF16 = jnp.bfloat16

D_MODEL = 1024
DEPTH = 4
GRID_W = 64
N_HEADS = 16
HEAD_DIM = 64
KV_HEADS = 4
GROUPS = N_HEADS // KV_HEADS
MLA_Q_LORA = 384
MLA_KV_LORA = 256
MLA_NOPE = 64
MLA_ROPE = 32
MLA_V = 64
WINDOW = 128
ROPE_THETA = 10000.0
EPS = 1e-6
NEG_INF = -1e30

LOG2E = 1.4426950408889634
GQA_Q_SCALE = HEAD_DIM ** -0.5 * LOG2E
MLA_Q_SCALE = (MLA_NOPE + MLA_ROPE) ** -0.5 * LOG2E

LANES = 128
TOKEN_TILE = 256
PROJ_TILE = 512
KEY_CHUNK = 256
OUT_BLOCK = 256
CTX_LAYER_LAGS = (4, 4, 2)
LAT_STAGE_LAGS = {0: (2, 2, True, 2), 1: (2, 2, False, 2), 2: (2, 2, False, 2)}
MLA_HEAD_PAD = 128
V_ROWS = HEAD_DIM + 16
MLA_IN_PAD = 1792
VMEM_LIMIT = 56 * 1024 * 1024
NT_DIMS = (((1,), (1,)), ((), ()))


def _silu(v):
    return v * jax.nn.sigmoid(v)


def _rms(v, g):
    return v * lax.rsqrt(jnp.mean(v * v, axis=-1, keepdims=True) + EPS) * g


def _mod_norm(x, g, mod):
    shift = mod[:, :D_MODEL]
    scale = mod[:, D_MODEL:2 * D_MODEL]
    return _rms(x, g) * (1.0 + scale) + shift


def _rope_tile(v, cos, sin_signed):
    lane = lax.broadcasted_iota(jnp.int32, v.shape, 1)
    nxt = pltpu.roll(v, LANES - 1, axis=1)
    prv = pltpu.roll(v, 1, axis=1)
    swapped = jnp.where((lane & 1) == 0, nxt, prv)
    return v * cos + swapped * sin_signed


def _ada_kernel(c_ref, w_ref, b_ref, o_ref):
    s = _silu(c_ref[...]).astype(BF16)
    o_ref[...] = jnp.dot(s, w_ref[...].astype(BF16), preferred_element_type=F32) + b_ref[...]


def _ada_mods(cc, w_ada, b_ada):
    tn = D_MODEL
    return pl.pallas_call(
        _ada_kernel,
        grid=(DEPTH, 3 * D_MODEL // tn),
        in_specs=[
            pl.BlockSpec((8, D_MODEL), lambda i, n: (0, 0)),
            pl.BlockSpec((None, D_MODEL, tn), lambda i, n: (i, 0, n)),
            pl.BlockSpec((None, 1, tn), lambda i, n: (i, 0, n)),
        ],
        out_specs=pl.BlockSpec((None, 8, tn), lambda i, n: (i, 0, n)),
        out_shape=jax.ShapeDtypeStruct((DEPTH, 8, 3 * D_MODEL), F32),
        compiler_params=pltpu.CompilerParams(vmem_limit_bytes=VMEM_LIMIT),
        name="ada_mods",
    )(cc, w_ada, b_ada.reshape(DEPTH, 1, 3 * D_MODEL))


def _store_queries(q_out, col0, blk):
    w = blk.shape[1]
    if len(q_out.shape) == 2:
        q_out[:, col0:col0 + w] = blk.astype(BF16)
        return
    blk_t = blk.T.astype(BF16)
    seq = q_out.shape[2]
    for e in range(q_out.shape[0]):
        q_out[e, col0:col0 + w, :] = blk_t[:, e * seq:(e + 1) * seq]


def _store_values_t(vt_out, vt, n_heads):
    if len(vt_out.shape) == 3:
        seq = vt_out.shape[2]
        for e in range(vt_out.shape[0]):
            _store_values_t(vt_out.at[e], vt[:, e * seq:(e + 1) * seq], n_heads)
        return
    v_rows = vt_out.shape[0] // n_heads
    if v_rows == HEAD_DIM:
        vt_out[...] = vt.astype(BF16)
        return
    ones = jnp.ones((v_rows - HEAD_DIM, vt.shape[1]), BF16)
    for h in range(n_heads):
        vt_out[v_rows * h:v_rows * h + HEAD_DIM, :] = vt[HEAD_DIM * h:HEAD_DIM * (h + 1), :].astype(BF16)
        vt_out[v_rows * h + HEAD_DIM:v_rows * (h + 1), :] = ones


def _mla_expand(ckv_bf, kpe, wk_ref, wvt_ref, k_out, vt_out):
    k = jnp.dot(ckv_bf, wk_ref[...], preferred_element_type=F32)
    lane = lax.broadcasted_iota(jnp.int32, kpe.shape, 1)
    rope_lanes = (lane >= MLA_NOPE) & (lane < MLA_NOPE + MLA_ROPE)
    tail = jnp.where(rope_lanes, pltpu.roll(kpe, MLA_NOPE, axis=1), 0.0)
    for hd in range(N_HEADS):
        pair = k[:, LANES * (hd // 2):LANES * (hd // 2 + 1)]
        if hd % 2:
            pair = pltpu.roll(pair, MLA_NOPE, axis=1)
        k_out[:, MLA_HEAD_PAD * hd:MLA_HEAD_PAD * (hd + 1)] = jnp.where(lane < MLA_NOPE, pair, tail).astype(BF16)
    vt = lax.dot_general(wvt_ref[...], ckv_bf, NT_DIMS, preferred_element_type=F32)
    _store_values_t(vt_out, vt, N_HEADS)


def _proj_mla_kernel(*refs, rope, ctx):
    refs = list(refs)
    x_ref, mod_ref, g_ref, win_ref, qn_ref, wuq_ref, kvn_ref, wk_ref, wvt_ref = refs[:9]
    pos = 9
    if rope:
        cq_ref, sq_ref, ck_ref, sk_ref = refs[pos:pos + 4]
        pos += 4
    q_out, k_out, vt_out, gate_out = refs[pos:pos + 4]
    pos += 4
    if ctx:
        ckv_out, kpe_out = refs[pos:pos + 2]

    h = _mod_norm(x_ref[...], g_ref[...], mod_ref[...]).astype(BF16)
    z = jnp.dot(h, win_ref[...], preferred_element_type=F32)
    cq = z[:, :MLA_Q_LORA]
    ckv = z[:, MLA_Q_LORA:MLA_Q_LORA + MLA_KV_LORA]
    gate = z[:, 640:640 + D_MODEL]
    kpe = z[:, 640 + D_MODEL:]
    gate_out[...] = _silu(gate).astype(BF16)

    qf = jnp.dot(_rms(cq, qn_ref[...]).astype(BF16), wuq_ref[...], preferred_element_type=F32)
    ckvn = _rms(ckv, kvn_ref[...])
    if ctx:
        ckv_out[...] = ckvn
        kpe_out[...] = kpe[:, :MLA_ROPE]
    if rope:
        kpe = _rope_tile(kpe, ck_ref[...], sk_ref[...])
    _mla_expand(ckvn.astype(BF16), kpe, wk_ref, wvt_ref, k_out, vt_out)
    for hd in range(N_HEADS):
        blk = qf[:, MLA_HEAD_PAD * hd:MLA_HEAD_PAD * (hd + 1)]
        if rope:
            blk = _rope_tile(blk, cq_ref[...], sq_ref[...])
        _store_queries(q_out, MLA_HEAD_PAD * hd, blk * MLA_Q_SCALE)


def _store_head_rows(out_ref, pair, blk):
    tokens = blk.shape[0]
    out_ref[pl.ds(2 * pair, tokens, stride=KV_HEADS), :] = blk
    out_ref[pl.ds(2 * pair + 1, tokens, stride=KV_HEADS), :] = pltpu.roll(blk, HEAD_DIM, axis=1)


def _proj_gqa_kernel(*refs, qk_norm, rope, ctx):
    refs = list(refs)
    x_ref, mod_ref, g_ref, win_ref = refs[:4]
    pos = 4
    if qk_norm:
        qn_ref, kn_ref = refs[pos:pos + 2]
        pos += 2
    if rope:
        cos_ref, sin_ref = refs[pos:pos + 2]
        pos += 2
    q_out, k_out, vt_out, gate_out = refs[pos:pos + 4]
    pos += 4
    if ctx:
        kc_out, vc_out = refs[pos:pos + 2]

    nq = N_HEADS * HEAD_DIM
    nk = KV_HEADS * HEAD_DIM
    h = _mod_norm(x_ref[...], g_ref[...], mod_ref[...]).astype(BF16)
    z = jnp.dot(h, win_ref[...], preferred_element_type=F32)
    tm = z.shape[0]
    lo = lax.broadcasted_iota(jnp.int32, (tm, LANES), 1) < HEAD_DIM
    n_q_tiles = nq // LANES
    for c in range((nq + nk) // LANES):
        blk = z[:, LANES * c:LANES * (c + 1)]
        is_q = c < n_q_tiles
        if qk_norm:
            sq = blk * blk
            s_lo = jnp.sum(jnp.where(lo, sq, 0.0), axis=-1, keepdims=True)
            s_hi = jnp.sum(jnp.where(lo, 0.0, sq), axis=-1, keepdims=True)
            inv = jnp.where(lo, lax.rsqrt(s_lo * (1.0 / HEAD_DIM) + EPS),
                            lax.rsqrt(s_hi * (1.0 / HEAD_DIM) + EPS))
            blk = blk * inv * (qn_ref[...] if is_q else kn_ref[...])
        if ctx and not is_q:
            _store_head_rows(kc_out, c - n_q_tiles, blk)
        if rope:
            blk = _rope_tile(blk, cos_ref[...], sin_ref[...])
        if is_q:
            _store_queries(q_out, LANES * c, blk * GQA_Q_SCALE)
        else:
            k_out[:, LANES * (c - n_q_tiles):LANES * (c - n_q_tiles + 1)] = blk.astype(BF16)
    v = z[:, nq + nk:nq + 2 * nk]
    if ctx:
        for c in range(nk // LANES):
            _store_head_rows(vc_out, c, v[:, LANES * c:LANES * (c + 1)])
    _store_values_t(vt_out, v.T, KV_HEADS)
    gate_out[...] = _silu(z[:, nq + 2 * nk:]).astype(BF16)


def _const_spec(shape):
    return pl.BlockSpec(shape, lambda i: (0,) * len(shape))


def _proj_latent_kernel(*refs, kind, n_in, tiles_per_seq):
    refs = list(refs)
    proj_in, cache_in = refs[:n_in], refs[n_in:n_in + 2]
    proj_out, cache_out = refs[n_in + 2:n_in + 6], refs[n_in + 6:]
    if kind == 0:
        _proj_mla_kernel(*proj_in, *proj_out, rope=True, ctx=False)
    else:
        _proj_gqa_kernel(*proj_in, *proj_out, qk_norm=(kind == 1), rope=True, ctx=False)

    @pl.when(pl.program_id(0) % tiles_per_seq == 0)
    def _():
        if kind == 0:
            _mla_expand(cache_in[0][...].astype(BF16), cache_in[1][...], proj_in[3 + 4], proj_in[3 + 5],
                        cache_out[0], cache_out[1])
        else:
            cache_out[0][...] = cache_in[0][...].astype(BF16)
            _store_values_t(cache_out[1], cache_in[1][...].T, KV_HEADS)


def _project_latent(kind, x, mods_l, norm_g, weights, tables, cache_raw, *, batch, seq):
    n_tok = batch * seq
    tm = PROJ_TILE
    tiles_per_seq = seq // tm
    n_past = cache_raw[0].shape[1]
    in_specs = [
        pl.BlockSpec((tm, D_MODEL), lambda i: (i, 0)),
        pl.BlockSpec((None, 1, 3 * D_MODEL), lambda i: (i // tiles_per_seq, 0, 0)),
        _const_spec((1, D_MODEL)),
    ]
    args = [x, mods_l, norm_g.reshape(1, D_MODEL)]
    for w in weights:
        in_specs.append(_const_spec(w.shape))
        args.append(w)
    for t in tables:
        in_specs.append(pl.BlockSpec((tm, LANES), lambda i: (i % tiles_per_seq, 0)))
        args.append(t)
    n_in = len(args)
    for c in cache_raw:
        in_specs.append(pl.BlockSpec((None,) + c.shape[1:], lambda i: (i // tiles_per_seq, 0, 0)))
        args.append(c)
    if kind == 0:
        qw, kw, vw = N_HEADS * MLA_HEAD_PAD, N_HEADS * MLA_HEAD_PAD, N_HEADS * HEAD_DIM
    else:
        qw, kw, vw = N_HEADS * HEAD_DIM, KV_HEADS * HEAD_DIM, KV_HEADS * HEAD_DIM
    body = functools.partial(_proj_latent_kernel, kind=kind, n_in=n_in, tiles_per_seq=tiles_per_seq)
    out_shape = [
        jax.ShapeDtypeStruct((n_tok, qw), BF16),
        jax.ShapeDtypeStruct((n_tok, kw), BF16),
        jax.ShapeDtypeStruct((batch, vw, seq), BF16),
        jax.ShapeDtypeStruct((n_tok, D_MODEL), BF16),
        jax.ShapeDtypeStruct((batch, n_past, kw), BF16),
        jax.ShapeDtypeStruct((batch, vw, n_past), BF16),
    ]
    out_specs = [
        pl.BlockSpec((tm, qw), lambda i: (i, 0)),
        pl.BlockSpec((tm, kw), lambda i: (i, 0)),
        pl.BlockSpec((None, vw, tm), lambda i: (i // tiles_per_seq, 0, i % tiles_per_seq)),
        pl.BlockSpec((tm, D_MODEL), lambda i: (i, 0)),
        pl.BlockSpec((None, n_past, kw), lambda i: (i // tiles_per_seq, 0, 0)),
        pl.BlockSpec((None, vw, n_past), lambda i: (i // tiles_per_seq, 0, 0)),
    ]
    return pl.pallas_call(
        body,
        grid=(n_tok // tm,),
        in_specs=in_specs,
        out_specs=out_specs,
        out_shape=out_shape,
        compiler_params=pltpu.CompilerParams(
            dimension_semantics=("arbitrary",), vmem_limit_bytes=VMEM_LIMIT),
        name=f"proj_k{kind}_lat",
    )(*args)


def _attn_kernel(*refs, head_w, group, has_cache, has_sink, band, final, lag_b, lag_c, interleave,
                 subs, own_keys, seq):
    refs = list(refs)
    q_ref = refs[0]
    n_own = subs + 2 if band else 1
    kn_refs = refs[1:1 + n_own]
    vtn_refs = refs[1 + n_own:1 + 2 * n_own]
    kn_ref, vtn_ref = kn_refs[0], vtn_refs[0]
    pos = 1 + 2 * n_own
    if has_cache:
        kc_ref, vtc_ref = refs[pos:pos + 2]
        pos += 2
    gate_ref, x_ref, mod_ref, wo_ref = refs[pos:pos + 4]
    pos += 4
    if has_sink:
        sink_ref = refs[pos]
        pos += 1
    if final:
        fg_ref = refs[pos]
        pos += 1
    o_ref, ot_scr, s_scr, p_scr, g_scr, y_scr = refs[pos:pos + 6]
    if band:
        bias_scr = refs[pos + 6]

    tq = TOKEN_TILE
    ch = KEY_CHUNK
    slots = s_scr.shape[0]
    segs = []
    if has_cache:
        segs.append((kc_ref.shape[0],
                     lambda e, r, n, cols: kc_ref[r:r + n, cols],
                     lambda e, rows: vtc_ref[rows, :]))
    band_of = {}
    if own_keys:
        segs.append((seq,
                     lambda e, r, n, cols: kn_ref[e * seq + r:e * seq + r + n, cols],
                     lambda e, rows: vtn_ref[e, rows, :]))
    elif band:
        for a in range(3):
            band_of[len(segs)] = a
            segs.append((tq,
                         lambda e, r, n, cols, a=a: kn_refs[e + a][r:r + n, cols],
                         lambda e, rows, a=a: vtn_refs[e + a][rows, :]))
    else:
        segs.append((seq,
                     lambda e, r, n, cols: kn_ref[r:r + n, cols],
                     lambda e, rows: vtn_ref[rows, :]))
    seg_rows = []
    chunks = []
    off = 0
    for si, (n_seg, _, _) in enumerate(segs):
        seg_rows.append((off, n_seg))
        chunks += [(si, r, off + r) for r in range(0, n_seg, ch)]
        off += n_seg
    if interleave:
        score_chunks = [c + (ch,) for c in chunks]
    else:
        score_chunks = [(si, 0, r0, n) for si, (r0, n) in enumerate(seg_rows)]
    if band:
        row = lax.broadcasted_iota(jnp.int32, (tq, tq), 0)
        col = lax.broadcasted_iota(jnp.int32, (tq, tq), 1)
        for e in range(subs):
            tile = pl.program_id(1) * subs + e
            for a in range(3):
                key_block = tile - 1 + a
                ok = ((jnp.abs((a - 1) * tq + row - col) <= WINDOW)
                      & (key_block >= 0) & (key_block < seq // tq))
                bias_scr[e, a] = jnp.where(ok, 0.0, NEG_INF)

    assert lag_b >= 1 and lag_c >= 1 and slots > max(lag_b, lag_c)
    row_max = {}
    den_part = {}
    v_rows = vtn_ref.shape[-2] * group // N_HEADS
    den_on_mxu = v_rows > HEAD_DIM

    def scores_and_max(i):
        e, hd = divmod(i, N_HEADS)
        kh = hd // group
        q_cols = slice(head_w * hd, head_w * (hd + 1))
        qh = q_ref[e, q_cols, :] if own_keys else q_ref[e * tq:(e + 1) * tq, q_cols]
        mx = None
        for si, r, g, n in score_chunks:
            kk = segs[si][1](e, r, n, slice(head_w * kh, head_w * (kh + 1)))
            if own_keys:
                s = jnp.dot(kk, qh, preferred_element_type=F32)
            else:
                s = lax.dot_general(kk, qh, NT_DIMS, preferred_element_type=F32)
            if si in band_of:
                s = s + bias_scr[e, band_of[si], r:r + n, :]
            s_scr[i % slots, g:g + n, :] = s
            part = jnp.max(s.reshape(n // 8, 8, tq), axis=0)
            mx = part if mx is None else jnp.maximum(mx, part)
            yield
        m = jnp.max(mx, axis=0, keepdims=True)
        if has_sink:
            m = jnp.maximum(m, sink_ref[hd] * LOG2E)
        row_max[i] = m

    def exponentials(i):
        hd = i % N_HEADS
        m = row_max.pop(i)
        mb = jnp.broadcast_to(m, (8, tq))
        tot = None
        for _, _, g in chunks:
            p = jnp.exp2(s_scr[i % slots, g:g + ch, :].reshape(ch // 8, 8, tq) - mb)
            if not den_on_mxu:
                part = jnp.sum(p, axis=0)
                tot = part if tot is None else tot + part
            p_scr[i % slots, g:g + ch, :] = p.reshape(ch, tq).astype(BF16)
            yield
        den = None if den_on_mxu else jnp.sum(tot, axis=0, keepdims=True)
        if has_sink:
            sink = jnp.exp2(sink_ref[hd] * LOG2E - m)
            den = sink if den is None else den + sink
        den_part[i] = den

    def weighted_values(i):
        e, hd = divmod(i, N_HEADS)
        kh = hd // group
        acc = None
        for si, (_, _, values_t) in enumerate(segs):
            r0, n = seg_rows[si]
            vt = values_t(e, slice(v_rows * kh, v_rows * (kh + 1)))
            part = jnp.dot(vt, p_scr[i % slots, r0:r0 + n, :], preferred_element_type=F32)
            acc = part if acc is None else acc + part
            yield
        den = den_part.pop(i)
        if den_on_mxu:
            mxu_sum = acc[HEAD_DIM:HEAD_DIM + 1, :]
            den = mxu_sum if den is None else den + mxu_sum
        ot_scr[e, HEAD_DIM * hd:HEAD_DIM * (hd + 1), :] = acc[:HEAD_DIM, :] / den

    def finish(e):
        rows = slice(e * tq, (e + 1) * tq)
        for c0 in range(0, D_MODEL, OUT_BLOCK):
            cols = slice(c0, c0 + OUT_BLOCK)
            o = ot_scr[e, cols, :].T
            g_scr[:, cols] = (o * gate_ref[rows, cols].astype(F32)).astype(BF16)
            yield
        for c0 in range(0, D_MODEL, OUT_BLOCK):
            cols = slice(c0, c0 + OUT_BLOCK)
            y = jnp.dot(g_scr[...], wo_ref[:, cols], preferred_element_type=F32)
            xn = x_ref[rows, cols] + mod_ref[:, 2 * D_MODEL + c0:2 * D_MODEL + c0 + OUT_BLOCK] * y
            if final:
                y_scr[:, cols] = xn
            else:
                o_ref[rows, cols] = xn
            yield
        if final:
            o_ref[rows, :] = _rms(y_scr[...], fg_ref[...])

    n_items = subs * N_HEADS
    for step in range(n_items + lag_b + lag_c + 1):
        live = []
        if step < n_items:
            live.append(scores_and_max(step))
        if 0 <= step - lag_b < n_items:
            live.append(exponentials(step - lag_b))
        done = step - lag_b - lag_c
        if 0 <= done < n_items:
            live.append(weighted_values(done))
        if done >= N_HEADS and done % N_HEADS == 0:
            live.append(finish(done // N_HEADS - 1))
        if not interleave:
            for g in live:
                for _ in g:
                    pass
            live = []
        while live:
            live = [g for g in live if next(g, True) is None]


def _attend_latent(kind, q, k_new, vt_new, cache, gate, x, mods_l, w_out, sink, final_g, *, batch, seq,
                   lags):
    lag_b, lag_c, interleave, subs = lags
    tq = TOKEN_TILE
    n_tok = batch * seq
    band = kind == 2
    head_w = MLA_HEAD_PAD if kind == 0 else HEAD_DIM
    kw = k_new.shape[1]
    vw = vt_new.shape[1]
    assert seq % (subs * tq) == 0
    grid = (batch, seq // (subs * tq))
    tok_spec = lambda w: pl.BlockSpec((subs * tq, w), lambda b, t: (b * grid[1] + t, 0))
    qw = q.shape[1]
    if band:
        n_blk = seq // tq
        block = lambda j: (lambda b, t: jnp.clip(subs * t - 1 + j, 0, n_blk - 1))
        in_specs = [tok_spec(qw)]
        in_specs += [pl.BlockSpec((tq, kw), lambda b, t, f=block(j): (b * n_blk + f(b, t), 0))
                     for j in range(subs + 2)]
        in_specs += [pl.BlockSpec((None, vw, tq), lambda b, t, f=block(j): (b, 0, f(b, t)))
                     for j in range(subs + 2)]
    else:
        in_specs = [
            tok_spec(qw),
            pl.BlockSpec((seq, kw), lambda b, t: (b, 0)),
            pl.BlockSpec((None, vw, seq), lambda b, t: (b, 0, 0)),
        ]
    n_own = subs + 2 if band else 1
    args = [q] + [k_new] * n_own + [vt_new] * n_own
    kc, vtc = cache
    n_c = kc.shape[1]
    n_keys = n_c + (3 * tq if band else seq)
    in_specs += [
        pl.BlockSpec((None, n_c, kw), lambda b, t: (b, 0, 0)),
        pl.BlockSpec((None, vw, n_c), lambda b, t: (b, 0, 0)),
    ]
    args += [kc, vtc]
    in_specs += [
        tok_spec(D_MODEL),
        tok_spec(D_MODEL),
        pl.BlockSpec((None, 1, 3 * D_MODEL), lambda b, t: (b, 0, 0)),
        pl.BlockSpec((D_MODEL, D_MODEL), lambda b, t: (0, 0)),
    ]
    args += [gate, x, mods_l, w_out]
    if sink is not None:
        in_specs.append(pl.BlockSpec(memory_space=pltpu.SMEM))
        args.append(sink)
    if final_g is not None:
        in_specs.append(pl.BlockSpec((1, D_MODEL), lambda b, t: (0, 0)))
        args.append(final_g.reshape(1, D_MODEL))
    body = functools.partial(
        _attn_kernel,
        head_w=head_w,
        group=1 if kind == 0 else GROUPS,
        has_cache=True,
        has_sink=sink is not None,
        band=band,
        final=final_g is not None,
        lag_b=lag_b,
        lag_c=lag_c,
        interleave=interleave,
        subs=subs,
        own_keys=False,
        seq=seq,
    )
    slots = max(lag_b, lag_c) + 1
    scratch = [pltpu.VMEM((subs, N_HEADS * HEAD_DIM, tq), F32),
               pltpu.VMEM((slots, n_keys, tq), F32),
               pltpu.VMEM((slots, n_keys, tq), BF16),
               pltpu.VMEM((tq, D_MODEL), BF16),
               pltpu.VMEM((tq, D_MODEL), F32)]
    if band:
        scratch.append(pltpu.VMEM((subs, 3, tq, tq), F32))
    return pl.pallas_call(
        body,
        grid=grid,
        in_specs=in_specs,
        out_specs=tok_spec(D_MODEL),
        out_shape=jax.ShapeDtypeStruct((n_tok, D_MODEL), F32),
        scratch_shapes=scratch,
        compiler_params=pltpu.CompilerParams(
            dimension_semantics=("arbitrary", "arbitrary"), vmem_limit_bytes=VMEM_LIMIT),
        name=f"attn_k{kind}_lat",
    )(*args)


def _ctx_layer_kernel(*refs, kind, n_weights, has_sink, final, lag_b, lag_c, subs, seq):
    refs = list(refs)
    x_ref, mod_ref, g_ref = refs[:3]
    w_refs = refs[3:3 + n_weights]
    pos = 3 + n_weights
    tail_in = refs[pos:pos + 1 + has_sink + final]
    pos += len(tail_in)
    o_ref, ca_ref, cb_ref = refs[pos:pos + 3]
    q_s, k_s, vt_s, gate_s = refs[pos + 3:pos + 7]
    attn_scratch = refs[pos + 7:]
    if kind == 0:
        _proj_mla_kernel(x_ref, mod_ref, g_ref, *w_refs, q_s, k_s, vt_s, gate_s, ca_ref, cb_ref,
                         rope=False, ctx=True)
    else:
        _proj_gqa_kernel(x_ref, mod_ref, g_ref, *w_refs, q_s, k_s, vt_s, gate_s, ca_ref, cb_ref,
                         qk_norm=(kind == 1), rope=False, ctx=True)
    _attn_kernel(q_s, k_s, vt_s, gate_s, x_ref, mod_ref, *tail_in, o_ref, *attn_scratch,
                 head_w=MLA_HEAD_PAD if kind == 0 else HEAD_DIM, group=1 if kind == 0 else GROUPS,
                 has_cache=False, has_sink=has_sink, band=False, final=final, lag_b=lag_b, lag_c=lag_c,
                 interleave=True, subs=subs, own_keys=True, seq=seq)


def _ctx_layer(kind, x, mods_l, norm_g, weights, w_out, sink, final_g, *, batch, seq, lags):
    lag_b, lag_c, subs = lags
    tq = TOKEN_TILE
    assert seq == tq
    n_tok = batch * seq
    tm = subs * seq
    tok_spec = lambda w: pl.BlockSpec((tm, w), lambda i: (i, 0))
    in_specs = [tok_spec(D_MODEL), pl.BlockSpec((None, 1, 3 * D_MODEL), lambda i: (4, 0, 0)),
                _const_spec((1, D_MODEL))]
    args = [x, mods_l, norm_g.reshape(1, D_MODEL)]
    for w in weights:
        in_specs.append(_const_spec(w.shape))
        args.append(w)
    in_specs.append(_const_spec((D_MODEL, D_MODEL)))
    args.append(w_out)
    if sink is not None:
        in_specs.append(pl.BlockSpec(memory_space=pltpu.SMEM))
        args.append(sink)
    if final_g is not None:
        in_specs.append(_const_spec((1, D_MODEL)))
        args.append(final_g.reshape(1, D_MODEL))
    if kind == 0:
        qw, kw, vw = N_HEADS * MLA_HEAD_PAD, N_HEADS * MLA_HEAD_PAD, N_HEADS * V_ROWS
        cache_shapes = [(n_tok, MLA_KV_LORA), (n_tok, MLA_ROPE)]
        cache_blocks = [(tm, MLA_KV_LORA), (tm, MLA_ROPE)]
    else:
        qw, kw, vw = N_HEADS * HEAD_DIM, KV_HEADS * HEAD_DIM, KV_HEADS * V_ROWS
        cache_shapes = [(n_tok * KV_HEADS, LANES)] * 2
        cache_blocks = [(tm * KV_HEADS, LANES)] * 2
    out_shape = [jax.ShapeDtypeStruct((n_tok, D_MODEL), F32)]
    out_specs = [tok_spec(D_MODEL)]
    for s, b in zip(cache_shapes, cache_blocks):
        out_shape.append(jax.ShapeDtypeStruct(s, F32))
        out_specs.append(pl.BlockSpec(b, lambda i: (i, 0)))
    slots = max(lag_b, lag_c) + 1
    scratch = [pltpu.VMEM((subs, qw, seq), BF16),
               pltpu.VMEM((tm, kw), BF16),
               pltpu.VMEM((subs, vw, seq), BF16),
               pltpu.VMEM((tm, D_MODEL), BF16),
               pltpu.VMEM((subs, N_HEADS * HEAD_DIM, tq), F32),
               pltpu.VMEM((slots, seq, tq), F32),
               pltpu.VMEM((slots, seq, tq), BF16),
               pltpu.VMEM((tq, D_MODEL), BF16),
               pltpu.VMEM((tq, D_MODEL), F32)]
    body = functools.partial(
        _ctx_layer_kernel, kind=kind, n_weights=len(weights), has_sink=sink is not None,
        final=final_g is not None, lag_b=lag_b, lag_c=lag_c, subs=subs, seq=seq)
    return pl.pallas_call(
        body,
        grid=(n_tok // tm,),
        in_specs=in_specs,
        out_specs=out_specs,
        out_shape=out_shape,
        scratch_shapes=scratch,
        compiler_params=pltpu.CompilerParams(
            dimension_semantics=("arbitrary",), vmem_limit_bytes=VMEM_LIMIT),
        name=f"layer_k{kind}_ctx",
    )(*args)


def _rope_lane_tables(seq, rot_dim):
    rows = seq // GRID_W
    row = jnp.repeat(jnp.arange(rows), GRID_W).astype(F32)
    col = jnp.tile(jnp.arange(GRID_W), rows).astype(F32)
    nf = rot_dim // 4
    freqs = ROPE_THETA ** (-jnp.arange(nf, dtype=F32) / nf)
    ang = jnp.concatenate([row[:, None] * freqs, col[:, None] * freqs], axis=-1)
    cos, sin = jnp.cos(ang), jnp.sin(ang)
    cos_l = jnp.repeat(cos, 2, axis=-1)
    sin_l = jnp.stack([-sin, sin], axis=-1).reshape(seq, rot_dim)
    return cos_l, sin_l


def _embed_lanes(table, start, fill):
    seq, w = table.shape
    return jnp.concatenate(
        [jnp.full((seq, start), fill, F32), table, jnp.full((seq, LANES - start - w), fill, F32)], axis=1)


def _mla_weights(w_in, q_norm, w_uq, kv_norm, w_ukv):
    c0 = MLA_Q_LORA + MLA_KV_LORA
    win = jnp.concatenate(
        [w_in[:, :c0], w_in[:, c0 + MLA_ROPE:], w_in[:, c0:c0 + MLA_ROPE],
         jnp.zeros((D_MODEL, MLA_IN_PAD - w_in.shape[1]), F32)], axis=1).astype(BF16)
    hq = MLA_NOPE + MLA_ROPE
    wuq = jnp.pad(w_uq.reshape(MLA_Q_LORA, N_HEADS, hq),
                  ((0, 0), (0, 0), (0, MLA_HEAD_PAD - hq))).reshape(MLA_Q_LORA, -1).astype(BF16)
    wkv = w_ukv.reshape(MLA_KV_LORA, N_HEADS, MLA_NOPE + MLA_V)
    wk = wkv[:, :, :MLA_NOPE].reshape(MLA_KV_LORA, -1).astype(BF16)
    wvt = wkv[:, :, MLA_NOPE:].reshape(MLA_KV_LORA, -1).T.astype(BF16)
    return [win, q_norm.reshape(1, -1), wuq, kv_norm.reshape(1, -1), wk, wvt]


def kernel(x_prompt, x_sample, cache_mla_ckv, cache_mla_kpe, cache_gqa_k, cache_gqa_v,
           cache_swa_k, cache_swa_v, c, c_ctx, norm_g, w_ada, b_ada, w_out,
           mla_w_in, mla_q_norm, mla_w_uq, mla_kv_norm, mla_w_ukv,
           gqa_w_in, gqa_q_norm, gqa_k_norm, swa_w_in, swa_sink, final_norm_g):
    bc, sc, _ = x_prompt.shape
    bl, sl, _ = x_sample.shape
    n_past = cache_mla_ckv.shape[2]
    kvw = KV_HEADS * HEAD_DIM

    cc = jnp.concatenate([c, c_ctx[None, :], jnp.zeros((8 - bl - 1, D_MODEL), F32)], axis=0)
    mods = _ada_mods(cc, w_ada, b_ada).reshape(DEPTH, 8, 1, 3 * D_MODEL)

    cos_h, sin_h = _rope_lane_tables(sl, HEAD_DIM)
    gqa_tables = [jnp.tile(cos_h, (1, 2)), jnp.tile(sin_h, (1, 2))]
    cos_r, sin_r = _rope_lane_tables(sl, MLA_ROPE)
    mla_tables = [_embed_lanes(cos_r, MLA_NOPE, 1.0), _embed_lanes(sin_r, MLA_NOPE, 0.0),
                  _embed_lanes(cos_r, 0, 1.0), _embed_lanes(sin_r, 0, 0.0)]

    w_out_bf = w_out.astype(BF16)

    xc = x_prompt.reshape(bc * sc, D_MODEL)
    xl = x_sample.reshape(bl * sl, D_MODEL)
    new_caches = {0: ([], []), 1: ([], []), 2: ([], [])}
    for i in range(DEPTH):
        kind, j = i % 3, i // 3
        mods_l = mods[i]
        wo = w_out_bf[i]
        final_g = final_norm_g if i == DEPTH - 1 else None
        sink = None
        if kind == 0:
            weights = _mla_weights(mla_w_in[j], mla_q_norm[j], mla_w_uq[j], mla_kv_norm[j], mla_w_ukv[j])
            tables = mla_tables
            cache_raw = (cache_mla_ckv[:, j],
                         jnp.pad(cache_mla_kpe[:, j], ((0, 0), (0, 0), (0, LANES - MLA_ROPE))))
        else:
            tables = gqa_tables
            if kind == 1:
                weights = [gqa_w_in[j].astype(BF16),
                           jnp.tile(gqa_q_norm[j], 2).reshape(1, LANES),
                           jnp.tile(gqa_k_norm[j], 2).reshape(1, LANES)]
                cache_raw = (cache_gqa_k[:, j].reshape(bl, n_past, kvw),
                             cache_gqa_v[:, j].reshape(bl, n_past, kvw))
            else:
                weights = [swa_w_in[j].astype(BF16)]
                sink = swa_sink[j]
                cache_raw = (cache_swa_k[:, j].reshape(bl, n_past, kvw),
                             cache_swa_v[:, j].reshape(bl, n_past, kvw))

        xc, c_a, c_b = _ctx_layer(kind, xc, mods_l, norm_g[i], weights, wo, sink, final_g,
                                  batch=bc, seq=sc, lags=CTX_LAYER_LAGS)
        new_caches[kind][0].append(c_a)
        new_caches[kind][1].append(c_b)

        q, k, vt, gate, kc, vtc = _project_latent(kind, xl, mods_l, norm_g[i], weights, tables, cache_raw,
                                                  batch=bl, seq=sl)
        xl = _attend_latent(kind, q, k, vt, (kc, vtc), gate, xl, mods_l, wo, sink, final_g,
                            batch=bl, seq=sl, lags=LAT_STAGE_LAGS[kind])

    def stack(parts, tail):
        if len(tail) == 2:
            parts = [p[:, :HEAD_DIM] for p in parts]
        return jnp.stack([p.reshape((bc, sc) + tail) for p in parts], axis=1)

    return (xc.reshape(bc, sc, D_MODEL), xl.reshape(bl, sl, D_MODEL),
            stack(new_caches[0][0], (MLA_KV_LORA,)), stack(new_caches[0][1], (MLA_ROPE,)),
            stack(new_caches[1][0], (KV_HEADS, HEAD_DIM)), stack(new_caches[1][1], (KV_HEADS, HEAD_DIM)),
            stack(new_caches[2][0], (KV_HEADS, HEAD_DIM)), stack(new_caches[2][1], (KV_HEADS, HEAD_DIM)))
```

```python
import functools

import jax
import jax.numpy as jnp
from jax import lax
from jax.experimental import pallas as pl
from jax.experimental.pallas import tpu as pltpu

F32 = jnp.float32
BF16 = jnp.bfloat16

D_MODEL = 1024
DEPTH = 4
GRID_W = 64
N_HEADS = 16
HEAD_DIM = 64
KV_HEADS = 4
GROUPS = N_HEADS // KV_HEADS
MLA_Q_LORA = 384
MLA_KV_LORA = 256
MLA_NOPE = 64
MLA_ROPE = 32
MLA_V = 64
WINDOW = 128
ROPE_THETA = 10000.0
EPS = 1e-6
NEG_INF = -1e30

LOG2E = 1.4426950408889634
GQA_Q_SCALE = HEAD_DIM ** -0.5 * LOG2E
MLA_Q_SCALE = (MLA_NOPE + MLA_ROPE) ** -0.5 * LOG2E

LANES = 128
TOKEN_TILE = 256
PROJ_TILE = 512
KEY_CHUNK = 256
OUT_BLOCK = 256
CTX_LAYER_LAGS = (4, 4, 2)
LAT_STAGE_LAGS = {0: (2, 2, True, 2), 1: (2, 2, False, 2), 2: (2, 2, False, 2)}
MLA_HEAD_PAD = 128
V_ROWS = HEAD_DIM + 16
MLA_IN_PAD = 1792
VMEM_LIMIT = 56 * 1024 * 1024
NT_DIMS = (((1,), (1,)), ((), ()))


def _silu(v):
    return v * jax.nn.sigmoid(v)


def _rms(v, g):
    return v * lax.rsqrt(jnp.mean(v * v, axis=-1, keepdims=True) + EPS) * g


def _mod_norm(x, g, mod):
    shift = mod[:, :D_MODEL]
    scale = mod[:, D_MODEL:2 * D_MODEL]
    return _rms(x, g) * (1.0 + scale) + shift


def _rope_tile(v, cos, sin_signed):
    lane = lax.broadcasted_iota(jnp.int32, v.shape, 1)
    nxt = pltpu.roll(v, LANES - 1, axis=1)
    prv = pltpu.roll(v, 1, axis=1)
    swapped = jnp.where((lane & 1) == 0, nxt, prv)
    return v * cos + swapped * sin_signed


def _ada_kernel(c_ref, w_ref, b_ref, o_ref):
    s = _silu(c_ref[...]).astype(BF16)
    o_ref[...] = jnp.dot(s, w_ref[...].astype(BF16), preferred_element_type=F32) + b_ref[...]


def _ada_mods(cc, w_ada, b_ada):
    tn = D_MODEL
    return pl.pallas_call(
        _ada_kernel,
        grid=(DEPTH, 3 * D_MODEL // tn),
        in_specs=[
            pl.BlockSpec((8, D_MODEL), lambda i, n: (0, 0)),
            pl.BlockSpec((None, D_MODEL, tn), lambda i, n: (i, 0, n)),
            pl.BlockSpec((None, 1, tn), lambda i, n: (i, 0, n)),
        ],
        out_specs=pl.BlockSpec((None, 8, tn), lambda i, n: (i, 0, n)),
        out_shape=jax.ShapeDtypeStruct((DEPTH, 8, 3 * D_MODEL), F32),
        compiler_params=pltpu.CompilerParams(vmem_limit_bytes=VMEM_LIMIT),
        name="ada_mods",
    )(cc, w_ada, b_ada.reshape(DEPTH, 1, 3 * D_MODEL))


def _store_queries(q_out, col0, blk):
    w = blk.shape[1]
    if len(q_out.shape) == 2:
        q_out[:, col0:col0 + w] = blk.astype(BF16)
        return
    blk_t = blk.T.astype(BF16)
    seq = q_out.shape[2]
    for e in range(q_out.shape[0]):
        q_out[e, col0:col0 + w, :] = blk_t[:, e * seq:(e + 1) * seq]


def _store_values_t(vt_out, vt, n_heads):
    if len(vt_out.shape) == 3:
        seq = vt_out.shape[2]
        for e in range(vt_out.shape[0]):
            _store_values_t(vt_out.at[e], vt[:, e * seq:(e + 1) * seq], n_heads)
        return
    v_rows = vt_out.shape[0] // n_heads
    if v_rows == HEAD_DIM:
        vt_out[...] = vt.astype(BF16)
        return
    ones = jnp.ones((v_rows - HEAD_DIM, vt.shape[1]), BF16)
    for h in range(n_heads):
        vt_out[v_rows * h:v_rows * h + HEAD_DIM, :] = vt[HEAD_DIM * h:HEAD_DIM * (h + 1), :].astype(BF16)
        vt_out[v_rows * h + HEAD_DIM:v_rows * (h + 1), :] = ones


def _mla_expand(ckv_bf, kpe, wk_ref, wvt_ref, k_out, vt_out):
    k = jnp.dot(ckv_bf, wk_ref[...], preferred_element_type=F32)
    lane = lax.broadcasted_iota(jnp.int32, kpe.shape, 1)
    rope_lanes = (lane >= MLA_NOPE) & (lane < MLA_NOPE + MLA_ROPE)
    tail = jnp.where(rope_lanes, pltpu.roll(kpe, MLA_NOPE, axis=1), 0.0)
    for hd in range(N_HEADS):
        pair = k[:, LANES * (hd // 2):LANES * (hd // 2 + 1)]
        if hd % 2:
            pair = pltpu.roll(pair, MLA_NOPE, axis=1)
        k_out[:, MLA_HEAD_PAD * hd:MLA_HEAD_PAD * (hd + 1)] = jnp.where(lane < MLA_NOPE, pair, tail).astype(BF16)
    vt = lax.dot_general(wvt_ref[...], ckv_bf, NT_DIMS, preferred_element_type=F32)
    _store_values_t(vt_out, vt, N_HEADS)


def _proj_mla_kernel(*refs, rope, ctx):
    refs = list(refs)
    x_ref, mod_ref, g_ref, win_ref, qn_ref, wuq_ref, kvn_ref, wk_ref, wvt_ref = refs[:9]
    pos = 9
    if rope:
        cq_ref, sq_ref, ck_ref, sk_ref = refs[pos:pos + 4]
        pos += 4
    q_out, k_out, vt_out, gate_out = refs[pos:pos + 4]
    pos += 4
    if ctx:
        ckv_out, kpe_out = refs[pos:pos + 2]

    h = _mod_norm(x_ref[...], g_ref[...], mod_ref[...]).astype(BF16)
    z = jnp.dot(h, win_ref[...], preferred_element_type=F32)
    cq = z[:, :MLA_Q_LORA]
    ckv = z[:, MLA_Q_LORA:MLA_Q_LORA + MLA_KV_LORA]
    gate = z[:, 640:640 + D_MODEL]
    kpe = z[:, 640 + D_MODEL:]
    gate_out[...] = _silu(gate).astype(BF16)

    qf = jnp.dot(_rms(cq, qn_ref[...]).astype(BF16), wuq_ref[...], preferred_element_type=F32)
    ckvn = _rms(ckv, kvn_ref[...])
    if ctx:
        ckv_out[...] = ckvn
        kpe_out[...] = kpe[:, :MLA_ROPE]
    if rope:
        kpe = _rope_tile(kpe, ck_ref[...], sk_ref[...])
    _mla_expand(ckvn.astype(BF16), kpe, wk_ref, wvt_ref, k_out, vt_out)
    for hd in range(N_HEADS):
        blk = qf[:, MLA_HEAD_PAD * hd:MLA_HEAD_PAD * (hd + 1)]
        if rope:
            blk = _rope_tile(blk, cq_ref[...], sq_ref[...])
        _store_queries(q_out, MLA_HEAD_PAD * hd, blk * MLA_Q_SCALE)


def _store_head_rows(out_ref, pair, blk):
    tokens = blk.shape[0]
    out_ref[pl.ds(2 * pair, tokens, stride=KV_HEADS), :] = blk
    out_ref[pl.ds(2 * pair + 1, tokens, stride=KV_HEADS), :] = pltpu.roll(blk, HEAD_DIM, axis=1)


def _proj_gqa_kernel(*refs, qk_norm, rope, ctx):
    refs = list(refs)
    x_ref, mod_ref, g_ref, win_ref = refs[:4]
    pos = 4
    if qk_norm:
        qn_ref, kn_ref = refs[pos:pos + 2]
        pos += 2
    if rope:
        cos_ref, sin_ref = refs[pos:pos + 2]
        pos += 2
    q_out, k_out, vt_out, gate_out = refs[pos:pos + 4]
    pos += 4
    if ctx:
        kc_out, vc_out = refs[pos:pos + 2]

    nq = N_HEADS * HEAD_DIM
    nk = KV_HEADS * HEAD_DIM
    h = _mod_norm(x_ref[...], g_ref[...], mod_ref[...]).astype(BF16)
    z = jnp.dot(h, win_ref[...], preferred_element_type=F32)
    tm = z.shape[0]
    lo = lax.broadcasted_iota(jnp.int32, (tm, LANES), 1) < HEAD_DIM
    n_q_tiles = nq // LANES
    for c in range((nq + nk) // LANES):
        blk = z[:, LANES * c:LANES * (c + 1)]
        is_q = c < n_q_tiles
        if qk_norm:
            sq = blk * blk
            s_lo = jnp.sum(jnp.where(lo, sq, 0.0), axis=-1, keepdims=True)
            s_hi = jnp.sum(jnp.where(lo, 0.0, sq), axis=-1, keepdims=True)
            inv = jnp.where(lo, lax.rsqrt(s_lo * (1.0 / HEAD_DIM) + EPS),
                            lax.rsqrt(s_hi * (1.0 / HEAD_DIM) + EPS))
            blk = blk * inv * (qn_ref[...] if is_q else kn_ref[...])
        if ctx and not is_q:
            _store_head_rows(kc_out, c - n_q_tiles, blk)
        if rope:
            blk = _rope_tile(blk, cos_ref[...], sin_ref[...])
        if is_q:
            _store_queries(q_out, LANES * c, blk * GQA_Q_SCALE)
        else:
            k_out[:, LANES * (c - n_q_tiles):LANES * (c - n_q_tiles + 1)] = blk.astype(BF16)
    v = z[:, nq + nk:nq + 2 * nk]
    if ctx:
        for c in range(nk // LANES):
            _store_head_rows(vc_out, c, v[:, LANES * c:LANES * (c + 1)])
    _store_values_t(vt_out, v.T, KV_HEADS)
    gate_out[...] = _silu(z[:, nq + 2 * nk:]).astype(BF16)


def _const_spec(shape):
    return pl.BlockSpec(shape, lambda i: (0,) * len(shape))


def _proj_latent_kernel(*refs, kind, n_in, tiles_per_seq):
    refs = list(refs)
    proj_in, cache_in = refs[:n_in], refs[n_in:n_in + 2]
    proj_out, cache_out = refs[n_in + 2:n_in + 6], refs[n_in + 6:]
    if kind == 0:
        _proj_mla_kernel(*proj_in, *proj_out, rope=True, ctx=False)
    else:
        _proj_gqa_kernel(*proj_in, *proj_out, qk_norm=(kind == 1), rope=True, ctx=False)

    @pl.when(pl.program_id(0) % tiles_per_seq == 0)
    def _():
        if kind == 0:
            _mla_expand(cache_in[0][...].astype(BF16), cache_in[1][...], proj_in[3 + 4], proj_in[3 + 5],
                        cache_out[0], cache_out[1])
        else:
            cache_out[0][...] = cache_in[0][...].astype(BF16)
            _store_values_t(cache_out[1], cache_in[1][...].T, KV_HEADS)


def _project_latent(kind, x, mods_l, norm_g, weights, tables, cache_raw, *, batch, seq):
    n_tok = batch * seq
    tm = PROJ_TILE
    tiles_per_seq = seq // tm
    n_past = cache_raw[0].shape[1]
    in_specs = [
        pl.BlockSpec((tm, D_MODEL), lambda i: (i, 0)),
        pl.BlockSpec((None, 1, 3 * D_MODEL), lambda i: (i // tiles_per_seq, 0, 0)),
        _const_spec((1, D_MODEL)),
    ]
    args = [x, mods_l, norm_g.reshape(1, D_MODEL)]
    for w in weights:
        in_specs.append(_const_spec(w.shape))
        args.append(w)
    for t in tables:
        in_specs.append(pl.BlockSpec((tm, LANES), lambda i: (i % tiles_per_seq, 0)))
        args.append(t)
    n_in = len(args)
    for c in cache_raw:
        in_specs.append(pl.BlockSpec((None,) + c.shape[1:], lambda i: (i // tiles_per_seq, 0, 0)))
        args.append(c)
    if kind == 0:
        qw, kw, vw = N_HEADS * MLA_HEAD_PAD, N_HEADS * MLA_HEAD_PAD, N_HEADS * HEAD_DIM
    else:
        qw, kw, vw = N_HEADS * HEAD_DIM, KV_HEADS * HEAD_DIM, KV_HEADS * HEAD_DIM
    body = functools.partial(_proj_latent_kernel, kind=kind, n_in=n_in, tiles_per_seq=tiles_per_seq)
    out_shape = [
        jax.ShapeDtypeStruct((n_tok, qw), BF16),
        jax.ShapeDtypeStruct((n_tok, kw), BF16),
        jax.ShapeDtypeStruct((batch, vw, seq), BF16),
        jax.ShapeDtypeStruct((n_tok, D_MODEL), BF16),
        jax.ShapeDtypeStruct((batch, n_past, kw), BF16),
        jax.ShapeDtypeStruct((batch, vw, n_past), BF16),
    ]
    out_specs = [
        pl.BlockSpec((tm, qw), lambda i: (i, 0)),
        pl.BlockSpec((tm, kw), lambda i: (i, 0)),
        pl.BlockSpec((None, vw, tm), lambda i: (i // tiles_per_seq, 0, i % tiles_per_seq)),
        pl.BlockSpec((tm, D_MODEL), lambda i: (i, 0)),
        pl.BlockSpec((None, n_past, kw), lambda i: (i // tiles_per_seq, 0, 0)),
        pl.BlockSpec((None, vw, n_past), lambda i: (i // tiles_per_seq, 0, 0)),
    ]
    return pl.pallas_call(
        body,
        grid=(n_tok // tm,),
        in_specs=in_specs,
        out_specs=out_specs,
        out_shape=out_shape,
        compiler_params=pltpu.CompilerParams(
            dimension_semantics=("arbitrary",), vmem_limit_bytes=VMEM_LIMIT),
        name=f"proj_k{kind}_lat",
    )(*args)


def _attn_kernel(*refs, head_w, group, has_cache, has_sink, band, final, lag_b, lag_c, interleave,
                 subs, own_keys, seq):
    refs = list(refs)
    q_ref = refs[0]
    n_own = subs + 2 if band else 1
    kn_refs = refs[1:1 + n_own]
    vtn_refs = refs[1 + n_own:1 + 2 * n_own]
    kn_ref, vtn_ref = kn_refs[0], vtn_refs[0]
    pos = 1 + 2 * n_own
    if has_cache:
        kc_ref, vtc_ref = refs[pos:pos + 2]
        pos += 2
    gate_ref, x_ref, mod_ref, wo_ref = refs[pos:pos + 4]
    pos += 4
    if has_sink:
        sink_ref = refs[pos]
        pos += 1
    if final:
        fg_ref = refs[pos]
        pos += 1
    o_ref, ot_scr, s_scr, p_scr, g_scr, y_scr = refs[pos:pos + 6]
    if band:
        bias_scr = refs[pos + 6]

    tq = TOKEN_TILE
    ch = KEY_CHUNK
    slots = s_scr.shape[0]
    segs = []
    if has_cache:
        segs.append((kc_ref.shape[0],
                     lambda e, r, n, cols: kc_ref[r:r + n, cols],
                     lambda e, rows: vtc_ref[rows, :]))
    band_of = {}
    if own_keys:
        segs.append((seq,
                     lambda e, r, n, cols: kn_ref[e * seq + r:e * seq + r + n, cols],
                     lambda e, rows: vtn_ref[e, rows, :]))
    elif band:
        for a in range(3):
            band_of[len(segs)] = a
            segs.append((tq,
                         lambda e, r, n, cols, a=a: kn_refs[e + a][r:r + n, cols],
                         lambda e, rows, a=a: vtn_refs[e + a][rows, :]))
    else:
        segs.append((seq,
                     lambda e, r, n, cols: kn_ref[r:r + n, cols],
                     lambda e, rows: vtn_ref[rows, :]))
    seg_rows = []
    chunks = []
    off = 0
    for si, (n_seg, _, _) in enumerate(segs):
        seg_rows.append((off, n_seg))
        chunks += [(si, r, off + r) for r in range(0, n_seg, ch)]
        off += n_seg
    if interleave:
        score_chunks = [c + (ch,) for c in chunks]
    else:
        score_chunks = [(si, 0, r0, n) for si, (r0, n) in enumerate(seg_rows)]
    if band:
        row = lax.broadcasted_iota(jnp.int32, (tq, tq), 0)
        col = lax.broadcasted_iota(jnp.int32, (tq, tq), 1)
        for e in range(subs):
            tile = pl.program_id(1) * subs + e
            for a in range(3):
                key_block = tile - 1 + a
                ok = ((jnp.abs((a - 1) * tq + row - col) <= WINDOW)
                      & (key_block >= 0) & (key_block < seq // tq))
                bias_scr[e, a] = jnp.where(ok, 0.0, NEG_INF)

    assert lag_b >= 1 and lag_c >= 1 and slots > max(lag_b, lag_c)
    row_max = {}
    den_part = {}
    v_rows = vtn_ref.shape[-2] * group // N_HEADS
    den_on_mxu = v_rows > HEAD_DIM

    def scores_and_max(i):
        e, hd = divmod(i, N_HEADS)
        kh = hd // group
        q_cols = slice(head_w * hd, head_w * (hd + 1))
        qh = q_ref[e, q_cols, :] if own_keys else q_ref[e * tq:(e + 1) * tq, q_cols]
        mx = None
        for si, r, g, n in score_chunks:
            kk = segs[si][1](e, r, n, slice(head_w * kh, head_w * (kh + 1)))
            if own_keys:
                s = jnp.dot(kk, qh, preferred_element_type=F32)
            else:
                s = lax.dot_general(kk, qh, NT_DIMS, preferred_element_type=F32)
            if si in band_of:
                s = s + bias_scr[e, band_of[si], r:r + n, :]
            s_scr[i % slots, g:g + n, :] = s
            part = jnp.max(s.reshape(n // 8, 8, tq), axis=0)
            mx = part if mx is None else jnp.maximum(mx, part)
            yield
        m = jnp.max(mx, axis=0, keepdims=True)
        if has_sink:
            m = jnp.maximum(m, sink_ref[hd] * LOG2E)
        row_max[i] = m

    def exponentials(i):
        hd = i % N_HEADS
        m = row_max.pop(i)
        mb = jnp.broadcast_to(m, (8, tq))
        tot = None
        for _, _, g in chunks:
            p = jnp.exp2(s_scr[i % slots, g:g + ch, :].reshape(ch // 8, 8, tq) - mb)
            if not den_on_mxu:
                part = jnp.sum(p, axis=0)
                tot = part if tot is None else tot + part
            p_scr[i % slots, g:g + ch, :] = p.reshape(ch, tq).astype(BF16)
            yield
        den = None if den_on_mxu else jnp.sum(tot, axis=0, keepdims=True)
        if has_sink:
            sink = jnp.exp2(sink_ref[hd] * LOG2E - m)
            den = sink if den is None else den + sink
        den_part[i] = den

    def weighted_values(i):
        e, hd = divmod(i, N_HEADS)
        kh = hd // group
        acc = None
        for si, (_, _, values_t) in enumerate(segs):
            r0, n = seg_rows[si]
            vt = values_t(e, slice(v_rows * kh, v_rows * (kh + 1)))
            part = jnp.dot(vt, p_scr[i % slots, r0:r0 + n, :], preferred_element_type=F32)
            acc = part if acc is None else acc + part
            yield
        den = den_part.pop(i)
        if den_on_mxu:
            mxu_sum = acc[HEAD_DIM:HEAD_DIM + 1, :]
            den = mxu_sum if den is None else den + mxu_sum
        ot_scr[e, HEAD_DIM * hd:HEAD_DIM * (hd + 1), :] = acc[:HEAD_DIM, :] / den

    def finish(e):
        rows = slice(e * tq, (e + 1) * tq)
        for c0 in range(0, D_MODEL, OUT_BLOCK):
            cols = slice(c0, c0 + OUT_BLOCK)
            o = ot_scr[e, cols, :].T
            g_scr[:, cols] = (o * gate_ref[rows, cols].astype(F32)).astype(BF16)
            yield
        for c0 in range(0, D_MODEL, OUT_BLOCK):
            cols = slice(c0, c0 + OUT_BLOCK)
            y = jnp.dot(g_scr[...], wo_ref[:, cols], preferred_element_type=F32)
            xn = x_ref[rows, cols] + mod_ref[:, 2 * D_MODEL + c0:2 * D_MODEL + c0 + OUT_BLOCK] * y
            if final:
                y_scr[:, cols] = xn
            else:
                o_ref[rows, cols] = xn
            yield
        if final:
            o_ref[rows, :] = _rms(y_scr[...], fg_ref[...])

    n_items = subs * N_HEADS
    for step in range(n_items + lag_b + lag_c + 1):
        live = []
        if step < n_items:
            live.append(scores_and_max(step))
        if 0 <= step - lag_b < n_items:
            live.append(exponentials(step - lag_b))
        done = step - lag_b - lag_c
        if 0 <= done < n_items:
            live.append(weighted_values(done))
        if done >= N_HEADS and done % N_HEADS == 0:
            live.append(finish(done // N_HEADS - 1))
        if not interleave:
            for g in live:
                for _ in g:
                    pass
            live = []
        while live:
            live = [g for g in live if next(g, True) is None]


def _attend_latent(kind, q, k_new, vt_new, cache, gate, x, mods_l, w_out, sink, final_g, *, batch, seq,
                   lags):
    lag_b, lag_c, interleave, subs = lags
    tq = TOKEN_TILE
    n_tok = batch * seq
    band = kind == 2
    head_w = MLA_HEAD_PAD if kind == 0 else HEAD_DIM
    kw = k_new.shape[1]
    vw = vt_new.shape[1]
    assert seq % (subs * tq) == 0
    grid = (batch, seq // (subs * tq))
    tok_spec = lambda w: pl.BlockSpec((subs * tq, w), lambda b, t: (b * grid[1] + t, 0))
    qw = q.shape[1]
    if band:
        n_blk = seq // tq
        block = lambda j: (lambda b, t: jnp.clip(subs * t - 1 + j, 0, n_blk - 1))
        in_specs = [tok_spec(qw)]
        in_specs += [pl.BlockSpec((tq, kw), lambda b, t, f=block(j): (b * n_blk + f(b, t), 0))
                     for j in range(subs + 2)]
        in_specs += [pl.BlockSpec((None, vw, tq), lambda b, t, f=block(j): (b, 0, f(b, t)))
                     for j in range(subs + 2)]
    else:
        in_specs = [
            tok_spec(qw),
            pl.BlockSpec((seq, kw), lambda b, t: (b, 0)),
            pl.BlockSpec((None, vw, seq), lambda b, t: (b, 0, 0)),
        ]
    n_own = subs + 2 if band else 1
    args = [q] + [k_new] * n_own + [vt_new] * n_own
    kc, vtc = cache
    n_c = kc.shape[1]
    n_keys = n_c + (3 * tq if band else seq)
    in_specs += [
        pl.BlockSpec((None, n_c, kw), lambda b, t: (b, 0, 0)),
        pl.BlockSpec((None, vw, n_c), lambda b, t: (b, 0, 0)),
    ]
    args += [kc, vtc]
    in_specs += [
        tok_spec(D_MODEL),
        tok_spec(D_MODEL),
        pl.BlockSpec((None, 1, 3 * D_MODEL), lambda b, t: (b, 0, 0)),
        pl.BlockSpec((D_MODEL, D_MODEL), lambda b, t: (0, 0)),
    ]
    args += [gate, x, mods_l, w_out]
    if sink is not None:
        in_specs.append(pl.BlockSpec(memory_space=pltpu.SMEM))
        args.append(sink)
    if final_g is not None:
        in_specs.append(pl.BlockSpec((1, D_MODEL), lambda b, t: (0, 0)))
        args.append(final_g.reshape(1, D_MODEL))
    body = functools.partial(
        _attn_kernel,
        head_w=head_w,
        group=1 if kind == 0 else GROUPS,
        has_cache=True,
        has_sink=sink is not None,
        band=band,
        final=final_g is not None,
        lag_b=lag_b,
        lag_c=lag_c,
        interleave=interleave,
        subs=subs,
        own_keys=False,
        seq=seq,
    )
    slots = max(lag_b, lag_c) + 1
    scratch = [pltpu.VMEM((subs, N_HEADS * HEAD_DIM, tq), F32),
               pltpu.VMEM((slots, n_keys, tq), F32),
               pltpu.VMEM((slots, n_keys, tq), BF16),
               pltpu.VMEM((tq, D_MODEL), BF16),
               pltpu.VMEM((tq, D_MODEL), F32)]
    if band:
        scratch.append(pltpu.VMEM((subs, 3, tq, tq), F32))
    return pl.pallas_call(
        body,
        grid=grid,
        in_specs=in_specs,
        out_specs=tok_spec(D_MODEL),
        out_shape=jax.ShapeDtypeStruct((n_tok, D_MODEL), F32),
        scratch_shapes=scratch,
        compiler_params=pltpu.CompilerParams(
            dimension_semantics=("arbitrary", "arbitrary"), vmem_limit_bytes=VMEM_LIMIT),
        name=f"attn_k{kind}_lat",
    )(*args)


def _ctx_layer_kernel(*refs, kind, n_weights, has_sink, final, lag_b, lag_c, subs, seq, n_prev):
    refs = list(refs)
    x_ref, mod_ref, g_ref = refs[:3]
    w_refs = refs[3:3 + n_weights]
    pos = 3 + n_weights
    tail_in = refs[pos:pos + 1 + has_sink + final]
    pos += len(tail_in)
    prev_refs = refs[pos:pos + 2 * n_prev]
    pos += 2 * n_prev
    o_ref, ca_ref, cb_ref = refs[pos:pos + 3]
    q_s, k_s, vt_s, gate_s = refs[pos + 3:pos + 7]
    attn_scratch = refs[pos + 7:pos + 12]
    if n_prev:
        ca_out, cb_out = ca_ref, cb_ref
        ca_ref, cb_ref = refs[pos + 12:pos + 14]
    if kind == 0:
        _proj_mla_kernel(x_ref, mod_ref, g_ref, *w_refs, q_s, k_s, vt_s, gate_s, ca_ref, cb_ref,
                         rope=False, ctx=True)
    else:
        _proj_gqa_kernel(x_ref, mod_ref, g_ref, *w_refs, q_s, k_s, vt_s, gate_s, ca_ref, cb_ref,
                         qk_norm=(kind == 1), rope=False, ctx=True)
    if n_prev:
        for e in range(subs):
            rows = slice(e * seq, (e + 1) * seq)
            for layer in range(n_prev):
                ca_out[e, layer] = prev_refs[2 * layer][rows, :]
                cb_out[e, layer] = prev_refs[2 * layer + 1][rows, :]
            ca_out[e, n_prev] = ca_ref[rows, :]
            cb_out[e, n_prev] = cb_ref[rows, :]
    _attn_kernel(q_s, k_s, vt_s, gate_s, x_ref, mod_ref, *tail_in, o_ref, *attn_scratch,
                 head_w=MLA_HEAD_PAD if kind == 0 else HEAD_DIM, group=1 if kind == 0 else GROUPS,
                 has_cache=False, has_sink=has_sink, band=False, final=final, lag_b=lag_b, lag_c=lag_c,
                 interleave=True, subs=subs, own_keys=True, seq=seq)


def _ctx_layer(kind, x, mods_l, norm_g, weights, w_out, sink, final_g, *, batch, seq, lags, prev=()):
    lag_b, lag_c, subs = lags
    n_prev = len(prev)
    tq = TOKEN_TILE
    assert seq == tq
    n_tok = batch * seq
    tm = subs * seq
    tok_spec = lambda w: pl.BlockSpec((tm, w), lambda i: (i, 0))
    in_specs = [tok_spec(D_MODEL), pl.BlockSpec((None, 1, 3 * D_MODEL), lambda i: (4, 0, 0)),
                _const_spec((1, D_MODEL))]
    args = [x, mods_l, norm_g.reshape(1, D_MODEL)]
    for w in weights:
        in_specs.append(_const_spec(w.shape))
        args.append(w)
    in_specs.append(_const_spec((D_MODEL, D_MODEL)))
    args.append(w_out)
    if sink is not None:
        in_specs.append(pl.BlockSpec(memory_space=pltpu.SMEM))
        args.append(sink)
    if final_g is not None:
        in_specs.append(_const_spec((1, D_MODEL)))
        args.append(final_g.reshape(1, D_MODEL))
    if kind == 0:
        qw, kw, vw = N_HEADS * MLA_HEAD_PAD, N_HEADS * MLA_HEAD_PAD, N_HEADS * V_ROWS
        cache_shapes = [(n_tok, MLA_KV_LORA), (n_tok, MLA_ROPE)]
        cache_blocks = [(tm, MLA_KV_LORA), (tm, MLA_ROPE)]
    else:
        qw, kw, vw = N_HEADS * HEAD_DIM, KV_HEADS * HEAD_DIM, KV_HEADS * V_ROWS
        cache_shapes = [(n_tok * KV_HEADS, LANES)] * 2
        cache_blocks = [(tm * KV_HEADS, LANES)] * 2
    for pa, pb in prev:
        in_specs += [pl.BlockSpec(cache_blocks[0], lambda i: (i, 0)),
                     pl.BlockSpec(cache_blocks[1], lambda i: (i, 0))]
        args += [pa, pb]
    out_shape = [jax.ShapeDtypeStruct((n_tok, D_MODEL), F32)]
    out_specs = [tok_spec(D_MODEL)]
    for s, b in zip(cache_shapes, cache_blocks):
        if n_prev:
            out_shape.append(jax.ShapeDtypeStruct((batch, n_prev + 1, seq, s[1]), F32))
            out_specs.append(pl.BlockSpec((subs, n_prev + 1, seq, s[1]), lambda i: (i, 0, 0, 0)))
        else:
            out_shape.append(jax.ShapeDtypeStruct(s, F32))
            out_specs.append(pl.BlockSpec(b, lambda i: (i, 0)))
    slots = max(lag_b, lag_c) + 1
    scratch = [pltpu.VMEM((subs, qw, seq), BF16),
               pltpu.VMEM((tm, kw), BF16),
               pltpu.VMEM((subs, vw, seq), BF16),
               pltpu.VMEM((tm, D_MODEL), BF16),
               pltpu.VMEM((subs, N_HEADS * HEAD_DIM, tq), F32),
               pltpu.VMEM((slots, seq, tq), F32),
               pltpu.VMEM((slots, seq, tq), BF16),
               pltpu.VMEM((tq, D_MODEL), BF16),
               pltpu.VMEM((tq, D_MODEL), F32)]
    if n_prev:
        scratch += [pltpu.VMEM(b, F32) for b in cache_blocks]
    body = functools.partial(
        _ctx_layer_kernel, kind=kind, n_weights=len(weights), has_sink=sink is not None,
        final=final_g is not None, lag_b=lag_b, lag_c=lag_c, subs=subs, seq=seq, n_prev=n_prev)
    return pl.pallas_call(
        body,
        grid=(n_tok // tm,),
        in_specs=in_specs,
        out_specs=out_specs,
        out_shape=out_shape,
        scratch_shapes=scratch,
        compiler_params=pltpu.CompilerParams(
            dimension_semantics=("arbitrary",), vmem_limit_bytes=VMEM_LIMIT),
        name=f"layer_k{kind}_ctx",
    )(*args)


def _rope_lane_tables(seq, rot_dim):
    rows = seq // GRID_W
    row = jnp.repeat(jnp.arange(rows), GRID_W).astype(F32)
    col = jnp.tile(jnp.arange(GRID_W), rows).astype(F32)
    nf = rot_dim // 4
    freqs = ROPE_THETA ** (-jnp.arange(nf, dtype=F32) / nf)
    ang = jnp.concatenate([row[:, None] * freqs, col[:, None] * freqs], axis=-1)
    cos, sin = jnp.cos(ang), jnp.sin(ang)
    cos_l = jnp.repeat(cos, 2, axis=-1)
    sin_l = jnp.stack([-sin, sin], axis=-1).reshape(seq, rot_dim)
    return cos_l, sin_l


def _embed_lanes(table, start, fill):
    seq, w = table.shape
    return jnp.concatenate(
        [jnp.full((seq, start), fill, F32), table, jnp.full((seq, LANES - start - w), fill, F32)], axis=1)


def _mla_weights(w_in, q_norm, w_uq, kv_norm, w_ukv):
    c0 = MLA_Q_LORA + MLA_KV_LORA
    win = jnp.concatenate(
        [w_in[:, :c0], w_in[:, c0 + MLA_ROPE:], w_in[:, c0:c0 + MLA_ROPE],
         jnp.zeros((D_MODEL, MLA_IN_PAD - w_in.shape[1]), F32)], axis=1).astype(BF16)
    hq = MLA_NOPE + MLA_ROPE
    wuq = jnp.pad(w_uq.reshape(MLA_Q_LORA, N_HEADS, hq),
                  ((0, 0), (0, 0), (0, MLA_HEAD_PAD - hq))).reshape(MLA_Q_LORA, -1).astype(BF16)
    wkv = w_ukv.reshape(MLA_KV_LORA, N_HEADS, MLA_NOPE + MLA_V)
    wk = wkv[:, :, :MLA_NOPE].reshape(MLA_KV_LORA, -1).astype(BF16)
    wvt = wkv[:, :, MLA_NOPE:].reshape(MLA_KV_LORA, -1).T.astype(BF16)
    return [win, q_norm.reshape(1, -1), wuq, kv_norm.reshape(1, -1), wk, wvt]


def kernel(x_prompt, x_sample, cache_mla_ckv, cache_mla_kpe, cache_gqa_k, cache_gqa_v,
           cache_swa_k, cache_swa_v, c, c_ctx, norm_g, w_ada, b_ada, w_out,
           mla_w_in, mla_q_norm, mla_w_uq, mla_kv_norm, mla_w_ukv,
           gqa_w_in, gqa_q_norm, gqa_k_norm, swa_w_in, swa_sink, final_norm_g):
    bc, sc, _ = x_prompt.shape
    bl, sl, _ = x_sample.shape
    n_past = cache_mla_ckv.shape[2]
    kvw = KV_HEADS * HEAD_DIM

    cc = jnp.concatenate([c, c_ctx[None, :], jnp.zeros((8 - bl - 1, D_MODEL), F32)], axis=0)
    mods = _ada_mods(cc, w_ada, b_ada).reshape(DEPTH, 8, 1, 3 * D_MODEL)

    cos_h, sin_h = _rope_lane_tables(sl, HEAD_DIM)
    gqa_tables = [jnp.tile(cos_h, (1, 2)), jnp.tile(sin_h, (1, 2))]
    cos_r, sin_r = _rope_lane_tables(sl, MLA_ROPE)
    mla_tables = [_embed_lanes(cos_r, MLA_NOPE, 1.0), _embed_lanes(sin_r, MLA_NOPE, 0.0),
                  _embed_lanes(cos_r, 0, 1.0), _embed_lanes(sin_r, 0, 0.0)]

    w_out_bf = w_out.astype(BF16)

    xc = x_prompt.reshape(bc * sc, D_MODEL)
    xl = x_sample.reshape(bl * sl, D_MODEL)
    new_caches = {0: ([], []), 1: ([], []), 2: ([], [])}
    for i in range(DEPTH):
        kind, j = i % 3, i // 3
        mods_l = mods[i]
        wo = w_out_bf[i]
        final_g = final_norm_g if i == DEPTH - 1 else None
        sink = None
        if kind == 0:
            weights = _mla_weights(mla_w_in[j], mla_q_norm[j], mla_w_uq[j], mla_kv_norm[j], mla_w_ukv[j])
            tables = mla_tables
            cache_raw = (cache_mla_ckv[:, j],
                         jnp.pad(cache_mla_kpe[:, j], ((0, 0), (0, 0), (0, LANES - MLA_ROPE))))
        else:
            tables = gqa_tables
            if kind == 1:
                weights = [gqa_w_in[j].astype(BF16),
                           jnp.tile(gqa_q_norm[j], 2).reshape(1, LANES),
                           jnp.tile(gqa_k_norm[j], 2).reshape(1, LANES)]
                cache_raw = (cache_gqa_k[:, j].reshape(bl, n_past, kvw),
                             cache_gqa_v[:, j].reshape(bl, n_past, kvw))
            else:
                weights = [swa_w_in[j].astype(BF16)]
                sink = swa_sink[j]
                cache_raw = (cache_swa_k[:, j].reshape(bl, n_past, kvw),
                             cache_swa_v[:, j].reshape(bl, n_past, kvw))

        last_mla = kind == 0 and j == mla_w_in.shape[0] - 1
        prev = tuple(zip(*new_caches[0])) if last_mla else ()
        xc, c_a, c_b = _ctx_layer(kind, xc, mods_l, norm_g[i], weights, wo, sink, final_g,
                                  batch=bc, seq=sc, lags=CTX_LAYER_LAGS, prev=prev)
        if last_mla:
            mla_stacked = (c_a, c_b)
        else:
            new_caches[kind][0].append(c_a)
            new_caches[kind][1].append(c_b)

        q, k, vt, gate, kc, vtc = _project_latent(kind, xl, mods_l, norm_g[i], weights, tables, cache_raw,
                                                  batch=bl, seq=sl)
        xl = _attend_latent(kind, q, k, vt, (kc, vtc), gate, xl, mods_l, wo, sink, final_g,
                            batch=bl, seq=sl, lags=LAT_STAGE_LAGS[kind])

    def stack(parts, tail):
        if len(tail) == 2:
            parts = [p[:, :HEAD_DIM] for p in parts]
        return jnp.stack([p.reshape((bc, sc) + tail) for p in parts], axis=1)

    return (xc.reshape(bc, sc, D_MODEL), xl.reshape(bl, sl, D_MODEL),
            mla_stacked[0].reshape(bc, -1, sc, MLA_KV_LORA), mla_stacked[1].reshape(bc, -1, sc, MLA_ROPE),
            stack(new_caches[1][0], (KV_HEADS, HEAD_DIM)), stack(new_caches[1][1], (KV_HEADS, HEAD_DIM)),
            stack(new_caches[2][0], (KV_HEADS, HEAD_DIM)), stack(new_caches[2][1], (KV_HEADS, HEAD_DIM)))
```

```python
import functools

import jax
import jax.numpy as jnp
from jax import lax
from jax.experimental import pallas as pl
from jax.experimental.pallas import tpu as pltpu

F32 = jnp.float32
BF16 = jnp.bfloat16

D_MODEL = 1024
DEPTH = 4
GRID_W = 64
N_HEADS = 16
HEAD_DIM = 64
KV_HEADS = 4
GROUPS = N_HEADS // KV_HEADS
MLA_Q_LORA = 384
MLA_KV_LORA = 256
MLA_NOPE = 64
MLA_ROPE = 32
MLA_V = 64
WINDOW = 128
ROPE_THETA = 10000.0
EPS = 1e-6
NEG_INF = -1e30

LOG2E = 1.4426950408889634
GQA_Q_SCALE = HEAD_DIM ** -0.5 * LOG2E
MLA_Q_SCALE = (MLA_NOPE + MLA_ROPE) ** -0.5 * LOG2E

LANES = 128
TOKEN_TILE = 256
PROJ_TILE = 512
KEY_CHUNK = 256
OUT_BLOCK = 256
CTX_LAYER_LAGS = (4, 4, 2)
LAT_STAGE_LAGS = {0: (2, 2, True, 2), 1: (2, 2, False, 2), 2: (2, 2, False, 2)}
MLA_HEAD_PAD = 128
V_ROWS = HEAD_DIM + 16
MLA_IN_PAD = 1792
VMEM_LIMIT = 56 * 1024 * 1024
NT_DIMS = (((1,), (1,)), ((), ()))


def _silu(v):
    return v * jax.nn.sigmoid(v)


def _rms(v, g):
    return v * lax.rsqrt(jnp.mean(v * v, axis=-1, keepdims=True) + EPS) * g


def _mod_norm(x, g, mod):
    shift = mod[:, :D_MODEL]
    scale = mod[:, D_MODEL:2 * D_MODEL]
    return _rms(x, g) * (1.0 + scale) + shift


def _even_lanes(shape):
    return (lax.broadcasted_iota(jnp.int32, shape, 1) & 1) == 0


def _rope_tile(v, cos, sin_signed, even):
    nxt = pltpu.roll(v, LANES - 1, axis=1)
    prv = pltpu.roll(v, 1, axis=1)
    return v * cos + jnp.where(even, nxt, prv) * sin_signed


def _ada_kernel(c_ref, w_ref, b_ref, o_ref):
    s = _silu(c_ref[...]).astype(BF16)
    o_ref[...] = jnp.dot(s, w_ref[...].astype(BF16), preferred_element_type=F32) + b_ref[...]


def _ada_mods(cc, w_ada, b_ada):
    tn = D_MODEL
    return pl.pallas_call(
        _ada_kernel,
        grid=(DEPTH, 3 * D_MODEL // tn),
        in_specs=[
            pl.BlockSpec((8, D_MODEL), lambda i, n: (0, 0)),
            pl.BlockSpec((None, D_MODEL, tn), lambda i, n: (i, 0, n)),
            pl.BlockSpec((None, 1, tn), lambda i, n: (i, 0, n)),
        ],
        out_specs=pl.BlockSpec((None, 8, tn), lambda i, n: (i, 0, n)),
        out_shape=jax.ShapeDtypeStruct((DEPTH, 8, 3 * D_MODEL), F32),
        compiler_params=pltpu.CompilerParams(vmem_limit_bytes=VMEM_LIMIT),
        name="ada_mods",
    )(cc, w_ada, b_ada.reshape(DEPTH, 1, 3 * D_MODEL))


def _store_queries(q_out, col0, blk):
    w = blk.shape[1]
    if len(q_out.shape) == 2:
        q_out[:, col0:col0 + w] = blk.astype(BF16)
        return
    blk_t = blk.T.astype(BF16)
    seq = q_out.shape[2]
    for e in range(q_out.shape[0]):
        q_out[e, col0:col0 + w, :] = blk_t[:, e * seq:(e + 1) * seq]


def _store_values_t(vt_out, vt, n_heads):
    if len(vt_out.shape) == 3:
        seq = vt_out.shape[2]
        for e in range(vt_out.shape[0]):
            _store_values_t(vt_out.at[e], vt[:, e * seq:(e + 1) * seq], n_heads)
        return
    v_rows = vt_out.shape[0] // n_heads
    if v_rows == HEAD_DIM:
        vt_out[...] = vt.astype(BF16)
        return
    ones = jnp.ones((v_rows - HEAD_DIM, vt.shape[1]), BF16)
    for h in range(n_heads):
        vt_out[v_rows * h:v_rows * h + HEAD_DIM, :] = vt[HEAD_DIM * h:HEAD_DIM * (h + 1), :].astype(BF16)
        vt_out[v_rows * h + HEAD_DIM:v_rows * (h + 1), :] = ones


def _mla_expand(ckv_bf, kpe, wk_ref, wvt_ref, k_out, vt_out):
    k = jnp.dot(ckv_bf, wk_ref[...], preferred_element_type=F32)
    lane = lax.broadcasted_iota(jnp.int32, kpe.shape, 1)
    rope_lanes = (lane >= MLA_NOPE) & (lane < MLA_NOPE + MLA_ROPE)
    tail = jnp.where(rope_lanes, pltpu.roll(kpe, MLA_NOPE, axis=1), 0.0)
    for hd in range(N_HEADS):
        pair = k[:, LANES * (hd // 2):LANES * (hd // 2 + 1)]
        if hd % 2:
            pair = pltpu.roll(pair, MLA_NOPE, axis=1)
        k_out[:, MLA_HEAD_PAD * hd:MLA_HEAD_PAD * (hd + 1)] = jnp.where(lane < MLA_NOPE, pair, tail).astype(BF16)
    vt = lax.dot_general(wvt_ref[...], ckv_bf, NT_DIMS, preferred_element_type=F32)
    _store_values_t(vt_out, vt, N_HEADS)


def _proj_mla_kernel(*refs, rope, ctx):
    refs = list(refs)
    x_ref, mod_ref, g_ref, win_ref, qn_ref, wuq_ref, kvn_ref, wk_ref, wvt_ref = refs[:9]
    pos = 9
    if rope:
        cq_ref, sq_ref, ck_ref, sk_ref = refs[pos:pos + 4]
        pos += 4
    q_out, k_out, vt_out, gate_out = refs[pos:pos + 4]
    pos += 4
    if ctx:
        ckv_out, kpe_out = refs[pos:pos + 2]

    h = _mod_norm(x_ref[...], g_ref[...], mod_ref[...]).astype(BF16)
    z = jnp.dot(h, win_ref[...], preferred_element_type=F32)
    cq = z[:, :MLA_Q_LORA]
    ckv = z[:, MLA_Q_LORA:MLA_Q_LORA + MLA_KV_LORA]
    gate = z[:, 640:640 + D_MODEL]
    kpe = z[:, 640 + D_MODEL:]
    gate_out[...] = _silu(gate).astype(BF16)

    qf = jnp.dot(_rms(cq, qn_ref[...]).astype(BF16), wuq_ref[...], preferred_element_type=F32)
    ckvn = _rms(ckv, kvn_ref[...])
    if ctx:
        ckv_out[...] = ckvn
        kpe_out[...] = kpe[:, :MLA_ROPE]
    if rope:
        even = _even_lanes(kpe.shape)
        cos_q, sin_q = cq_ref[...], sq_ref[...]
        kpe = _rope_tile(kpe, ck_ref[...], sk_ref[...], even)
    _mla_expand(ckvn.astype(BF16), kpe, wk_ref, wvt_ref, k_out, vt_out)
    for hd in range(N_HEADS):
        blk = qf[:, MLA_HEAD_PAD * hd:MLA_HEAD_PAD * (hd + 1)]
        if rope:
            blk = _rope_tile(blk, cos_q, sin_q, even)
        _store_queries(q_out, MLA_HEAD_PAD * hd, blk * MLA_Q_SCALE)


def _store_head_rows(out_ref, pair, blk):
    tokens = blk.shape[0]
    out_ref[pl.ds(2 * pair, tokens, stride=KV_HEADS), :] = blk
    out_ref[pl.ds(2 * pair + 1, tokens, stride=KV_HEADS), :] = pltpu.roll(blk, HEAD_DIM, axis=1)


def _proj_gqa_kernel(*refs, qk_norm, rope, ctx):
    refs = list(refs)
    x_ref, mod_ref, g_ref, win_ref = refs[:4]
    pos = 4
    if qk_norm:
        qn_ref, kn_ref = refs[pos:pos + 2]
        pos += 2
    if rope:
        cos_ref, sin_ref = refs[pos:pos + 2]
        pos += 2
    q_out, k_out, vt_out, gate_out = refs[pos:pos + 4]
    pos += 4
    if ctx:
        kc_out, vc_out = refs[pos:pos + 2]

    nq = N_HEADS * HEAD_DIM
    nk = KV_HEADS * HEAD_DIM
    h = _mod_norm(x_ref[...], g_ref[...], mod_ref[...]).astype(BF16)
    z = jnp.dot(h, win_ref[...], preferred_element_type=F32)
    tm = z.shape[0]
    lo = lax.broadcasted_iota(jnp.int32, (tm, LANES), 1) < HEAD_DIM
    if rope:
        even = _even_lanes((tm, LANES))
        cos, sin = cos_ref[...], sin_ref[...]
    n_q_tiles = nq // LANES
    for c in range((nq + nk) // LANES):
        blk = z[:, LANES * c:LANES * (c + 1)]
        is_q = c < n_q_tiles
        if qk_norm:
            sq = blk * blk
            s_lo = jnp.sum(jnp.where(lo, sq, 0.0), axis=-1, keepdims=True)
            s_hi = jnp.sum(jnp.where(lo, 0.0, sq), axis=-1, keepdims=True)
            inv = jnp.where(lo, lax.rsqrt(s_lo * (1.0 / HEAD_DIM) + EPS),
                            lax.rsqrt(s_hi * (1.0 / HEAD_DIM) + EPS))
            blk = blk * inv * (qn_ref[...] if is_q else kn_ref[...])
        if ctx and not is_q:
            _store_head_rows(kc_out, c - n_q_tiles, blk)
        if rope:
            blk = _rope_tile(blk, cos, sin, even)
        if is_q:
            _store_queries(q_out, LANES * c, blk * GQA_Q_SCALE)
        else:
            k_out[:, LANES * (c - n_q_tiles):LANES * (c - n_q_tiles + 1)] = blk.astype(BF16)
    v = z[:, nq + nk:nq + 2 * nk]
    if ctx:
        for c in range(nk // LANES):
            _store_head_rows(vc_out, c, v[:, LANES * c:LANES * (c + 1)])
    _store_values_t(vt_out, v.T, KV_HEADS)
    gate_out[...] = _silu(z[:, nq + 2 * nk:]).astype(BF16)


def _const_spec(shape):
    return pl.BlockSpec(shape, lambda i: (0,) * len(shape))


def _proj_latent_kernel(*refs, kind, n_in, tiles_per_seq):
    refs = list(refs)
    proj_in, cache_in = refs[:n_in], refs[n_in:n_in + 2]
    proj_out, cache_out = refs[n_in + 2:n_in + 6], refs[n_in + 6:]
    if kind == 0:
        _proj_mla_kernel(*proj_in, *proj_out, rope=True, ctx=False)
    else:
        _proj_gqa_kernel(*proj_in, *proj_out, qk_norm=(kind == 1), rope=True, ctx=False)

    @pl.when(pl.program_id(0) % tiles_per_seq == 0)
    def _():
        if kind == 0:
            _mla_expand(cache_in[0][...].astype(BF16), cache_in[1][...], proj_in[3 + 4], proj_in[3 + 5],
                        cache_out[0], cache_out[1])
        else:
            cache_out[0][...] = cache_in[0][...].astype(BF16)
            _store_values_t(cache_out[1], cache_in[1][...].T, KV_HEADS)


def _project_latent(kind, x, mods_l, norm_g, weights, tables, cache_raw, *, batch, seq):
    n_tok = batch * seq
    tm = PROJ_TILE
    tiles_per_seq = seq // tm
    n_past = cache_raw[0].shape[1]
    in_specs = [
        pl.BlockSpec((tm, D_MODEL), lambda i: (i, 0)),
        pl.BlockSpec((None, 1, 3 * D_MODEL), lambda i: (i // tiles_per_seq, 0, 0)),
        _const_spec((1, D_MODEL)),
    ]
    args = [x, mods_l, norm_g.reshape(1, D_MODEL)]
    for w in weights:
        in_specs.append(_const_spec(w.shape))
        args.append(w)
    for t in tables:
        in_specs.append(pl.BlockSpec((tm, LANES), lambda i: (i % tiles_per_seq, 0)))
        args.append(t)
    n_in = len(args)
    for c in cache_raw:
        in_specs.append(pl.BlockSpec((None,) + c.shape[1:], lambda i: (i // tiles_per_seq, 0, 0)))
        args.append(c)
    if kind == 0:
        qw, kw, vw = N_HEADS * MLA_HEAD_PAD, N_HEADS * MLA_HEAD_PAD, N_HEADS * HEAD_DIM
    else:
        qw, kw, vw = N_HEADS * HEAD_DIM, KV_HEADS * HEAD_DIM, KV_HEADS * HEAD_DIM
    body = functools.partial(_proj_latent_kernel, kind=kind, n_in=n_in, tiles_per_seq=tiles_per_seq)
    out_shape = [
        jax.ShapeDtypeStruct((n_tok, qw), BF16),
        jax.ShapeDtypeStruct((n_tok, kw), BF16),
        jax.ShapeDtypeStruct((batch, vw, seq), BF16),
        jax.ShapeDtypeStruct((n_tok, D_MODEL), BF16),
        jax.ShapeDtypeStruct((batch, n_past, kw), BF16),
        jax.ShapeDtypeStruct((batch, vw, n_past), BF16),
    ]
    out_specs = [
        pl.BlockSpec((tm, qw), lambda i: (i, 0)),
        pl.BlockSpec((tm, kw), lambda i: (i, 0)),
        pl.BlockSpec((None, vw, tm), lambda i: (i // tiles_per_seq, 0, i % tiles_per_seq)),
        pl.BlockSpec((tm, D_MODEL), lambda i: (i, 0)),
        pl.BlockSpec((None, n_past, kw), lambda i: (i // tiles_per_seq, 0, 0)),
        pl.BlockSpec((None, vw, n_past), lambda i: (i // tiles_per_seq, 0, 0)),
    ]
    return pl.pallas_call(
        body,
        grid=(n_tok // tm,),
        in_specs=in_specs,
        out_specs=out_specs,
        out_shape=out_shape,
        compiler_params=pltpu.CompilerParams(
            dimension_semantics=("arbitrary",), vmem_limit_bytes=VMEM_LIMIT),
        name=f"proj_k{kind}_lat",
    )(*args)


def _attn_kernel(*refs, head_w, group, has_cache, has_sink, band, final, lag_b, lag_c, interleave,
                 subs, own_keys, seq):
    refs = list(refs)
    q_ref = refs[0]
    n_own = subs + 2 if band else 1
    kn_refs = refs[1:1 + n_own]
    vtn_refs = refs[1 + n_own:1 + 2 * n_own]
    kn_ref, vtn_ref = kn_refs[0], vtn_refs[0]
    pos = 1 + 2 * n_own
    if has_cache:
        kc_ref, vtc_ref = refs[pos:pos + 2]
        pos += 2
    gate_ref, x_ref, mod_ref, wo_ref = refs[pos:pos + 4]
    pos += 4
    if has_sink:
        sink_ref = refs[pos]
        pos += 1
    if final:
        fg_ref = refs[pos]
        pos += 1
    o_ref, ot_scr, s_scr, p_scr, g_scr, y_scr = refs[pos:pos + 6]
    if band:
        bias_scr = refs[pos + 6]

    tq = TOKEN_TILE
    ch = KEY_CHUNK
    slots = s_scr.shape[0]
    segs = []
    if has_cache:
        segs.append((kc_ref.shape[0],
                     lambda e, r, n, cols: kc_ref[r:r + n, cols],
                     lambda e, rows: vtc_ref[rows, :]))
    band_of = {}
    if own_keys:
        segs.append((seq,
                     lambda e, r, n, cols: kn_ref[e * seq + r:e * seq + r + n, cols],
                     lambda e, rows: vtn_ref[e, rows, :]))
    elif band:
        for a in range(3):
            band_of[len(segs)] = a
            segs.append((tq,
                         lambda e, r, n, cols, a=a: kn_refs[e + a][r:r + n, cols],
                         lambda e, rows, a=a: vtn_refs[e + a][rows, :]))
    else:
        segs.append((seq,
                     lambda e, r, n, cols: kn_ref[r:r + n, cols],
                     lambda e, rows: vtn_ref[rows, :]))
    seg_rows = []
    chunks = []
    off = 0
    for si, (n_seg, _, _) in enumerate(segs):
        seg_rows.append((off, n_seg))
        chunks += [(si, r, off + r) for r in range(0, n_seg, ch)]
        off += n_seg
    if interleave:
        score_chunks = [c + (ch,) for c in chunks]
    else:
        score_chunks = [(si, 0, r0, n) for si, (r0, n) in enumerate(seg_rows)]
    if band:
        row = lax.broadcasted_iota(jnp.int32, (tq, tq), 0)
        col = lax.broadcasted_iota(jnp.int32, (tq, tq), 1)
        for e in range(subs):
            tile = pl.program_id(1) * subs + e
            for a in range(3):
                key_block = tile - 1 + a
                ok = ((jnp.abs((a - 1) * tq + row - col) <= WINDOW)
                      & (key_block >= 0) & (key_block < seq // tq))
                bias_scr[e, a] = jnp.where(ok, 0.0, NEG_INF)

    assert lag_b >= 1 and lag_c >= 1 and slots > max(lag_b, lag_c)
    row_max = {}
    den_part = {}
    v_rows = vtn_ref.shape[-2] * group // N_HEADS
    den_on_mxu = v_rows > HEAD_DIM

    def scores_and_max(i):
        e, hd = divmod(i, N_HEADS)
        kh = hd // group
        q_cols = slice(head_w * hd, head_w * (hd + 1))
        qh = q_ref[e, q_cols, :] if own_keys else q_ref[e * tq:(e + 1) * tq, q_cols]
        mx = None
        for si, r, g, n in score_chunks:
            kk = segs[si][1](e, r, n, slice(head_w * kh, head_w * (kh + 1)))
            if own_keys:
                s = jnp.dot(kk, qh, preferred_element_type=F32)
            else:
                s = lax.dot_general(kk, qh, NT_DIMS, preferred_element_type=F32)
            if si in band_of:
                s = s + bias_scr[e, band_of[si], r:r + n, :]
            s_scr[i % slots, g:g + n, :] = s
            part = jnp.max(s.reshape(n // 8, 8, tq), axis=0)
            mx = part if mx is None else jnp.maximum(mx, part)
            yield
        m = jnp.max(mx, axis=0, keepdims=True)
        if has_sink:
            m = jnp.maximum(m, sink_ref[hd] * LOG2E)
        row_max[i] = m

    def exponentials(i):
        hd = i % N_HEADS
        m = row_max.pop(i)
        mb = jnp.broadcast_to(m, (8, tq))
        tot = None
        for _, _, g in chunks:
            p = jnp.exp2(s_scr[i % slots, g:g + ch, :].reshape(ch // 8, 8, tq) - mb)
            if not den_on_mxu:
                part = jnp.sum(p, axis=0)
                tot = part if tot is None else tot + part
            p_scr[i % slots, g:g + ch, :] = p.reshape(ch, tq).astype(BF16)
            yield
        den = None if den_on_mxu else jnp.sum(tot, axis=0, keepdims=True)
        if has_sink:
            sink = jnp.exp2(sink_ref[hd] * LOG2E - m)
            den = sink if den is None else den + sink
        den_part[i] = den

    def weighted_values(i):
        e, hd = divmod(i, N_HEADS)
        kh = hd // group
        acc = None
        for si, (_, _, values_t) in enumerate(segs):
            r0, n = seg_rows[si]
            vt = values_t(e, slice(v_rows * kh, v_rows * (kh + 1)))
            part = jnp.dot(vt, p_scr[i % slots, r0:r0 + n, :], preferred_element_type=F32)
            acc = part if acc is None else acc + part
            yield
        den = den_part.pop(i)
        if den_on_mxu:
            mxu_sum = acc[HEAD_DIM:HEAD_DIM + 1, :]
            den = mxu_sum if den is None else den + mxu_sum
        ot_scr[e, HEAD_DIM * hd:HEAD_DIM * (hd + 1), :] = acc[:HEAD_DIM, :] / den

    def finish(e):
        rows = slice(e * tq, (e + 1) * tq)
        for c0 in range(0, D_MODEL, OUT_BLOCK):
            cols = slice(c0, c0 + OUT_BLOCK)
            o = ot_scr[e, cols, :].T
            g_scr[:, cols] = (o * gate_ref[rows, cols].astype(F32)).astype(BF16)
            yield
        for c0 in range(0, D_MODEL, OUT_BLOCK):
            cols = slice(c0, c0 + OUT_BLOCK)
            y = jnp.dot(g_scr[...], wo_ref[:, cols], preferred_element_type=F32)
            xn = x_ref[rows, cols] + mod_ref[:, 2 * D_MODEL + c0:2 * D_MODEL + c0 + OUT_BLOCK] * y
            if final:
                y_scr[:, cols] = xn
            else:
                o_ref[rows, cols] = xn
            yield
        if final:
            o_ref[rows, :] = _rms(y_scr[...], fg_ref[...])

    n_items = subs * N_HEADS
    for step in range(n_items + lag_b + lag_c + 1):
        live = []
        if step < n_items:
            live.append(scores_and_max(step))
        if 0 <= step - lag_b < n_items:
            live.append(exponentials(step - lag_b))
        done = step - lag_b - lag_c
        if 0 <= done < n_items:
            live.append(weighted_values(done))
        if done >= N_HEADS and done % N_HEADS == 0:
            live.append(finish(done // N_HEADS - 1))
        if not interleave:
            for g in live:
                for _ in g:
                    pass
            live = []
        while live:
            live = [g for g in live if next(g, True) is None]


def _attend_latent(kind, q, k_new, vt_new, cache, gate, x, mods_l, w_out, sink, final_g, *, batch, seq,
                   lags):
    lag_b, lag_c, interleave, subs = lags
    tq = TOKEN_TILE
    n_tok = batch * seq
    band = kind == 2
    head_w = MLA_HEAD_PAD if kind == 0 else HEAD_DIM
    kw = k_new.shape[1]
    vw = vt_new.shape[1]
    assert seq % (subs * tq) == 0
    grid = (batch, seq // (subs * tq))
    tok_spec = lambda w: pl.BlockSpec((subs * tq, w), lambda b, t: (b * grid[1] + t, 0))
    qw = q.shape[1]
    if band:
        n_blk = seq // tq
        block = lambda j: (lambda b, t: jnp.clip(subs * t - 1 + j, 0, n_blk - 1))
        in_specs = [tok_spec(qw)]
        in_specs += [pl.BlockSpec((tq, kw), lambda b, t, f=block(j): (b * n_blk + f(b, t), 0))
                     for j in range(subs + 2)]
        in_specs += [pl.BlockSpec((None, vw, tq), lambda b, t, f=block(j): (b, 0, f(b, t)))
                     for j in range(subs + 2)]
    else:
        in_specs = [
            tok_spec(qw),
            pl.BlockSpec((seq, kw), lambda b, t: (b, 0)),
            pl.BlockSpec((None, vw, seq), lambda b, t: (b, 0, 0)),
        ]
    n_own = subs + 2 if band else 1
    args = [q] + [k_new] * n_own + [vt_new] * n_own
    kc, vtc = cache
    n_c = kc.shape[1]
    n_keys = n_c + (3 * tq if band else seq)
    in_specs += [
        pl.BlockSpec((None, n_c, kw), lambda b, t: (b, 0, 0)),
        pl.BlockSpec((None, vw, n_c), lambda b, t: (b, 0, 0)),
    ]
    args += [kc, vtc]
    in_specs += [
        tok_spec(D_MODEL),
        tok_spec(D_MODEL),
        pl.BlockSpec((None, 1, 3 * D_MODEL), lambda b, t: (b, 0, 0)),
        pl.BlockSpec((D_MODEL, D_MODEL), lambda b, t: (0, 0)),
    ]
    args += [gate, x, mods_l, w_out]
    if sink is not None:
        in_specs.append(pl.BlockSpec(memory_space=pltpu.SMEM))
        args.append(sink)
    if final_g is not None:
        in_specs.append(pl.BlockSpec((1, D_MODEL), lambda b, t: (0, 0)))
        args.append(final_g.reshape(1, D_MODEL))
    body = functools.partial(
        _attn_kernel,
        head_w=head_w,
        group=1 if kind == 0 else GROUPS,
        has_cache=True,
        has_sink=sink is not None,
        band=band,
        final=final_g is not None,
        lag_b=lag_b,
        lag_c=lag_c,
        interleave=interleave,
        subs=subs,
        own_keys=False,
        seq=seq,
    )
    slots = max(lag_b, lag_c) + 1
    scratch = [pltpu.VMEM((subs, N_HEADS * HEAD_DIM, tq), F32),
               pltpu.VMEM((slots, n_keys, tq), F32),
               pltpu.VMEM((slots, n_keys, tq), BF16),
               pltpu.VMEM((tq, D_MODEL), BF16),
               pltpu.VMEM((tq, D_MODEL), F32)]
    if band:
        scratch.append(pltpu.VMEM((subs, 3, tq, tq), F32))
    return pl.pallas_call(
        body,
        grid=grid,
        in_specs=in_specs,
        out_specs=tok_spec(D_MODEL),
        out_shape=jax.ShapeDtypeStruct((n_tok, D_MODEL), F32),
        scratch_shapes=scratch,
        compiler_params=pltpu.CompilerParams(
            dimension_semantics=("arbitrary", "arbitrary"), vmem_limit_bytes=VMEM_LIMIT),
        name=f"attn_k{kind}_lat",
    )(*args)


def _ctx_layer_kernel(*refs, kind, n_weights, has_sink, final, lag_b, lag_c, subs, seq, n_prev):
    refs = list(refs)
    x_ref, mod_ref, g_ref = refs[:3]
    w_refs = refs[3:3 + n_weights]
    pos = 3 + n_weights
    tail_in = refs[pos:pos + 1 + has_sink + final]
    pos += len(tail_in)
    prev_refs = refs[pos:pos + 2 * n_prev]
    pos += 2 * n_prev
    o_ref, ca_ref, cb_ref = refs[pos:pos + 3]
    q_s, k_s, vt_s, gate_s = refs[pos + 3:pos + 7]
    attn_scratch = refs[pos + 7:pos + 12]
    if n_prev:
        ca_out, cb_out = ca_ref, cb_ref
        ca_ref, cb_ref = refs[pos + 12:pos + 14]
    if kind == 0:
        _proj_mla_kernel(x_ref, mod_ref, g_ref, *w_refs, q_s, k_s, vt_s, gate_s, ca_ref, cb_ref,
                         rope=False, ctx=True)
    else:
        _proj_gqa_kernel(x_ref, mod_ref, g_ref, *w_refs, q_s, k_s, vt_s, gate_s, ca_ref, cb_ref,
                         qk_norm=(kind == 1), rope=False, ctx=True)
    if n_prev:
        for e in range(subs):
            rows = slice(e * seq, (e + 1) * seq)
            for layer in range(n_prev):
                ca_out[e, layer] = prev_refs[2 * layer][rows, :]
                cb_out[e, layer] = prev_refs[2 * layer + 1][rows, :]
            ca_out[e, n_prev] = ca_ref[rows, :]
            cb_out[e, n_prev] = cb_ref[rows, :]
    _attn_kernel(q_s, k_s, vt_s, gate_s, x_ref, mod_ref, *tail_in, o_ref, *attn_scratch,
                 head_w=MLA_HEAD_PAD if kind == 0 else HEAD_DIM, group=1 if kind == 0 else GROUPS,
                 has_cache=False, has_sink=has_sink, band=False, final=final, lag_b=lag_b, lag_c=lag_c,
                 interleave=True, subs=subs, own_keys=True, seq=seq)


def _ctx_layer(kind, x, mods_l, norm_g, weights, w_out, sink, final_g, *, batch, seq, lags, prev=()):
    lag_b, lag_c, subs = lags
    n_prev = len(prev)
    tq = TOKEN_TILE
    assert seq == tq
    n_tok = batch * seq
    tm = subs * seq
    tok_spec = lambda w: pl.BlockSpec((tm, w), lambda i: (i, 0))
    in_specs = [tok_spec(D_MODEL), pl.BlockSpec((None, 1, 3 * D_MODEL), lambda i: (4, 0, 0)),
                _const_spec((1, D_MODEL))]
    args = [x, mods_l, norm_g.reshape(1, D_MODEL)]
    for w in weights:
        in_specs.append(_const_spec(w.shape))
        args.append(w)
    in_specs.append(_const_spec((D_MODEL, D_MODEL)))
    args.append(w_out)
    if sink is not None:
        in_specs.append(pl.BlockSpec(memory_space=pltpu.SMEM))
        args.append(sink)
    if final_g is not None:
        in_specs.append(_const_spec((1, D_MODEL)))
        args.append(final_g.reshape(1, D_MODEL))
    if kind == 0:
        qw, kw, vw = N_HEADS * MLA_HEAD_PAD, N_HEADS * MLA_HEAD_PAD, N_HEADS * V_ROWS
        cache_shapes = [(n_tok, MLA_KV_LORA), (n_tok, MLA_ROPE)]
        cache_blocks = [(tm, MLA_KV_LORA), (tm, MLA_ROPE)]
    else:
        qw, kw, vw = N_HEADS * HEAD_DIM, KV_HEADS * HEAD_DIM, KV_HEADS * V_ROWS
        cache_shapes = [(n_tok * KV_HEADS, LANES)] * 2
        cache_blocks = [(tm * KV_HEADS, LANES)] * 2
    for pa, pb in prev:
        in_specs += [pl.BlockSpec(cache_blocks[0], lambda i: (i, 0)),
                     pl.BlockSpec(cache_blocks[1], lambda i: (i, 0))]
        args += [pa, pb]
    out_shape = [jax.ShapeDtypeStruct((n_tok, D_MODEL), F32)]
    out_specs = [tok_spec(D_MODEL)]
    for s, b in zip(cache_shapes, cache_blocks):
        if n_prev:
            out_shape.append(jax.ShapeDtypeStruct((batch, n_prev + 1, seq, s[1]), F32))
            out_specs.append(pl.BlockSpec((subs, n_prev + 1, seq, s[1]), lambda i: (i, 0, 0, 0)))
        else:
            out_shape.append(jax.ShapeDtypeStruct(s, F32))
            out_specs.append(pl.BlockSpec(b, lambda i: (i, 0)))
    slots = max(lag_b, lag_c) + 1
    scratch = [pltpu.VMEM((subs, qw, seq), BF16),
               pltpu.VMEM((tm, kw), BF16),
               pltpu.VMEM((subs, vw, seq), BF16),
               pltpu.VMEM((tm, D_MODEL), BF16),
               pltpu.VMEM((subs, N_HEADS * HEAD_DIM, tq), F32),
               pltpu.VMEM((slots, seq, tq), F32),
               pltpu.VMEM((slots, seq, tq), BF16),
               pltpu.VMEM((tq, D_MODEL), BF16),
               pltpu.VMEM((tq, D_MODEL), F32)]
    if n_prev:
        scratch += [pltpu.VMEM(b, F32) for b in cache_blocks]
    body = functools.partial(
        _ctx_layer_kernel, kind=kind, n_weights=len(weights), has_sink=sink is not None,
        final=final_g is not None, lag_b=lag_b, lag_c=lag_c, subs=subs, seq=seq, n_prev=n_prev)
    return pl.pallas_call(
        body,
        grid=(n_tok // tm,),
        in_specs=in_specs,
        out_specs=out_specs,
        out_shape=out_shape,
        scratch_shapes=scratch,
        compiler_params=pltpu.CompilerParams(
            dimension_semantics=("arbitrary",), vmem_limit_bytes=VMEM_LIMIT),
        name=f"layer_k{kind}_ctx",
    )(*args)


def _rope_lane_tables(seq, rot_dim):
    rows = seq // GRID_W
    row = jnp.repeat(jnp.arange(rows), GRID_W).astype(F32)
    col = jnp.tile(jnp.arange(GRID_W), rows).astype(F32)
    nf = rot_dim // 4
    freqs = ROPE_THETA ** (-jnp.arange(nf, dtype=F32) / nf)
    ang = jnp.concatenate([row[:, None] * freqs, col[:, None] * freqs], axis=-1)
    cos, sin = jnp.cos(ang), jnp.sin(ang)
    cos_l = jnp.repeat(cos, 2, axis=-1)
    sin_l = jnp.stack([-sin, sin], axis=-1).reshape(seq, rot_dim)
    return cos_l, sin_l


def _embed_lanes(table, start, fill):
    seq, w = table.shape
    return jnp.concatenate(
        [jnp.full((seq, start), fill, F32), table, jnp.full((seq, LANES - start - w), fill, F32)], axis=1)


def _mla_weights(w_in, q_norm, w_uq, kv_norm, w_ukv):
    c0 = MLA_Q_LORA + MLA_KV_LORA
    win = jnp.concatenate(
        [w_in[:, :c0], w_in[:, c0 + MLA_ROPE:], w_in[:, c0:c0 + MLA_ROPE],
         jnp.zeros((D_MODEL, MLA_IN_PAD - w_in.shape[1]), F32)], axis=1).astype(BF16)
    hq = MLA_NOPE + MLA_ROPE
    wuq = jnp.pad(w_uq.reshape(MLA_Q_LORA, N_HEADS, hq),
                  ((0, 0), (0, 0), (0, MLA_HEAD_PAD - hq))).reshape(MLA_Q_LORA, -1).astype(BF16)
    wkv = w_ukv.reshape(MLA_KV_LORA, N_HEADS, MLA_NOPE + MLA_V)
    wk = wkv[:, :, :MLA_NOPE].reshape(MLA_KV_LORA, -1).astype(BF16)
    wvt = wkv[:, :, MLA_NOPE:].reshape(MLA_KV_LORA, -1).T.astype(BF16)
    return [win, q_norm.reshape(1, -1), wuq, kv_norm.reshape(1, -1), wk, wvt]


def kernel(x_prompt, x_sample, cache_mla_ckv, cache_mla_kpe, cache_gqa_k, cache_gqa_v,
           cache_swa_k, cache_swa_v, c, c_ctx, norm_g, w_ada, b_ada, w_out,
           mla_w_in, mla_q_norm, mla_w_uq, mla_kv_norm, mla_w_ukv,
           gqa_w_in, gqa_q_norm, gqa_k_norm, swa_w_in, swa_sink, final_norm_g):
    bc, sc, _ = x_prompt.shape
    bl, sl, _ = x_sample.shape
    n_past = cache_mla_ckv.shape[2]
    kvw = KV_HEADS * HEAD_DIM

    cc = jnp.concatenate([c, c_ctx[None, :], jnp.zeros((8 - bl - 1, D_MODEL), F32)], axis=0)
    mods = _ada_mods(cc, w_ada, b_ada).reshape(DEPTH, 8, 1, 3 * D_MODEL)

    cos_h, sin_h = _rope_lane_tables(sl, HEAD_DIM)
    gqa_tables = [jnp.tile(cos_h, (1, 2)), jnp.tile(sin_h, (1, 2))]
    cos_r, sin_r = _rope_lane_tables(sl, MLA_ROPE)
    mla_tables = [_embed_lanes(cos_r, MLA_NOPE, 1.0), _embed_lanes(sin_r, MLA_NOPE, 0.0),
                  _embed_lanes(cos_r, 0, 1.0), _embed_lanes(sin_r, 0, 0.0)]

    w_out_bf = w_out.astype(BF16)

    xc = x_prompt.reshape(bc * sc, D_MODEL)
    xl = x_sample.reshape(bl * sl, D_MODEL)
    new_caches = {0: ([], []), 1: ([], []), 2: ([], [])}
    for i in range(DEPTH):
        kind, j = i % 3, i // 3
        mods_l = mods[i]
        wo = w_out_bf[i]
        final_g = final_norm_g if i == DEPTH - 1 else None
        sink = None
        if kind == 0:
            weights = _mla_weights(mla_w_in[j], mla_q_norm[j], mla_w_uq[j], mla_kv_norm[j], mla_w_ukv[j])
            tables = mla_tables
            cache_raw = (cache_mla_ckv[:, j],
                         jnp.pad(cache_mla_kpe[:, j], ((0, 0), (0, 0), (0, LANES - MLA_ROPE))))
        else:
            tables = gqa_tables
            if kind == 1:
                weights = [gqa_w_in[j].astype(BF16),
                           jnp.tile(gqa_q_norm[j], 2).reshape(1, LANES),
                           jnp.tile(gqa_k_norm[j], 2).reshape(1, LANES)]
                cache_raw = (cache_gqa_k[:, j].reshape(bl, n_past, kvw),
                             cache_gqa_v[:, j].reshape(bl, n_past, kvw))
            else:
                weights = [swa_w_in[j].astype(BF16)]
                sink = swa_sink[j]
                cache_raw = (cache_swa_k[:, j].reshape(bl, n_past, kvw),
                             cache_swa_v[:, j].reshape(bl, n_past, kvw))

        last_mla = kind == 0 and j == mla_w_in.shape[0] - 1
        prev = tuple(zip(*new_caches[0])) if last_mla else ()
        xc, c_a, c_b = _ctx_layer(kind, xc, mods_l, norm_g[i], weights, wo, sink, final_g,
                                  batch=bc, seq=sc, lags=CTX_LAYER_LAGS, prev=prev)
        if last_mla:
            mla_stacked = (c_a, c_b)
        else:
            new_caches[kind][0].append(c_a)
            new_caches[kind][1].append(c_b)

        q, k, vt, gate, kc, vtc = _project_latent(kind, xl, mods_l, norm_g[i], weights, tables, cache_raw,
                                                  batch=bl, seq=sl)
        xl = _attend_latent(kind, q, k, vt, (kc, vtc), gate, xl, mods_l, wo, sink, final_g,
                            batch=bl, seq=sl, lags=LAT_STAGE_LAGS[kind])

    def stack(parts, tail):
        if len(tail) == 2:
            parts = [p[:, :HEAD_DIM] for p in parts]
        return jnp.stack([p.reshape((bc, sc) + tail) for p in parts], axis=1)

    return (xc.reshape(bc, sc, D_MODEL), xl.reshape(bl, sl, D_MODEL),
            mla_stacked[0].reshape(bc, -1, sc, MLA_KV_LORA), mla_stacked[1].reshape(bc, -1, sc, MLA_ROPE),
            stack(new_caches[1][0], (KV_HEADS, HEAD_DIM)), stack(new_caches[1][1], (KV_HEADS, HEAD_DIM)),
            stack(new_caches[2][0], (KV_HEADS, HEAD_DIM)), stack(new_caches[2][1], (KV_HEADS, HEAD_DIM)))
```
